```python
import math
import numpy as np
import jax
import jax.numpy as jnp
from jax import lax

D_MODEL = 1024
BATCH = 4
SEQ = 8192
DEPTH = 4

GRID_W = 64
CTX_LEN = 256
EPS = 1e-6
ROPE_BASE = 10000.0

NA_HEADS = 8
NA_DH = 64
NA_WIN_H = 8
NA_WIN_W = 16
GDN_HEADS = 4
GDN_DK = 128
GDN_DV = 128
GDN_CONV = 3
GDN_CHUNK = 64
MLA_HEADS = 8
MLA_NOPE = 64
MLA_ROPE = 32
MLA_V = 64
MLA_Q_RANK = 256
MLA_KV_RANK = 128
MLA_QBLOCK = 128
ML_HEADS = 4
ML_DK = 64
ML_DV = 128
ML_CHUNK = 64
N_BRANCH = 4
BRANCH_W = 512
N_EXPERTS = 32
TOP_K = 4
D_FF = 1024
SWIGLU_ALPHA = 1.702
SWIGLU_LIMIT = 7.0
MOE_BLOCK = 256

IN_SPLITS = (
    NA_HEADS * NA_DH, NA_HEADS * NA_DH, NA_HEADS * NA_DH,
    GDN_HEADS * (2 * GDN_DK + GDN_DV), GDN_HEADS * GDN_DV,
    2 * GDN_HEADS, 2 * GDN_HEADS,
    MLA_Q_RANK, MLA_KV_RANK + MLA_ROPE,
    ML_HEADS * ML_DK, ML_HEADS * ML_DK, ML_HEADS * ML_DV, ML_HEADS * ML_DV,
    2 * ML_HEADS, 2 * ML_HEADS,
)
D_IN = sum(IN_SPLITS)

kernel_name = 'hybrid_diffusion_parallel_mixers_moe'


def rmsnorm(x, g):
    xf = x.astype(jnp.float32)
    y = xf * lax.rsqrt(jnp.mean(xf * xf, axis=-1, keepdims=True) + EPS)
    return (y * g.astype(jnp.float32)).astype(x.dtype)


def l2norm(x):
    xf = x.astype(jnp.float32)
    return (xf * lax.rsqrt(jnp.sum(xf * xf, axis=-1, keepdims=True) + EPS)).astype(x.dtype)


def softmax_f32(s, dtype):
    return jax.nn.softmax(s.astype(jnp.float32), axis=-1).astype(dtype)


def to_heads(t, n_heads):
    b, s, _ = t.shape
    return t.reshape(b, s, n_heads, -1).transpose(0, 2, 1, 3)


def from_heads(t):
    b, h, s, d = t.shape
    return t.transpose(0, 2, 1, 3).reshape(b, s, h * d)


def flip_t(t):
    return jnp.flip(t, axis=2)


def short_conv(x, w):
    k = w.shape[0]
    pad = k // 2
    s = x.shape[1]
    xp = jnp.pad(x, ((0, 0), (pad, pad), (0, 0)))
    y = xp[:, 0:s] * w[0]
    for j in range(1, k):
        y = y + xp[:, j:j + s] * w[j]
    return y


def rope2d(x):
    s = x.shape[-2]
    half = x.shape[-1] // 2
    t = jnp.arange(s)
    inv = ROPE_BASE ** (-jnp.arange(0, half, 2, dtype=jnp.float32) / half)

    def rot(xa, pos):
        ang = pos.astype(jnp.float32)[:, None] * inv
        cos, sin = jnp.cos(ang).astype(x.dtype), jnp.sin(ang).astype(x.dtype)
        x1, x2 = jnp.split(xa, 2, axis=-1)
        return jnp.concatenate([x1 * cos - x2 * sin, x2 * cos + x1 * sin], axis=-1)

    return jnp.concatenate([rot(x[..., :half], t // GRID_W), rot(x[..., half:], t % GRID_W)], axis=-1)


def context_attention(q, k, v):
    s = jnp.einsum('bhqd,bhkd->bhqk', q, k) * (q.shape[-1] ** -0.5)
    return from_heads(jnp.einsum('bhqk,bhkd->bhqd', softmax_f32(s, v.dtype), v))


def neighborhood_attention(q, k, v, k_ctx, v_ctx, rpb):
    b, s, _ = q.shape
    rows = s // GRID_W
    wh, ww = min(NA_WIN_H, rows), NA_WIN_W
    grid = lambda t: t.reshape(b, rows, GRID_W, NA_HEADS, NA_DH)
    qg = jnp.moveaxis(grid(q) * (NA_DH ** -0.5), 1, 0)
    kg, vg = grid(k), grid(v)
    cols = np.arange(GRID_W)
    c0 = np.clip(cols - ww // 2, 0, GRID_W - ww)
    col_idx = c0[:, None] + np.arange(ww)[None, :]
    rpb_cols = rpb[:, :, col_idx - cols[:, None] + (NA_WIN_W - 1)]

    def row_block(args):
        r, q_r = args
        r0 = jnp.clip(r - wh // 2, 0, rows - wh)
        k_r = lax.dynamic_slice_in_dim(kg, r0, wh, axis=1)[:, :, col_idx]
        v_r = lax.dynamic_slice_in_dim(vg, r0, wh, axis=1)[:, :, col_idx]
        bias = jnp.take(rpb_cols, r0 + jnp.arange(wh) - r + (NA_WIN_H - 1), axis=1)
        s_win = jnp.einsum('bchd,bacwhd->bhcaw', q_r, k_r) + bias.transpose(0, 2, 1, 3)
        s_ctx = jnp.einsum('bchd,bhjd->bhcj', q_r, k_ctx)
        p = softmax_f32(jnp.concatenate([s_win.reshape(b, NA_HEADS, GRID_W, wh * ww), s_ctx], axis=-1), v.dtype)
        p_win = p[..., :wh * ww].reshape(b, NA_HEADS, GRID_W, wh, ww)
        return (jnp.einsum('bhcaw,bacwhd->bchd', p_win, v_r)
                + jnp.einsum('bhcj,bhjd->bchd', p[..., wh * ww:], v_ctx))

    o = lax.map(row_block, (jnp.arange(rows), qg))
    return jnp.moveaxis(o, 0, 1).reshape(b, s, NA_HEADS * NA_DH)


def mla_expand(pq, pkv, q_norm_g, w_uq, kv_norm_g, w_ukv, rotate):
    q = to_heads(rmsnorm(pq, q_norm_g) @ w_uq, MLA_HEADS)
    kv = to_heads(rmsnorm(pkv[..., :MLA_KV_RANK], kv_norm_g) @ w_ukv, MLA_HEADS)
    q_nope, q_pe = q[..., :MLA_NOPE], q[..., MLA_NOPE:]
    k_nope, v = kv[..., :MLA_NOPE], kv[..., MLA_NOPE:]
    k_pe = pkv[..., MLA_KV_RANK:]
    if rotate:
        q_pe, k_pe = rope2d(q_pe), rope2d(k_pe)
    return q_nope, q_pe, k_nope, k_pe, v


def mla_scores(qn, qp, kn, kp):
    s = jnp.einsum('bhqd,bhkd->bhqk', qn, kn) + jnp.einsum('bhqr,bkr->bhqk', qp, kp)
    return s * ((MLA_NOPE + MLA_ROPE) ** -0.5)


def mla_latent_attention(qn, qp, kn, kp, v):
    b, h, s, _ = qn.shape
    nb = s // MLA_QBLOCK
    blocks = lambda t: jnp.moveaxis(t.reshape(b, h, nb, MLA_QBLOCK, t.shape[-1]), 2, 0)

    def q_block(args):
        qn_b, qp_b = args
        return jnp.einsum('bhqk,bhkd->bhqd', softmax_f32(mla_scores(qn_b, qp_b, kn, kp), v.dtype), v)

    o = lax.map(q_block, (blocks(qn), blocks(qp)))
    return from_heads(jnp.moveaxis(o, 0, 2).reshape(b, h, s, MLA_V))


def gdn_prep(p_qkv, p_a, p_b, conv_w, a_log, dt_bias):
    b, s, _ = p_a.shape
    qkv = jax.nn.silu(short_conv(p_qkv, conv_w))
    q, k, v = jnp.split(qkv, [GDN_HEADS * GDN_DK, 2 * GDN_HEADS * GDN_DK], axis=-1)
    q = l2norm(to_heads(q, GDN_HEADS)) * (GDN_DK ** -0.5)
    k = l2norm(to_heads(k, GDN_HEADS))
    v = to_heads(v, GDN_HEADS)
    a = p_a.reshape(b, s, 2, GDN_HEADS).astype(jnp.float32)
    log_alpha = -jnp.exp(a_log.astype(jnp.float32)) * jax.nn.softplus(a + dt_bias.astype(jnp.float32))
    beta = jax.nn.sigmoid(p_b.reshape(b, s, 2, GDN_HEADS).astype(jnp.float32))
    return q, k, v, log_alpha.transpose(2, 0, 3, 1), beta.transpose(2, 0, 3, 1)


def gdn_scan(q, k, v, log_a, beta, state):
    f32 = jnp.float32
    q, k, v, log_a, beta = (t.astype(f32) for t in (q, k, v, log_a, beta))
    b, h, s, dk = q.shape
    dv = v.shape[-1]
    L = GDN_CHUNK
    nc = s // L
    chunk = lambda t: t.reshape(b, h, nc, L, *t.shape[3:])
    q, k, v, log_a, beta = (chunk(t) for t in (q, k, v, log_a, beta))
    g = jnp.cumsum(log_a, axis=-1)
    idx = jnp.arange(L)
    decay = jnp.exp(jnp.where(idx[:, None] >= idx[None, :], g[..., :, None] - g[..., None, :], -jnp.inf))
    strict = jnp.where(idx[:, None] > idx[None, :], decay, 0.0)
    a_mat = jnp.eye(L, dtype=f32) + beta[..., :, None] * strict * jnp.einsum('bhnid,bhnjd->bhnij', k, k)
    rhs = jnp.concatenate([beta[..., None] * v, (beta * jnp.exp(g))[..., None] * k], axis=-1)
    sol = lax.linalg.triangular_solve(a_mat, rhs, left_side=True, lower=True, unit_diagonal=True)
    u0, w = sol[..., :dv], sol[..., dv:]
    qk = decay * jnp.einsum('bhnid,bhnjd->bhnij', q, k)
    g_end = g[..., -1]
    k_end = jnp.exp(g_end[..., None] - g)[..., None] * k
    q_dec = jnp.exp(g)[..., None] * q

    def step(S, xs):
        u0_c, w_c, qk_c, k_end_c, q_dec_c, e_end = xs
        u = u0_c - jnp.einsum('bhlk,bhvk->bhlv', w_c, S)
        o = jnp.einsum('bhlk,bhvk->bhlv', q_dec_c, S) + jnp.einsum('bhij,bhjv->bhiv', qk_c, u)
        S = e_end[..., None, None] * S + jnp.einsum('bhlv,bhlk->bhvk', u, k_end_c)
        return S, o

    xs = tuple(jnp.moveaxis(t, 2, 0) for t in (u0, w, qk, k_end, q_dec, jnp.exp(g_end)))
    state, o = lax.scan(step, state, xs)
    return jnp.moveaxis(o, 0, 2).reshape(b, h, s, dv), state


def gdn_output(o, p_g, norm_g):
    b, s, _ = p_g.shape
    y = rmsnorm(o.transpose(0, 2, 1, 3), norm_g) * jax.nn.silu(p_g.reshape(b, s, GDN_HEADS, GDN_DV).astype(jnp.float32))
    return y.reshape(b, s, GDN_HEADS * GDN_DV).astype(p_g.dtype)


def mlstm_prep(pq, pk, pv, p_i, p_f, b_if):
    b, s, _ = p_i.shape
    q = to_heads(pq, ML_HEADS) * (ML_DK ** -0.5)
    k = to_heads(pk, ML_HEADS)
    v = to_heads(pv, ML_HEADS)
    b_if = b_if.astype(jnp.float32)
    i_pre = p_i.reshape(b, s, 2, ML_HEADS).astype(jnp.float32) + b_if[:, 0]
    log_f = jax.nn.log_sigmoid(p_f.reshape(b, s, 2, ML_HEADS).astype(jnp.float32) + b_if[:, 1])
    return q, k, v, i_pre.transpose(2, 0, 3, 1), log_f.transpose(2, 0, 3, 1)


def mlstm_scan(q, k, v, i_pre, log_f, state):
    f32 = jnp.float32
    q, k, v, i_pre, log_f = (t.astype(f32) for t in (q, k, v, i_pre, log_f))
    b, h, s, dk = q.shape
    dv = v.shape[-1]
    L = ML_CHUNK
    nc = s // L
    chunk = lambda t: t.reshape(b, h, nc, L, *t.shape[3:])
    q, k, v, i_pre, log_f = (chunk(t) for t in (q, k, v, i_pre, log_f))
    fcum = jnp.cumsum(log_f, axis=-1)
    idx = jnp.arange(L)
    d_log = jnp.where(idx[:, None] >= idx[None, :],
                      fcum[..., :, None] - fcum[..., None, :] + i_pre[..., None, :], -jnp.inf)
    qk = jnp.einsum('bhnid,bhnjd->bhnij', q, k)
    f_end = fcum[..., -1]
    w_log = f_end[..., None] - fcum + i_pre
    d_max = jnp.max(d_log, axis=-1)
    w_max = jnp.max(w_log, axis=-1)

    def step(carry, xs):
        C, n, m = carry
        q_c, k_c, v_c, fc_c, dl_c, qk_c, fe_c, wl_c, dm_c, wm_c = xs
        m_q = jnp.maximum(fc_c + m[..., None], dm_c)
        inter = jnp.exp(fc_c + m[..., None] - m_q)
        intra = jnp.exp(dl_c - m_q[..., None]) * qk_c
        num = inter[..., None] * jnp.einsum('bhlk,bhvk->bhlv', q_c, C) + jnp.einsum('bhij,bhjv->bhiv', intra, v_c)
        den = inter * jnp.einsum('bhlk,bhk->bhl', q_c, n) + jnp.sum(intra, axis=-1)
        h_c = num / jnp.maximum(jnp.abs(den), jnp.exp(-m_q))[..., None]
        m_new = jnp.maximum(fe_c + m, wm_c)
        dec = jnp.exp(fe_c + m - m_new)
        wk = jnp.exp(wl_c - m_new[..., None])
        C = dec[..., None, None] * C + jnp.einsum('bhlv,bhlk->bhvk', wk[..., None] * v_c, k_c)
        n = dec[..., None] * n + jnp.einsum('bhl,bhlk->bhk', wk, k_c)
        return (C, n, m_new), h_c

    xs = tuple(jnp.moveaxis(t, 2, 0) for t in (q, k, v, fcum, d_log, qk, f_end, w_log, d_max, w_max))
    state, o = lax.scan(step, state, xs)
    return jnp.moveaxis(o, 0, 2).reshape(b, h, s, dv), state


def mlstm_output(o, p_o, norm_g):
    b, s, _ = p_o.shape
    y = rmsnorm(o.transpose(0, 2, 1, 3), norm_g) * jax.nn.sigmoid(p_o.reshape(b, s, ML_HEADS, ML_DV).astype(jnp.float32))
    return y.reshape(b, s, ML_HEADS * ML_DV).astype(p_o.dtype)


def bidirectional(scan_fn, state0, ctx_in, lat_in):
    o_ctx, o_lat = 0.0, 0.0
    for d in range(2):
        c_in = ctx_in[:3] + (ctx_in[3][d], ctx_in[4][d])
        l_in = lat_in[:3] + (lat_in[3][d], lat_in[4][d])
        if d == 1:
            c_in = tuple(flip_t(t) for t in c_in)
            l_in = tuple(flip_t(t) for t in l_in)
        oc, st = scan_fn(*c_in, state0)
        ol, _ = scan_fn(*l_in, st)
        if d == 1:
            oc, ol = flip_t(oc), flip_t(ol)
        o_ctx = o_ctx + oc
        o_lat = o_lat + ol
    return o_ctx, o_lat


def merge(h, ys, w_mg, b_mg, w_bo, w_out):
    z = jnp.einsum('btnw,nwd->btnd', jnp.stack(ys, axis=2), w_bo)
    g = jax.nn.sigmoid(h @ w_mg + b_mg).reshape(*h.shape[:-1], N_BRANCH, D_MODEL)
    return jnp.sum(g * z, axis=2) @ w_out


def token_mixer(h_lat, h_ctx, w_in, na_rpb, gdn_conv_w, gdn_a_log, gdn_dt_bias, gdn_norm_g,
                mla_q_norm_g, mla_w_uq, mla_kv_norm_g, mla_w_ukv, ml_b_if, ml_norm_g,
                w_mg, b_mg, w_bo, w_out, with_ctx_out):
    bsz = h_lat.shape[0]
    cuts = np.cumsum(IN_SPLITS)[:-1].tolist()
    pl = jnp.split(h_lat @ w_in, cuts, axis=-1)
    pc = jnp.split(h_ctx @ w_in, cuts, axis=-1)
    k_ctx_na, v_ctx_na = to_heads(pc[1], NA_HEADS), to_heads(pc[2], NA_HEADS)
    ya_lat = neighborhood_attention(pl[0], pl[1], pl[2], k_ctx_na, v_ctx_na, na_rpb)
    gdn_in = lambda p: gdn_prep(p[3], p[5], p[6], gdn_conv_w, gdn_a_log, gdn_dt_bias)
    s0 = jnp.zeros((bsz, GDN_HEADS, GDN_DV, GDN_DK), jnp.float32)
    ob_ctx, ob_lat = bidirectional(gdn_scan, s0, gdn_in(pc), gdn_in(pl))
    cq = mla_expand(pc[7], pc[8], mla_q_norm_g, mla_w_uq, mla_kv_norm_g, mla_w_ukv, False)
    lq = mla_expand(pl[7], pl[8], mla_q_norm_g, mla_w_uq, mla_kv_norm_g, mla_w_ukv, True)
    kn_all = jnp.concatenate([cq[2], lq[2]], axis=2)
    kp_all = jnp.concatenate([cq[3], lq[3]], axis=1)
    v_all = jnp.concatenate([cq[4], lq[4]], axis=2)
    yc_lat = mla_latent_attention(lq[0], lq[1], kn_all, kp_all, v_all)
    ml_in = lambda p: mlstm_prep(p[9], p[10], p[11], p[13], p[14], ml_b_if)
    st0 = (jnp.zeros((bsz, ML_HEADS, ML_DV, ML_DK), jnp.float32),
           jnp.zeros((bsz, ML_HEADS, ML_DK), jnp.float32),
           jnp.zeros((bsz, ML_HEADS), jnp.float32))
    od_ctx, od_lat = bidirectional(mlstm_scan, st0, ml_in(pc), ml_in(pl))
    out_lat = merge(h_lat, [ya_lat, gdn_output(ob_lat, pl[4], gdn_norm_g), yc_lat,
                            mlstm_output(od_lat, pl[12], ml_norm_g)], w_mg, b_mg, w_bo, w_out)
    if not with_ctx_out:
        return out_lat, None
    ya_ctx = context_attention(to_heads(pc[0], NA_HEADS), k_ctx_na, v_ctx_na)
    yc_ctx = from_heads(jnp.einsum('bhqk,bhkd->bhqd',
                                   softmax_f32(mla_scores(cq[0], cq[1], cq[2], cq[3]), cq[4].dtype), cq[4]))
    out_ctx = merge(h_ctx, [ya_ctx, gdn_output(ob_ctx, pc[4], gdn_norm_g), yc_ctx,
                            mlstm_output(od_ctx, pc[12], ml_norm_g)], w_mg, b_mg, w_bo, w_out)
    return out_lat, out_ctx


def moe_ffn(t, router_w, router_b, w_gu, b_gu, w_dn, b_dn):
    n, d = t.shape
    logits = t.astype(jnp.float32) @ router_w.astype(jnp.float32) + router_b.astype(jnp.float32)
    top_v, top_i = lax.top_k(logits, TOP_K)
    top_w = jax.nn.softmax(top_v, axis=-1).astype(t.dtype)
    nk = n * TOP_K
    e_flat = top_i.reshape(nk)
    w_flat = top_w.reshape(nk)
    tok_flat = jnp.arange(nk, dtype=jnp.int32) // TOP_K
    order = jnp.argsort(e_flat, stable=True)
    e_s, tok_s, w_s = e_flat[order], tok_flat[order], w_flat[order]
    counts = jnp.bincount(e_flat, length=N_EXPERTS)
    padded = (counts + MOE_BLOCK - 1) // MOE_BLOCK * MOE_BLOCK
    pad_end = jnp.cumsum(padded)
    dest = (pad_end - padded)[e_s] + jnp.arange(nk) - (jnp.cumsum(counts) - counts)[e_s]
    n_blocks = -(-nk // MOE_BLOCK) + N_EXPERTS
    buf = jnp.zeros((n_blocks * MOE_BLOCK, d), t.dtype).at[dest].set(t[tok_s])
    block_e = jnp.minimum(jnp.searchsorted(pad_end, jnp.arange(n_blocks) * MOE_BLOCK, side='right'), N_EXPERTS - 1)

    def expert_block(args):
        xb, e = args
        gu = xb @ w_gu[e] + b_gu[e]
        gate = jnp.minimum(gu[:, :D_FF], SWIGLU_LIMIT)
        up = jnp.clip(gu[:, D_FF:], -SWIGLU_LIMIT, SWIGLU_LIMIT)
        return ((up + 1.0) * gate * jax.nn.sigmoid(SWIGLU_ALPHA * gate)) @ w_dn[e] + b_dn[e]

    y = lax.map(expert_block, (buf.reshape(n_blocks, MOE_BLOCK, d), block_e)).reshape(-1, d)
    return jax.ops.segment_sum(y[dest] * w_s[:, None], tok_s, num_segments=n)


def setup_inputs(seed: int = 0) -> dict:
    key = jax.random.key(seed)
    ks = iter(jax.random.split(key, 40))
    L, D = DEPTH, D_MODEL

    def nrm(shape, scale):
        return jax.random.normal(next(ks), shape, jnp.float32) * scale

    def gain(shape):
        return 1.0 + nrm(shape, 0.02)

    dt = jnp.exp(jax.random.uniform(next(ks), (L, 2, GDN_HEADS), jnp.float32, math.log(1e-3), math.log(1e-1)))
    gdn_dt_bias = dt + jnp.log(-jnp.expm1(-dt))
    gdn_a_log = jnp.log(jax.random.uniform(next(ks), (L, 2, GDN_HEADS), jnp.float32, 1.0, 16.0))
    ml_b_if = jnp.stack([nrm((L, 2, ML_HEADS), 0.1), 3.0 + nrm((L, 2, ML_HEADS), 0.5)], axis=2)
    return {
        'x': nrm((BATCH, SEQ, D), 1.0),
        'c': nrm((BATCH, D), 1.0),
        'ctx': nrm((BATCH, CTX_LEN, D), 1.0),
        'c_ctx': nrm((D,), 1.0),
        'w_ada': nrm((L, D, 6 * D), 0.5 * D ** -0.5),
        'b_ada': nrm((L, 6 * D), 0.02),
        'norm1_g': gain((L, D)),
        'norm2_g': gain((L, D)),
        'w_in': nrm((L, D, D_IN), D ** -0.5),
        'na_rpb': nrm((L, NA_HEADS, 2 * NA_WIN_H - 1, 2 * NA_WIN_W - 1), 0.1),
        'gdn_conv_w': nrm((L, GDN_CONV, GDN_HEADS * (2 * GDN_DK + GDN_DV)), GDN_CONV ** -0.5),
        'gdn_a_log': gdn_a_log,
        'gdn_dt_bias': gdn_dt_bias,
        'gdn_norm_g': gain((L, GDN_DV)),
        'mla_q_norm_g': gain((L, MLA_Q_RANK)),
        'mla_w_uq': nrm((L, MLA_Q_RANK, MLA_HEADS * (MLA_NOPE + MLA_ROPE)), MLA_Q_RANK ** -0.5),
        'mla_kv_norm_g': gain((L, MLA_KV_RANK)),
        'mla_w_ukv': nrm((L, MLA_KV_RANK, MLA_HEADS * (MLA_NOPE + MLA_V)), MLA_KV_RANK ** -0.5),
        'ml_b_if': ml_b_if,
        'ml_norm_g': gain((L, ML_DV)),
        'w_mg': nrm((L, D, N_BRANCH * D), D ** -0.5),
        'b_mg': nrm((L, N_BRANCH * D), 0.02),
        'w_bo': nrm((L, N_BRANCH, BRANCH_W, D), BRANCH_W ** -0.5),
        'w_out': nrm((L, D, D), D ** -0.5),
        'router_w': nrm((L, D, N_EXPERTS), D ** -0.5),
        'router_b': nrm((L, N_EXPERTS), 0.01),
        'w_gu': nrm((L, N_EXPERTS, D, 2 * D_FF), D ** -0.5),
        'b_gu': nrm((L, N_EXPERTS, 2 * D_FF), 0.01),
        'w_dn': nrm((L, N_EXPERTS, D_FF, D), D_FF ** -0.5),
        'b_dn': nrm((L, N_EXPERTS, D), 0.01),
        'final_g': gain((D,)),
    }


def reference(x, c, ctx, c_ctx, w_ada, b_ada, norm1_g, norm2_g, w_in, na_rpb, gdn_conv_w, gdn_a_log,
              gdn_dt_bias, gdn_norm_g, mla_q_norm_g, mla_w_uq, mla_kv_norm_g, mla_w_ukv, ml_b_if, ml_norm_g,
              w_mg, b_mg, w_bo, w_out, router_w, router_b, w_gu, b_gu, w_dn, b_dn, final_g):
    silu_c = jax.nn.silu(c)
    silu_cc = jax.nn.silu(c_ctx)
    for l in range(DEPTH):
        with_ctx_out = l < DEPTH - 1
        mod_lat = (silu_c @ w_ada[l] + b_ada[l])[:, None, :]
        mod_ctx = silu_cc @ w_ada[l] + b_ada[l]
        sh1, sc1, g1, sh2, sc2, g2 = jnp.split(mod_lat, 6, axis=-1)
        csh1, csc1, cg1, csh2, csc2, cg2 = jnp.split(mod_ctx, 6, axis=-1)
        h_lat = rmsnorm(x, norm1_g[l]) * (1.0 + sc1) + sh1
        h_ctx = rmsnorm(ctx, norm1_g[l]) * (1.0 + csc1) + csh1
        m_lat, m_ctx = token_mixer(h_lat, h_ctx, w_in[l], na_rpb[l], gdn_conv_w[l], gdn_a_log[l], gdn_dt_bias[l],
                                   gdn_norm_g[l], mla_q_norm_g[l], mla_w_uq[l], mla_kv_norm_g[l], mla_w_ukv[l],
                                   ml_b_if[l], ml_norm_g[l], w_mg[l], b_mg[l], w_bo[l], w_out[l], with_ctx_out)
        x = x + g1 * m_lat
        h2_lat = rmsnorm(x, norm2_g[l]) * (1.0 + sc2) + sh2
        if with_ctx_out:
            ctx = ctx + cg1 * m_ctx
            h2_ctx = rmsnorm(ctx, norm2_g[l]) * (1.0 + csc2) + csh2
            n_ctx = h2_ctx.shape[0] * h2_ctx.shape[1]
            tokens = jnp.concatenate([h2_ctx.reshape(-1, D_MODEL), h2_lat.reshape(-1, D_MODEL)], axis=0)
            f = moe_ffn(tokens, router_w[l], router_b[l], w_gu[l], b_gu[l], w_dn[l], b_dn[l])
            ctx = ctx + cg2 * f[:n_ctx].reshape(ctx.shape)
            f_lat = f[n_ctx:].reshape(x.shape)
        else:
            f_lat = moe_ffn(h2_lat.reshape(-1, D_MODEL), router_w[l], router_b[l], w_gu[l], b_gu[l],
                            w_dn[l], b_dn[l]).reshape(x.shape)
        x = x + g2 * f_lat
    return rmsnorm(x, final_g)
```

```python
import functools
import math

import numpy as np
import jax
import jax.numpy as jnp
from jax import lax
from jax.experimental import pallas as pl
from jax.experimental.pallas import tpu as pltpu

F32 = jnp.float32
BF16 = jnp.bfloat16

GRID_W = 64
EPS = 1e-6
ROPE_BASE = 10000.0
NA_HEADS, NA_DH, NA_WIN_H, NA_WIN_W = 8, 64, 8, 16
GDN_HEADS, GDN_DK, GDN_DV, GDN_CHUNK = 4, 128, 128, 64
MLA_HEADS, MLA_NOPE, MLA_ROPE, MLA_V, MLA_Q_RANK, MLA_KV_RANK = 8, 64, 32, 64, 256, 128
ML_HEADS, ML_DK, ML_DV, ML_CHUNK = 4, 64, 128, 64
N_BRANCH, BRANCH_W = 4, 512
N_EXPERTS, TOP_K, D_FF = 32, 4, 1024
SWIGLU_ALPHA, SWIGLU_LIMIT = 1.702, 7.0
MOE_ROWS = 256

IN_SPLITS = (
    NA_HEADS * NA_DH, NA_HEADS * NA_DH, NA_HEADS * NA_DH,
    GDN_HEADS * (2 * GDN_DK + GDN_DV), GDN_HEADS * GDN_DV,
    2 * GDN_HEADS, 2 * GDN_HEADS,
    MLA_Q_RANK, MLA_KV_RANK + MLA_ROPE,
    ML_HEADS * ML_DK, ML_HEADS * ML_DK, ML_HEADS * ML_DV, ML_HEADS * ML_DV,
    2 * ML_HEADS, 2 * ML_HEADS,
)

VMEM_LIMIT_BYTES = 56 * 1024 * 1024


def _params(*sem):
    return pltpu.CompilerParams(dimension_semantics=sem, vmem_limit_bytes=VMEM_LIMIT_BYTES)


def _mm_kernel(x_ref, w_ref, o_ref):
    o_ref[...] = jnp.dot(x_ref[...].astype(BF16), w_ref[...].astype(BF16),
                         preferred_element_type=F32).astype(o_ref.dtype)


def pmm(x, w, out_dtype=F32, tm=512, tn=512):
    m, k = x.shape
    n = w.shape[1]
    tm, tn = min(tm, m), min(tn, n)
    if n % tn:
        tn = n
    assert m % tm == 0 and n % tn == 0, (m, n, tm, tn)
    return pl.pallas_call(
        _mm_kernel,
        grid=(m // tm, n // tn),
        in_specs=[pl.BlockSpec((tm, k), lambda i, j: (i, 0)),
                  pl.BlockSpec((k, tn), lambda i, j: (0, j))],
        out_specs=pl.BlockSpec((tm, tn), lambda i, j: (i, j)),
        out_shape=jax.ShapeDtypeStruct((m, n), out_dtype),
        compiler_params=_params("parallel", "parallel"),
        name="pmm",
    )(x, w)


def _norm_mod_kernel(x_ref, g_ref, sc_ref, sh_ref, o_ref):
    x = x_ref[0].astype(F32)
    y = x * lax.rsqrt(jnp.mean(x * x, axis=-1, keepdims=True) + EPS)
    o_ref[0] = ((y * g_ref[...]) * (1.0 + sc_ref[0]) + sh_ref[0]).astype(o_ref.dtype)


def norm_mod(x, g, sc, sh, out_dtype=BF16, ts=512):
    b, t, d = x.shape
    ts = min(ts, t)
    assert t % ts == 0
    per_batch = sc.shape[0] == b and b > 1
    mod_map = (lambda i, j: (i, 0, 0)) if per_batch else (lambda i, j: (0, 0, 0))
    return pl.pallas_call(
        _norm_mod_kernel,
        grid=(b, t // ts),
        in_specs=[pl.BlockSpec((1, ts, d), lambda i, j: (i, j, 0)),
                  pl.BlockSpec((1, d), lambda i, j: (0, 0)),
                  pl.BlockSpec((1, 1, d), mod_map),
                  pl.BlockSpec((1, 1, d), mod_map)],
        out_specs=pl.BlockSpec((1, ts, d), lambda i, j: (i, j, 0)),
        out_shape=jax.ShapeDtypeStruct((b, t, d), out_dtype),
        compiler_params=_params("parallel", "parallel"),
        name="norm_mod",
    )(x, g.reshape(1, d), sc.reshape(-1, 1, d), sh.reshape(-1, 1, d))


def _attn_kernel(q_ref, k_ref, v_ref, o_ref, *, tk):
    q = q_ref[0, 0]
    tq = q.shape[0]
    dv = v_ref.shape[-1]
    n_chunks = k_ref.shape[2] // tk

    def body(c, carry):
        m, l, acc = carry
        off = pl.multiple_of(c * tk, tk)
        k = k_ref[0, 0, pl.ds(off, tk), :]
        v = v_ref[0, 0, pl.ds(off, tk), :]
        s = lax.dot_general(q, k, (((1,), (1,)), ((), ())), preferred_element_type=F32)
        m_new = jnp.maximum(m, jnp.max(s, axis=-1, keepdims=True))
        alpha = jnp.exp(m - m_new)
        p = jnp.exp(s - m_new)
        l = alpha * l + jnp.sum(p, axis=-1, keepdims=True)
        acc = alpha * acc + jnp.dot(p.astype(BF16), v, preferred_element_type=F32)
        return m_new, l, acc

    init = (jnp.full((tq, 1), -jnp.inf, F32), jnp.zeros((tq, 1), F32), jnp.zeros((tq, dv), F32))
    _, l, acc = lax.fori_loop(0, n_chunks, body, init)
    o_ref[0, 0] = (acc / l).astype(o_ref.dtype)


def attention(q, k, v, tq=512, tk=768):
    b, h, s, dq = q.shape
    t, dv = k.shape[2], v.shape[3]
    tq = min(tq, s)
    if t % tk:
        tk = t
    assert s % tq == 0
    return pl.pallas_call(
        functools.partial(_attn_kernel, tk=tk),
        grid=(b, h, s // tq),
        in_specs=[pl.BlockSpec((1, 1, tq, dq), lambda i, j, n: (i, j, n, 0)),
                  pl.BlockSpec((1, 1, t, dq), lambda i, j, n: (i, j, 0, 0)),
                  pl.BlockSpec((1, 1, t, dv), lambda i, j, n: (i, j, 0, 0))],
        out_specs=pl.BlockSpec((1, 1, tq, dv), lambda i, j, n: (i, j, n, 0)),
        out_shape=jax.ShapeDtypeStruct((b, h, s, dv), F32),
        compiler_params=_params("parallel", "parallel", "parallel"),
        name="attention",
    )(q, k, v)


def _merge_kernel(h_ref, ya_ref, yb_ref, yc_ref, yd_ref, x_ref, g1_ref,
                  wmg_ref, bmg_ref, wbo_ref, wout_ref, o_ref):
    d = o_ref.shape[-1]
    h = h_ref[0]
    acc = jnp.zeros((h.shape[0], d), F32)
    for n, y_ref in enumerate((ya_ref, yb_ref, yc_ref, yd_ref)):
        z = jnp.dot(y_ref[0], wbo_ref[n], preferred_element_type=F32)
        gate = jnp.dot(h, wmg_ref[:, n * d:(n + 1) * d], preferred_element_type=F32) + bmg_ref[:, n * d:(n + 1) * d]
        acc = acc + jax.nn.sigmoid(gate) * z
    m = jnp.dot(acc.astype(BF16), wout_ref[...], preferred_element_type=F32)
    o_ref[0] = x_ref[0] + g1_ref[0] * m


def merge_residual(h, ys, x, g1, w_mg, b_mg, w_bo, w_out, ts=256):
    b, t, d = x.shape
    ts = min(ts, t)
    assert t % ts == 0
    per_batch = g1.shape[0] == b and b > 1
    mod_map = (lambda i, j: (i, 0, 0)) if per_batch else (lambda i, j: (0, 0, 0))
    row = lambda w: pl.BlockSpec((1, ts, w), lambda i, j: (i, j, 0))
    full = lambda a: pl.BlockSpec(a.shape, lambda i, j: (0,) * a.ndim)
    b_mg = b_mg.reshape(1, -1)
    return pl.pallas_call(
        _merge_kernel,
        grid=(b, t // ts),
        in_specs=[row(d)] + [row(BRANCH_W)] * 4 + [row(d), pl.BlockSpec((1, 1, d), mod_map),
                                                 full(w_mg), full(b_mg), full(w_bo), full(w_out)],
        out_specs=row(d),
        out_shape=jax.ShapeDtypeStruct((b, t, d), F32),
        compiler_params=_params("parallel", "parallel"),
        name="merge_residual",
    )(h, *ys, x, g1.reshape(-1, 1, d), w_mg, b_mg, w_bo, w_out)


def _expert_kernel(be_ref, x_ref, wgu_ref, bgu_ref, wdn_ref, bdn_ref, o_ref, wgu_s, wdn_s):
    i = pl.program_id(0)
    fresh = jnp.logical_or(i == 0, be_ref[i] != be_ref[jnp.maximum(i - 1, 0)])

    @pl.when(fresh)
    def _():
        wgu_s[...] = wgu_ref[0].astype(BF16)
        wdn_s[...] = wdn_ref[0].astype(BF16)

    gu = jnp.dot(x_ref[...], wgu_s[...], preferred_element_type=F32) + bgu_ref[0]
    gate = jnp.minimum(gu[:, :D_FF], SWIGLU_LIMIT)
    up = jnp.clip(gu[:, D_FF:], -SWIGLU_LIMIT, SWIGLU_LIMIT)
    act = (up + 1.0) * gate * jax.nn.sigmoid(SWIGLU_ALPHA * gate)
    o_ref[...] = jnp.dot(act.astype(BF16), wdn_s[...], preferred_element_type=F32) + bdn_ref[0]


def expert_blocks(buf, block_e, w_gu, b_gu, w_dn, b_dn):
    rows, d = buf.shape
    n_blocks = rows // MOE_ROWS
    e = w_gu.shape[0]
    grid_spec = pltpu.PrefetchScalarGridSpec(
        num_scalar_prefetch=1,
        grid=(n_blocks,),
        in_specs=[pl.BlockSpec((MOE_ROWS, d), lambda i, be: (i, 0)),
                  pl.BlockSpec((1, d, 2 * D_FF), lambda i, be: (be[i], 0, 0)),
                  pl.BlockSpec((1, 1, 2 * D_FF), lambda i, be: (be[i], 0, 0)),
                  pl.BlockSpec((1, D_FF, d), lambda i, be: (be[i], 0, 0)),
                  pl.BlockSpec((1, 1, d), lambda i, be: (be[i], 0, 0))],
        out_specs=pl.BlockSpec((MOE_ROWS, d), lambda i, be: (i, 0)),
        scratch_shapes=[pltpu.VMEM((d, 2 * D_FF), BF16), pltpu.VMEM((D_FF, d), BF16)],
    )
    return pl.pallas_call(
        _expert_kernel,
        grid_spec=grid_spec,
        out_shape=jax.ShapeDtypeStruct((rows, d), F32),
        compiler_params=_params("arbitrary"),
        name="expert_blocks",
    )(block_e, buf, w_gu, b_gu.reshape(e, 1, -1), w_dn, b_dn.reshape(e, 1, -1))


def moe_ffn(t, router_w, router_b, w_gu, b_gu, w_dn, b_dn):
    n, d = t.shape
    logits = jnp.dot(t, router_w, precision=lax.Precision.HIGHEST) + router_b
    top_v, top_i = lax.top_k(logits, TOP_K)
    top_w = jax.nn.softmax(top_v, axis=-1)
    nk = n * TOP_K
    e_flat = top_i.reshape(nk)
    onehot = (e_flat[:, None] == jnp.arange(N_EXPERTS)[None, :]).astype(jnp.int32)
    csum = jnp.cumsum(onehot, axis=0)
    rank = jnp.take_along_axis(csum, e_flat[:, None], axis=1)[:, 0] - 1
    counts = csum[-1]
    padded = (counts + MOE_ROWS - 1) // MOE_ROWS * MOE_ROWS
    pad_end = jnp.cumsum(padded)
    dest = (pad_end - padded)[e_flat] + rank
    n_blocks = -(-nk // MOE_ROWS) + N_EXPERTS
    tok = jnp.arange(nk, dtype=jnp.int32) // TOP_K
    buf = jnp.zeros((n_blocks * MOE_ROWS, d), BF16).at[dest].set(t.astype(BF16)[tok])
    block_e = jnp.minimum(jnp.searchsorted(pad_end, jnp.arange(n_blocks) * MOE_ROWS, side='right'),
                          N_EXPERTS - 1).astype(jnp.int32)
    y = expert_blocks(buf, block_e, w_gu, b_gu, w_dn, b_dn)
    return jnp.sum(y[dest].reshape(n, TOP_K, d) * top_w[..., None], axis=1)


def rmsnorm(x, g):
    xf = x.astype(F32)
    return xf * lax.rsqrt(jnp.mean(xf * xf, axis=-1, keepdims=True) + EPS) * g


def l2norm(x):
    return x * lax.rsqrt(jnp.sum(x * x, axis=-1, keepdims=True) + EPS)


def to_heads(t, n_heads):
    b, s, _ = t.shape
    return t.reshape(b, s, n_heads, -1).transpose(0, 2, 1, 3)


def from_heads(t):
    b, h, s, d = t.shape
    return t.transpose(0, 2, 1, 3).reshape(b, s, h * d)


def flip_t(t):
    return jnp.flip(t, axis=2)


def short_conv(x, w):
    k = w.shape[0]
    pad = k // 2
    s = x.shape[1]
    xp = jnp.pad(x, ((0, 0), (pad, pad), (0, 0)))
    y = xp[:, 0:s] * w[0]
    for j in range(1, k):
        y = y + xp[:, j:j + s] * w[j]
    return y


def rope2d(x):
    s = x.shape[-2]
    half = x.shape[-1] // 2
    t = jnp.arange(s)
    inv = ROPE_BASE ** (-jnp.arange(0, half, 2, dtype=F32) / half)

    def rot(xa, pos):
        ang = pos.astype(F32)[:, None] * inv
        cos, sin = jnp.cos(ang), jnp.sin(ang)
        x1, x2 = jnp.split(xa, 2, axis=-1)
        return jnp.concatenate([x1 * cos - x2 * sin, x2 * cos + x1 * sin], axis=-1)

    return jnp.concatenate([rot(x[..., :half], t // GRID_W), rot(x[..., half:], t % GRID_W)], axis=-1)


def neighborhood_attention(q, k, v, k_ctx, v_ctx, rpb):
    b, s, _ = q.shape
    rows = s // GRID_W
    wh, ww = min(NA_WIN_H, rows), NA_WIN_W
    grid = lambda t: t.reshape(b, rows, GRID_W, NA_HEADS, NA_DH)
    qg = jnp.moveaxis(grid(q) * (NA_DH ** -0.5), 1, 0)
    kg, vg = grid(k), grid(v)
    cols = np.arange(GRID_W)
    c0 = np.clip(cols - ww // 2, 0, GRID_W - ww)
    col_idx = c0[:, None] + np.arange(ww)[None, :]
    rpb_cols = rpb[:, :, col_idx - cols[:, None] + (NA_WIN_W - 1)]

    def row_block(args):
        r, q_r = args
        r0 = jnp.clip(r - wh // 2, 0, rows - wh)
        k_r = lax.dynamic_slice_in_dim(kg, r0, wh, axis=1)[:, :, col_idx]
        v_r = lax.dynamic_slice_in_dim(vg, r0, wh, axis=1)[:, :, col_idx]
        bias = jnp.take(rpb_cols, r0 + jnp.arange(wh) - r + (NA_WIN_H - 1), axis=1)
        s_win = jnp.einsum('bchd,bacwhd->bhcaw', q_r, k_r) + bias.transpose(0, 2, 1, 3)
        s_ctx = jnp.einsum('bchd,bhjd->bhcj', q_r, k_ctx)
        p = jax.nn.softmax(jnp.concatenate([s_win.reshape(b, NA_HEADS, GRID_W, wh * ww), s_ctx], axis=-1), axis=-1)
        p_win = p[..., :wh * ww].reshape(b, NA_HEADS, GRID_W, wh, ww)
        return (jnp.einsum('bhcaw,bacwhd->bchd', p_win, v_r)
                + jnp.einsum('bhcj,bhjd->bchd', p[..., wh * ww:], v_ctx))

    o = lax.map(row_block, (jnp.arange(rows), qg))
    return jnp.moveaxis(o, 0, 1).reshape(b, s, NA_HEADS * NA_DH)


def mla_expand(pq, pkv, q_norm_g, w_uq, kv_norm_g, w_ukv, rotate):
    b, t, _ = pq.shape
    q = pmm(rmsnorm(pq, q_norm_g).reshape(b * t, -1), w_uq).reshape(b, t, -1)
    kv = pmm(rmsnorm(pkv[..., :MLA_KV_RANK], kv_norm_g).reshape(b * t, -1), w_ukv).reshape(b, t, -1)
    q, kv = to_heads(q, MLA_HEADS), to_heads(kv, MLA_HEADS)
    q_nope, q_pe = q[..., :MLA_NOPE], q[..., MLA_NOPE:]
    k_nope, v = kv[..., :MLA_NOPE], kv[..., MLA_NOPE:]
    k_pe = pkv[..., MLA_KV_RANK:]
    if rotate:
        q_pe, k_pe = rope2d(q_pe), rope2d(k_pe)
    return q_nope, q_pe, k_nope, k_pe, v


def mla_attention(q_parts, k_parts):
    qn, qp = q_parts
    kn, kp, v = k_parts
    scale = (MLA_NOPE + MLA_ROPE) ** -0.5
    dq = MLA_NOPE + MLA_ROPE
    pad = (-dq) % 128
    q = jnp.concatenate([qn, qp], axis=-1) * scale
    k = jnp.concatenate([kn, jnp.broadcast_to(kp[:, None], kn.shape[:3] + (MLA_ROPE,))], axis=-1)
    q = jnp.pad(q, ((0, 0),) * 3 + ((0, pad),)).astype(BF16)
    k = jnp.pad(k, ((0, 0),) * 3 + ((0, pad),)).astype(BF16)
    return from_heads(attention(q, k, v.astype(BF16)))


def context_attention(q, k, v):
    return from_heads(attention((q * (q.shape[-1] ** -0.5)).astype(BF16), k.astype(BF16), v.astype(BF16)))


def gdn_prep(p_qkv, p_a, p_b, conv_w, a_log, dt_bias):
    b, s, _ = p_a.shape
    qkv = jax.nn.silu(short_conv(p_qkv, conv_w))
    q, k, v = jnp.split(qkv, [GDN_HEADS * GDN_DK, 2 * GDN_HEADS * GDN_DK], axis=-1)
    q = l2norm(to_heads(q, GDN_HEADS)) * (GDN_DK ** -0.5)
    k = l2norm(to_heads(k, GDN_HEADS))
    v = to_heads(v, GDN_HEADS)
    a = p_a.reshape(b, s, 2, GDN_HEADS)
    log_alpha = -jnp.exp(a_log) * jax.nn.softplus(a + dt_bias)
    beta = jax.nn.sigmoid(p_b.reshape(b, s, 2, GDN_HEADS))
    return q, k, v, log_alpha.transpose(2, 0, 3, 1), beta.transpose(2, 0, 3, 1)


def gdn_scan(q, k, v, log_a, beta, state):
    b, h, s, dk = q.shape
    dv = v.shape[-1]
    L = GDN_CHUNK
    nc = s // L
    chunk = lambda t: t.reshape(b, h, nc, L, *t.shape[3:])
    q, k, v, log_a, beta = (chunk(t) for t in (q, k, v, log_a, beta))
    g = jnp.cumsum(log_a, axis=-1)
    idx = jnp.arange(L)
    decay = jnp.exp(jnp.where(idx[:, None] >= idx[None, :], g[..., :, None] - g[..., None, :], -jnp.inf))
    strict = jnp.where(idx[:, None] > idx[None, :], decay, 0.0)
    a_mat = jnp.eye(L, dtype=F32) + beta[..., :, None] * strict * jnp.einsum('bhnid,bhnjd->bhnij', k, k)
    rhs = jnp.concatenate([beta[..., None] * v, (beta * jnp.exp(g))[..., None] * k], axis=-1)
    sol = lax.linalg.triangular_solve(a_mat, rhs, left_side=True, lower=True, unit_diagonal=True)
    u0, w = sol[..., :dv], sol[..., dv:]
    qk = decay * jnp.einsum('bhnid,bhnjd->bhnij', q, k)
    g_end = g[..., -1]
    k_end = jnp.exp(g_end[..., None] - g)[..., None] * k
    q_dec = jnp.exp(g)[..., None] * q

    def step(S, xs):
        u0_c, w_c, qk_c, k_end_c, q_dec_c, e_end = xs
        u = u0_c - jnp.einsum('bhlk,bhvk->bhlv', w_c, S)
        o = jnp.einsum('bhlk,bhvk->bhlv', q_dec_c, S) + jnp.einsum('bhij,bhjv->bhiv', qk_c, u)
        S = e_end[..., None, None] * S + jnp.einsum('bhlv,bhlk->bhvk', u, k_end_c)
        return S, o

    xs = tuple(jnp.moveaxis(t, 2, 0) for t in (u0, w, qk, k_end, q_dec, jnp.exp(g_end)))
    state, o = lax.scan(step, state, xs)
    return jnp.moveaxis(o, 0, 2).reshape(b, h, s, dv), state


def gdn_output(o, p_g, norm_g):
    b, s, _ = p_g.shape
    y = rmsnorm(o.transpose(0, 2, 1, 3), norm_g) * jax.nn.silu(p_g.reshape(b, s, GDN_HEADS, GDN_DV))
    return y.reshape(b, s, GDN_HEADS * GDN_DV)


def mlstm_prep(pq, pk, pv, p_i, p_f, b_if):
    b, s, _ = p_i.shape
    q = to_heads(pq, ML_HEADS) * (ML_DK ** -0.5)
    k = to_heads(pk, ML_HEADS)
    v = to_heads(pv, ML_HEADS)
    i_pre = p_i.reshape(b, s, 2, ML_HEADS) + b_if[:, 0]
    log_f = jax.nn.log_sigmoid(p_f.reshape(b, s, 2, ML_HEADS) + b_if[:, 1])
    return q, k, v, i_pre.transpose(2, 0, 3, 1), log_f.transpose(2, 0, 3, 1)


def mlstm_scan(q, k, v, i_pre, log_f, state):
    b, h, s, dk = q.shape
    dv = v.shape[-1]
    L = ML_CHUNK
    nc = s // L
    chunk = lambda t: t.reshape(b, h, nc, L, *t.shape[3:])
    q, k, v, i_pre, log_f = (chunk(t) for t in (q, k, v, i_pre, log_f))
    fcum = jnp.cumsum(log_f, axis=-1)
    idx = jnp.arange(L)
    d_log = jnp.where(idx[:, None] >= idx[None, :],
                      fcum[..., :, None] - fcum[..., None, :] + i_pre[..., None, :], -jnp.inf)
    qk = jnp.einsum('bhnid,bhnjd->bhnij', q, k)
    f_end = fcum[..., -1]
    w_log = f_end[..., None] - fcum + i_pre
    d_max = jnp.max(d_log, axis=-1)
    w_max = jnp.max(w_log, axis=-1)

    def step(carry, xs):
        C, n, m = carry
        q_c, k_c, v_c, fc_c, dl_c, qk_c, fe_c, wl_c, dm_c, wm_c = xs
        m_q = jnp.maximum(fc_c + m[..., None], dm_c)
        inter = jnp.exp(fc_c + m[..., None] - m_q)
        intra = jnp.exp(dl_c - m_q[..., None]) * qk_c
        num = inter[..., None] * jnp.einsum('bhlk,bhvk->bhlv', q_c, C) + jnp.einsum('bhij,bhjv->bhiv', intra, v_c)
        den = inter * jnp.einsum('bhlk,bhk->bhl', q_c, n) + jnp.sum(intra, axis=-1)
        h_c = num / jnp.maximum(jnp.abs(den), jnp.exp(-m_q))[..., None]
        m_new = jnp.maximum(fe_c + m, wm_c)
        dec = jnp.exp(fe_c + m - m_new)
        wk = jnp.exp(wl_c - m_new[..., None])
        C = dec[..., None, None] * C + jnp.einsum('bhlv,bhlk->bhvk', wk[..., None] * v_c, k_c)
        n = dec[..., None] * n + jnp.einsum('bhl,bhlk->bhk', wk, k_c)
        return (C, n, m_new), h_c

    xs = tuple(jnp.moveaxis(t, 2, 0) for t in (q, k, v, fcum, d_log, qk, f_end, w_log, d_max, w_max))
    state, o = lax.scan(step, state, xs)
    return jnp.moveaxis(o, 0, 2).reshape(b, h, s, dv), state


def mlstm_output(o, p_o, norm_g):
    b, s, _ = p_o.shape
    y = rmsnorm(o.transpose(0, 2, 1, 3), norm_g) * jax.nn.sigmoid(p_o.reshape(b, s, ML_HEADS, ML_DV))
    return y.reshape(b, s, ML_HEADS * ML_DV)


def bidirectional(scan_fn, state0, ctx_in, lat_in):
    o_ctx, o_lat = 0.0, 0.0
    for d in range(2):
        c_in = ctx_in[:3] + (ctx_in[3][d], ctx_in[4][d])
        l_in = lat_in[:3] + (lat_in[3][d], lat_in[4][d])
        if d == 1:
            c_in = tuple(flip_t(t) for t in c_in)
            l_in = tuple(flip_t(t) for t in l_in)
        oc, st = scan_fn(*c_in, state0)
        ol, _ = scan_fn(*l_in, st)
        if d == 1:
            oc, ol = flip_t(oc), flip_t(ol)
        o_ctx = o_ctx + oc
        o_lat = o_lat + ol
    return o_ctx, o_lat


def token_mixer(h_lat, h_ctx, lw, with_ctx_out):
    bsz, s, d = h_lat.shape
    c_len = h_ctx.shape[1]
    cuts = np.cumsum(IN_SPLITS)[:-1].tolist()
    d_in = sum(IN_SPLITS)
    pl_ = jnp.split(pmm(h_lat.reshape(bsz * s, d), lw['w_in']).reshape(bsz, s, -1)[..., :d_in], cuts, axis=-1)
    pc = jnp.split(pmm(h_ctx.reshape(bsz * c_len, d), lw['w_in']).reshape(bsz, c_len, -1)[..., :d_in], cuts, axis=-1)
    k_ctx_na, v_ctx_na = to_heads(pc[1], NA_HEADS), to_heads(pc[2], NA_HEADS)
    ya_lat = neighborhood_attention(pl_[0], pl_[1], pl_[2], k_ctx_na, v_ctx_na, lw['na_rpb'])
    gdn_in = lambda p: gdn_prep(p[3], p[5], p[6], lw['gdn_conv_w'], lw['gdn_a_log'], lw['gdn_dt_bias'])
    s0 = jnp.zeros((bsz, GDN_HEADS, GDN_DV, GDN_DK), F32)
    ob_ctx, ob_lat = bidirectional(gdn_scan, s0, gdn_in(pc), gdn_in(pl_))
    cq = mla_expand(pc[7], pc[8], lw['mla_q_norm_g'], lw['mla_w_uq'], lw['mla_kv_norm_g'], lw['mla_w_ukv'], False)
    lq = mla_expand(pl_[7], pl_[8], lw['mla_q_norm_g'], lw['mla_w_uq'], lw['mla_kv_norm_g'], lw['mla_w_ukv'], True)
    kn_all = jnp.concatenate([cq[2], lq[2]], axis=2)
    kp_all = jnp.concatenate([cq[3], lq[3]], axis=1)
    v_all = jnp.concatenate([cq[4], lq[4]], axis=2)
    yc_lat = mla_attention((lq[0], lq[1]), (kn_all, kp_all, v_all))
    ml_in = lambda p: mlstm_prep(p[9], p[10], p[11], p[13], p[14], lw['ml_b_if'])
    st0 = (jnp.zeros((bsz, ML_HEADS, ML_DV, ML_DK), F32),
           jnp.zeros((bsz, ML_HEADS, ML_DK), F32),
           jnp.zeros((bsz, ML_HEADS), F32))
    od_ctx, od_lat = bidirectional(mlstm_scan, st0, ml_in(pc), ml_in(pl_))
    ys_lat = [ya_lat, gdn_output(ob_lat, pl_[4], lw['gdn_norm_g']), yc_lat,
              mlstm_output(od_lat, pl_[12], lw['ml_norm_g'])]
    if not with_ctx_out:
        return ys_lat, None
    ya_ctx = context_attention(to_heads(pc[0], NA_HEADS), k_ctx_na, v_ctx_na)
    yc_ctx = mla_attention((cq[0], cq[1]), (cq[2], cq[3], cq[4]))
    ys_ctx = [ya_ctx, gdn_output(ob_ctx, pc[4], lw['gdn_norm_g']), yc_ctx,
              mlstm_output(od_ctx, pc[12], lw['ml_norm_g'])]
    return ys_lat, ys_ctx


def kernel(x, c, ctx, c_ctx, w_ada, b_ada, norm1_g, norm2_g, w_in, na_rpb, gdn_conv_w, gdn_a_log, gdn_dt_bias, gdn_norm_g, mla_q_norm_g, mla_w_uq, mla_kv_norm_g, mla_w_ukv, ml_b_if, ml_norm_g, w_mg, b_mg, w_bo, w_out, router_w, router_b, w_gu, b_gu, w_dn, b_dn, final_g):
    depth = w_ada.shape[0]
    bsz, s, d = x.shape
    cond = jnp.concatenate([jax.nn.silu(c), jax.nn.silu(c_ctx)[None]], axis=0)
    cond = jnp.pad(cond, ((0, (-cond.shape[0]) % 8), (0, 0)))
    w_mg_b, w_bo_b, w_out_b = (t.astype(BF16) for t in (w_mg, w_bo, w_out))
    w_in_b = jnp.pad(w_in, ((0, 0), (0, 0), (0, (-w_in.shape[-1]) % 512))).astype(BF16)
    w_uq_b, w_ukv_b = mla_w_uq.astype(BF16), mla_w_ukv.astype(BF16)
    for l in range(depth):
        with_ctx_out = l < depth - 1
        mod = pmm(cond, w_ada[l], tn=1024) + b_ada[l]
        sh1, sc1, g1, sh2, sc2, g2 = jnp.split(mod[:bsz], 6, axis=-1)
        csh1, csc1, cg1, csh2, csc2, cg2 = jnp.split(mod[bsz:bsz + 1], 6, axis=-1)
        lw = dict(w_in=w_in_b[l], na_rpb=na_rpb[l], gdn_conv_w=gdn_conv_w[l], gdn_a_log=gdn_a_log[l],
                  gdn_dt_bias=gdn_dt_bias[l], gdn_norm_g=gdn_norm_g[l], mla_q_norm_g=mla_q_norm_g[l],
                  mla_w_uq=w_uq_b[l], mla_kv_norm_g=mla_kv_norm_g[l], mla_w_ukv=w_ukv_b[l],
                  ml_b_if=ml_b_if[l], ml_norm_g=ml_norm_g[l])
        h_lat = norm_mod(x, norm1_g[l], sc1, sh1)
        h_ctx = norm_mod(ctx, norm1_g[l], csc1, csh1)
        ys_lat, ys_ctx = token_mixer(h_lat, h_ctx, lw, with_ctx_out)
        x = merge_residual(h_lat, [y.astype(BF16) for y in ys_lat], x, g1, w_mg_b[l], b_mg[l], w_bo_b[l], w_out_b[l])
        h2_lat = norm_mod(x, norm2_g[l], sc2, sh2, out_dtype=F32)
        if with_ctx_out:
            ctx = merge_residual(h_ctx, [y.astype(BF16) for y in ys_ctx], ctx, cg1, w_mg_b[l], b_mg[l], w_bo_b[l], w_out_b[l])
            h2_ctx = norm_mod(ctx, norm2_g[l], csc2, csh2, out_dtype=F32)
            n_ctx = h2_ctx.shape[0] * h2_ctx.shape[1]
            tokens = jnp.concatenate([h2_ctx.reshape(-1, d), h2_lat.reshape(-1, d)], axis=0)
            f = moe_ffn(tokens, router_w[l], router_b[l], w_gu[l], b_gu[l], w_dn[l], b_dn[l])
            ctx = ctx + cg2 * f[:n_ctx].reshape(ctx.shape)
            f_lat = f[n_ctx:].reshape(x.shape)
        else:
            f_lat = moe_ffn(h2_lat.reshape(-1, d), router_w[l], router_b[l], w_gu[l], b_gu[l],
                            w_dn[l], b_dn[l]).reshape(x.shape)
        x = x + g2[:, None, :] * f_lat
    return rmsnorm(x, final_g)
```

```python
import functools
import math

import numpy as np
import jax
import jax.numpy as jnp
from jax import lax
from jax.experimental import pallas as pl
from jax.experimental.pallas import tpu as pltpu

F32 = jnp.float32
BF16 = jnp.bfloat16

GRID_W = 64
EPS = 1e-6
ROPE_BASE = 10000.0
NA_HEADS, NA_DH, NA_WIN_H, NA_WIN_W = 8, 64, 8, 16
GDN_HEADS, GDN_DK, GDN_DV, GDN_CHUNK = 4, 128, 128, 64
MLA_HEADS, MLA_NOPE, MLA_ROPE, MLA_V, MLA_Q_RANK, MLA_KV_RANK = 8, 64, 32, 64, 256, 128
ML_HEADS, ML_DK, ML_DV, ML_CHUNK = 4, 64, 128, 64
N_BRANCH, BRANCH_W = 4, 512
N_EXPERTS, TOP_K, D_FF = 32, 4, 1024
SWIGLU_ALPHA, SWIGLU_LIMIT = 1.702, 7.0
MOE_ROWS = 256

IN_SPLITS = (
    NA_HEADS * NA_DH, NA_HEADS * NA_DH, NA_HEADS * NA_DH,
    GDN_HEADS * (2 * GDN_DK + GDN_DV), GDN_HEADS * GDN_DV,
    2 * GDN_HEADS, 2 * GDN_HEADS,
    MLA_Q_RANK, MLA_KV_RANK + MLA_ROPE,
    ML_HEADS * ML_DK, ML_HEADS * ML_DK, ML_HEADS * ML_DV, ML_HEADS * ML_DV,
    2 * ML_HEADS, 2 * ML_HEADS,
)

VMEM_LIMIT_BYTES = 56 * 1024 * 1024


def _params(*sem):
    return pltpu.CompilerParams(dimension_semantics=sem, vmem_limit_bytes=VMEM_LIMIT_BYTES)


def _mm_kernel(x_ref, w_ref, o_ref):
    o_ref[...] = jnp.dot(x_ref[...].astype(BF16), w_ref[...].astype(BF16),
                         preferred_element_type=F32).astype(o_ref.dtype)


def pmm(x, w, out_dtype=F32, tm=512, tn=512):
    m, k = x.shape
    n = w.shape[1]
    tm, tn = min(tm, m), min(tn, n)
    if n % tn:
        tn = n
    assert m % tm == 0 and n % tn == 0, (m, n, tm, tn)
    return pl.pallas_call(
        _mm_kernel,
        grid=(m // tm, n // tn),
        in_specs=[pl.BlockSpec((tm, k), lambda i, j: (i, 0)),
                  pl.BlockSpec((k, tn), lambda i, j: (0, j))],
        out_specs=pl.BlockSpec((tm, tn), lambda i, j: (i, j)),
        out_shape=jax.ShapeDtypeStruct((m, n), out_dtype),
        compiler_params=_params("parallel", "parallel"),
        name="pmm",
    )(x, w)


def _norm_mod_kernel(x_ref, g_ref, sc_ref, sh_ref, o_ref):
    x = x_ref[0].astype(F32)
    y = x * lax.rsqrt(jnp.mean(x * x, axis=-1, keepdims=True) + EPS)
    o_ref[0] = ((y * g_ref[...]) * (1.0 + sc_ref[0]) + sh_ref[0]).astype(o_ref.dtype)


def norm_mod(x, g, sc, sh, out_dtype=BF16, ts=512):
    b, t, d = x.shape
    ts = min(ts, t)
    assert t % ts == 0
    per_batch = sc.shape[0] == b and b > 1
    mod_map = (lambda i, j: (i, 0, 0)) if per_batch else (lambda i, j: (0, 0, 0))
    return pl.pallas_call(
        _norm_mod_kernel,
        grid=(b, t // ts),
        in_specs=[pl.BlockSpec((1, ts, d), lambda i, j: (i, j, 0)),
                  pl.BlockSpec((1, d), lambda i, j: (0, 0)),
                  pl.BlockSpec((1, 1, d), mod_map),
                  pl.BlockSpec((1, 1, d), mod_map)],
        out_specs=pl.BlockSpec((1, ts, d), lambda i, j: (i, j, 0)),
        out_shape=jax.ShapeDtypeStruct((b, t, d), out_dtype),
        compiler_params=_params("parallel", "parallel"),
        name="norm_mod",
    )(x, g.reshape(1, d), sc.reshape(-1, 1, d), sh.reshape(-1, 1, d))


def _attn_kernel(q_ref, k_ref, v_ref, o_ref, *, tk):
    q = q_ref[0, 0]
    tq = q.shape[0]
    dv = v_ref.shape[-1]
    n_chunks = k_ref.shape[2] // tk

    def body(c, carry):
        m, l, acc = carry
        off = pl.multiple_of(c * tk, tk)
        k = k_ref[0, 0, pl.ds(off, tk), :]
        v = v_ref[0, 0, pl.ds(off, tk), :]
        s = lax.dot_general(q, k, (((1,), (1,)), ((), ())), preferred_element_type=F32)
        m_new = jnp.maximum(m, jnp.max(s, axis=-1, keepdims=True))
        alpha = jnp.exp(m - m_new)
        p = jnp.exp(s - m_new)
        l = alpha * l + jnp.sum(p, axis=-1, keepdims=True)
        acc = alpha * acc + jnp.dot(p.astype(BF16), v, preferred_element_type=F32)
        return m_new, l, acc

    init = (jnp.full((tq, 1), -jnp.inf, F32), jnp.zeros((tq, 1), F32), jnp.zeros((tq, dv), F32))
    _, l, acc = lax.fori_loop(0, n_chunks, body, init)
    o_ref[0, 0] = (acc / l).astype(o_ref.dtype)


def attention(q, k, v, tq=512, tk=768):
    b, h, s, dq = q.shape
    t, dv = k.shape[2], v.shape[3]
    tq = min(tq, s)
    if t % tk:
        tk = t
    assert s % tq == 0
    return pl.pallas_call(
        functools.partial(_attn_kernel, tk=tk),
        grid=(b, h, s // tq),
        in_specs=[pl.BlockSpec((1, 1, tq, dq), lambda i, j, n: (i, j, n, 0)),
                  pl.BlockSpec((1, 1, t, dq), lambda i, j, n: (i, j, 0, 0)),
                  pl.BlockSpec((1, 1, t, dv), lambda i, j, n: (i, j, 0, 0))],
        out_specs=pl.BlockSpec((1, 1, tq, dv), lambda i, j, n: (i, j, n, 0)),
        out_shape=jax.ShapeDtypeStruct((b, h, s, dv), F32),
        compiler_params=_params("parallel", "parallel", "parallel"),
        name="attention",
    )(q, k, v)


def _merge_kernel(h_ref, ya_ref, yb_ref, yc_ref, yd_ref, x_ref, g1_ref,
                  wmg_ref, bmg_ref, wbo_ref, wout_ref, o_ref):
    d = o_ref.shape[-1]
    h = h_ref[0]
    acc = jnp.zeros((h.shape[0], d), F32)
    for n, y_ref in enumerate((ya_ref, yb_ref, yc_ref, yd_ref)):
        z = jnp.dot(y_ref[0], wbo_ref[n], preferred_element_type=F32)
        gate = jnp.dot(h, wmg_ref[:, n * d:(n + 1) * d], preferred_element_type=F32) + bmg_ref[:, n * d:(n + 1) * d]
        acc = acc + jax.nn.sigmoid(gate) * z
    m = jnp.dot(acc.astype(BF16), wout_ref[...], preferred_element_type=F32)
    o_ref[0] = x_ref[0] + g1_ref[0] * m


def merge_residual(h, ys, x, g1, w_mg, b_mg, w_bo, w_out, ts=256):
    b, t, d = x.shape
    ts = min(ts, t)
    assert t % ts == 0
    per_batch = g1.shape[0] == b and b > 1
    mod_map = (lambda i, j: (i, 0, 0)) if per_batch else (lambda i, j: (0, 0, 0))
    row = lambda w: pl.BlockSpec((1, ts, w), lambda i, j: (i, j, 0))
    full = lambda a: pl.BlockSpec(a.shape, lambda i, j: (0,) * a.ndim)
    b_mg = b_mg.reshape(1, -1)
    return pl.pallas_call(
        _merge_kernel,
        grid=(b, t // ts),
        in_specs=[row(d)] + [row(BRANCH_W)] * 4 + [row(d), pl.BlockSpec((1, 1, d), mod_map),
                                                 full(w_mg), full(b_mg), full(w_bo), full(w_out)],
        out_specs=row(d),
        out_shape=jax.ShapeDtypeStruct((b, t, d), F32),
        compiler_params=_params("parallel", "parallel"),
        name="merge_residual",
    )(h, *ys, x, g1.reshape(-1, 1, d), w_mg, b_mg, w_bo, w_out)


def _expert_kernel(be_ref, x_ref, wgu_ref, bgu_ref, wdn_ref, bdn_ref, o_ref, wgu_s, wdn_s):
    i = pl.program_id(0)
    fresh = jnp.logical_or(i == 0, be_ref[i] != be_ref[jnp.maximum(i - 1, 0)])

    @pl.when(fresh)
    def _():
        wgu_s[...] = wgu_ref[0].astype(BF16)
        wdn_s[...] = wdn_ref[0].astype(BF16)

    gu = jnp.dot(x_ref[...], wgu_s[...], preferred_element_type=F32) + bgu_ref[0]
    gate = jnp.minimum(gu[:, :D_FF], SWIGLU_LIMIT)
    up = jnp.clip(gu[:, D_FF:], -SWIGLU_LIMIT, SWIGLU_LIMIT)
    act = (up + 1.0) * gate * jax.nn.sigmoid(SWIGLU_ALPHA * gate)
    o_ref[...] = jnp.dot(act.astype(BF16), wdn_s[...], preferred_element_type=F32) + bdn_ref[0]


def expert_blocks(buf, block_e, w_gu, b_gu, w_dn, b_dn):
    rows, d = buf.shape
    n_blocks = rows // MOE_ROWS
    e = w_gu.shape[0]
    grid_spec = pltpu.PrefetchScalarGridSpec(
        num_scalar_prefetch=1,
        grid=(n_blocks,),
        in_specs=[pl.BlockSpec((MOE_ROWS, d), lambda i, be: (i, 0)),
                  pl.BlockSpec((1, d, 2 * D_FF), lambda i, be: (be[i], 0, 0)),
                  pl.BlockSpec((1, 1, 2 * D_FF), lambda i, be: (be[i], 0, 0)),
                  pl.BlockSpec((1, D_FF, d), lambda i, be: (be[i], 0, 0)),
                  pl.BlockSpec((1, 1, d), lambda i, be: (be[i], 0, 0))],
        out_specs=pl.BlockSpec((MOE_ROWS, d), lambda i, be: (i, 0)),
        scratch_shapes=[pltpu.VMEM((d, 2 * D_FF), BF16), pltpu.VMEM((D_FF, d), BF16)],
    )
    return pl.pallas_call(
        _expert_kernel,
        grid_spec=grid_spec,
        out_shape=jax.ShapeDtypeStruct((rows, d), F32),
        compiler_params=_params("arbitrary"),
        name="expert_blocks",
    )(block_e, buf, w_gu, b_gu.reshape(e, 1, -1), w_dn, b_dn.reshape(e, 1, -1))


def moe_ffn(t, router_w, router_b, w_gu, b_gu, w_dn, b_dn):
    n, d = t.shape
    logits = jnp.dot(t, router_w, precision=lax.Precision.HIGHEST) + router_b
    top_v, top_i = lax.top_k(logits, TOP_K)
    top_w = jax.nn.softmax(top_v, axis=-1)
    nk = n * TOP_K
    e_flat = top_i.reshape(nk)
    onehot = (e_flat[:, None] == jnp.arange(N_EXPERTS)[None, :]).astype(jnp.int32)
    csum = jnp.cumsum(onehot, axis=0)
    rank = jnp.take_along_axis(csum, e_flat[:, None], axis=1)[:, 0] - 1
    counts = csum[-1]
    padded = (counts + MOE_ROWS - 1) // MOE_ROWS * MOE_ROWS
    pad_end = jnp.cumsum(padded)
    dest = (pad_end - padded)[e_flat] + rank
    n_blocks = -(-nk // MOE_ROWS) + N_EXPERTS
    tok = jnp.arange(nk, dtype=jnp.int32) // TOP_K
    src = jnp.full((n_blocks * MOE_ROWS,), n, jnp.int32).at[dest].set(tok)
    buf = jnp.concatenate([t.astype(BF16), jnp.zeros((1, d), BF16)], axis=0)[src]
    block_e = jnp.minimum(jnp.searchsorted(pad_end, jnp.arange(n_blocks) * MOE_ROWS, side='right'),
                          N_EXPERTS - 1).astype(jnp.int32)
    y = expert_blocks(buf, block_e, w_gu, b_gu, w_dn, b_dn)
    return jnp.sum(y[dest].reshape(n, TOP_K, d) * top_w[..., None], axis=1)


def rmsnorm(x, g):
    xf = x.astype(F32)
    return xf * lax.rsqrt(jnp.mean(xf * xf, axis=-1, keepdims=True) + EPS) * g


def l2norm(x):
    return x * lax.rsqrt(jnp.sum(x * x, axis=-1, keepdims=True) + EPS)


def to_heads(t, n_heads):
    b, s, _ = t.shape
    return t.reshape(b, s, n_heads, -1).transpose(0, 2, 1, 3)


def from_heads(t):
    b, h, s, d = t.shape
    return t.transpose(0, 2, 1, 3).reshape(b, s, h * d)


def flip_t(t):
    return jnp.flip(t, axis=2)


def short_conv(x, w):
    k = w.shape[0]
    pad = k // 2
    s = x.shape[1]
    xp = jnp.pad(x, ((0, 0), (pad, pad), (0, 0)))
    y = xp[:, 0:s] * w[0]
    for j in range(1, k):
        y = y + xp[:, j:j + s] * w[j]
    return y


def rope2d(x):
    s = x.shape[-2]
    half = x.shape[-1] // 2
    t = jnp.arange(s)
    inv = ROPE_BASE ** (-jnp.arange(0, half, 2, dtype=F32) / half)

    def rot(xa, pos):
        ang = pos.astype(F32)[:, None] * inv
        cos, sin = jnp.cos(ang), jnp.sin(ang)
        x1, x2 = jnp.split(xa, 2, axis=-1)
        return jnp.concatenate([x1 * cos - x2 * sin, x2 * cos + x1 * sin], axis=-1)

    return jnp.concatenate([rot(x[..., :half], t // GRID_W), rot(x[..., half:], t % GRID_W)], axis=-1)


NA_QROWS = 8
NA_KROWS = 4
NA_NEG = -1e30


def na_bias_tables(rpb, rows):
    n_steps = rows // NA_QROWS
    wh, ww = min(NA_WIN_H, rows), NA_WIN_W
    n_kblocks = rows // NA_KROWS
    il = np.arange(NA_QROWS)[:, None, None, None]
    qc = np.arange(GRID_W)[None, :, None, None]
    ka = np.arange(4 * NA_KROWS)[None, None, :, None]
    kc = np.arange(GRID_W)[None, None, None, :]
    c0 = np.clip(qc - ww // 2, 0, GRID_W - ww)
    col_ok = (kc >= c0) & (kc < c0 + ww)
    dc = np.clip(kc - qc + (NA_WIN_W - 1), 0, 2 * NA_WIN_W - 2)
    shape = (NA_QROWS, GRID_W, 4 * NA_KROWS, GRID_W)
    tables = []
    for i in (0, min(1, n_steps - 1), n_steps - 1):
        q = NA_QROWS * i + il
        blk = 2 * i - 1 + ka // NA_KROWS
        blk_c = np.clip(blk, 0, n_kblocks - 1)
        kr = NA_KROWS * blk_c + ka % NA_KROWS
        r0 = np.clip(q - wh // 2, 0, rows - wh)
        ok = (blk == blk_c) & (kr >= r0) & (kr < r0 + wh) & col_ok
        dr = np.clip(kr - q + (NA_WIN_H - 1), 0, 2 * NA_WIN_H - 2)
        dr, dcb, ok = (np.broadcast_to(t, shape) for t in (dr, dc, ok))
        tab = jnp.where(ok[None], rpb[:, dr, dcb], NA_NEG)
        tables.append(tab.reshape(rpb.shape[0], NA_QROWS * GRID_W, 4 * NA_KROWS * GRID_W))
    return jnp.stack(tables)


def _na_kernel(q_ref, k0_ref, k1_ref, k2_ref, k3_ref, v0_ref, v1_ref, v2_ref, v3_ref,
               kc_ref, vc_ref, bias_ref, o_ref):
    kw = NA_KROWS * GRID_W
    outs = []
    for hh in range(2):
        sl = slice(hh * NA_DH, (hh + 1) * NA_DH)
        q = (q_ref[0][:, sl] * (NA_DH ** -0.5)).astype(BF16)
        ks = [r[0][:, sl].astype(BF16) for r in (k0_ref, k1_ref, k2_ref, k3_ref, kc_ref)]
        vs = [r[0][:, sl].astype(BF16) for r in (v0_ref, v1_ref, v2_ref, v3_ref, vc_ref)]
        ss = [lax.dot_general(q, kk, (((1,), (1,)), ((), ())), preferred_element_type=F32) for kk in ks]
        ss = [ss[j] + bias_ref[0, 0, hh, :, j * kw:(j + 1) * kw] for j in range(4)] + [ss[4]]
        m = functools.reduce(jnp.maximum, [jnp.max(t, axis=-1, keepdims=True) for t in ss])
        ps = [jnp.exp(t - m) for t in ss]
        l = functools.reduce(jnp.add, [jnp.sum(t, axis=-1, keepdims=True) for t in ps])
        o = functools.reduce(jnp.add, [jnp.dot(p.astype(BF16), vv, preferred_element_type=F32)
                                       for p, vv in zip(ps, vs)])
        outs.append(o / l)
    o_ref[0] = jnp.concatenate(outs, axis=-1).astype(o_ref.dtype)


def neighborhood_attention(p_lat, p_ctx, rpb):
    b, s, _ = p_lat.shape
    c_len = p_ctx.shape[1]
    rows = s // GRID_W
    assert rows % NA_QROWS == 0 and rows >= 2 * NA_QROWS
    n_steps = rows // NA_QROWS
    n_kblocks = rows // NA_KROWS
    hw = 2 * NA_DH
    n_pairs = NA_HEADS // 2
    bias = na_bias_tables(rpb, rows)
    bias = bias.reshape(3, n_pairs, 2, *bias.shape[2:])
    tq, tkv = NA_QROWS * GRID_W, NA_KROWS * GRID_W

    def view(j, col0):
        return pl.BlockSpec((1, tkv, hw),
                            lambda hp, i, n: (i, jnp.clip(2 * n - 1 + j, 0, n_kblocks - 1), col0 + hp))

    ctx_spec = lambda col0: pl.BlockSpec((1, c_len, hw), lambda hp, i, n: (i, 0, col0 + hp))
    bias_spec = pl.BlockSpec(
        (1, 1, 2, tq, 4 * tkv),
        lambda hp, i, n: (jnp.where(n == 0, 0, jnp.where(n == n_steps - 1, 2, 1)), hp, 0, 0, 0))
    return pl.pallas_call(
        _na_kernel,
        grid=(n_pairs, b, n_steps),
        in_specs=[pl.BlockSpec((1, tq, hw), lambda hp, i, n: (i, n, hp))]
                 + [view(j, n_pairs) for j in range(4)] + [view(j, 2 * n_pairs) for j in range(4)]
                 + [ctx_spec(n_pairs), ctx_spec(2 * n_pairs), bias_spec],
        out_specs=pl.BlockSpec((1, tq, hw), lambda hp, i, n: (i, n, hp)),
        out_shape=jax.ShapeDtypeStruct((b, s, NA_HEADS * NA_DH), BF16),
        compiler_params=_params("parallel", "parallel", "arbitrary"),
        name="neighborhood_attention",
    )(p_lat, *([p_lat] * 8), p_ctx, p_ctx, bias)


def mla_expand(pq, pkv, q_norm_g, w_uq, kv_norm_g, w_ukv, rotate):
    b, t, _ = pq.shape
    q = pmm(rmsnorm(pq, q_norm_g).reshape(b * t, -1), w_uq).reshape(b, t, -1)
    kv = pmm(rmsnorm(pkv[..., :MLA_KV_RANK], kv_norm_g).reshape(b * t, -1), w_ukv).reshape(b, t, -1)
    q, kv = to_heads(q, MLA_HEADS), to_heads(kv, MLA_HEADS)
    q_nope, q_pe = q[..., :MLA_NOPE], q[..., MLA_NOPE:]
    k_nope, v = kv[..., :MLA_NOPE], kv[..., MLA_NOPE:]
    k_pe = pkv[..., MLA_KV_RANK:]
    if rotate:
        q_pe, k_pe = rope2d(q_pe), rope2d(k_pe)
    return q_nope, q_pe, k_nope, k_pe, v


def mla_attention(q_parts, k_parts):
    qn, qp = q_parts
    kn, kp, v = k_parts
    scale = (MLA_NOPE + MLA_ROPE) ** -0.5
    dq = MLA_NOPE + MLA_ROPE
    pad = (-dq) % 128
    q = jnp.concatenate([qn, qp], axis=-1) * scale
    k = jnp.concatenate([kn, jnp.broadcast_to(kp[:, None], kn.shape[:3] + (MLA_ROPE,))], axis=-1)
    q = jnp.pad(q, ((0, 0),) * 3 + ((0, pad),)).astype(BF16)
    k = jnp.pad(k, ((0, 0),) * 3 + ((0, pad),)).astype(BF16)
    return from_heads(attention(q, k, v.astype(BF16)))


def context_attention(q, k, v):
    return from_heads(attention((q * (q.shape[-1] ** -0.5)).astype(BF16), k.astype(BF16), v.astype(BF16)))


def gdn_prep(p_qkv, p_a, p_b, conv_w, a_log, dt_bias):
    b, s, _ = p_a.shape
    qkv = jax.nn.silu(short_conv(p_qkv, conv_w))
    q, k, v = jnp.split(qkv, [GDN_HEADS * GDN_DK, 2 * GDN_HEADS * GDN_DK], axis=-1)
    q = l2norm(to_heads(q, GDN_HEADS)) * (GDN_DK ** -0.5)
    k = l2norm(to_heads(k, GDN_HEADS))
    v = to_heads(v, GDN_HEADS)
    a = p_a.reshape(b, s, 2, GDN_HEADS)
    log_alpha = -jnp.exp(a_log) * jax.nn.softplus(a + dt_bias)
    beta = jax.nn.sigmoid(p_b.reshape(b, s, 2, GDN_HEADS))
    return q, k, v, log_alpha.transpose(2, 0, 3, 1), beta.transpose(2, 0, 3, 1)


def gdn_scan(q, k, v, log_a, beta, state):
    b, h, s, dk = q.shape
    dv = v.shape[-1]
    L = GDN_CHUNK
    nc = s // L
    chunk = lambda t: t.reshape(b, h, nc, L, *t.shape[3:])
    q, k, v, log_a, beta = (chunk(t) for t in (q, k, v, log_a, beta))
    g = jnp.cumsum(log_a, axis=-1)
    idx = jnp.arange(L)
    decay = jnp.exp(jnp.where(idx[:, None] >= idx[None, :], g[..., :, None] - g[..., None, :], -jnp.inf))
    strict = jnp.where(idx[:, None] > idx[None, :], decay, 0.0)
    a_mat = jnp.eye(L, dtype=F32) + beta[..., :, None] * strict * jnp.einsum('bhnid,bhnjd->bhnij', k, k)
    rhs = jnp.concatenate([beta[..., None] * v, (beta * jnp.exp(g))[..., None] * k], axis=-1)
    sol = lax.linalg.triangular_solve(a_mat, rhs, left_side=True, lower=True, unit_diagonal=True)
    u0, w = sol[..., :dv], sol[..., dv:]
    qk = decay * jnp.einsum('bhnid,bhnjd->bhnij', q, k)
    g_end = g[..., -1]
    k_end = jnp.exp(g_end[..., None] - g)[..., None] * k
    q_dec = jnp.exp(g)[..., None] * q

    def step(S, xs):
        u0_c, w_c, qk_c, k_end_c, q_dec_c, e_end = xs
        u = u0_c - jnp.einsum('bhlk,bhvk->bhlv', w_c, S)
        o = jnp.einsum('bhlk,bhvk->bhlv', q_dec_c, S) + jnp.einsum('bhij,bhjv->bhiv', qk_c, u)
        S = e_end[..., None, None] * S + jnp.einsum('bhlv,bhlk->bhvk', u, k_end_c)
        return S, o

    xs = tuple(jnp.moveaxis(t, 2, 0) for t in (u0, w, qk, k_end, q_dec, jnp.exp(g_end)))
    state, o = lax.scan(step, state, xs)
    return jnp.moveaxis(o, 0, 2).reshape(b, h, s, dv), state


def gdn_output(o, p_g, norm_g):
    b, s, _ = p_g.shape
    y = rmsnorm(o.transpose(0, 2, 1, 3), norm_g) * jax.nn.silu(p_g.reshape(b, s, GDN_HEADS, GDN_DV))
    return y.reshape(b, s, GDN_HEADS * GDN_DV)


def mlstm_prep(pq, pk, pv, p_i, p_f, b_if):
    b, s, _ = p_i.shape
    q = to_heads(pq, ML_HEADS) * (ML_DK ** -0.5)
    k = to_heads(pk, ML_HEADS)
    v = to_heads(pv, ML_HEADS)
    i_pre = p_i.reshape(b, s, 2, ML_HEADS) + b_if[:, 0]
    log_f = jax.nn.log_sigmoid(p_f.reshape(b, s, 2, ML_HEADS) + b_if[:, 1])
    return q, k, v, i_pre.transpose(2, 0, 3, 1), log_f.transpose(2, 0, 3, 1)


def mlstm_scan(q, k, v, i_pre, log_f, state):
    b, h, s, dk = q.shape
    dv = v.shape[-1]
    L = ML_CHUNK
    nc = s // L
    chunk = lambda t: t.reshape(b, h, nc, L, *t.shape[3:])
    q, k, v, i_pre, log_f = (chunk(t) for t in (q, k, v, i_pre, log_f))
    fcum = jnp.cumsum(log_f, axis=-1)
    idx = jnp.arange(L)
    d_log = jnp.where(idx[:, None] >= idx[None, :],
                      fcum[..., :, None] - fcum[..., None, :] + i_pre[..., None, :], -jnp.inf)
    qk = jnp.einsum('bhnid,bhnjd->bhnij', q, k)
    f_end = fcum[..., -1]
    w_log = f_end[..., None] - fcum + i_pre
    d_max = jnp.max(d_log, axis=-1)
    w_max = jnp.max(w_log, axis=-1)

    def step(carry, xs):
        C, n, m = carry
        q_c, k_c, v_c, fc_c, dl_c, qk_c, fe_c, wl_c, dm_c, wm_c = xs
        m_q = jnp.maximum(fc_c + m[..., None], dm_c)
        inter = jnp.exp(fc_c + m[..., None] - m_q)
        intra = jnp.exp(dl_c - m_q[..., None]) * qk_c
        num = inter[..., None] * jnp.einsum('bhlk,bhvk->bhlv', q_c, C) + jnp.einsum('bhij,bhjv->bhiv', intra, v_c)
        den = inter * jnp.einsum('bhlk,bhk->bhl', q_c, n) + jnp.sum(intra, axis=-1)
        h_c = num / jnp.maximum(jnp.abs(den), jnp.exp(-m_q))[..., None]
        m_new = jnp.maximum(fe_c + m, wm_c)
        dec = jnp.exp(fe_c + m - m_new)
        wk = jnp.exp(wl_c - m_new[..., None])
        C = dec[..., None, None] * C + jnp.einsum('bhlv,bhlk->bhvk', wk[..., None] * v_c, k_c)
        n = dec[..., None] * n + jnp.einsum('bhl,bhlk->bhk', wk, k_c)
        return (C, n, m_new), h_c

    xs = tuple(jnp.moveaxis(t, 2, 0) for t in (q, k, v, fcum, d_log, qk, f_end, w_log, d_max, w_max))
    state, o = lax.scan(step, state, xs)
    return jnp.moveaxis(o, 0, 2).reshape(b, h, s, dv), state


def mlstm_output(o, p_o, norm_g):
    b, s, _ = p_o.shape
    y = rmsnorm(o.transpose(0, 2, 1, 3), norm_g) * jax.nn.sigmoid(p_o.reshape(b, s, ML_HEADS, ML_DV))
    return y.reshape(b, s, ML_HEADS * ML_DV)


def bidirectional(scan_fn, state0, ctx_in, lat_in):
    o_ctx, o_lat = 0.0, 0.0
    for d in range(2):
        c_in = ctx_in[:3] + (ctx_in[3][d], ctx_in[4][d])
        l_in = lat_in[:3] + (lat_in[3][d], lat_in[4][d])
        if d == 1:
            c_in = tuple(flip_t(t) for t in c_in)
            l_in = tuple(flip_t(t) for t in l_in)
        oc, st = scan_fn(*c_in, state0)
        ol, _ = scan_fn(*l_in, st)
        if d == 1:
            oc, ol = flip_t(oc), flip_t(ol)
        o_ctx = o_ctx + oc
        o_lat = o_lat + ol
    return o_ctx, o_lat


def token_mixer(h_lat, h_ctx, lw, with_ctx_out):
    bsz, s, d = h_lat.shape
    c_len = h_ctx.shape[1]
    cuts = np.cumsum(IN_SPLITS)[:-1].tolist()
    d_in = sum(IN_SPLITS)
    p_lat = pmm(h_lat.reshape(bsz * s, d), lw['w_in']).reshape(bsz, s, -1)
    p_ctx = pmm(h_ctx.reshape(bsz * c_len, d), lw['w_in']).reshape(bsz, c_len, -1)
    pl_ = jnp.split(p_lat[..., :d_in], cuts, axis=-1)
    pc = jnp.split(p_ctx[..., :d_in], cuts, axis=-1)
    k_ctx_na, v_ctx_na = to_heads(pc[1], NA_HEADS), to_heads(pc[2], NA_HEADS)
    ya_lat = neighborhood_attention(p_lat, p_ctx, lw['na_rpb'])
    gdn_in = lambda p: gdn_prep(p[3], p[5], p[6], lw['gdn_conv_w'], lw['gdn_a_log'], lw['gdn_dt_bias'])
    s0 = jnp.zeros((bsz, GDN_HEADS, GDN_DV, GDN_DK), F32)
    ob_ctx, ob_lat = bidirectional(gdn_scan, s0, gdn_in(pc), gdn_in(pl_))
    cq = mla_expand(pc[7], pc[8], lw['mla_q_norm_g'], lw['mla_w_uq'], lw['mla_kv_norm_g'], lw['mla_w_ukv'], False)
    lq = mla_expand(pl_[7], pl_[8], lw['mla_q_norm_g'], lw['mla_w_uq'], lw['mla_kv_norm_g'], lw['mla_w_ukv'], True)
    kn_all = jnp.concatenate([cq[2], lq[2]], axis=2)
    kp_all = jnp.concatenate([cq[3], lq[3]], axis=1)
    v_all = jnp.concatenate([cq[4], lq[4]], axis=2)
    yc_lat = mla_attention((lq[0], lq[1]), (kn_all, kp_all, v_all))
    ml_in = lambda p: mlstm_prep(p[9], p[10], p[11], p[13], p[14], lw['ml_b_if'])
    st0 = (jnp.zeros((bsz, ML_HEADS, ML_DV, ML_DK), F32),
           jnp.zeros((bsz, ML_HEADS, ML_DK), F32),
           jnp.zeros((bsz, ML_HEADS), F32))
    od_ctx, od_lat = bidirectional(mlstm_scan, st0, ml_in(pc), ml_in(pl_))
    ys_lat = [ya_lat, gdn_output(ob_lat, pl_[4], lw['gdn_norm_g']), yc_lat,
              mlstm_output(od_lat, pl_[12], lw['ml_norm_g'])]
    if not with_ctx_out:
        return ys_lat, None
    ya_ctx = context_attention(to_heads(pc[0], NA_HEADS), k_ctx_na, v_ctx_na)
    yc_ctx = mla_attention((cq[0], cq[1]), (cq[2], cq[3], cq[4]))
    ys_ctx = [ya_ctx, gdn_output(ob_ctx, pc[4], lw['gdn_norm_g']), yc_ctx,
              mlstm_output(od_ctx, pc[12], lw['ml_norm_g'])]
    return ys_lat, ys_ctx


def kernel(x, c, ctx, c_ctx, w_ada, b_ada, norm1_g, norm2_g, w_in, na_rpb, gdn_conv_w, gdn_a_log, gdn_dt_bias, gdn_norm_g, mla_q_norm_g, mla_w_uq, mla_kv_norm_g, mla_w_ukv, ml_b_if, ml_norm_g, w_mg, b_mg, w_bo, w_out, router_w, router_b, w_gu, b_gu, w_dn, b_dn, final_g):
    depth = w_ada.shape[0]
    bsz, s, d = x.shape
    cond = jnp.concatenate([jax.nn.silu(c), jax.nn.silu(c_ctx)[None]], axis=0)
    cond = jnp.pad(cond, ((0, (-cond.shape[0]) % 8), (0, 0)))
    w_mg_b, w_bo_b, w_out_b = (t.astype(BF16) for t in (w_mg, w_bo, w_out))
    w_in_b = jnp.pad(w_in, ((0, 0), (0, 0), (0, (-w_in.shape[-1]) % 512))).astype(BF16)
    w_uq_b, w_ukv_b = mla_w_uq.astype(BF16), mla_w_ukv.astype(BF16)
    for l in range(depth):
        with_ctx_out = l < depth - 1
        mod = pmm(cond, w_ada[l], tn=1024) + b_ada[l]
        sh1, sc1, g1, sh2, sc2, g2 = jnp.split(mod[:bsz], 6, axis=-1)
        csh1, csc1, cg1, csh2, csc2, cg2 = jnp.split(mod[bsz:bsz + 1], 6, axis=-1)
        lw = dict(w_in=w_in_b[l], na_rpb=na_rpb[l], gdn_conv_w=gdn_conv_w[l], gdn_a_log=gdn_a_log[l],
                  gdn_dt_bias=gdn_dt_bias[l], gdn_norm_g=gdn_norm_g[l], mla_q_norm_g=mla_q_norm_g[l],
                  mla_w_uq=w_uq_b[l], mla_kv_norm_g=mla_kv_norm_g[l], mla_w_ukv=w_ukv_b[l],
                  ml_b_if=ml_b_if[l], ml_norm_g=ml_norm_g[l])
        h_lat = norm_mod(x, norm1_g[l], sc1, sh1)
        h_ctx = norm_mod(ctx, norm1_g[l], csc1, csh1)
        ys_lat, ys_ctx = token_mixer(h_lat, h_ctx, lw, with_ctx_out)
        x = merge_residual(h_lat, [y.astype(BF16) for y in ys_lat], x, g1, w_mg_b[l], b_mg[l], w_bo_b[l], w_out_b[l])
        h2_lat = norm_mod(x, norm2_g[l], sc2, sh2, out_dtype=F32)
        if with_ctx_out:
            ctx = merge_residual(h_ctx, [y.astype(BF16) for y in ys_ctx], ctx, cg1, w_mg_b[l], b_mg[l], w_bo_b[l], w_out_b[l])
            h2_ctx = norm_mod(ctx, norm2_g[l], csc2, csh2, out_dtype=F32)
            n_ctx = h2_ctx.shape[0] * h2_ctx.shape[1]
            tokens = jnp.concatenate([h2_ctx.reshape(-1, d), h2_lat.reshape(-1, d)], axis=0)
            f = moe_ffn(tokens, router_w[l], router_b[l], w_gu[l], b_gu[l], w_dn[l], b_dn[l])
            ctx = ctx + cg2 * f[:n_ctx].reshape(ctx.shape)
            f_lat = f[n_ctx:].reshape(x.shape)
        else:
            f_lat = moe_ffn(h2_lat.reshape(-1, d), router_w[l], router_b[l], w_gu[l], b_gu[l],
                            w_dn[l], b_dn[l]).reshape(x.shape)
        x = x + g2[:, None, :] * f_lat
    return rmsnorm(x, final_g)
```

```python
import functools
import math

import numpy as np
import jax
import jax.numpy as jnp
from jax import lax
from jax.experimental import pallas as pl
from jax.experimental.pallas import tpu as pltpu

F32 = jnp.float32
BF16 = jnp.bfloat16

GRID_W = 64
EPS = 1e-6
ROPE_BASE = 10000.0
NA_HEADS, NA_DH, NA_WIN_H, NA_WIN_W = 8, 64, 8, 16
GDN_HEADS, GDN_DK, GDN_DV, GDN_CHUNK = 4, 128, 128, 64
MLA_HEADS, MLA_NOPE, MLA_ROPE, MLA_V, MLA_Q_RANK, MLA_KV_RANK = 8, 64, 32, 64, 256, 128
ML_HEADS, ML_DK, ML_DV, ML_CHUNK = 4, 64, 128, 64
N_BRANCH, BRANCH_W = 4, 512
N_EXPERTS, TOP_K, D_FF = 32, 4, 1024
SWIGLU_ALPHA, SWIGLU_LIMIT = 1.702, 7.0
MOE_ROWS = 256

IN_SPLITS = (
    NA_HEADS * NA_DH, NA_HEADS * NA_DH, NA_HEADS * NA_DH,
    GDN_HEADS * (2 * GDN_DK + GDN_DV), GDN_HEADS * GDN_DV,
    2 * GDN_HEADS, 2 * GDN_HEADS,
    MLA_Q_RANK, MLA_KV_RANK + MLA_ROPE,
    ML_HEADS * ML_DK, ML_HEADS * ML_DK, ML_HEADS * ML_DV, ML_HEADS * ML_DV,
    2 * ML_HEADS, 2 * ML_HEADS,
)

VMEM_LIMIT_BYTES = 56 * 1024 * 1024


def _params(*sem):
    return pltpu.CompilerParams(dimension_semantics=sem, vmem_limit_bytes=VMEM_LIMIT_BYTES)


def _mm_kernel(x_ref, w_ref, o_ref):
    o_ref[...] = jnp.dot(x_ref[...].astype(BF16), w_ref[...].astype(BF16),
                         preferred_element_type=F32).astype(o_ref.dtype)


def pmm(x, w, out_dtype=F32, tm=512, tn=512):
    m, k = x.shape
    n = w.shape[1]
    tm, tn = min(tm, m), min(tn, n)
    if n % tn:
        tn = n
    assert m % tm == 0 and n % tn == 0, (m, n, tm, tn)
    return pl.pallas_call(
        _mm_kernel,
        grid=(m // tm, n // tn),
        in_specs=[pl.BlockSpec((tm, k), lambda i, j: (i, 0)),
                  pl.BlockSpec((k, tn), lambda i, j: (0, j))],
        out_specs=pl.BlockSpec((tm, tn), lambda i, j: (i, j)),
        out_shape=jax.ShapeDtypeStruct((m, n), out_dtype),
        compiler_params=_params("parallel", "parallel"),
        name="pmm",
    )(x, w)


def _norm_mod_kernel(x_ref, g_ref, sc_ref, sh_ref, o_ref):
    x = x_ref[0].astype(F32)
    y = x * lax.rsqrt(jnp.mean(x * x, axis=-1, keepdims=True) + EPS)
    o_ref[0] = ((y * g_ref[...]) * (1.0 + sc_ref[0]) + sh_ref[0]).astype(o_ref.dtype)


def norm_mod(x, g, sc, sh, out_dtype=BF16, ts=512):
    b, t, d = x.shape
    ts = min(ts, t)
    assert t % ts == 0
    per_batch = sc.shape[0] == b and b > 1
    mod_map = (lambda i, j: (i, 0, 0)) if per_batch else (lambda i, j: (0, 0, 0))
    return pl.pallas_call(
        _norm_mod_kernel,
        grid=(b, t // ts),
        in_specs=[pl.BlockSpec((1, ts, d), lambda i, j: (i, j, 0)),
                  pl.BlockSpec((1, d), lambda i, j: (0, 0)),
                  pl.BlockSpec((1, 1, d), mod_map),
                  pl.BlockSpec((1, 1, d), mod_map)],
        out_specs=pl.BlockSpec((1, ts, d), lambda i, j: (i, j, 0)),
        out_shape=jax.ShapeDtypeStruct((b, t, d), out_dtype),
        compiler_params=_params("parallel", "parallel"),
        name="norm_mod",
    )(x, g.reshape(1, d), sc.reshape(-1, 1, d), sh.reshape(-1, 1, d))


def _attn_kernel(q_ref, k_ref, v_ref, o_ref, *, tk):
    q = q_ref[0, 0]
    tq = q.shape[0]
    dv = v_ref.shape[-1]
    n_chunks = k_ref.shape[2] // tk

    def body(c, carry):
        m, l, acc = carry
        off = pl.multiple_of(c * tk, tk)
        k = k_ref[0, 0, pl.ds(off, tk), :]
        v = v_ref[0, 0, pl.ds(off, tk), :]
        s = lax.dot_general(q, k, (((1,), (1,)), ((), ())), preferred_element_type=F32)
        m_new = jnp.maximum(m, jnp.max(s, axis=-1, keepdims=True))
        alpha = jnp.exp(m - m_new)
        p = jnp.exp(s - m_new)
        l = alpha * l + jnp.sum(p, axis=-1, keepdims=True)
        acc = alpha * acc + jnp.dot(p.astype(BF16), v, preferred_element_type=F32)
        return m_new, l, acc

    init = (jnp.full((tq, 1), -jnp.inf, F32), jnp.zeros((tq, 1), F32), jnp.zeros((tq, dv), F32))
    _, l, acc = lax.fori_loop(0, n_chunks, body, init)
    o_ref[0, 0] = (acc / l).astype(o_ref.dtype)


def attention(q, k, v, tq=512, tk=768):
    b, h, s, dq = q.shape
    t, dv = k.shape[2], v.shape[3]
    tq = min(tq, s)
    if t % tk:
        tk = t
    assert s % tq == 0
    return pl.pallas_call(
        functools.partial(_attn_kernel, tk=tk),
        grid=(b, h, s // tq),
        in_specs=[pl.BlockSpec((1, 1, tq, dq), lambda i, j, n: (i, j, n, 0)),
                  pl.BlockSpec((1, 1, t, dq), lambda i, j, n: (i, j, 0, 0)),
                  pl.BlockSpec((1, 1, t, dv), lambda i, j, n: (i, j, 0, 0))],
        out_specs=pl.BlockSpec((1, 1, tq, dv), lambda i, j, n: (i, j, n, 0)),
        out_shape=jax.ShapeDtypeStruct((b, h, s, dv), F32),
        compiler_params=_params("parallel", "parallel", "parallel"),
        name="attention",
    )(q, k, v)


def _merge_kernel(h_ref, ya_ref, yb_ref, yc_ref, yd_ref, x_ref, g1_ref,
                  wmg_ref, bmg_ref, wbo_ref, wout_ref, o_ref):
    d = o_ref.shape[-1]
    h = h_ref[0]
    acc = jnp.zeros((h.shape[0], d), F32)
    for n, y_ref in enumerate((ya_ref, yb_ref, yc_ref, yd_ref)):
        z = jnp.dot(y_ref[0], wbo_ref[n], preferred_element_type=F32)
        gate = jnp.dot(h, wmg_ref[:, n * d:(n + 1) * d], preferred_element_type=F32) + bmg_ref[:, n * d:(n + 1) * d]
        acc = acc + jax.nn.sigmoid(gate) * z
    m = jnp.dot(acc.astype(BF16), wout_ref[...], preferred_element_type=F32)
    o_ref[0] = x_ref[0] + g1_ref[0] * m


def merge_residual(h, ys, x, g1, w_mg, b_mg, w_bo, w_out, ts=256):
    b, t, d = x.shape
    ts = min(ts, t)
    assert t % ts == 0
    per_batch = g1.shape[0] == b and b > 1
    mod_map = (lambda i, j: (i, 0, 0)) if per_batch else (lambda i, j: (0, 0, 0))
    row = lambda w: pl.BlockSpec((1, ts, w), lambda i, j: (i, j, 0))
    full = lambda a: pl.BlockSpec(a.shape, lambda i, j: (0,) * a.ndim)
    b_mg = b_mg.reshape(1, -1)
    return pl.pallas_call(
        _merge_kernel,
        grid=(b, t // ts),
        in_specs=[row(d)] + [row(BRANCH_W)] * 4 + [row(d), pl.BlockSpec((1, 1, d), mod_map),
                                                 full(w_mg), full(b_mg), full(w_bo), full(w_out)],
        out_specs=row(d),
        out_shape=jax.ShapeDtypeStruct((b, t, d), F32),
        compiler_params=_params("parallel", "parallel"),
        name="merge_residual",
    )(h, *ys, x, g1.reshape(-1, 1, d), w_mg, b_mg, w_bo, w_out)


def _expert_kernel(be_ref, x_ref, wgu_ref, bgu_ref, wdn_ref, bdn_ref, o_ref, wgu_s, wdn_s):
    i = pl.program_id(0)
    fresh = jnp.logical_or(i == 0, be_ref[i] != be_ref[jnp.maximum(i - 1, 0)])

    @pl.when(fresh)
    def _():
        wgu_s[...] = wgu_ref[0].astype(BF16)
        wdn_s[...] = wdn_ref[0].astype(BF16)

    gu = jnp.dot(x_ref[...], wgu_s[...], preferred_element_type=F32) + bgu_ref[0]
    gate = jnp.minimum(gu[:, :D_FF], SWIGLU_LIMIT)
    up = jnp.clip(gu[:, D_FF:], -SWIGLU_LIMIT, SWIGLU_LIMIT)
    act = (up + 1.0) * gate * jax.nn.sigmoid(SWIGLU_ALPHA * gate)
    o_ref[...] = jnp.dot(act.astype(BF16), wdn_s[...], preferred_element_type=F32) + bdn_ref[0]


def expert_blocks(buf, block_e, w_gu, b_gu, w_dn, b_dn):
    rows, d = buf.shape
    n_blocks = rows // MOE_ROWS
    e = w_gu.shape[0]
    grid_spec = pltpu.PrefetchScalarGridSpec(
        num_scalar_prefetch=1,
        grid=(n_blocks,),
        in_specs=[pl.BlockSpec((MOE_ROWS, d), lambda i, be: (i, 0)),
                  pl.BlockSpec((1, d, 2 * D_FF), lambda i, be: (be[i], 0, 0)),
                  pl.BlockSpec((1, 1, 2 * D_FF), lambda i, be: (be[i], 0, 0)),
                  pl.BlockSpec((1, D_FF, d), lambda i, be: (be[i], 0, 0)),
                  pl.BlockSpec((1, 1, d), lambda i, be: (be[i], 0, 0))],
        out_specs=pl.BlockSpec((MOE_ROWS, d), lambda i, be: (i, 0)),
        scratch_shapes=[pltpu.VMEM((d, 2 * D_FF), BF16), pltpu.VMEM((D_FF, d), BF16)],
    )
    return pl.pallas_call(
        _expert_kernel,
        grid_spec=grid_spec,
        out_shape=jax.ShapeDtypeStruct((rows, d), F32),
        compiler_params=_params("arbitrary"),
        name="expert_blocks",
    )(block_e, buf, w_gu, b_gu.reshape(e, 1, -1), w_dn, b_dn.reshape(e, 1, -1))


def moe_ffn(t, router_w, router_b, w_gu, b_gu, w_dn, b_dn):
    n, d = t.shape
    logits = jnp.dot(t, router_w, precision=lax.Precision.HIGHEST) + router_b
    top_v, top_i = lax.top_k(logits, TOP_K)
    top_w = jax.nn.softmax(top_v, axis=-1)
    nk = n * TOP_K
    e_flat = top_i.reshape(nk)
    onehot = (e_flat[:, None] == jnp.arange(N_EXPERTS)[None, :]).astype(jnp.int32)
    csum = jnp.cumsum(onehot, axis=0)
    rank = jnp.take_along_axis(csum, e_flat[:, None], axis=1)[:, 0] - 1
    counts = csum[-1]
    padded = (counts + MOE_ROWS - 1) // MOE_ROWS * MOE_ROWS
    pad_end = jnp.cumsum(padded)
    dest = (pad_end - padded)[e_flat] + rank
    n_blocks = -(-nk // MOE_ROWS) + N_EXPERTS
    tok = jnp.arange(nk, dtype=jnp.int32) // TOP_K
    src = jnp.full((n_blocks * MOE_ROWS,), n, jnp.int32).at[dest].set(tok)
    buf = jnp.concatenate([t.astype(BF16), jnp.zeros((1, d), BF16)], axis=0)[src]
    block_e = jnp.minimum(jnp.searchsorted(pad_end, jnp.arange(n_blocks) * MOE_ROWS, side='right'),
                          N_EXPERTS - 1).astype(jnp.int32)
    y = expert_blocks(buf, block_e, w_gu, b_gu, w_dn, b_dn)
    return jnp.sum(y[dest].reshape(n, TOP_K, d) * top_w[..., None], axis=1)


def rmsnorm(x, g):
    xf = x.astype(F32)
    return xf * lax.rsqrt(jnp.mean(xf * xf, axis=-1, keepdims=True) + EPS) * g


def l2norm(x):
    return x * lax.rsqrt(jnp.sum(x * x, axis=-1, keepdims=True) + EPS)


def to_heads(t, n_heads):
    b, s, _ = t.shape
    return t.reshape(b, s, n_heads, -1).transpose(0, 2, 1, 3)


def from_heads(t):
    b, h, s, d = t.shape
    return t.transpose(0, 2, 1, 3).reshape(b, s, h * d)


def flip_t(t):
    return jnp.flip(t, axis=2)


def short_conv(x, w):
    k = w.shape[0]
    pad = k // 2
    s = x.shape[1]
    xp = jnp.pad(x, ((0, 0), (pad, pad), (0, 0)))
    y = xp[:, 0:s] * w[0]
    for j in range(1, k):
        y = y + xp[:, j:j + s] * w[j]
    return y


def rope2d(x):
    s = x.shape[-2]
    half = x.shape[-1] // 2
    t = jnp.arange(s)
    inv = ROPE_BASE ** (-jnp.arange(0, half, 2, dtype=F32) / half)

    def rot(xa, pos):
        ang = pos.astype(F32)[:, None] * inv
        cos, sin = jnp.cos(ang), jnp.sin(ang)
        x1, x2 = jnp.split(xa, 2, axis=-1)
        return jnp.concatenate([x1 * cos - x2 * sin, x2 * cos + x1 * sin], axis=-1)

    return jnp.concatenate([rot(x[..., :half], t // GRID_W), rot(x[..., half:], t % GRID_W)], axis=-1)


NA_QROWS = 8
NA_KROWS = 4
NA_NEG = -1e30


def na_bias_tables(rpb, rows):
    n_steps = rows // NA_QROWS
    wh, ww = min(NA_WIN_H, rows), NA_WIN_W
    n_kblocks = rows // NA_KROWS
    n_heads = rpb.shape[0]
    width = 2 * GRID_W - 1
    lo = GRID_W - NA_WIN_W
    padded = jnp.full(rpb.shape[:2] + (width,), NA_NEG, F32).at[..., lo:lo + 2 * NA_WIN_W - 1].set(rpb)
    tiles = jnp.stack([padded[..., GRID_W - 1 - qc:width - qc] for qc in range(GRID_W)], axis=2)
    qc, kc = np.arange(GRID_W)[:, None], np.arange(GRID_W)[None, :]
    c0 = np.clip(qc - ww // 2, 0, GRID_W - ww)
    tiles = jnp.where((kc >= c0) & (kc < c0 + ww), tiles, NA_NEG)
    masked = jnp.full((n_heads, GRID_W, GRID_W), NA_NEG, F32)
    tables = []
    for i in (0, min(1, n_steps - 1), n_steps - 1):
        q_tiles = []
        for il in range(NA_QROWS):
            q = NA_QROWS * i + il
            r0 = min(max(q - wh // 2, 0), rows - wh)
            k_tiles = []
            for ka in range(4 * NA_KROWS):
                blk = 2 * i - 1 + ka // NA_KROWS
                kr = NA_KROWS * blk + ka % NA_KROWS
                ok = 0 <= blk < n_kblocks and r0 <= kr < r0 + wh
                k_tiles.append(tiles[:, kr - q + NA_WIN_H - 1] if ok else masked)
            q_tiles.append(jnp.concatenate(k_tiles, axis=-1))
        tables.append(jnp.concatenate(q_tiles, axis=-2))
    return jnp.stack(tables)


def _na_kernel(q_ref, k0_ref, k1_ref, k2_ref, k3_ref, v0_ref, v1_ref, v2_ref, v3_ref,
               kc_ref, vc_ref, bias_ref, o_ref):
    kw = NA_KROWS * GRID_W
    outs = []
    for hh in range(2):
        sl = slice(hh * NA_DH, (hh + 1) * NA_DH)
        q = (q_ref[0][:, sl] * (NA_DH ** -0.5)).astype(BF16)
        ks = [r[0][:, sl].astype(BF16) for r in (k0_ref, k1_ref, k2_ref, k3_ref, kc_ref)]
        vs = [r[0][:, sl].astype(BF16) for r in (v0_ref, v1_ref, v2_ref, v3_ref, vc_ref)]
        ss = [lax.dot_general(q, kk, (((1,), (1,)), ((), ())), preferred_element_type=F32) for kk in ks]
        ss = [ss[j] + bias_ref[0, 0, hh, :, j * kw:(j + 1) * kw] for j in range(4)] + [ss[4]]
        m = functools.reduce(jnp.maximum, [jnp.max(t, axis=-1, keepdims=True) for t in ss])
        ps = [jnp.exp(t - m) for t in ss]
        l = functools.reduce(jnp.add, [jnp.sum(t, axis=-1, keepdims=True) for t in ps])
        o = functools.reduce(jnp.add, [jnp.dot(p.astype(BF16), vv, preferred_element_type=F32)
                                       for p, vv in zip(ps, vs)])
        outs.append(o / l)
    o_ref[0] = jnp.concatenate(outs, axis=-1).astype(o_ref.dtype)


def neighborhood_attention(p_lat, p_ctx, rpb):
    b, s, _ = p_lat.shape
    c_len = p_ctx.shape[1]
    rows = s // GRID_W
    assert rows % NA_QROWS == 0 and rows >= 2 * NA_QROWS
    n_steps = rows // NA_QROWS
    n_kblocks = rows // NA_KROWS
    hw = 2 * NA_DH
    n_pairs = NA_HEADS // 2
    bias = na_bias_tables(rpb, rows)
    bias = bias.reshape(3, n_pairs, 2, *bias.shape[2:])
    tq, tkv = NA_QROWS * GRID_W, NA_KROWS * GRID_W

    def view(j, col0):
        return pl.BlockSpec((1, tkv, hw),
                            lambda hp, i, n: (i, jnp.clip(2 * n - 1 + j, 0, n_kblocks - 1), col0 + hp))

    ctx_spec = lambda col0: pl.BlockSpec((1, c_len, hw), lambda hp, i, n: (i, 0, col0 + hp))
    bias_spec = pl.BlockSpec(
        (1, 1, 2, tq, 4 * tkv),
        lambda hp, i, n: (jnp.where(n == 0, 0, jnp.where(n == n_steps - 1, 2, 1)), hp, 0, 0, 0))
    return pl.pallas_call(
        _na_kernel,
        grid=(n_pairs, b, n_steps),
        in_specs=[pl.BlockSpec((1, tq, hw), lambda hp, i, n: (i, n, hp))]
                 + [view(j, n_pairs) for j in range(4)] + [view(j, 2 * n_pairs) for j in range(4)]
                 + [ctx_spec(n_pairs), ctx_spec(2 * n_pairs), bias_spec],
        out_specs=pl.BlockSpec((1, tq, hw), lambda hp, i, n: (i, n, hp)),
        out_shape=jax.ShapeDtypeStruct((b, s, NA_HEADS * NA_DH), BF16),
        compiler_params=_params("parallel", "parallel", "arbitrary"),
        name="neighborhood_attention",
    )(p_lat, *([p_lat] * 8), p_ctx, p_ctx, bias)


def mla_expand(pq, pkv, q_norm_g, w_uq, kv_norm_g, w_ukv, rotate):
    b, t, _ = pq.shape
    q = pmm(rmsnorm(pq, q_norm_g).reshape(b * t, -1), w_uq).reshape(b, t, -1)
    kv = pmm(rmsnorm(pkv[..., :MLA_KV_RANK], kv_norm_g).reshape(b * t, -1), w_ukv).reshape(b, t, -1)
    q, kv = to_heads(q, MLA_HEADS), to_heads(kv, MLA_HEADS)
    q_nope, q_pe = q[..., :MLA_NOPE], q[..., MLA_NOPE:]
    k_nope, v = kv[..., :MLA_NOPE], kv[..., MLA_NOPE:]
    k_pe = pkv[..., MLA_KV_RANK:]
    if rotate:
        q_pe, k_pe = rope2d(q_pe), rope2d(k_pe)
    return q_nope, q_pe, k_nope, k_pe, v


def mla_attention(q_parts, k_parts):
    qn, qp = q_parts
    kn, kp, v = k_parts
    scale = (MLA_NOPE + MLA_ROPE) ** -0.5
    dq = MLA_NOPE + MLA_ROPE
    pad = (-dq) % 128
    q = jnp.concatenate([qn, qp], axis=-1) * scale
    k = jnp.concatenate([kn, jnp.broadcast_to(kp[:, None], kn.shape[:3] + (MLA_ROPE,))], axis=-1)
    q = jnp.pad(q, ((0, 0),) * 3 + ((0, pad),)).astype(BF16)
    k = jnp.pad(k, ((0, 0),) * 3 + ((0, pad),)).astype(BF16)
    return from_heads(attention(q, k, v.astype(BF16)))


def context_attention(q, k, v):
    return from_heads(attention((q * (q.shape[-1] ** -0.5)).astype(BF16), k.astype(BF16), v.astype(BF16)))


def gdn_prep(p_qkv, p_a, p_b, conv_w, a_log, dt_bias):
    b, s, _ = p_a.shape
    qkv = jax.nn.silu(short_conv(p_qkv, conv_w))
    q, k, v = jnp.split(qkv, [GDN_HEADS * GDN_DK, 2 * GDN_HEADS * GDN_DK], axis=-1)
    q = (l2norm(q.reshape(b, s, GDN_HEADS, GDN_DK)) * (GDN_DK ** -0.5)).reshape(b, s, -1)
    k = l2norm(k.reshape(b, s, GDN_HEADS, GDN_DK)).reshape(b, s, -1)
    log_alpha = -jnp.exp(a_log.reshape(-1)) * jax.nn.softplus(p_a + dt_bias.reshape(-1))
    return jnp.concatenate([q, k, v], axis=-1), jnp.concatenate([log_alpha, jax.nn.sigmoid(p_b)], axis=-1)


def _dot(a, b, precise=False):
    if precise:
        return jnp.dot(a, b, preferred_element_type=F32, precision=lax.Precision.HIGHEST)
    return jnp.dot(a.astype(BF16), b.astype(BF16), preferred_element_type=F32)


def _dot_nt(a, b):
    return lax.dot_general(a.astype(BF16), b.astype(BF16), (((1,), (1,)), ((), ())), preferred_element_type=F32)


def _dot_tn(a, b):
    return lax.dot_general(a.astype(BF16), b.astype(BF16), (((0,), (0,)), ((), ())), preferred_element_type=F32)


def _order_masks(length, reverse):
    r = lax.broadcasted_iota(jnp.int32, (length, length), 0)
    c = lax.broadcasted_iota(jnp.int32, (length, length), 1)
    if reverse:
        return r <= c, r < c, r >= c
    return r >= c, r > c, r <= c


def _gdn_kernel(xf_ref, gf_ref, rf_ref, xb_ref, gb_ref, rb_ref, of_ref, ob_ref, s_ref):
    @pl.when(pl.program_id(1) == 0)
    def _():
        s_ref[...] = jnp.zeros_like(s_ref)

    nh, dk, dv, length = GDN_HEADS, GDN_DK, GDN_DV, GDN_CHUNK
    r = lax.broadcasted_iota(jnp.int32, (length, length), 0)
    c = lax.broadcasted_iota(jnp.int32, (length, length), 1)
    eye = (r == c).astype(F32)
    seqs = []
    for d, (x_ref, g_ref, r_ref, o_ref) in enumerate(((xf_ref, gf_ref, rf_ref, of_ref),
                                                      (xb_ref, gb_ref, rb_ref, ob_ref))):
        incl, strict, incl_t = _order_masks(length, d == 1)
        gates = g_ref[0]
        la_col = gates[:, d * nh:(d + 1) * nh]
        be_col = gates[:, (2 + d) * nh:(3 + d) * nh]
        la_row = r_ref[0, 0][d * nh:(d + 1) * nh, :]
        g_cols = _dot(incl.astype(F32), la_col, precise=True)
        g_rows = _dot(la_row, incl_t.astype(F32), precise=True)
        g_ends = jnp.sum(la_col, axis=0, keepdims=True)
        for h in range(nh):
            q = x_ref[0, :, h * dk:(h + 1) * dk]
            k = x_ref[0, :, (nh + h) * dk:(nh + h + 1) * dk]
            v = x_ref[0, :, 2 * nh * dk + h * dv:2 * nh * dk + (h + 1) * dv]
            gc, gr, ge, bc = g_cols[:, h:h + 1], g_rows[h:h + 1, :], g_ends[:, h:h + 1], be_col[:, h:h + 1]
            decay = jnp.exp(jnp.where(incl, gc - gr, -jnp.inf))
            qk = decay * _dot_nt(q, k)
            p = -(bc * jnp.where(strict, decay, 0.0) * _dot_nt(k, k))
            eg = jnp.exp(gc)
            seqs.append(dict(o_ref=o_ref, h=h, idx=d * nh + h, q_dec=eg * q, qk=qk, p=p, t=eye + p,
                             rhs=jnp.concatenate([bc * v, (bc * eg) * k], axis=-1),
                             k_end=jnp.exp(ge - gc) * k, e_end=jnp.exp(ge)))
    for _ in range(int(math.log2(length)) - 1):
        for s in seqs:
            s['p'] = _dot(s['p'], s['p'])
        for s in seqs:
            s['t'] = s['t'] + _dot(s['t'], s['p'])
    for s in seqs:
        s['sol'] = _dot(s['t'], s['rhs'])
    for s in seqs:
        s['s_old'] = s_ref[s['idx']]
        s['u'] = s['sol'][:, :dv] - _dot_nt(s['sol'][:, dv:], s['s_old'])
    for s in seqs:
        h = s['h']
        s['o_ref'][0, :, h * dv:(h + 1) * dv] = _dot_nt(s['q_dec'], s['s_old']) + _dot(s['qk'], s['u'])
        s_ref[s['idx']] = s['e_end'] * s['s_old'] + _dot_tn(s['u'], s['k_end'])


def _mlstm_kernel(xf_ref, gf_ref, rf_ref, xb_ref, gb_ref, rb_ref, of_ref, ob_ref, c_ref, n_ref, m_ref):
    @pl.when(pl.program_id(1) == 0)
    def _():
        c_ref[...] = jnp.zeros_like(c_ref)
        n_ref[...] = jnp.zeros_like(n_ref)
        m_ref[...] = jnp.zeros_like(m_ref)

    nh, dk, dv, length = ML_HEADS, ML_DK, ML_DV, ML_CHUNK
    seqs = []
    for d, (x_ref, g_ref, r_ref, o_ref) in enumerate(((xf_ref, gf_ref, rf_ref, of_ref),
                                                      (xb_ref, gb_ref, rb_ref, ob_ref))):
        incl, _, incl_t = _order_masks(length, d == 1)
        gates = g_ref[0]
        i_col = gates[:, d * nh:(d + 1) * nh]
        f_col = gates[:, (2 + d) * nh:(3 + d) * nh]
        rows = r_ref[0, 0]
        i_row = rows[d * nh:(d + 1) * nh, :]
        f_row = rows[(2 + d) * nh:(3 + d) * nh, :]
        fc_cols = _dot(incl.astype(F32), f_col, precise=True)
        fc_rows = _dot(f_row, incl_t.astype(F32), precise=True)
        f_ends = jnp.sum(f_col, axis=0, keepdims=True)
        for h in range(nh):
            q = x_ref[0, :, h * dk:(h + 1) * dk]
            k = x_ref[0, :, (nh + h) * dk:(nh + h + 1) * dk]
            v = x_ref[0, :, 2 * nh * dk + h * dv:2 * nh * dk + (h + 1) * dv]
            fc, fr, fe = fc_cols[:, h:h + 1], fc_rows[h:h + 1, :], f_ends[:, h:h + 1]
            ic, ir = i_col[:, h:h + 1], i_row[h:h + 1, :]
            idx = d * nh + h
            c_old, n_old, m_old = c_ref[idx], n_ref[idx], m_ref[idx][:, 0:1]
            d_log = jnp.where(incl, fc - fr + ir, -jnp.inf)
            w_log = fe - fc + ic
            m_q = jnp.maximum(fc + m_old, jnp.max(d_log, axis=-1, keepdims=True))
            m_new = jnp.maximum(fe + m_old, jnp.max(w_log, axis=0, keepdims=True))
            seqs.append(dict(o_ref=o_ref, h=h, idx=idx, q=q, k=k, v=v, c_old=c_old, n_old=n_old, m_q=m_q,
                             m_new=m_new, inter=jnp.exp(fc + m_old - m_q), decay=jnp.exp(d_log - m_q),
                             dec=jnp.exp(fe + m_old - m_new), wk=jnp.exp(w_log - m_new)))
    for s in seqs:
        s['intra'] = s['decay'] * _dot_nt(s['q'], s['k'])
    for s in seqs:
        s['num'] = s['inter'] * _dot_nt(s['q'], s['c_old']) + _dot(s['intra'], s['v'])
    for s in seqs:
        h, idx = s['h'], s['idx']
        den = (s['inter'] * jnp.sum(s['q'] * s['n_old'], axis=-1, keepdims=True)
               + jnp.sum(s['intra'], axis=-1, keepdims=True))
        s['o_ref'][0, :, h * dv:(h + 1) * dv] = s['num'] / jnp.maximum(jnp.abs(den), jnp.exp(-s['m_q']))
        c_ref[idx] = s['dec'] * s['c_old'] + _dot_tn(s['wk'] * s['v'], s['k'])
        n_ref[idx] = s['dec'] * s['n_old'] + jnp.sum(s['wk'] * s['k'], axis=0, keepdims=True)
        m_ref[idx] = jnp.broadcast_to(s['m_new'], m_ref.shape[1:])


def bidirectional_scan(kernel_fn, x, gates, n_ctx_chunks, length, out_w, scratch_shapes, name):
    b, t, xw = x.shape
    gw = gates.shape[-1]
    nc = t // length
    assert t % length == 0
    gates_t = gates.reshape(b, nc, length, gw).transpose(0, 1, 3, 2)
    fwd = lambda n: n
    bwd = lambda n: jnp.where(n < n_ctx_chunks, n_ctx_chunks - 1 - n, nc - 1 - (n - n_ctx_chunks))
    specs = lambda f: [pl.BlockSpec((1, length, xw), lambda i, n: (i, f(n), 0)),
                       pl.BlockSpec((1, length, gw), lambda i, n: (i, f(n), 0)),
                       pl.BlockSpec((1, 1, gw, length), lambda i, n: (i, f(n), 0, 0))]
    out = lambda f: pl.BlockSpec((1, length, out_w), lambda i, n: (i, f(n), 0))
    return pl.pallas_call(
        kernel_fn,
        grid=(b, nc),
        in_specs=specs(fwd) + specs(bwd),
        out_specs=[out(fwd), out(bwd)],
        out_shape=[jax.ShapeDtypeStruct((b, t, out_w), F32)] * 2,
        scratch_shapes=scratch_shapes,
        compiler_params=_params("parallel", "arbitrary"),
        name=name,
    )(x, gates, gates_t, x, gates, gates_t)


def gdn_scan(x, gates, n_ctx_chunks):
    state = pltpu.VMEM((2 * GDN_HEADS, GDN_DV, GDN_DK), F32)
    return bidirectional_scan(_gdn_kernel, x, gates, n_ctx_chunks, GDN_CHUNK, GDN_HEADS * GDN_DV, [state], "gdn_scan")


def gdn_output(o, p_g, norm_g):
    b, s, _ = p_g.shape
    y = rmsnorm(o.reshape(b, s, GDN_HEADS, GDN_DV), norm_g) * jax.nn.silu(p_g.reshape(b, s, GDN_HEADS, GDN_DV))
    return y.reshape(b, s, GDN_HEADS * GDN_DV)


def mlstm_prep(pq, pk, pv, p_i, p_f, b_if):
    i_pre = p_i + b_if[:, 0].reshape(-1)
    log_f = jax.nn.log_sigmoid(p_f + b_if[:, 1].reshape(-1))
    return jnp.concatenate([pq * (ML_DK ** -0.5), pk, pv], axis=-1), jnp.concatenate([i_pre, log_f], axis=-1)


def mlstm_scan(x, gates, n_ctx_chunks):
    n_seq = 2 * ML_HEADS
    scratch = [pltpu.VMEM((n_seq, ML_DV, ML_DK), F32), pltpu.VMEM((n_seq, 1, ML_DK), F32),
               pltpu.VMEM((n_seq, 1, 128), F32)]
    return bidirectional_scan(_mlstm_kernel, x, gates, n_ctx_chunks, ML_CHUNK, ML_HEADS * ML_DV, scratch, "mlstm_scan")


def mlstm_output(o, p_o, norm_g):
    b, s, _ = p_o.shape
    y = rmsnorm(o.reshape(b, s, ML_HEADS, ML_DV), norm_g) * jax.nn.sigmoid(p_o.reshape(b, s, ML_HEADS, ML_DV))
    return y.reshape(b, s, ML_HEADS * ML_DV)


def token_mixer(h_lat, h_ctx, lw, with_ctx_out):
    bsz, s, d = h_lat.shape
    c_len = h_ctx.shape[1]
    cuts = np.cumsum(IN_SPLITS)[:-1].tolist()
    d_in = sum(IN_SPLITS)
    p_lat = pmm(h_lat.reshape(bsz * s, d), lw['w_in']).reshape(bsz, s, -1)
    p_ctx = pmm(h_ctx.reshape(bsz * c_len, d), lw['w_in']).reshape(bsz, c_len, -1)
    pl_ = jnp.split(p_lat[..., :d_in], cuts, axis=-1)
    pc = jnp.split(p_ctx[..., :d_in], cuts, axis=-1)
    k_ctx_na, v_ctx_na = to_heads(pc[1], NA_HEADS), to_heads(pc[2], NA_HEADS)
    ya_lat = neighborhood_attention(p_lat, p_ctx, lw['na_rpb'])
    gdn_in = lambda p: gdn_prep(p[3], p[5], p[6], lw['gdn_conv_w'], lw['gdn_a_log'], lw['gdn_dt_bias'])
    cat = lambda a, b_: tuple(jnp.concatenate(t, axis=1) for t in zip(a, b_))
    ob_f, ob_b = gdn_scan(*cat(gdn_in(pc), gdn_in(pl_)), c_len // GDN_CHUNK)
    ob = ob_f + ob_b
    ob_ctx, ob_lat = ob[:, :c_len], ob[:, c_len:]
    cq = mla_expand(pc[7], pc[8], lw['mla_q_norm_g'], lw['mla_w_uq'], lw['mla_kv_norm_g'], lw['mla_w_ukv'], False)
    lq = mla_expand(pl_[7], pl_[8], lw['mla_q_norm_g'], lw['mla_w_uq'], lw['mla_kv_norm_g'], lw['mla_w_ukv'], True)
    kn_all = jnp.concatenate([cq[2], lq[2]], axis=2)
    kp_all = jnp.concatenate([cq[3], lq[3]], axis=1)
    v_all = jnp.concatenate([cq[4], lq[4]], axis=2)
    yc_lat = mla_attention((lq[0], lq[1]), (kn_all, kp_all, v_all))
    ml_in = lambda p: mlstm_prep(p[9], p[10], p[11], p[13], p[14], lw['ml_b_if'])
    od_f, od_b = mlstm_scan(*cat(ml_in(pc), ml_in(pl_)), c_len // ML_CHUNK)
    od = od_f + od_b
    od_ctx, od_lat = od[:, :c_len], od[:, c_len:]
    ys_lat = [ya_lat, gdn_output(ob_lat, pl_[4], lw['gdn_norm_g']), yc_lat,
              mlstm_output(od_lat, pl_[12], lw['ml_norm_g'])]
    if not with_ctx_out:
        return ys_lat, None
    ya_ctx = context_attention(to_heads(pc[0], NA_HEADS), k_ctx_na, v_ctx_na)
    yc_ctx = mla_attention((cq[0], cq[1]), (cq[2], cq[3], cq[4]))
    ys_ctx = [ya_ctx, gdn_output(ob_ctx, pc[4], lw['gdn_norm_g']), yc_ctx,
              mlstm_output(od_ctx, pc[12], lw['ml_norm_g'])]
    return ys_lat, ys_ctx


def kernel(x, c, ctx, c_ctx, w_ada, b_ada, norm1_g, norm2_g, w_in, na_rpb, gdn_conv_w, gdn_a_log, gdn_dt_bias, gdn_norm_g, mla_q_norm_g, mla_w_uq, mla_kv_norm_g, mla_w_ukv, ml_b_if, ml_norm_g, w_mg, b_mg, w_bo, w_out, router_w, router_b, w_gu, b_gu, w_dn, b_dn, final_g):
    depth = w_ada.shape[0]
    bsz, s, d = x.shape
    cond = jnp.concatenate([jax.nn.silu(c), jax.nn.silu(c_ctx)[None]], axis=0)
    cond = jnp.pad(cond, ((0, (-cond.shape[0]) % 8), (0, 0)))
    w_mg_b, w_bo_b, w_out_b = (t.astype(BF16) for t in (w_mg, w_bo, w_out))
    w_in_b = jnp.pad(w_in, ((0, 0), (0, 0), (0, (-w_in.shape[-1]) % 512))).astype(BF16)
    w_uq_b, w_ukv_b = mla_w_uq.astype(BF16), mla_w_ukv.astype(BF16)
    for l in range(depth):
        with_ctx_out = l < depth - 1
        mod = pmm(cond, w_ada[l], tn=1024) + b_ada[l]
        sh1, sc1, g1, sh2, sc2, g2 = jnp.split(mod[:bsz], 6, axis=-1)
        csh1, csc1, cg1, csh2, csc2, cg2 = jnp.split(mod[bsz:bsz + 1], 6, axis=-1)
        lw = dict(w_in=w_in_b[l], na_rpb=na_rpb[l], gdn_conv_w=gdn_conv_w[l], gdn_a_log=gdn_a_log[l],
                  gdn_dt_bias=gdn_dt_bias[l], gdn_norm_g=gdn_norm_g[l], mla_q_norm_g=mla_q_norm_g[l],
                  mla_w_uq=w_uq_b[l], mla_kv_norm_g=mla_kv_norm_g[l], mla_w_ukv=w_ukv_b[l],
                  ml_b_if=ml_b_if[l], ml_norm_g=ml_norm_g[l])
        h_lat = norm_mod(x, norm1_g[l], sc1, sh1)
        h_ctx = norm_mod(ctx, norm1_g[l], csc1, csh1)
        ys_lat, ys_ctx = token_mixer(h_lat, h_ctx, lw, with_ctx_out)
        x = merge_residual(h_lat, [y.astype(BF16) for y in ys_lat], x, g1, w_mg_b[l], b_mg[l], w_bo_b[l], w_out_b[l])
        h2_lat = norm_mod(x, norm2_g[l], sc2, sh2, out_dtype=F32)
        if with_ctx_out:
            ctx = merge_residual(h_ctx, [y.astype(BF16) for y in ys_ctx], ctx, cg1, w_mg_b[l], b_mg[l], w_bo_b[l], w_out_b[l])
            h2_ctx = norm_mod(ctx, norm2_g[l], csc2, csh2, out_dtype=F32)
            n_ctx = h2_ctx.shape[0] * h2_ctx.shape[1]
            tokens = jnp.concatenate([h2_ctx.reshape(-1, d), h2_lat.reshape(-1, d)], axis=0)
            f = moe_ffn(tokens, router_w[l], router_b[l], w_gu[l], b_gu[l], w_dn[l], b_dn[l])
            ctx = ctx + cg2 * f[:n_ctx].reshape(ctx.shape)
            f_lat = f[n_ctx:].reshape(x.shape)
        else:
            f_lat = moe_ffn(h2_lat.reshape(-1, d), router_w[l], router_b[l], w_gu[l], b_gu[l],
                            w_dn[l], b_dn[l]).reshape(x.shape)
        x = x + g2[:, None, :] * f_lat
    return rmsnorm(x, final_g)
```

```python
import functools
import math

import numpy as np
import jax
import jax.numpy as jnp
from jax import lax
from jax.experimental import pallas as pl
from jax.experimental.pallas import tpu as pltpu

F32 = jnp.float32
BF16 = jnp.bfloat16

GRID_W = 64
EPS = 1e-6
ROPE_BASE = 10000.0
NA_HEADS, NA_DH, NA_WIN_H, NA_WIN_W = 8, 64, 8, 16
GDN_HEADS, GDN_DK, GDN_DV, GDN_CHUNK = 4, 128, 128, 64
MLA_HEADS, MLA_NOPE, MLA_ROPE, MLA_V, MLA_Q_RANK, MLA_KV_RANK = 8, 64, 32, 64, 256, 128
ML_HEADS, ML_DK, ML_DV, ML_CHUNK = 4, 64, 128, 64
N_BRANCH, BRANCH_W = 4, 512
N_EXPERTS, TOP_K, D_FF = 32, 4, 1024
SWIGLU_ALPHA, SWIGLU_LIMIT = 1.702, 7.0
MOE_ROWS = 256
LANES = 128

IN_SPLITS = (
    NA_HEADS * NA_DH, NA_HEADS * NA_DH, NA_HEADS * NA_DH,
    GDN_HEADS * (2 * GDN_DK + GDN_DV), GDN_HEADS * GDN_DV,
    2 * GDN_HEADS, 2 * GDN_HEADS,
    MLA_Q_RANK, MLA_KV_RANK + MLA_ROPE,
    ML_HEADS * ML_DK, ML_HEADS * ML_DK, ML_HEADS * ML_DV, ML_HEADS * ML_DV,
    2 * ML_HEADS, 2 * ML_HEADS,
)
COL_NA = 0
COL_GDN = 1536
COL_GDN_GATE = 3072
COL_MLA = 3584
COL_ML = 4096
COL_ML_GATE = 5120
COL_SMALL = 5632
P_WIDTH = 5760
MLA_BLOCK = COL_ML - COL_MLA
SH1, SC1, G1, SH2, SC2, G2 = range(6)

VMEM_LIMIT_BYTES = 56 * 1024 * 1024


def _params(*sem):
    return pltpu.CompilerParams(dimension_semantics=sem, vmem_limit_bytes=VMEM_LIMIT_BYTES)


def _full(a):
    return pl.BlockSpec(a.shape, lambda *_: (0,) * a.ndim)


def _dot(a, b, precise=False):
    if precise:
        return jnp.dot(a, b, preferred_element_type=F32, precision=lax.Precision.HIGHEST)
    return jnp.dot(a.astype(BF16), b.astype(BF16), preferred_element_type=F32)


def _dot_nt(a, b):
    return lax.dot_general(a.astype(BF16), b.astype(BF16), (((1,), (1,)), ((), ())), preferred_element_type=F32)


def _dot_tn(a, b):
    return lax.dot_general(a.astype(BF16), b.astype(BF16), (((0,), (0,)), ((), ())), preferred_element_type=F32)


def _softplus(x):
    return jnp.maximum(x, 0.0) + jnp.log(1.0 + jnp.exp(-jnp.abs(x)))


def _rms(x):
    return x * lax.rsqrt(jnp.mean(x * x, axis=-1, keepdims=True) + EPS)


def _mm_kernel(x_ref, w_ref, o_ref):
    o_ref[...] = _dot(x_ref[...], w_ref[...]).astype(o_ref.dtype)


def pmm(x, w, out_dtype=F32, tm=512, tn=512):
    m, k = x.shape
    n = w.shape[1]
    tm, tn = math.gcd(tm, m), min(tn, n)
    if n % tn:
        tn = n
    assert m % tm == 0 and n % tn == 0, (m, n, tm, tn)
    return pl.pallas_call(
        _mm_kernel,
        grid=(m // tm, n // tn),
        in_specs=[pl.BlockSpec((tm, k), lambda i, j: (i, 0)),
                  pl.BlockSpec((k, tn), lambda i, j: (0, j))],
        out_specs=pl.BlockSpec((tm, tn), lambda i, j: (i, j)),
        out_shape=jax.ShapeDtypeStruct((m, n), out_dtype),
        compiler_params=_params("parallel", "parallel"),
        name="pmm",
    )(x, w)


def reorder_w_in(w_in):
    o = np.cumsum((0,) + IN_SPLITS)
    piece = lambda i: w_in[..., o[i]:o[i + 1]]
    zeros = lambda n: jnp.zeros(w_in.shape[:-1] + (n,), w_in.dtype)
    cols = [piece(0), piece(1), piece(2), piece(3), piece(4),
            piece(7), piece(8), zeros(MLA_BLOCK - IN_SPLITS[7] - IN_SPLITS[8]),
            piece(9), piece(10), piece(11), piece(12),
            piece(5), piece(6), piece(13), piece(14), zeros(P_WIDTH - COL_SMALL - 8 * GDN_HEADS)]
    out = jnp.concatenate(cols, axis=-1)
    assert out.shape[-1] == P_WIDTH
    return out


def _mod_row(n_lat_blocks, ctx_row):
    return lambda i, j: jnp.where(j >= n_lat_blocks, ctx_row, i)


def _norm_mod_kernel(x_ref, g_ref, sc_ref, sh_ref, o_ref):
    y = _rms(x_ref[0]) * g_ref[...]
    o_ref[0] = (y * (1.0 + sc_ref[0, 0]) + sh_ref[0, 0]).astype(o_ref.dtype)


def norm_mod(x, g, mods, i_sc, i_sh, n_lat, out_dtype=BF16, ts=256):
    b, t, d = x.shape
    ts = min(ts, t - n_lat) if t > n_lat else ts
    assert t % ts == 0 and n_lat % ts == 0
    row = _mod_row(n_lat // ts, b)
    return pl.pallas_call(
        _norm_mod_kernel,
        grid=(b, t // ts),
        in_specs=[pl.BlockSpec((1, ts, d), lambda i, j: (i, j, 0)),
                  pl.BlockSpec((1, d), lambda i, j: (0, 0)),
                  pl.BlockSpec((1, 1, 1, d), lambda i, j: (row(i, j), i_sc, 0, 0)),
                  pl.BlockSpec((1, 1, 1, d), lambda i, j: (row(i, j), i_sh, 0, 0))],
        out_specs=pl.BlockSpec((1, ts, d), lambda i, j: (i, j, 0)),
        out_shape=jax.ShapeDtypeStruct((b, t, d), out_dtype),
        compiler_params=_params("parallel", "parallel"),
        name="norm_mod",
    )(x, g.reshape(1, d), mods, mods)


def _norm_router_kernel(x_ref, g_ref, sc_ref, sh_ref, rw_ref, rb_ref, h_ref, lg_ref):
    h = (_rms(x_ref[0]) * g_ref[...]) * (1.0 + sc_ref[0, 0]) + sh_ref[0, 0]
    h_ref[0] = h.astype(h_ref.dtype)
    lg_ref[0] = _dot(h, rw_ref[...], precise=True) + rb_ref[...]


def norm_router(x, g, mods, n_lat, router_w, router_b, ts=256):
    b, t, d = x.shape
    ts = min(ts, t - n_lat) if t > n_lat else ts
    assert t % ts == 0 and n_lat % ts == 0
    row = _mod_row(n_lat // ts, b)
    e = router_w.shape[1]
    rb = router_b.reshape(1, e)
    return pl.pallas_call(
        _norm_router_kernel,
        grid=(b, t // ts),
        in_specs=[pl.BlockSpec((1, ts, d), lambda i, j: (i, j, 0)),
                  pl.BlockSpec((1, d), lambda i, j: (0, 0)),
                  pl.BlockSpec((1, 1, 1, d), lambda i, j: (row(i, j), SC2, 0, 0)),
                  pl.BlockSpec((1, 1, 1, d), lambda i, j: (row(i, j), SH2, 0, 0)),
                  _full(router_w), _full(rb)],
        out_specs=[pl.BlockSpec((1, ts, d), lambda i, j: (i, j, 0)),
                   pl.BlockSpec((1, ts, e), lambda i, j: (i, j, 0))],
        out_shape=[jax.ShapeDtypeStruct((b, t, d), BF16), jax.ShapeDtypeStruct((b, t, e), F32)],
        compiler_params=_params("parallel", "parallel"),
        name="norm_router",
    )(x, g.reshape(1, d), mods, mods, router_w, rb)


def _gated_residual_kernel(x_ref, f_ref, g_ref, o_ref):
    o_ref[0] = x_ref[0] + g_ref[0, 0] * f_ref[0]


def gated_residual(x, f, mods, i_gate, n_lat, ts=256):
    b, t, d = x.shape
    ts = min(ts, t - n_lat) if t > n_lat else ts
    row = _mod_row(n_lat // ts, b)
    blk = pl.BlockSpec((1, ts, d), lambda i, j: (i, j, 0))
    return pl.pallas_call(
        _gated_residual_kernel,
        grid=(b, t // ts),
        in_specs=[blk, blk, pl.BlockSpec((1, 1, 1, d), lambda i, j: (row(i, j), i_gate, 0, 0))],
        out_specs=blk,
        out_shape=jax.ShapeDtypeStruct((b, t, d), F32),
        compiler_params=_params("parallel", "parallel"),
        name="gated_residual",
    )(x, f, mods)


def _pair_attn_kernel(q_ref, k_ref, v_ref, o_ref, *, tk):
    dq, dv = q_ref.shape[-1] // 2, v_ref.shape[-1] // 2
    tq = q_ref.shape[1]
    outs = []
    for hh in range(2):
        q = q_ref[0][:, hh * dq:(hh + 1) * dq]
        m = jnp.full((tq, 1), -jnp.inf, F32)
        l = jnp.zeros((tq, 1), F32)
        acc = jnp.zeros((tq, dv), F32)
        for c in range(k_ref.shape[1] // tk):
            k = k_ref[0, c * tk:(c + 1) * tk, hh * dq:(hh + 1) * dq]
            v = v_ref[0, c * tk:(c + 1) * tk, hh * dv:(hh + 1) * dv]
            s = lax.dot_general(q, k, (((1,), (1,)), ((), ())), preferred_element_type=F32)
            m_new = jnp.maximum(m, jnp.max(s, axis=-1, keepdims=True))
            alpha = jnp.exp(m - m_new)
            p = jnp.exp(s - m_new)
            l = alpha * l + jnp.sum(p, axis=-1, keepdims=True)
            acc = alpha * acc + jnp.dot(p.astype(BF16), v, preferred_element_type=F32)
            m = m_new
        outs.append(acc / l)
    o_ref[0] = jnp.concatenate(outs, axis=-1).astype(o_ref.dtype)


def pair_attention(q, k, v, n_q, n_heads, tq=512):
    b, t, hq = k.shape
    hv = v.shape[-1]
    dq2, dv2 = 2 * hq // n_heads, 2 * hv // n_heads
    tq = min(tq, n_q)
    tk = next(c for c in (768, 512, 384, 256, 128) if t % c == 0)
    assert n_q % tq == 0 and dq2 % LANES == 0 and dv2 % LANES == 0
    return pl.pallas_call(
        functools.partial(_pair_attn_kernel, tk=tk),
        grid=(b, n_heads // 2, n_q // tq),
        in_specs=[pl.BlockSpec((1, tq, dq2), lambda i, hp, n: (i, n, hp)),
                  pl.BlockSpec((1, t, dq2), lambda i, hp, n: (i, 0, hp)),
                  pl.BlockSpec((1, t, dv2), lambda i, hp, n: (i, 0, hp))],
        out_specs=pl.BlockSpec((1, tq, dv2), lambda i, hp, n: (i, n, hp)),
        out_shape=jax.ShapeDtypeStruct((b, n_q, hv), BF16),
        compiler_params=_params("parallel", "parallel", "parallel"),
        name="pair_attention",
    )(q, k, v)


NA_QROWS = 8
NA_KROWS = 4
NA_NEG = -1e30


def na_bias_tables(rpb, rows):
    n_steps = rows // NA_QROWS
    wh, ww = min(NA_WIN_H, rows), NA_WIN_W
    n_kblocks = rows // NA_KROWS
    n_heads = rpb.shape[0]
    width = 2 * GRID_W - 1
    lo = GRID_W - NA_WIN_W
    padded = jnp.full(rpb.shape[:2] + (width,), NA_NEG, F32).at[..., lo:lo + 2 * NA_WIN_W - 1].set(rpb)
    tiles = jnp.stack([padded[..., GRID_W - 1 - qc:width - qc] for qc in range(GRID_W)], axis=2)
    qc, kc = np.arange(GRID_W)[:, None], np.arange(GRID_W)[None, :]
    c0 = np.clip(qc - ww // 2, 0, GRID_W - ww)
    tiles = jnp.where((kc >= c0) & (kc < c0 + ww), tiles, NA_NEG)
    masked = jnp.full((n_heads, GRID_W, GRID_W), NA_NEG, F32)
    tables = []
    for i in (0, min(1, n_steps - 1), n_steps - 1):
        q_tiles = []
        for il in range(NA_QROWS):
            q = NA_QROWS * i + il
            r0 = min(max(q - wh // 2, 0), rows - wh)
            k_tiles = []
            for ka in range(4 * NA_KROWS):
                blk = 2 * i - 1 + ka // NA_KROWS
                kr = NA_KROWS * blk + ka % NA_KROWS
                ok = 0 <= blk < n_kblocks and r0 <= kr < r0 + wh
                k_tiles.append(tiles[:, kr - q + NA_WIN_H - 1] if ok else masked)
            q_tiles.append(jnp.concatenate(k_tiles, axis=-1))
        tables.append(jnp.concatenate(q_tiles, axis=-2))
    return jnp.stack(tables)


def _na_kernel(q_ref, k0_ref, k1_ref, k2_ref, k3_ref, v0_ref, v1_ref, v2_ref, v3_ref,
               kc_ref, vc_ref, bias_ref, o_ref):
    kw = NA_KROWS * GRID_W
    outs = []
    for hh in range(2):
        sl = slice(hh * NA_DH, (hh + 1) * NA_DH)
        q = (q_ref[0][:, sl] * (NA_DH ** -0.5)).astype(BF16)
        ks = [r[0][:, sl].astype(BF16) for r in (k0_ref, k1_ref, k2_ref, k3_ref, kc_ref)]
        vs = [r[0][:, sl].astype(BF16) for r in (v0_ref, v1_ref, v2_ref, v3_ref, vc_ref)]
        ss = [lax.dot_general(q, kk, (((1,), (1,)), ((), ())), preferred_element_type=F32) for kk in ks]
        ss = [ss[j] + bias_ref[0, 0, hh, :, j * kw:(j + 1) * kw] for j in range(4)] + [ss[4]]
        m = functools.reduce(jnp.maximum, [jnp.max(t, axis=-1, keepdims=True) for t in ss])
        ps = [jnp.exp(t - m) for t in ss]
        l = functools.reduce(jnp.add, [jnp.sum(t, axis=-1, keepdims=True) for t in ps])
        o = functools.reduce(jnp.add, [jnp.dot(p.astype(BF16), vv, preferred_element_type=F32)
                                       for p, vv in zip(ps, vs)])
        outs.append(o / l)
    o_ref[0] = jnp.concatenate(outs, axis=-1).astype(o_ref.dtype)


def neighborhood_attention(p3, rpb, n_lat):
    b, t, _ = p3.shape
    c_len = t - n_lat
    rows = n_lat // GRID_W
    tq, tkv = NA_QROWS * GRID_W, NA_KROWS * GRID_W
    assert rows % NA_QROWS == 0 and rows >= 2 * NA_QROWS and n_lat % c_len == 0
    n_steps = rows // NA_QROWS
    n_kblocks = rows // NA_KROWS
    hw = 2 * NA_DH
    n_pairs = NA_HEADS // 2
    bias = na_bias_tables(rpb, rows)
    bias = bias.reshape(3, n_pairs, 2, *bias.shape[2:])

    def view(j, col0):
        return pl.BlockSpec((1, tkv, hw),
                            lambda hp, i, n: (i, jnp.clip(2 * n - 1 + j, 0, n_kblocks - 1), col0 + hp))

    ctx_spec = lambda col0: pl.BlockSpec((1, c_len, hw), lambda hp, i, n: (i, n_lat // c_len, col0 + hp))
    bias_spec = pl.BlockSpec(
        (1, 1, 2, tq, 4 * tkv),
        lambda hp, i, n: (jnp.where(n == 0, 0, jnp.where(n == n_steps - 1, 2, 1)), hp, 0, 0, 0))
    return pl.pallas_call(
        _na_kernel,
        grid=(n_pairs, b, n_steps),
        in_specs=[pl.BlockSpec((1, tq, hw), lambda hp, i, n: (i, n, hp))]
                 + [view(j, n_pairs) for j in range(4)] + [view(j, 2 * n_pairs) for j in range(4)]
                 + [ctx_spec(n_pairs), ctx_spec(2 * n_pairs), bias_spec],
        out_specs=pl.BlockSpec((1, tq, hw), lambda hp, i, n: (i, n, hp)),
        out_shape=jax.ShapeDtypeStruct((b, n_lat, NA_HEADS * NA_DH), BF16),
        compiler_params=_params("parallel", "parallel", "arbitrary"),
        name="neighborhood_attention",
    )(*([p3] * 11), bias)


def rope_tables(n_lat, n_ctx):
    half = MLA_ROPE // 2
    tpos = np.arange(n_lat)
    inv = ROPE_BASE ** (-jnp.arange(0, half, 2, dtype=F32) / half)
    pos = jnp.stack([tpos // GRID_W, tpos % GRID_W], axis=1).astype(F32)
    ang = pos[:, :, None] * inv
    ang = jnp.concatenate([ang, ang], axis=-1).reshape(n_lat, MLA_ROPE)
    pad = ((0, n_ctx), (MLA_NOPE, LANES - MLA_NOPE - MLA_ROPE))
    return (jnp.pad(jnp.cos(ang), pad, constant_values=1.0), jnp.pad(jnp.sin(ang), pad))


def _rope_swap():
    half = MLA_ROPE // 2
    quarter = half // 2
    r = np.zeros((MLA_ROPE, MLA_ROPE), np.float32)
    for j in range(MLA_ROPE):
        if j % half < quarter:
            r[j + quarter, j] = -1.0
        else:
            r[j - quarter, j] = 1.0
    return r


def mla_weights(w_uq, w_ukv):
    dqh = MLA_NOPE + MLA_ROPE
    lead = w_uq.shape[:-1]
    wq = w_uq.reshape(*lead, MLA_HEADS, dqh)
    padh = lambda a: jnp.pad(a, ((0, 0),) * (a.ndim - 1) + ((0, LANES - a.shape[-1]),))
    wq_p = padh(wq).reshape(*lead, MLA_HEADS * LANES)
    rot = jnp.einsum('...hr,rs->...hs', wq[..., MLA_NOPE:], jnp.asarray(_rope_swap()))
    wqr_p = jnp.pad(rot, ((0, 0),) * (rot.ndim - 1) + ((MLA_NOPE, LANES - dqh),)).reshape(*lead, MLA_HEADS * LANES)
    lead = w_ukv.shape[:-1]
    wkv = w_ukv.reshape(*lead, MLA_HEADS, MLA_NOPE + MLA_V)
    wk_p = padh(wkv[..., :MLA_NOPE]).reshape(*lead, MLA_HEADS * LANES)
    wv_p = wkv[..., MLA_NOPE:].reshape(*lead, MLA_HEADS * MLA_V)
    return wq_p.astype(BF16), wqr_p.astype(BF16), jnp.concatenate([wk_p, wv_p], axis=-1).astype(BF16)


def _mla_prep_kernel(p_ref, cos_ref, sin_ref, qg_ref, kvg_ref, wq_ref, wqr_ref, wkv_ref, e_ref, er_ref,
                     q_out, k_out, v_out):
    blk = p_ref[0]
    nq = _rms(blk[:, :MLA_Q_RANK]) * qg_ref[...]
    nkv = _rms(blk[:, MLA_Q_RANK:MLA_Q_RANK + MLA_KV_RANK]) * kvg_ref[...]
    kpe = blk[:, MLA_Q_RANK + MLA_KV_RANK:MLA_Q_RANK + MLA_KV_RANK + MLA_ROPE]
    cos, sin = cos_ref[...], sin_ref[...]
    qf, qr = _dot(nq, wq_ref[...]), _dot(nq, wqr_ref[...])
    kv = _dot(nkv, wkv_ref[...])
    kpe_tile = _dot(kpe, e_ref[...], precise=True) * cos + _dot(kpe, er_ref[...], precise=True) * sin
    scale = (MLA_NOPE + MLA_ROPE) ** -0.5
    for h in range(MLA_HEADS):
        sl = slice(h * LANES, (h + 1) * LANES)
        q_out[0, :, sl] = ((qf[:, sl] * cos + qr[:, sl] * sin) * scale).astype(q_out.dtype)
        k_out[0, :, sl] = (kv[:, sl] + kpe_tile).astype(k_out.dtype)
    v_out[0] = kv[:, MLA_HEADS * LANES:].astype(v_out.dtype)


def mla_prep(p3, cos, sin, q_norm_g, kv_norm_g, wq, wqr, wkv, tb=256):
    b, t, _ = p3.shape
    tb = next(c for c in (tb, 128) if t % c == 0)
    swap = _rope_swap()
    place = np.zeros((MLA_ROPE, LANES), np.float32)
    place[np.arange(MLA_ROPE), MLA_NOPE + np.arange(MLA_ROPE)] = 1.0
    e, er = jnp.asarray(place), jnp.asarray(swap @ place)
    qg, kvg = q_norm_g.reshape(1, -1), kv_norm_g.reshape(1, -1)
    row = lambda w: pl.BlockSpec((1, tb, w), lambda i, j: (i, j, 0))
    tab = pl.BlockSpec((tb, LANES), lambda i, j: (j, 0))
    return pl.pallas_call(
        _mla_prep_kernel,
        grid=(b, t // tb),
        in_specs=[pl.BlockSpec((1, tb, MLA_BLOCK), lambda i, j: (i, j, COL_MLA // MLA_BLOCK)), tab, tab]
                 + [_full(a) for a in (qg, kvg, wq, wqr, wkv, e, er)],
        out_specs=[row(MLA_HEADS * LANES), row(MLA_HEADS * LANES), row(MLA_HEADS * MLA_V)],
        out_shape=[jax.ShapeDtypeStruct((b, t, MLA_HEADS * LANES), BF16),
                   jax.ShapeDtypeStruct((b, t, MLA_HEADS * LANES), BF16),
                   jax.ShapeDtypeStruct((b, t, MLA_HEADS * MLA_V), BF16)],
        compiler_params=_params("parallel", "parallel"),
        name="mla_prep",
    )(p3, cos, sin, qg, kvg, wq, wqr, wkv, e, er)


def _gdn_prep_kernel(x_ref, prev_ref, next_ref, w_ref, o_ref, *, n_lat, n_tok):
    tb = x_ref.shape[1]
    start = pl.program_id(1) * tb
    x = x_ref[0]
    seg_first = jnp.logical_or(start == 0, start == n_lat)
    seg_last = jnp.logical_or(start + tb == n_lat, start + tb == n_tok)
    prev_row = jnp.where(seg_first, 0.0, prev_ref[0][7:8, :])
    next_row = jnp.where(seg_last, 0.0, next_ref[0][0:1, :])
    rows = lax.broadcasted_iota(jnp.int32, (tb, 1), 0)
    x_m1 = jnp.where(rows == 0, prev_row, pltpu.roll(x, 1, axis=0))
    x_p1 = jnp.where(rows == tb - 1, next_row, pltpu.roll(x, tb - 1, axis=0))
    y = x_m1 * w_ref[0:1, :] + x * w_ref[1:2, :] + x_p1 * w_ref[2:3, :]
    y = y * jax.nn.sigmoid(y)
    for g in range(2 * GDN_HEADS):
        sl = slice(g * GDN_DK, (g + 1) * GDN_DK)
        yh = y[:, sl]
        yh = yh * lax.rsqrt(jnp.sum(yh * yh, axis=-1, keepdims=True) + EPS)
        o_ref[0, :, sl] = yh * (GDN_DK ** -0.5) if g < GDN_HEADS else yh
    o_ref[0, :, 2 * GDN_HEADS * GDN_DK:] = y[:, 2 * GDN_HEADS * GDN_DK:]


def gdn_prep(p3, conv_w, n_lat, tb=256):
    b, t, _ = p3.shape
    w = GDN_HEADS * (2 * GDN_DK + GDN_DV)
    tb = min(tb, t - n_lat)
    assert t % tb == 0 and n_lat % tb == 0 and tb % 8 == 0 and COL_GDN % w == 0
    col = COL_GDN // w
    sub = tb // 8
    return pl.pallas_call(
        functools.partial(_gdn_prep_kernel, n_lat=n_lat, n_tok=t),
        grid=(b, t // tb),
        in_specs=[pl.BlockSpec((1, tb, w), lambda i, j: (i, j, col)),
                  pl.BlockSpec((1, 8, w), lambda i, j: (i, jnp.maximum(j * sub - 1, 0), col)),
                  pl.BlockSpec((1, 8, w), lambda i, j: (i, jnp.minimum((j + 1) * sub, t // 8 - 1), col)),
                  _full(conv_w)],
        out_specs=pl.BlockSpec((1, tb, w), lambda i, j: (i, j, 0)),
        out_shape=jax.ShapeDtypeStruct((b, t, w), F32),
        compiler_params=_params("parallel", "parallel"),
        name="gdn_prep",
    )(p3, p3, p3, conv_w)


def _order_masks(length, reverse):
    r = lax.broadcasted_iota(jnp.int32, (length, length), 0)
    c = lax.broadcasted_iota(jnp.int32, (length, length), 1)
    if reverse:
        return r <= c, r < c, r >= c
    return r >= c, r > c, r <= c


def _as_row(col, mask):
    return jnp.sum(jnp.where(mask, col, 0.0), axis=0, keepdims=True)


def _gdn_kernel(xf_ref, sf_ref, xb_ref, sb_ref, alog_ref, dtb_ref, of_ref, ob_ref, s_ref):
    @pl.when(pl.program_id(1) == 0)
    def _():
        s_ref[...] = jnp.zeros_like(s_ref)

    nh, dk, dv, length = GDN_HEADS, GDN_DK, GDN_DV, GDN_CHUNK
    r = lax.broadcasted_iota(jnp.int32, (length, length), 0)
    c = lax.broadcasted_iota(jnp.int32, (length, length), 1)
    eye = (r == c).astype(F32)
    seqs = []
    for d, (x_ref, sm_ref, o_ref) in enumerate(((xf_ref, sf_ref, of_ref), (xb_ref, sb_ref, ob_ref))):
        incl, strict, incl_t = _order_masks(length, d == 1)
        small = sm_ref[0]
        hs = slice(d * nh, (d + 1) * nh)
        la_col = -jnp.exp(alog_ref[:, hs]) * _softplus(small[:, hs] + dtb_ref[:, hs])
        be_col = jax.nn.sigmoid(small[:, 2 * nh + d * nh:2 * nh + (d + 1) * nh])
        g_cols = _dot(incl.astype(F32), la_col, precise=True)
        g_ends = jnp.sum(la_col, axis=0, keepdims=True)
        for h in range(nh):
            q = x_ref[0, :, h * dk:(h + 1) * dk]
            k = x_ref[0, :, (nh + h) * dk:(nh + h + 1) * dk]
            v = x_ref[0, :, 2 * nh * dk + h * dv:2 * nh * dk + (h + 1) * dv]
            gc, ge, bc = g_cols[:, h:h + 1], g_ends[:, h:h + 1], be_col[:, h:h + 1]
            gr = _as_row(la_col[:, h:h + 1], incl_t)
            decay = jnp.exp(jnp.where(incl, gc - gr, -jnp.inf))
            qk = decay * _dot_nt(q, k)
            p = -(bc * jnp.where(strict, decay, 0.0) * _dot_nt(k, k))
            eg = jnp.exp(gc)
            seqs.append(dict(o_ref=o_ref, h=h, idx=d * nh + h, q_dec=eg * q, qk=qk, p=p, t=eye + p,
                             rhs=jnp.concatenate([bc * v, (bc * eg) * k], axis=-1),
                             k_end=jnp.exp(ge - gc) * k, e_end=jnp.exp(ge)))
    for _ in range(int(math.log2(length)) - 1):
        for s in seqs:
            s['p'] = _dot(s['p'], s['p'])
        for s in seqs:
            s['t'] = s['t'] + _dot(s['t'], s['p'])
    for s in seqs:
        s['sol'] = _dot(s['t'], s['rhs'])
    for s in seqs:
        s['s_old'] = s_ref[s['idx']]
        s['u'] = s['sol'][:, :dv] - _dot_nt(s['sol'][:, dv:], s['s_old'])
    for s in seqs:
        h = s['h']
        s['o_ref'][0, :, h * dv:(h + 1) * dv] = _dot_nt(s['q_dec'], s['s_old']) + _dot(s['qk'], s['u'])
        s_ref[s['idx']] = s['e_end'] * s['s_old'] + _dot_tn(s['u'], s['k_end'])


def _mlstm_kernel(xf_ref, sf_ref, xb_ref, sb_ref, bi_ref, bf_ref, of_ref, ob_ref, c_ref, n_ref, m_ref):
    @pl.when(pl.program_id(1) == 0)
    def _():
        c_ref[...] = jnp.zeros_like(c_ref)
        n_ref[...] = jnp.zeros_like(n_ref)
        m_ref[...] = jnp.zeros_like(m_ref)

    nh, dk, dv, length = ML_HEADS, ML_DK, ML_DV, ML_CHUNK
    r = lax.broadcasted_iota(jnp.int32, (length, length), 0)
    c = lax.broadcasted_iota(jnp.int32, (length, length), 1)
    eye = r == c
    col_i, col_f = 4 * GDN_HEADS, 4 * GDN_HEADS + 2 * nh
    seqs = []
    for d, (x_ref, sm_ref, o_ref) in enumerate(((xf_ref, sf_ref, of_ref), (xb_ref, sb_ref, ob_ref))):
        incl, _, incl_t = _order_masks(length, d == 1)
        small = sm_ref[0]
        hs = slice(d * nh, (d + 1) * nh)
        i_col = small[:, col_i + d * nh:col_i + (d + 1) * nh] + bi_ref[:, hs]
        f_col = -_softplus(-(small[:, col_f + d * nh:col_f + (d + 1) * nh] + bf_ref[:, hs]))
        fc_cols = _dot(incl.astype(F32), f_col, precise=True)
        f_ends = jnp.sum(f_col, axis=0, keepdims=True)
        for h in range(nh):
            q = x_ref[0, :, h * dk:(h + 1) * dk] * (dk ** -0.5)
            k = x_ref[0, :, (nh + h) * dk:(nh + h + 1) * dk]
            v = x_ref[0, :, 2 * nh * dk + h * dv:2 * nh * dk + (h + 1) * dv]
            fc, fe, ic = fc_cols[:, h:h + 1], f_ends[:, h:h + 1], i_col[:, h:h + 1]
            fr, ir = _as_row(f_col[:, h:h + 1], incl_t), _as_row(ic, eye)
            idx = d * nh + h
            c_old, n_old, m_old = c_ref[idx], n_ref[idx], m_ref[idx][:, 0:1]
            d_log = jnp.where(incl, fc - fr + ir, -jnp.inf)
            w_log = fe - fc + ic
            m_q = jnp.maximum(fc + m_old, jnp.max(d_log, axis=-1, keepdims=True))
            m_new = jnp.maximum(fe + m_old, jnp.max(w_log, axis=0, keepdims=True))
            seqs.append(dict(o_ref=o_ref, h=h, idx=idx, q=q, k=k, v=v, c_old=c_old, n_old=n_old, m_q=m_q,
                             m_new=m_new, inter=jnp.exp(fc + m_old - m_q), decay=jnp.exp(d_log - m_q),
                             dec=jnp.exp(fe + m_old - m_new), wk=jnp.exp(w_log - m_new)))
    for s in seqs:
        s['intra'] = s['decay'] * _dot_nt(s['q'], s['k'])
    for s in seqs:
        s['num'] = s['inter'] * _dot_nt(s['q'], s['c_old']) + _dot(s['intra'], s['v'])
    for s in seqs:
        h, idx = s['h'], s['idx']
        den = (s['inter'] * jnp.sum(s['q'] * s['n_old'], axis=-1, keepdims=True)
               + jnp.sum(s['intra'], axis=-1, keepdims=True))
        s['o_ref'][0, :, h * dv:(h + 1) * dv] = s['num'] / jnp.maximum(jnp.abs(den), jnp.exp(-s['m_q']))
        c_ref[idx] = s['dec'] * s['c_old'] + _dot_tn(s['wk'] * s['v'], s['k'])
        n_ref[idx] = s['dec'] * s['n_old'] + jnp.sum(s['wk'] * s['k'], axis=0, keepdims=True)
        m_ref[idx] = jnp.broadcast_to(s['m_new'], m_ref.shape[1:])


def bidirectional_scan(kernel_fn, x, x_col, x_w, p3, params, n_lat, length, out_w, scratch_shapes, name):
    b, t, _ = p3.shape
    nl, nc = n_lat // length, t // length
    ncx = nc - nl
    assert t % length == 0 and n_lat % length == 0
    fwd = lambda n: jnp.where(n < ncx, nl + n, n - ncx)
    bwd = lambda n: jnp.where(n < ncx, nc - 1 - n, nl - 1 - (n - ncx))
    specs = lambda f: [pl.BlockSpec((1, length, x_w), lambda i, n: (i, f(n), x_col)),
                       pl.BlockSpec((1, length, LANES), lambda i, n: (i, f(n), COL_SMALL // LANES))]
    out = lambda f: pl.BlockSpec((1, length, out_w), lambda i, n: (i, f(n), 0))
    return pl.pallas_call(
        kernel_fn,
        grid=(b, nc),
        in_specs=specs(fwd) + specs(bwd) + [_full(a) for a in params],
        out_specs=[out(fwd), out(bwd)],
        out_shape=[jax.ShapeDtypeStruct((b, t, out_w), F32)] * 2,
        scratch_shapes=scratch_shapes,
        compiler_params=_params("parallel", "arbitrary"),
        name=name,
    )(x, p3, x, p3, *params)


def gdn_scan(xg, p3, a_log, dt_bias, n_lat):
    state = pltpu.VMEM((2 * GDN_HEADS, GDN_DV, GDN_DK), F32)
    params = (a_log.reshape(1, -1), dt_bias.reshape(1, -1))
    return bidirectional_scan(_gdn_kernel, xg, 0, xg.shape[-1], p3, params, n_lat, GDN_CHUNK,
                              GDN_HEADS * GDN_DV, [state], "gdn_scan")


def mlstm_scan(p3, b_if, n_lat):
    n_seq = 2 * ML_HEADS
    w = ML_HEADS * (2 * ML_DK + ML_DV)
    scratch = [pltpu.VMEM((n_seq, ML_DV, ML_DK), F32), pltpu.VMEM((n_seq, 1, ML_DK), F32),
               pltpu.VMEM((n_seq, 1, LANES), F32)]
    params = (b_if[:, 0].reshape(1, -1), b_if[:, 1].reshape(1, -1))
    return bidirectional_scan(_mlstm_kernel, p3, COL_ML // w, w, p3, params, n_lat, ML_CHUNK,
                              ML_HEADS * ML_DV, scratch, "mlstm_scan")


def _gated_head_norm(of_ref, ob_ref, gate, g_ref, n_heads):
    o = of_ref[0] + ob_ref[0]
    d = o.shape[-1] // n_heads
    parts = [_rms(o[:, h * d:(h + 1) * d]) * g_ref[...] for h in range(n_heads)]
    return (jnp.concatenate(parts, axis=-1) * gate).astype(BF16)


def _merge_kernel(h_ref, ya_ref, yc_ref, gof_ref, gob_ref, gg_ref, mof_ref, mob_ref, mg_ref, x_ref, g1_ref,
                  gn_ref, mn_ref, wmg_ref, bmg_ref, wbo_ref, wout_ref, o_ref):
    d = o_ref.shape[-1]
    h = h_ref[0]
    gg = gg_ref[0]
    yb = _gated_head_norm(gof_ref, gob_ref, gg * jax.nn.sigmoid(gg), gn_ref, GDN_HEADS)
    yd = _gated_head_norm(mof_ref, mob_ref, jax.nn.sigmoid(mg_ref[0]), mn_ref, ML_HEADS)
    acc = jnp.zeros((h.shape[0], d), F32)
    for n, y in enumerate((ya_ref[0], yb, yc_ref[0], yd)):
        z = jnp.dot(y, wbo_ref[n], preferred_element_type=F32)
        gate = jnp.dot(h, wmg_ref[:, n * d:(n + 1) * d], preferred_element_type=F32) + bmg_ref[:, n * d:(n + 1) * d]
        acc = acc + jax.nn.sigmoid(gate) * z
    m = jnp.dot(acc.astype(BF16), wout_ref[...], preferred_element_type=F32)
    o_ref[0] = x_ref[0] + g1_ref[0, 0] * m


def merge_residual(h, ya, yc, gdn_o, ml_o, p3, x, mods, n_lat, n_rows, gdn_norm_g, ml_norm_g,
                   w_mg, b_mg, w_bo, w_out, ts=256):
    b, _, d = x.shape
    ts = min(ts, n_rows - n_lat) if n_rows > n_lat else ts
    assert n_rows % ts == 0 and n_lat % ts == 0
    mod_row = _mod_row(n_lat // ts, b)
    row = lambda w, col=0: pl.BlockSpec((1, ts, w), lambda i, j: (i, j, col))
    b_mg = b_mg.reshape(1, -1)
    gn, mn = gdn_norm_g.reshape(1, -1), ml_norm_g.reshape(1, -1)
    return pl.pallas_call(
        _merge_kernel,
        grid=(b, n_rows // ts),
        in_specs=[row(d), row(BRANCH_W), row(BRANCH_W),
                  row(BRANCH_W), row(BRANCH_W), row(BRANCH_W, COL_GDN_GATE // BRANCH_W),
                  row(BRANCH_W), row(BRANCH_W), row(BRANCH_W, COL_ML_GATE // BRANCH_W),
                  row(d), pl.BlockSpec((1, 1, 1, d), lambda i, j: (mod_row(i, j), G1, 0, 0))]
                 + [_full(a) for a in (gn, mn, w_mg, b_mg, w_bo, w_out)],
        out_specs=row(d),
        out_shape=jax.ShapeDtypeStruct((b, n_rows, d), F32),
        compiler_params=_params("parallel", "parallel"),
        name="merge_residual",
    )(h, ya, yc, gdn_o[0], gdn_o[1], p3, ml_o[0], ml_o[1], p3, x, mods, gn, mn, w_mg, b_mg, w_bo, w_out)


def _expert_kernel(be_ref, x_ref, wgu_ref, bgu_ref, wdn_ref, bdn_ref, o_ref, wgu_s, wdn_s):
    i = pl.program_id(0)
    fresh = jnp.logical_or(i == 0, be_ref[i] != be_ref[jnp.maximum(i - 1, 0)])

    @pl.when(fresh)
    def _():
        wgu_s[...] = wgu_ref[0].astype(BF16)
        wdn_s[...] = wdn_ref[0].astype(BF16)

    gu = jnp.dot(x_ref[...], wgu_s[...], preferred_element_type=F32) + bgu_ref[0]
    gate = jnp.minimum(gu[:, :D_FF], SWIGLU_LIMIT)
    up = jnp.clip(gu[:, D_FF:], -SWIGLU_LIMIT, SWIGLU_LIMIT)
    act = (up + 1.0) * gate * jax.nn.sigmoid(SWIGLU_ALPHA * gate)
    o_ref[...] = jnp.dot(act.astype(BF16), wdn_s[...], preferred_element_type=F32) + bdn_ref[0]


def expert_blocks(buf, block_e, w_gu, b_gu, w_dn, b_dn):
    rows, d = buf.shape
    n_blocks = rows // MOE_ROWS
    e = w_gu.shape[0]
    grid_spec = pltpu.PrefetchScalarGridSpec(
        num_scalar_prefetch=1,
        grid=(n_blocks,),
        in_specs=[pl.BlockSpec((MOE_ROWS, d), lambda i, be: (i, 0)),
                  pl.BlockSpec((1, d, 2 * D_FF), lambda i, be: (be[i], 0, 0)),
                  pl.BlockSpec((1, 1, 2 * D_FF), lambda i, be: (be[i], 0, 0)),
                  pl.BlockSpec((1, D_FF, d), lambda i, be: (be[i], 0, 0)),
                  pl.BlockSpec((1, 1, d), lambda i, be: (be[i], 0, 0))],
        out_specs=pl.BlockSpec((MOE_ROWS, d), lambda i, be: (i, 0)),
        scratch_shapes=[pltpu.VMEM((d, 2 * D_FF), BF16), pltpu.VMEM((D_FF, d), BF16)],
    )
    return pl.pallas_call(
        _expert_kernel,
        grid_spec=grid_spec,
        out_shape=jax.ShapeDtypeStruct((rows, d), F32),
        compiler_params=_params("arbitrary"),
        name="expert_blocks",
    )(block_e, buf, w_gu, b_gu.reshape(e, 1, -1), w_dn, b_dn.reshape(e, 1, -1))


def moe_ffn(t, logits, w_gu, b_gu, w_dn, b_dn):
    n, d = t.shape
    top_v, top_i = lax.top_k(logits, TOP_K)
    top_w = jax.nn.softmax(top_v, axis=-1)
    nk = n * TOP_K
    e_flat = top_i.reshape(nk)
    onehot = (e_flat[:, None] == jnp.arange(N_EXPERTS)[None, :]).astype(jnp.int32)
    csum = jnp.cumsum(onehot, axis=0)
    rank = jnp.take_along_axis(csum, e_flat[:, None], axis=1)[:, 0] - 1
    counts = csum[-1]
    padded = (counts + MOE_ROWS - 1) // MOE_ROWS * MOE_ROWS
    pad_end = jnp.cumsum(padded)
    dest = (pad_end - padded)[e_flat] + rank
    n_blocks = -(-nk // MOE_ROWS) + N_EXPERTS
    tok = jnp.arange(nk, dtype=jnp.int32) // TOP_K
    src = jnp.full((n_blocks * MOE_ROWS,), n, jnp.int32).at[dest].set(tok)
    buf = jnp.concatenate([t, jnp.zeros((1, d), t.dtype)], axis=0)[src]
    block_e = jnp.minimum(jnp.searchsorted(pad_end, jnp.arange(n_blocks) * MOE_ROWS, side='right'),
                          N_EXPERTS - 1).astype(jnp.int32)
    y = expert_blocks(buf, block_e, w_gu, b_gu, w_dn, b_dn)
    return jnp.sum(y[dest].reshape(n, TOP_K, d) * top_w[..., None], axis=1)


def kernel(x, c, ctx, c_ctx, w_ada, b_ada, norm1_g, norm2_g, w_in, na_rpb, gdn_conv_w, gdn_a_log, gdn_dt_bias, gdn_norm_g, mla_q_norm_g, mla_w_uq, mla_kv_norm_g, mla_w_ukv, ml_b_if, ml_norm_g, w_mg, b_mg, w_bo, w_out, router_w, router_b, w_gu, b_gu, w_dn, b_dn, final_g):
    depth = w_ada.shape[0]
    bsz, n_lat, d = x.shape
    n_ctx = ctx.shape[1]
    cond = jnp.concatenate([jax.nn.silu(c), jax.nn.silu(c_ctx)[None]], axis=0)
    cond = jnp.pad(cond, ((0, (-cond.shape[0]) % 8), (0, 0)))
    w_in_r = reorder_w_in(w_in).astype(BF16)
    w_mg_b, w_bo_b, w_out_b = (t.astype(BF16) for t in (w_mg, w_bo, w_out))
    wq, wqr, wkv = mla_weights(mla_w_uq, mla_w_ukv)
    cos, sin = rope_tables(n_lat, n_ctx)
    xs = jnp.concatenate([x, ctx], axis=1)
    n_tok = n_lat + n_ctx
    for l in range(depth):
        last = l == depth - 1
        n_rows = n_lat if last else n_tok
        mods = (pmm(cond, w_ada[l], tn=1024) + b_ada[l]).reshape(cond.shape[0], 6, 1, d)
        h = norm_mod(xs, norm1_g[l], mods, SC1, SH1, n_lat)
        p3 = pmm(h.reshape(bsz * n_tok, d), w_in_r[l], tn=640).reshape(bsz, n_tok, P_WIDTH)
        ya = neighborhood_attention(p3, na_rpb[l], n_lat)
        gdn_o = gdn_scan(gdn_prep(p3, gdn_conv_w[l], n_lat), p3, gdn_a_log[l], gdn_dt_bias[l], n_lat)
        ml_o = mlstm_scan(p3, ml_b_if[l], n_lat)
        q, k, v = mla_prep(p3, cos, sin, mla_q_norm_g[l], mla_kv_norm_g[l], wq[l], wqr[l], wkv[l])
        yc = pair_attention(q, k, v, n_lat, MLA_HEADS)
        if not last:
            na_ctx = p3[:, n_lat:, :3 * NA_HEADS * NA_DH]
            w = NA_HEADS * NA_DH
            ya_ctx = pair_attention((na_ctx[..., :w] * (NA_DH ** -0.5)).astype(BF16), na_ctx[..., w:2 * w].astype(BF16),
                                    na_ctx[..., 2 * w:].astype(BF16), n_ctx, NA_HEADS)
            yc_ctx = pair_attention(q[:, n_lat:], k[:, n_lat:], v[:, n_lat:], n_ctx, MLA_HEADS)
            ya = jnp.concatenate([ya, ya_ctx], axis=1)
            yc = jnp.concatenate([yc, yc_ctx], axis=1)
        xs = merge_residual(h, ya, yc, gdn_o, ml_o, p3, xs, mods, n_lat, n_rows, gdn_norm_g[l], ml_norm_g[l],
                            w_mg_b[l], b_mg[l], w_bo_b[l], w_out_b[l])
        h2, logits = norm_router(xs, norm2_g[l], mods, n_lat, router_w[l], router_b[l])
        f = moe_ffn(h2.reshape(bsz * n_rows, d), logits.reshape(bsz * n_rows, N_EXPERTS),
                    w_gu[l], b_gu[l], w_dn[l], b_dn[l])
        xs = gated_residual(xs, f.reshape(bsz, n_rows, d), mods, G2, n_lat)
    return norm_mod(xs, final_g, jnp.zeros((cond.shape[0], 6, 1, d), F32), SC1, SH1, n_lat, out_dtype=F32)
```

```python
import functools
import math

import numpy as np
import jax
import jax.numpy as jnp
from jax import lax
from jax.experimental import pallas as pl
from jax.experimental.pallas import tpu as pltpu

F32 = jnp.float32
BF16 = jnp.bfloat16

GRID_W = 64
EPS = 1e-6
ROPE_BASE = 10000.0
NA_HEADS, NA_DH, NA_WIN_H, NA_WIN_W = 8, 64, 8, 16
GDN_HEADS, GDN_DK, GDN_DV, GDN_CHUNK = 4, 128, 128, 64
MLA_HEADS, MLA_NOPE, MLA_ROPE, MLA_V, MLA_Q_RANK, MLA_KV_RANK = 8, 64, 32, 64, 256, 128
ML_HEADS, ML_DK, ML_DV, ML_CHUNK = 4, 64, 128, 64
N_BRANCH, BRANCH_W = 4, 512
N_EXPERTS, TOP_K, D_FF = 32, 4, 1024
SWIGLU_ALPHA, SWIGLU_LIMIT = 1.702, 7.0
MOE_ROWS = 256
LANES = 128

IN_SPLITS = (
    NA_HEADS * NA_DH, NA_HEADS * NA_DH, NA_HEADS * NA_DH,
    GDN_HEADS * (2 * GDN_DK + GDN_DV), GDN_HEADS * GDN_DV,
    2 * GDN_HEADS, 2 * GDN_HEADS,
    MLA_Q_RANK, MLA_KV_RANK + MLA_ROPE,
    ML_HEADS * ML_DK, ML_HEADS * ML_DK, ML_HEADS * ML_DV, ML_HEADS * ML_DV,
    2 * ML_HEADS, 2 * ML_HEADS,
)
COL_NA = 0
COL_GDN = 1536
COL_GDN_GATE = 3072
COL_MLA = 3584
COL_ML = 4096
COL_ML_GATE = 5120
COL_SMALL = 5632
P_WIDTH = 5760
MLA_BLOCK = COL_ML - COL_MLA
SH1, SC1, G1, SH2, SC2, G2 = range(6)

VMEM_LIMIT_BYTES = 56 * 1024 * 1024


def _params(*sem):
    return pltpu.CompilerParams(dimension_semantics=sem, vmem_limit_bytes=VMEM_LIMIT_BYTES)


def _full(a):
    return pl.BlockSpec(a.shape, lambda *_: (0,) * a.ndim)


def _dot(a, b, precise=False):
    if precise:
        return jnp.dot(a, b, preferred_element_type=F32, precision=lax.Precision.HIGHEST)
    return jnp.dot(a.astype(BF16), b.astype(BF16), preferred_element_type=F32)


def _dot_nt(a, b):
    return lax.dot_general(a.astype(BF16), b.astype(BF16), (((1,), (1,)), ((), ())), preferred_element_type=F32)


def _dot_tn(a, b):
    return lax.dot_general(a.astype(BF16), b.astype(BF16), (((0,), (0,)), ((), ())), preferred_element_type=F32)


def _softplus(x):
    return jnp.maximum(x, 0.0) + jnp.log(1.0 + jnp.exp(-jnp.abs(x)))


def _rms(x):
    return x * lax.rsqrt(jnp.mean(x * x, axis=-1, keepdims=True) + EPS)


def _mm_kernel(x_ref, w_ref, o_ref):
    o_ref[...] = _dot(x_ref[...], w_ref[...]).astype(o_ref.dtype)


def pmm(x, w, out_dtype=F32, tm=512, tn=512):
    m, k = x.shape
    n = w.shape[1]
    tm, tn = math.gcd(tm, m), min(tn, n)
    if n % tn:
        tn = n
    assert m % tm == 0 and n % tn == 0, (m, n, tm, tn)
    return pl.pallas_call(
        _mm_kernel,
        grid=(m // tm, n // tn),
        in_specs=[pl.BlockSpec((tm, k), lambda i, j: (i, 0)),
                  pl.BlockSpec((k, tn), lambda i, j: (0, j))],
        out_specs=pl.BlockSpec((tm, tn), lambda i, j: (i, j)),
        out_shape=jax.ShapeDtypeStruct((m, n), out_dtype),
        compiler_params=_params("parallel", "parallel"),
        name="pmm",
    )(x, w)


def reorder_w_in(w_in):
    o = np.cumsum((0,) + IN_SPLITS)
    piece = lambda i: w_in[..., o[i]:o[i + 1]]
    zeros = lambda n: jnp.zeros(w_in.shape[:-1] + (n,), w_in.dtype)
    cols = [piece(0), piece(1), piece(2), piece(3), piece(4),
            piece(7), piece(8), zeros(MLA_BLOCK - IN_SPLITS[7] - IN_SPLITS[8]),
            piece(9), piece(10), piece(11), piece(12),
            piece(5), piece(6), piece(13), piece(14), zeros(P_WIDTH - COL_SMALL - 8 * GDN_HEADS)]
    out = jnp.concatenate(cols, axis=-1)
    assert out.shape[-1] == P_WIDTH
    return out


def _mod_row(n_lat_blocks, ctx_row):
    return lambda i, j: jnp.where(j >= n_lat_blocks, ctx_row, i)


def _norm_mod_kernel(x_ref, g_ref, sc_ref, sh_ref, o_ref):
    y = _rms(x_ref[0]) * g_ref[...]
    o_ref[0] = (y * (1.0 + sc_ref[0, 0]) + sh_ref[0, 0]).astype(o_ref.dtype)


def norm_mod(x, g, mods, i_sc, i_sh, n_lat, out_dtype=BF16, ts=256):
    b, t, d = x.shape
    ts = min(ts, t - n_lat) if t > n_lat else ts
    assert t % ts == 0 and n_lat % ts == 0
    row = _mod_row(n_lat // ts, b)
    return pl.pallas_call(
        _norm_mod_kernel,
        grid=(b, t // ts),
        in_specs=[pl.BlockSpec((1, ts, d), lambda i, j: (i, j, 0)),
                  pl.BlockSpec((1, d), lambda i, j: (0, 0)),
                  pl.BlockSpec((1, 1, 1, d), lambda i, j: (row(i, j), i_sc, 0, 0)),
                  pl.BlockSpec((1, 1, 1, d), lambda i, j: (row(i, j), i_sh, 0, 0))],
        out_specs=pl.BlockSpec((1, ts, d), lambda i, j: (i, j, 0)),
        out_shape=jax.ShapeDtypeStruct((b, t, d), out_dtype),
        compiler_params=_params("parallel", "parallel"),
        name="norm_mod",
    )(x, g.reshape(1, d), mods, mods)


def _norm_router_kernel(x_ref, g_ref, sc_ref, sh_ref, rw_ref, rb_ref, h_ref, lg_ref):
    h = (_rms(x_ref[0]) * g_ref[...]) * (1.0 + sc_ref[0, 0]) + sh_ref[0, 0]
    h_ref[0] = h.astype(h_ref.dtype)
    lg_ref[0] = _dot(h, rw_ref[...], precise=True) + rb_ref[...]


def norm_router(x, g, mods, n_lat, router_w, router_b, ts=256):
    b, t, d = x.shape
    ts = min(ts, t - n_lat) if t > n_lat else ts
    assert t % ts == 0 and n_lat % ts == 0
    row = _mod_row(n_lat // ts, b)
    e = router_w.shape[1]
    rb = router_b.reshape(1, e)
    return pl.pallas_call(
        _norm_router_kernel,
        grid=(b, t // ts),
        in_specs=[pl.BlockSpec((1, ts, d), lambda i, j: (i, j, 0)),
                  pl.BlockSpec((1, d), lambda i, j: (0, 0)),
                  pl.BlockSpec((1, 1, 1, d), lambda i, j: (row(i, j), SC2, 0, 0)),
                  pl.BlockSpec((1, 1, 1, d), lambda i, j: (row(i, j), SH2, 0, 0)),
                  _full(router_w), _full(rb)],
        out_specs=[pl.BlockSpec((1, ts, d), lambda i, j: (i, j, 0)),
                   pl.BlockSpec((1, ts, e), lambda i, j: (i, j, 0))],
        out_shape=[jax.ShapeDtypeStruct((b, t, d), BF16), jax.ShapeDtypeStruct((b, t, e), F32)],
        compiler_params=_params("parallel", "parallel"),
        name="norm_router",
    )(x, g.reshape(1, d), mods, mods, router_w, rb)


def _combine_residual_kernel(x_ref, y_ref, w_ref, g_ref, o_ref):
    w = w_ref[0]
    f = functools.reduce(jnp.add, [w[:, k:k + 1] * y_ref[k, 0].astype(F32) for k in range(y_ref.shape[0])])
    o_ref[0] = x_ref[0] + g_ref[0, 0] * f


def combine_residual(x, y, w, mods, i_gate, n_lat, ts=256):
    b, t, d = x.shape
    n_slots = y.shape[0]
    ts = min(ts, t - n_lat) if t > n_lat else ts
    row = _mod_row(n_lat // ts, b)
    blk = pl.BlockSpec((1, ts, d), lambda i, j: (i, j, 0))
    return pl.pallas_call(
        _combine_residual_kernel,
        grid=(b, t // ts),
        in_specs=[blk, pl.BlockSpec((n_slots, 1, ts, d), lambda i, j: (0, i, j, 0)),
                  pl.BlockSpec((1, ts, n_slots), lambda i, j: (i, j, 0)),
                  pl.BlockSpec((1, 1, 1, d), lambda i, j: (row(i, j), i_gate, 0, 0))],
        out_specs=blk,
        out_shape=jax.ShapeDtypeStruct((b, t, d), F32),
        compiler_params=_params("parallel", "parallel"),
        name="combine_residual",
    )(x, y, w, mods)


def _pair_attn_kernel(q_ref, k_ref, v_ref, o_ref, *, tk):
    dq, dv = q_ref.shape[-1] // 2, v_ref.shape[-1] // 2
    tq = q_ref.shape[1]
    outs = []
    for hh in range(2):
        q = q_ref[0][:, hh * dq:(hh + 1) * dq]
        m = jnp.full((tq, 1), -jnp.inf, F32)
        l = jnp.zeros((tq, 1), F32)
        acc = jnp.zeros((tq, dv), F32)
        for c in range(k_ref.shape[1] // tk):
            k = k_ref[0, c * tk:(c + 1) * tk, hh * dq:(hh + 1) * dq]
            v = v_ref[0, c * tk:(c + 1) * tk, hh * dv:(hh + 1) * dv]
            s = lax.dot_general(q, k, (((1,), (1,)), ((), ())), preferred_element_type=F32)
            m_new = jnp.maximum(m, jnp.max(s, axis=-1, keepdims=True))
            alpha = jnp.exp(m - m_new)
            p = jnp.exp(s - m_new)
            l = alpha * l + jnp.sum(p, axis=-1, keepdims=True)
            acc = alpha * acc + jnp.dot(p.astype(BF16), v, preferred_element_type=F32)
            m = m_new
        outs.append(acc / l)
    o_ref[0] = jnp.concatenate(outs, axis=-1).astype(o_ref.dtype)


def pair_attention(q, k, v, n_q, n_heads, tq=512):
    b, t, hq = k.shape
    hv = v.shape[-1]
    dq2, dv2 = 2 * hq // n_heads, 2 * hv // n_heads
    tq = min(tq, n_q)
    tk = next(c for c in (768, 512, 384, 256, 128) if t % c == 0)
    assert n_q % tq == 0 and dq2 % LANES == 0 and dv2 % LANES == 0
    return pl.pallas_call(
        functools.partial(_pair_attn_kernel, tk=tk),
        grid=(b, n_heads // 2, n_q // tq),
        in_specs=[pl.BlockSpec((1, tq, dq2), lambda i, hp, n: (i, n, hp)),
                  pl.BlockSpec((1, t, dq2), lambda i, hp, n: (i, 0, hp)),
                  pl.BlockSpec((1, t, dv2), lambda i, hp, n: (i, 0, hp))],
        out_specs=pl.BlockSpec((1, tq, dv2), lambda i, hp, n: (i, n, hp)),
        out_shape=jax.ShapeDtypeStruct((b, n_q, hv), BF16),
        compiler_params=_params("parallel", "parallel", "parallel"),
        name="pair_attention",
    )(q, k, v)


NA_QROWS = 8
NA_KROWS = 4
NA_NEG = -1e30


def na_bias_tables(rpb, rows):
    n_steps = rows // NA_QROWS
    wh, ww = min(NA_WIN_H, rows), NA_WIN_W
    n_kblocks = rows // NA_KROWS
    n_heads = rpb.shape[0]
    width = 2 * GRID_W - 1
    lo = GRID_W - NA_WIN_W
    padded = jnp.full(rpb.shape[:2] + (width,), NA_NEG, F32).at[..., lo:lo + 2 * NA_WIN_W - 1].set(rpb)
    tiles = jnp.stack([padded[..., GRID_W - 1 - qc:width - qc] for qc in range(GRID_W)], axis=2)
    qc, kc = np.arange(GRID_W)[:, None], np.arange(GRID_W)[None, :]
    c0 = np.clip(qc - ww // 2, 0, GRID_W - ww)
    tiles = jnp.where((kc >= c0) & (kc < c0 + ww), tiles, NA_NEG)
    masked = jnp.full((n_heads, GRID_W, GRID_W), NA_NEG, F32)
    tables = []
    for i in (0, min(1, n_steps - 1), n_steps - 1):
        q_tiles = []
        for il in range(NA_QROWS):
            q = NA_QROWS * i + il
            r0 = min(max(q - wh // 2, 0), rows - wh)
            k_tiles = []
            for ka in range(4 * NA_KROWS):
                blk = 2 * i - 1 + ka // NA_KROWS
                kr = NA_KROWS * blk + ka % NA_KROWS
                ok = 0 <= blk < n_kblocks and r0 <= kr < r0 + wh
                k_tiles.append(tiles[:, kr - q + NA_WIN_H - 1] if ok else masked)
            q_tiles.append(jnp.concatenate(k_tiles, axis=-1))
        tables.append(jnp.concatenate(q_tiles, axis=-2))
    return jnp.stack(tables)


def _na_kernel(q_ref, k0_ref, k1_ref, k2_ref, k3_ref, v0_ref, v1_ref, v2_ref, v3_ref,
               kc_ref, vc_ref, bias_ref, o_ref):
    kw = NA_KROWS * GRID_W
    outs = []
    for hh in range(2):
        sl = slice(hh * NA_DH, (hh + 1) * NA_DH)
        q = (q_ref[0][:, sl] * (NA_DH ** -0.5)).astype(BF16)
        ks = [r[0][:, sl].astype(BF16) for r in (k0_ref, k1_ref, k2_ref, k3_ref, kc_ref)]
        vs = [r[0][:, sl].astype(BF16) for r in (v0_ref, v1_ref, v2_ref, v3_ref, vc_ref)]
        ss = [lax.dot_general(q, kk, (((1,), (1,)), ((), ())), preferred_element_type=F32) for kk in ks]
        ss = [ss[j] + bias_ref[0, 0, hh, :, j * kw:(j + 1) * kw] for j in range(4)] + [ss[4]]
        m = functools.reduce(jnp.maximum, [jnp.max(t, axis=-1, keepdims=True) for t in ss])
        ps = [jnp.exp(t - m) for t in ss]
        l = functools.reduce(jnp.add, [jnp.sum(t, axis=-1, keepdims=True) for t in ps])
        o = functools.reduce(jnp.add, [jnp.dot(p.astype(BF16), vv, preferred_element_type=F32)
                                       for p, vv in zip(ps, vs)])
        outs.append(o / l)
    o_ref[0] = jnp.concatenate(outs, axis=-1).astype(o_ref.dtype)


def neighborhood_attention(p3, rpb, n_lat):
    b, t, _ = p3.shape
    c_len = t - n_lat
    rows = n_lat // GRID_W
    tq, tkv = NA_QROWS * GRID_W, NA_KROWS * GRID_W
    assert rows % NA_QROWS == 0 and rows >= 2 * NA_QROWS and n_lat % c_len == 0
    n_steps = rows // NA_QROWS
    n_kblocks = rows // NA_KROWS
    hw = 2 * NA_DH
    n_pairs = NA_HEADS // 2
    bias = na_bias_tables(rpb, rows)
    bias = bias.reshape(3, n_pairs, 2, *bias.shape[2:])

    def view(j, col0):
        return pl.BlockSpec((1, tkv, hw),
                            lambda hp, i, n: (i, jnp.clip(2 * n - 1 + j, 0, n_kblocks - 1), col0 + hp))

    ctx_spec = lambda col0: pl.BlockSpec((1, c_len, hw), lambda hp, i, n: (i, n_lat // c_len, col0 + hp))
    bias_spec = pl.BlockSpec(
        (1, 1, 2, tq, 4 * tkv),
        lambda hp, i, n: (jnp.where(n == 0, 0, jnp.where(n == n_steps - 1, 2, 1)), hp, 0, 0, 0))
    return pl.pallas_call(
        _na_kernel,
        grid=(n_pairs, b, n_steps),
        in_specs=[pl.BlockSpec((1, tq, hw), lambda hp, i, n: (i, n, hp))]
                 + [view(j, n_pairs) for j in range(4)] + [view(j, 2 * n_pairs) for j in range(4)]
                 + [ctx_spec(n_pairs), ctx_spec(2 * n_pairs), bias_spec],
        out_specs=pl.BlockSpec((1, tq, hw), lambda hp, i, n: (i, n, hp)),
        out_shape=jax.ShapeDtypeStruct((b, n_lat, NA_HEADS * NA_DH), BF16),
        compiler_params=_params("parallel", "parallel", "arbitrary"),
        name="neighborhood_attention",
    )(*([p3] * 11), bias)


def rope_tables(n_lat, n_ctx):
    half = MLA_ROPE // 2
    tpos = np.arange(n_lat)
    inv = ROPE_BASE ** (-jnp.arange(0, half, 2, dtype=F32) / half)
    pos = jnp.stack([tpos // GRID_W, tpos % GRID_W], axis=1).astype(F32)
    ang = pos[:, :, None] * inv
    ang = jnp.concatenate([ang, ang], axis=-1).reshape(n_lat, MLA_ROPE)
    pad = ((0, n_ctx), (MLA_NOPE, LANES - MLA_NOPE - MLA_ROPE))
    return (jnp.pad(jnp.cos(ang), pad, constant_values=1.0), jnp.pad(jnp.sin(ang), pad))


def _rope_swap():
    half = MLA_ROPE // 2
    quarter = half // 2
    r = np.zeros((MLA_ROPE, MLA_ROPE), np.float32)
    for j in range(MLA_ROPE):
        if j % half < quarter:
            r[j + quarter, j] = -1.0
        else:
            r[j - quarter, j] = 1.0
    return r


def mla_weights(w_uq, w_ukv):
    dqh = MLA_NOPE + MLA_ROPE
    lead = w_uq.shape[:-1]
    wq = w_uq.reshape(*lead, MLA_HEADS, dqh)
    padh = lambda a: jnp.pad(a, ((0, 0),) * (a.ndim - 1) + ((0, LANES - a.shape[-1]),))
    wq_p = padh(wq).reshape(*lead, MLA_HEADS * LANES)
    rot = jnp.einsum('...hr,rs->...hs', wq[..., MLA_NOPE:], jnp.asarray(_rope_swap()))
    wqr_p = jnp.pad(rot, ((0, 0),) * (rot.ndim - 1) + ((MLA_NOPE, LANES - dqh),)).reshape(*lead, MLA_HEADS * LANES)
    lead = w_ukv.shape[:-1]
    wkv = w_ukv.reshape(*lead, MLA_HEADS, MLA_NOPE + MLA_V)
    wk_p = padh(wkv[..., :MLA_NOPE]).reshape(*lead, MLA_HEADS * LANES)
    wv_p = wkv[..., MLA_NOPE:].reshape(*lead, MLA_HEADS * MLA_V)
    return wq_p.astype(BF16), wqr_p.astype(BF16), jnp.concatenate([wk_p, wv_p], axis=-1).astype(BF16)


def _mla_prep_kernel(p_ref, cos_ref, sin_ref, qg_ref, kvg_ref, wq_ref, wqr_ref, wkv_ref, e_ref, er_ref,
                     q_out, k_out, v_out):
    blk = p_ref[0]
    nq = _rms(blk[:, :MLA_Q_RANK]) * qg_ref[...]
    nkv = _rms(blk[:, MLA_Q_RANK:MLA_Q_RANK + MLA_KV_RANK]) * kvg_ref[...]
    kpe = blk[:, MLA_Q_RANK + MLA_KV_RANK:MLA_Q_RANK + MLA_KV_RANK + MLA_ROPE]
    cos, sin = cos_ref[...], sin_ref[...]
    qf, qr = _dot(nq, wq_ref[...]), _dot(nq, wqr_ref[...])
    kv = _dot(nkv, wkv_ref[...])
    kpe_tile = _dot(kpe, e_ref[...], precise=True) * cos + _dot(kpe, er_ref[...], precise=True) * sin
    scale = (MLA_NOPE + MLA_ROPE) ** -0.5
    for h in range(MLA_HEADS):
        sl = slice(h * LANES, (h + 1) * LANES)
        q_out[0, :, sl] = ((qf[:, sl] * cos + qr[:, sl] * sin) * scale).astype(q_out.dtype)
        k_out[0, :, sl] = (kv[:, sl] + kpe_tile).astype(k_out.dtype)
    v_out[0] = kv[:, MLA_HEADS * LANES:].astype(v_out.dtype)


def mla_prep(p3, cos, sin, q_norm_g, kv_norm_g, wq, wqr, wkv, tb=256):
    b, t, _ = p3.shape
    tb = next(c for c in (tb, 128) if t % c == 0)
    swap = _rope_swap()
    place = np.zeros((MLA_ROPE, LANES), np.float32)
    place[np.arange(MLA_ROPE), MLA_NOPE + np.arange(MLA_ROPE)] = 1.0
    e, er = jnp.asarray(place), jnp.asarray(swap @ place)
    qg, kvg = q_norm_g.reshape(1, -1), kv_norm_g.reshape(1, -1)
    row = lambda w: pl.BlockSpec((1, tb, w), lambda i, j: (i, j, 0))
    tab = pl.BlockSpec((tb, LANES), lambda i, j: (j, 0))
    return pl.pallas_call(
        _mla_prep_kernel,
        grid=(b, t // tb),
        in_specs=[pl.BlockSpec((1, tb, MLA_BLOCK), lambda i, j: (i, j, COL_MLA // MLA_BLOCK)), tab, tab]
                 + [_full(a) for a in (qg, kvg, wq, wqr, wkv, e, er)],
        out_specs=[row(MLA_HEADS * LANES), row(MLA_HEADS * LANES), row(MLA_HEADS * MLA_V)],
        out_shape=[jax.ShapeDtypeStruct((b, t, MLA_HEADS * LANES), BF16),
                   jax.ShapeDtypeStruct((b, t, MLA_HEADS * LANES), BF16),
                   jax.ShapeDtypeStruct((b, t, MLA_HEADS * MLA_V), BF16)],
        compiler_params=_params("parallel", "parallel"),
        name="mla_prep",
    )(p3, cos, sin, qg, kvg, wq, wqr, wkv, e, er)


def _gdn_prep_kernel(x_ref, prev_ref, next_ref, w_ref, o_ref, *, n_lat, n_tok):
    tb = x_ref.shape[1]
    start = pl.program_id(1) * tb
    x = x_ref[0]
    seg_first = jnp.logical_or(start == 0, start == n_lat)
    seg_last = jnp.logical_or(start + tb == n_lat, start + tb == n_tok)
    prev_row = jnp.where(seg_first, 0.0, prev_ref[0][7:8, :])
    next_row = jnp.where(seg_last, 0.0, next_ref[0][0:1, :])
    rows = lax.broadcasted_iota(jnp.int32, (tb, 1), 0)
    x_m1 = jnp.where(rows == 0, prev_row, pltpu.roll(x, 1, axis=0))
    x_p1 = jnp.where(rows == tb - 1, next_row, pltpu.roll(x, tb - 1, axis=0))
    y = x_m1 * w_ref[0:1, :] + x * w_ref[1:2, :] + x_p1 * w_ref[2:3, :]
    y = y * jax.nn.sigmoid(y)
    for g in range(2 * GDN_HEADS):
        sl = slice(g * GDN_DK, (g + 1) * GDN_DK)
        yh = y[:, sl]
        yh = yh * lax.rsqrt(jnp.sum(yh * yh, axis=-1, keepdims=True) + EPS)
        o_ref[0, :, sl] = yh * (GDN_DK ** -0.5) if g < GDN_HEADS else yh
    o_ref[0, :, 2 * GDN_HEADS * GDN_DK:] = y[:, 2 * GDN_HEADS * GDN_DK:]


def gdn_prep(p3, conv_w, n_lat, tb=256):
    b, t, _ = p3.shape
    w = GDN_HEADS * (2 * GDN_DK + GDN_DV)
    tb = min(tb, t - n_lat)
    assert t % tb == 0 and n_lat % tb == 0 and tb % 8 == 0 and COL_GDN % w == 0
    col = COL_GDN // w
    sub = tb // 8
    return pl.pallas_call(
        functools.partial(_gdn_prep_kernel, n_lat=n_lat, n_tok=t),
        grid=(b, t // tb),
        in_specs=[pl.BlockSpec((1, tb, w), lambda i, j: (i, j, col)),
                  pl.BlockSpec((1, 8, w), lambda i, j: (i, jnp.maximum(j * sub - 1, 0), col)),
                  pl.BlockSpec((1, 8, w), lambda i, j: (i, jnp.minimum((j + 1) * sub, t // 8 - 1), col)),
                  _full(conv_w)],
        out_specs=pl.BlockSpec((1, tb, w), lambda i, j: (i, j, 0)),
        out_shape=jax.ShapeDtypeStruct((b, t, w), F32),
        compiler_params=_params("parallel", "parallel"),
        name="gdn_prep",
    )(p3, p3, p3, conv_w)


def _order_masks(length, reverse):
    r = lax.broadcasted_iota(jnp.int32, (length, length), 0)
    c = lax.broadcasted_iota(jnp.int32, (length, length), 1)
    if reverse:
        return r <= c, r < c, r >= c
    return r >= c, r > c, r <= c


def _as_row(col, mask):
    return jnp.sum(jnp.where(mask, col, 0.0), axis=0, keepdims=True)


def _gdn_kernel(xf_ref, sf_ref, xb_ref, sb_ref, alog_ref, dtb_ref, of_ref, ob_ref, s_ref):
    @pl.when(pl.program_id(1) == 0)
    def _():
        s_ref[...] = jnp.zeros_like(s_ref)

    nh, dk, dv, length = GDN_HEADS, GDN_DK, GDN_DV, GDN_CHUNK
    r = lax.broadcasted_iota(jnp.int32, (length, length), 0)
    c = lax.broadcasted_iota(jnp.int32, (length, length), 1)
    eye = (r == c).astype(F32)
    seqs = []
    for d, (x_ref, sm_ref, o_ref) in enumerate(((xf_ref, sf_ref, of_ref), (xb_ref, sb_ref, ob_ref))):
        incl, strict, incl_t = _order_masks(length, d == 1)
        small = sm_ref[0]
        hs = slice(d * nh, (d + 1) * nh)
        la_col = -jnp.exp(alog_ref[:, hs]) * _softplus(small[:, hs] + dtb_ref[:, hs])
        be_col = jax.nn.sigmoid(small[:, 2 * nh + d * nh:2 * nh + (d + 1) * nh])
        g_cols = _dot(incl.astype(F32), la_col, precise=True)
        g_ends = jnp.sum(la_col, axis=0, keepdims=True)
        for h in range(nh):
            q = x_ref[0, :, h * dk:(h + 1) * dk]
            k = x_ref[0, :, (nh + h) * dk:(nh + h + 1) * dk]
            v = x_ref[0, :, 2 * nh * dk + h * dv:2 * nh * dk + (h + 1) * dv]
            gc, ge, bc = g_cols[:, h:h + 1], g_ends[:, h:h + 1], be_col[:, h:h + 1]
            gr = _as_row(la_col[:, h:h + 1], incl_t)
            decay = jnp.exp(jnp.where(incl, gc - gr, -jnp.inf))
            qk = decay * _dot_nt(q, k)
            p = -(bc * jnp.where(strict, decay, 0.0) * _dot_nt(k, k))
            eg = jnp.exp(gc)
            seqs.append(dict(o_ref=o_ref, h=h, idx=d * nh + h, q_dec=eg * q, qk=qk, p=p, t=eye + p,
                             rhs=jnp.concatenate([bc * v, (bc * eg) * k], axis=-1),
                             k_end=jnp.exp(ge - gc) * k, e_end=jnp.exp(ge)))
    for _ in range(int(math.log2(length)) - 1):
        for s in seqs:
            s['p'] = _dot(s['p'], s['p'])
        for s in seqs:
            s['t'] = s['t'] + _dot(s['t'], s['p'])
    for s in seqs:
        s['sol'] = _dot(s['t'], s['rhs'])
    for s in seqs:
        s['s_old'] = s_ref[s['idx']]
        s['u'] = s['sol'][:, :dv] - _dot_nt(s['sol'][:, dv:], s['s_old'])
    for s in seqs:
        h = s['h']
        s['o_ref'][0, :, h * dv:(h + 1) * dv] = _dot_nt(s['q_dec'], s['s_old']) + _dot(s['qk'], s['u'])
        s_ref[s['idx']] = s['e_end'] * s['s_old'] + _dot_tn(s['u'], s['k_end'])


def _mlstm_kernel(xf_ref, sf_ref, xb_ref, sb_ref, bi_ref, bf_ref, of_ref, ob_ref, c_ref, n_ref, m_ref):
    @pl.when(pl.program_id(1) == 0)
    def _():
        c_ref[...] = jnp.zeros_like(c_ref)
        n_ref[...] = jnp.zeros_like(n_ref)
        m_ref[...] = jnp.zeros_like(m_ref)

    nh, dk, dv, length = ML_HEADS, ML_DK, ML_DV, ML_CHUNK
    r = lax.broadcasted_iota(jnp.int32, (length, length), 0)
    c = lax.broadcasted_iota(jnp.int32, (length, length), 1)
    eye = r == c
    col_i, col_f = 4 * GDN_HEADS, 4 * GDN_HEADS + 2 * nh
    seqs = []
    for d, (x_ref, sm_ref, o_ref) in enumerate(((xf_ref, sf_ref, of_ref), (xb_ref, sb_ref, ob_ref))):
        incl, _, incl_t = _order_masks(length, d == 1)
        small = sm_ref[0]
        hs = slice(d * nh, (d + 1) * nh)
        i_col = small[:, col_i + d * nh:col_i + (d + 1) * nh] + bi_ref[:, hs]
        f_col = -_softplus(-(small[:, col_f + d * nh:col_f + (d + 1) * nh] + bf_ref[:, hs]))
        fc_cols = _dot(incl.astype(F32), f_col, precise=True)
        f_ends = jnp.sum(f_col, axis=0, keepdims=True)
        for h in range(nh):
            q = x_ref[0, :, h * dk:(h + 1) * dk] * (dk ** -0.5)
            k = x_ref[0, :, (nh + h) * dk:(nh + h + 1) * dk]
            v = x_ref[0, :, 2 * nh * dk + h * dv:2 * nh * dk + (h + 1) * dv]
            fc, fe, ic = fc_cols[:, h:h + 1], f_ends[:, h:h + 1], i_col[:, h:h + 1]
            fr, ir = _as_row(f_col[:, h:h + 1], incl_t), _as_row(ic, eye)
            idx = d * nh + h
            c_old, n_old, m_old = c_ref[idx], n_ref[idx], m_ref[idx][:, 0:1]
            d_log = jnp.where(incl, fc - fr + ir, -jnp.inf)
            w_log = fe - fc + ic
            m_q = jnp.maximum(fc + m_old, jnp.max(d_log, axis=-1, keepdims=True))
            m_new = jnp.maximum(fe + m_old, jnp.max(w_log, axis=0, keepdims=True))
            seqs.append(dict(o_ref=o_ref, h=h, idx=idx, q=q, k=k, v=v, c_old=c_old, n_old=n_old, m_q=m_q,
                             m_new=m_new, inter=jnp.exp(fc + m_old - m_q), decay=jnp.exp(d_log - m_q),
                             dec=jnp.exp(fe + m_old - m_new), wk=jnp.exp(w_log - m_new)))
    for s in seqs:
        s['intra'] = s['decay'] * _dot_nt(s['q'], s['k'])
    for s in seqs:
        s['num'] = s['inter'] * _dot_nt(s['q'], s['c_old']) + _dot(s['intra'], s['v'])
    for s in seqs:
        h, idx = s['h'], s['idx']
        den = (s['inter'] * jnp.sum(s['q'] * s['n_old'], axis=-1, keepdims=True)
               + jnp.sum(s['intra'], axis=-1, keepdims=True))
        s['o_ref'][0, :, h * dv:(h + 1) * dv] = s['num'] / jnp.maximum(jnp.abs(den), jnp.exp(-s['m_q']))
        c_ref[idx] = s['dec'] * s['c_old'] + _dot_tn(s['wk'] * s['v'], s['k'])
        n_ref[idx] = s['dec'] * s['n_old'] + jnp.sum(s['wk'] * s['k'], axis=0, keepdims=True)
        m_ref[idx] = jnp.broadcast_to(s['m_new'], m_ref.shape[1:])


def bidirectional_scan(kernel_fn, x, x_col, x_w, p3, params, n_lat, length, out_w, scratch_shapes, name):
    b, t, _ = p3.shape
    nl, nc = n_lat // length, t // length
    ncx = nc - nl
    assert t % length == 0 and n_lat % length == 0
    fwd = lambda n: jnp.where(n < ncx, nl + n, n - ncx)
    bwd = lambda n: jnp.where(n < ncx, nc - 1 - n, nl - 1 - (n - ncx))
    specs = lambda f: [pl.BlockSpec((1, length, x_w), lambda i, n: (i, f(n), x_col)),
                       pl.BlockSpec((1, length, LANES), lambda i, n: (i, f(n), COL_SMALL // LANES))]
    out = lambda f: pl.BlockSpec((1, length, out_w), lambda i, n: (i, f(n), 0))
    return pl.pallas_call(
        kernel_fn,
        grid=(b, nc),
        in_specs=specs(fwd) + specs(bwd) + [_full(a) for a in params],
        out_specs=[out(fwd), out(bwd)],
        out_shape=[jax.ShapeDtypeStruct((b, t, out_w), F32)] * 2,
        scratch_shapes=scratch_shapes,
        compiler_params=_params("parallel", "arbitrary"),
        name=name,
    )(x, p3, x, p3, *params)


def gdn_scan(xg, p3, a_log, dt_bias, n_lat):
    state = pltpu.VMEM((2 * GDN_HEADS, GDN_DV, GDN_DK), F32)
    params = (a_log.reshape(1, -1), dt_bias.reshape(1, -1))
    return bidirectional_scan(_gdn_kernel, xg, 0, xg.shape[-1], p3, params, n_lat, GDN_CHUNK,
                              GDN_HEADS * GDN_DV, [state], "gdn_scan")


def mlstm_scan(p3, b_if, n_lat):
    n_seq = 2 * ML_HEADS
    w = ML_HEADS * (2 * ML_DK + ML_DV)
    scratch = [pltpu.VMEM((n_seq, ML_DV, ML_DK), F32), pltpu.VMEM((n_seq, 1, ML_DK), F32),
               pltpu.VMEM((n_seq, 1, LANES), F32)]
    params = (b_if[:, 0].reshape(1, -1), b_if[:, 1].reshape(1, -1))
    return bidirectional_scan(_mlstm_kernel, p3, COL_ML // w, w, p3, params, n_lat, ML_CHUNK,
                              ML_HEADS * ML_DV, scratch, "mlstm_scan")


def _gated_head_norm(of_ref, ob_ref, gate, g_ref, n_heads):
    o = of_ref[0] + ob_ref[0]
    d = o.shape[-1] // n_heads
    parts = [_rms(o[:, h * d:(h + 1) * d]) * g_ref[...] for h in range(n_heads)]
    return (jnp.concatenate(parts, axis=-1) * gate).astype(BF16)


def _merge_kernel(h_ref, ya_ref, yc_ref, gof_ref, gob_ref, gg_ref, mof_ref, mob_ref, mg_ref, x_ref, g1_ref,
                  gn_ref, mn_ref, wmg_ref, bmg_ref, wbo_ref, wout_ref, o_ref):
    d = o_ref.shape[-1]
    h = h_ref[0]
    gg = gg_ref[0]
    yb = _gated_head_norm(gof_ref, gob_ref, gg * jax.nn.sigmoid(gg), gn_ref, GDN_HEADS)
    yd = _gated_head_norm(mof_ref, mob_ref, jax.nn.sigmoid(mg_ref[0]), mn_ref, ML_HEADS)
    acc = jnp.zeros((h.shape[0], d), F32)
    for n, y in enumerate((ya_ref[0], yb, yc_ref[0], yd)):
        z = jnp.dot(y, wbo_ref[n], preferred_element_type=F32)
        gate = jnp.dot(h, wmg_ref[:, n * d:(n + 1) * d], preferred_element_type=F32) + bmg_ref[:, n * d:(n + 1) * d]
        acc = acc + jax.nn.sigmoid(gate) * z
    m = jnp.dot(acc.astype(BF16), wout_ref[...], preferred_element_type=F32)
    o_ref[0] = x_ref[0] + g1_ref[0, 0] * m


def merge_residual(h, ya, yc, gdn_o, ml_o, p3, x, mods, n_lat, n_rows, gdn_norm_g, ml_norm_g,
                   w_mg, b_mg, w_bo, w_out, ts=256):
    b, _, d = x.shape
    ts = min(ts, n_rows - n_lat) if n_rows > n_lat else ts
    assert n_rows % ts == 0 and n_lat % ts == 0
    mod_row = _mod_row(n_lat // ts, b)
    row = lambda w, col=0: pl.BlockSpec((1, ts, w), lambda i, j: (i, j, col))
    b_mg = b_mg.reshape(1, -1)
    gn, mn = gdn_norm_g.reshape(1, -1), ml_norm_g.reshape(1, -1)
    return pl.pallas_call(
        _merge_kernel,
        grid=(b, n_rows // ts),
        in_specs=[row(d), row(BRANCH_W), row(BRANCH_W),
                  row(BRANCH_W), row(BRANCH_W), row(BRANCH_W, COL_GDN_GATE // BRANCH_W),
                  row(BRANCH_W), row(BRANCH_W), row(BRANCH_W, COL_ML_GATE // BRANCH_W),
                  row(d), pl.BlockSpec((1, 1, 1, d), lambda i, j: (mod_row(i, j), G1, 0, 0))]
                 + [_full(a) for a in (gn, mn, w_mg, b_mg, w_bo, w_out)],
        out_specs=row(d),
        out_shape=jax.ShapeDtypeStruct((b, n_rows, d), F32),
        compiler_params=_params("parallel", "parallel"),
        name="merge_residual",
    )(h, ya, yc, gdn_o[0], gdn_o[1], p3, ml_o[0], ml_o[1], p3, x, mods, gn, mn, w_mg, b_mg, w_bo, w_out)


def _expert_kernel(be_ref, x_ref, wgu_ref, bgu_ref, wdn_ref, bdn_ref, o_ref, wgu_s, wdn_s):
    i = pl.program_id(0)
    fresh = jnp.logical_or(i == 0, be_ref[i] != be_ref[jnp.maximum(i - 1, 0)])

    @pl.when(fresh)
    def _():
        wgu_s[...] = wgu_ref[0].astype(BF16)
        wdn_s[...] = wdn_ref[0].astype(BF16)

    gu = jnp.dot(x_ref[...], wgu_s[...], preferred_element_type=F32) + bgu_ref[0]
    gate = jnp.minimum(gu[:, :D_FF], SWIGLU_LIMIT)
    up = jnp.clip(gu[:, D_FF:], -SWIGLU_LIMIT, SWIGLU_LIMIT)
    act = (up + 1.0) * gate * jax.nn.sigmoid(SWIGLU_ALPHA * gate)
    y = jnp.dot(act.astype(BF16), wdn_s[...], preferred_element_type=F32) + bdn_ref[0]
    o_ref[...] = y.astype(o_ref.dtype)


def expert_blocks(buf, block_e, w_gu, b_gu, w_dn, b_dn):
    rows, d = buf.shape
    n_blocks = rows // MOE_ROWS
    e = w_gu.shape[0]
    grid_spec = pltpu.PrefetchScalarGridSpec(
        num_scalar_prefetch=1,
        grid=(n_blocks,),
        in_specs=[pl.BlockSpec((MOE_ROWS, d), lambda i, be: (i, 0)),
                  pl.BlockSpec((1, d, 2 * D_FF), lambda i, be: (be[i], 0, 0)),
                  pl.BlockSpec((1, 1, 2 * D_FF), lambda i, be: (be[i], 0, 0)),
                  pl.BlockSpec((1, D_FF, d), lambda i, be: (be[i], 0, 0)),
                  pl.BlockSpec((1, 1, d), lambda i, be: (be[i], 0, 0))],
        out_specs=pl.BlockSpec((MOE_ROWS, d), lambda i, be: (i, 0)),
        scratch_shapes=[pltpu.VMEM((d, 2 * D_FF), BF16), pltpu.VMEM((D_FF, d), BF16)],
    )
    return pl.pallas_call(
        _expert_kernel,
        grid_spec=grid_spec,
        out_shape=jax.ShapeDtypeStruct((rows, d), BF16),
        compiler_params=_params("arbitrary"),
        name="expert_blocks",
    )(block_e, buf, w_gu, b_gu.reshape(e, 1, -1), w_dn, b_dn.reshape(e, 1, -1))


def moe_ffn(t, logits, w_gu, b_gu, w_dn, b_dn):
    n, d = t.shape
    top_v, top_i = lax.top_k(logits, TOP_K)
    top_w = jax.nn.softmax(top_v, axis=-1)
    nk = n * TOP_K
    e_flat = top_i.reshape(nk)
    onehot = (e_flat[:, None] == jnp.arange(N_EXPERTS)[None, :]).astype(jnp.int32)
    csum = jnp.cumsum(onehot, axis=0)
    rank = jnp.take_along_axis(csum, e_flat[:, None], axis=1)[:, 0] - 1
    counts = csum[-1]
    padded = (counts + MOE_ROWS - 1) // MOE_ROWS * MOE_ROWS
    pad_end = jnp.cumsum(padded)
    dest = (pad_end - padded)[e_flat] + rank
    n_blocks = -(-nk // MOE_ROWS) + N_EXPERTS
    tok = jnp.arange(nk, dtype=jnp.int32) // TOP_K
    src = jnp.full((n_blocks * MOE_ROWS,), n, jnp.int32).at[dest].set(tok)
    buf = jnp.concatenate([t, jnp.zeros((1, d), t.dtype)], axis=0)[src]
    block_start = jnp.arange(n_blocks, dtype=pad_end.dtype) * MOE_ROWS
    block_e = jnp.minimum(jnp.sum(pad_end[None, :] <= block_start[:, None], axis=1), N_EXPERTS - 1).astype(jnp.int32)
    y = expert_blocks(buf, block_e, w_gu, b_gu, w_dn, b_dn)
    return y[dest.reshape(n, TOP_K).T.reshape(nk)].reshape(TOP_K, n, d), top_w


def kernel(x, c, ctx, c_ctx, w_ada, b_ada, norm1_g, norm2_g, w_in, na_rpb, gdn_conv_w, gdn_a_log, gdn_dt_bias, gdn_norm_g, mla_q_norm_g, mla_w_uq, mla_kv_norm_g, mla_w_ukv, ml_b_if, ml_norm_g, w_mg, b_mg, w_bo, w_out, router_w, router_b, w_gu, b_gu, w_dn, b_dn, final_g):
    depth = w_ada.shape[0]
    bsz, n_lat, d = x.shape
    n_ctx = ctx.shape[1]
    cond = jnp.concatenate([jax.nn.silu(c), jax.nn.silu(c_ctx)[None]], axis=0)
    cond = jnp.pad(cond, ((0, (-cond.shape[0]) % 8), (0, 0)))
    w_in_r = reorder_w_in(w_in.astype(BF16))
    w_mg_b, w_bo_b, w_out_b = (t.astype(BF16) for t in (w_mg, w_bo, w_out))
    wq, wqr, wkv = mla_weights(mla_w_uq, mla_w_ukv)
    cos, sin = rope_tables(n_lat, n_ctx)
    xs = jnp.concatenate([x, ctx], axis=1)
    n_tok = n_lat + n_ctx
    for l in range(depth):
        last = l == depth - 1
        n_rows = n_lat if last else n_tok
        mods = (pmm(cond, w_ada[l], tn=1024) + b_ada[l]).reshape(cond.shape[0], 6, 1, d)
        h = norm_mod(xs, norm1_g[l], mods, SC1, SH1, n_lat)
        p3 = pmm(h.reshape(bsz * n_tok, d), w_in_r[l], tn=640).reshape(bsz, n_tok, P_WIDTH)
        ya = neighborhood_attention(p3, na_rpb[l], n_lat)
        gdn_o = gdn_scan(gdn_prep(p3, gdn_conv_w[l], n_lat), p3, gdn_a_log[l], gdn_dt_bias[l], n_lat)
        ml_o = mlstm_scan(p3, ml_b_if[l], n_lat)
        q, k, v = mla_prep(p3, cos, sin, mla_q_norm_g[l], mla_kv_norm_g[l], wq[l], wqr[l], wkv[l])
        yc = pair_attention(q, k, v, n_lat, MLA_HEADS)
        if not last:
            na_ctx = p3[:, n_lat:, :3 * NA_HEADS * NA_DH]
            w = NA_HEADS * NA_DH
            ya_ctx = pair_attention((na_ctx[..., :w] * (NA_DH ** -0.5)).astype(BF16), na_ctx[..., w:2 * w].astype(BF16),
                                    na_ctx[..., 2 * w:].astype(BF16), n_ctx, NA_HEADS)
            yc_ctx = pair_attention(q[:, n_lat:], k[:, n_lat:], v[:, n_lat:], n_ctx, MLA_HEADS)
            ya = jnp.concatenate([ya, ya_ctx], axis=1)
            yc = jnp.concatenate([yc, yc_ctx], axis=1)
        xs = merge_residual(h, ya, yc, gdn_o, ml_o, p3, xs, mods, n_lat, n_rows, gdn_norm_g[l], ml_norm_g[l],
                            w_mg_b[l], b_mg[l], w_bo_b[l], w_out_b[l])
        h2, logits = norm_router(xs, norm2_g[l], mods, n_lat, router_w[l], router_b[l])
        y, top_w = moe_ffn(h2.reshape(bsz * n_rows, d), logits.reshape(bsz * n_rows, N_EXPERTS),
                           w_gu[l], b_gu[l], w_dn[l], b_dn[l])
        xs = combine_residual(xs, y.reshape(TOP_K, bsz, n_rows, d), top_w.reshape(bsz, n_rows, TOP_K),
                              mods, G2, n_lat)
    return norm_mod(xs, final_g, jnp.zeros((cond.shape[0], 6, 1, d), F32), SC1, SH1, n_lat, out_dtype=F32)
```

```python
import functools
import math

import numpy as np
import jax
import jax.numpy as jnp
from jax import lax
from jax.experimental import pallas as pl
from jax.experimental.pallas import tpu as pltpu

F32 = jnp.float32
BF16 = jnp.bfloat16

GRID_W = 64
EPS = 1e-6
ROPE_BASE = 10000.0
NA_HEADS, NA_DH, NA_WIN_H, NA_WIN_W = 8, 64, 8, 16
GDN_HEADS, GDN_DK, GDN_DV, GDN_CHUNK = 4, 128, 128, 64
MLA_HEADS, MLA_NOPE, MLA_ROPE, MLA_V, MLA_Q_RANK, MLA_KV_RANK = 8, 64, 32, 64, 256, 128
ML_HEADS, ML_DK, ML_DV, ML_CHUNK = 4, 64, 128, 64
N_BRANCH, BRANCH_W = 4, 512
N_EXPERTS, TOP_K, D_FF = 32, 4, 1024
SWIGLU_ALPHA, SWIGLU_LIMIT = 1.702, 7.0
MOE_ROWS = 256
LANES = 128

IN_SPLITS = (
    NA_HEADS * NA_DH, NA_HEADS * NA_DH, NA_HEADS * NA_DH,
    GDN_HEADS * (2 * GDN_DK + GDN_DV), GDN_HEADS * GDN_DV,
    2 * GDN_HEADS, 2 * GDN_HEADS,
    MLA_Q_RANK, MLA_KV_RANK + MLA_ROPE,
    ML_HEADS * ML_DK, ML_HEADS * ML_DK, ML_HEADS * ML_DV, ML_HEADS * ML_DV,
    2 * ML_HEADS, 2 * ML_HEADS,
)
COL_NA = 0
COL_GDN = 1536
COL_GDN_GATE = 3072
COL_MLA = 3584
COL_ML = 4096
COL_ML_GATE = 5120
COL_SMALL = 5632
P_WIDTH = 5760
MLA_BLOCK = COL_ML - COL_MLA
SH1, SC1, G1, SH2, SC2, G2 = range(6)

VMEM_LIMIT_BYTES = 56 * 1024 * 1024


def _params(*sem):
    return pltpu.CompilerParams(dimension_semantics=sem, vmem_limit_bytes=VMEM_LIMIT_BYTES)


def _full(a):
    return pl.BlockSpec(a.shape, lambda *_: (0,) * a.ndim)


def _dot(a, b, precise=False):
    if precise:
        return jnp.dot(a, b, preferred_element_type=F32, precision=lax.Precision.HIGHEST)
    return jnp.dot(a.astype(BF16), b.astype(BF16), preferred_element_type=F32)


def _dot_nt(a, b):
    return lax.dot_general(a.astype(BF16), b.astype(BF16), (((1,), (1,)), ((), ())), preferred_element_type=F32)


def _dot_tn(a, b):
    return lax.dot_general(a.astype(BF16), b.astype(BF16), (((0,), (0,)), ((), ())), preferred_element_type=F32)


def _softplus(x):
    return jnp.maximum(x, 0.0) + jnp.log(1.0 + jnp.exp(-jnp.abs(x)))


def _rms(x):
    return x * lax.rsqrt(jnp.mean(x * x, axis=-1, keepdims=True) + EPS)


def _mm_kernel(x_ref, w_ref, o_ref):
    o_ref[...] = _dot(x_ref[...], w_ref[...]).astype(o_ref.dtype)


def pmm(x, w, out_dtype=F32, tm=512, tn=512):
    m, k = x.shape
    n = w.shape[1]
    tm, tn = math.gcd(tm, m), min(tn, n)
    if n % tn:
        tn = n
    assert m % tm == 0 and n % tn == 0, (m, n, tm, tn)
    return pl.pallas_call(
        _mm_kernel,
        grid=(m // tm, n // tn),
        in_specs=[pl.BlockSpec((tm, k), lambda i, j: (i, 0)),
                  pl.BlockSpec((k, tn), lambda i, j: (0, j))],
        out_specs=pl.BlockSpec((tm, tn), lambda i, j: (i, j)),
        out_shape=jax.ShapeDtypeStruct((m, n), out_dtype),
        compiler_params=_params("parallel", "parallel"),
        name="pmm",
    )(x, w)


def reorder_w_in(w_in):
    o = np.cumsum((0,) + IN_SPLITS)
    piece = lambda i: w_in[..., o[i]:o[i + 1]]
    zeros = lambda n: jnp.zeros(w_in.shape[:-1] + (n,), w_in.dtype)
    cols = [piece(0), piece(1), piece(2), piece(3), piece(4),
            piece(7), piece(8), zeros(MLA_BLOCK - IN_SPLITS[7] - IN_SPLITS[8]),
            piece(9), piece(10), piece(11), piece(12),
            piece(5), piece(6), piece(13), piece(14), zeros(P_WIDTH - COL_SMALL - 8 * GDN_HEADS)]
    out = jnp.concatenate(cols, axis=-1)
    assert out.shape[-1] == P_WIDTH
    return out


def _mod_row(n_lat_blocks, ctx_row):
    return lambda i, j: jnp.where(j >= n_lat_blocks, ctx_row, i)


def _norm_mod_kernel(x_ref, g_ref, sc_ref, sh_ref, o_ref):
    y = _rms(x_ref[0]) * g_ref[...]
    o_ref[0] = (y * (1.0 + sc_ref[0, 0]) + sh_ref[0, 0]).astype(o_ref.dtype)


def norm_mod(x, g, mods, i_sc, i_sh, n_lat, out_dtype=BF16, ts=256):
    b, t, d = x.shape
    ts = min(ts, t - n_lat) if t > n_lat else ts
    assert t % ts == 0 and n_lat % ts == 0
    row = _mod_row(n_lat // ts, b)
    return pl.pallas_call(
        _norm_mod_kernel,
        grid=(b, t // ts),
        in_specs=[pl.BlockSpec((1, ts, d), lambda i, j: (i, j, 0)),
                  pl.BlockSpec((1, d), lambda i, j: (0, 0)),
                  pl.BlockSpec((1, 1, 1, d), lambda i, j: (row(i, j), i_sc, 0, 0)),
                  pl.BlockSpec((1, 1, 1, d), lambda i, j: (row(i, j), i_sh, 0, 0))],
        out_specs=pl.BlockSpec((1, ts, d), lambda i, j: (i, j, 0)),
        out_shape=jax.ShapeDtypeStruct((b, t, d), out_dtype),
        compiler_params=_params("parallel", "parallel"),
        name="norm_mod",
    )(x, g.reshape(1, d), mods, mods)


def _norm_router_kernel(x_ref, g_ref, sc_ref, sh_ref, rw_ref, rb_ref, h_ref, lg_ref):
    h = (_rms(x_ref[0]) * g_ref[...]) * (1.0 + sc_ref[0, 0]) + sh_ref[0, 0]
    h_ref[0] = h.astype(h_ref.dtype)
    lg_ref[0] = _dot(h, rw_ref[...], precise=True) + rb_ref[...]


def norm_router(x, g, mods, n_lat, router_w, router_b, ts=256):
    b, t, d = x.shape
    ts = min(ts, t - n_lat) if t > n_lat else ts
    assert t % ts == 0 and n_lat % ts == 0
    row = _mod_row(n_lat // ts, b)
    e = router_w.shape[1]
    rb = router_b.reshape(1, e)
    return pl.pallas_call(
        _norm_router_kernel,
        grid=(b, t // ts),
        in_specs=[pl.BlockSpec((1, ts, d), lambda i, j: (i, j, 0)),
                  pl.BlockSpec((1, d), lambda i, j: (0, 0)),
                  pl.BlockSpec((1, 1, 1, d), lambda i, j: (row(i, j), SC2, 0, 0)),
                  pl.BlockSpec((1, 1, 1, d), lambda i, j: (row(i, j), SH2, 0, 0)),
                  _full(router_w), _full(rb)],
        out_specs=[pl.BlockSpec((1, ts, d), lambda i, j: (i, j, 0)),
                   pl.BlockSpec((1, ts, e), lambda i, j: (i, j, 0))],
        out_shape=[jax.ShapeDtypeStruct((b, t, d), BF16), jax.ShapeDtypeStruct((b, t, e), F32)],
        compiler_params=_params("parallel", "parallel"),
        name="norm_router",
    )(x, g.reshape(1, d), mods, mods, router_w, rb)


def _combine_residual_kernel(x_ref, y_ref, w_ref, g_ref, o_ref):
    w = w_ref[0]
    f = functools.reduce(jnp.add, [w[:, k:k + 1] * y_ref[k, 0].astype(F32) for k in range(y_ref.shape[0])])
    o_ref[0] = x_ref[0] + g_ref[0, 0] * f


def combine_residual(x, y, w, mods, i_gate, n_lat, ts=256):
    b, t, d = x.shape
    n_slots = y.shape[0]
    ts = min(ts, t - n_lat) if t > n_lat else ts
    row = _mod_row(n_lat // ts, b)
    blk = pl.BlockSpec((1, ts, d), lambda i, j: (i, j, 0))
    return pl.pallas_call(
        _combine_residual_kernel,
        grid=(b, t // ts),
        in_specs=[blk, pl.BlockSpec((n_slots, 1, ts, d), lambda i, j: (0, i, j, 0)),
                  pl.BlockSpec((1, ts, n_slots), lambda i, j: (i, j, 0)),
                  pl.BlockSpec((1, 1, 1, d), lambda i, j: (row(i, j), i_gate, 0, 0))],
        out_specs=blk,
        out_shape=jax.ShapeDtypeStruct((b, t, d), F32),
        compiler_params=_params("parallel", "parallel"),
        name="combine_residual",
    )(x, y, w, mods)


def _pair_attn_kernel(q_ref, k_ref, v_ref, o_ref, *, tk, dv):
    dq, dvp = q_ref.shape[-1] // 2, v_ref.shape[-1] // 2
    tq = q_ref.shape[1]
    sum_on_mxu = dvp > dv
    outs = []
    for hh in range(2):
        q = q_ref[0][:, hh * dq:(hh + 1) * dq]
        m = jnp.full((tq, 1), -jnp.inf, F32)
        l = jnp.zeros((tq, 1), F32)
        acc = jnp.zeros((tq, dvp), F32)
        for c in range(k_ref.shape[1] // tk):
            k = k_ref[0, c * tk:(c + 1) * tk, hh * dq:(hh + 1) * dq]
            v = v_ref[0, c * tk:(c + 1) * tk, hh * dvp:(hh + 1) * dvp]
            s = lax.dot_general(q, k, (((1,), (1,)), ((), ())), preferred_element_type=F32)
            m_new = jnp.maximum(m, jnp.max(s, axis=-1, keepdims=True))
            alpha = jnp.exp(m - m_new)
            p = jnp.exp(s - m_new)
            if not sum_on_mxu:
                l = alpha * l + jnp.sum(p, axis=-1, keepdims=True)
            acc = alpha * acc + jnp.dot(p.astype(BF16), v, preferred_element_type=F32)
            m = m_new
        outs.append(acc[:, :dv] / (acc[:, dv:dv + 1] if sum_on_mxu else l))
    o_ref[0] = jnp.concatenate(outs, axis=-1).astype(o_ref.dtype)


def pair_attention(q, k, v, n_q, n_heads, dv, tq=512):
    b, t, hq = k.shape
    dq2, dvp2 = 2 * hq // n_heads, 2 * v.shape[-1] // n_heads
    tq = min(tq, n_q)
    tk = next(c for c in (768, 512, 384, 256, 128) if t % c == 0)
    assert n_q % tq == 0 and dq2 % LANES == 0 and dvp2 % LANES == 0 and (2 * dv) % LANES == 0
    return pl.pallas_call(
        functools.partial(_pair_attn_kernel, tk=tk, dv=dv),
        grid=(b, n_heads // 2, n_q // tq),
        in_specs=[pl.BlockSpec((1, tq, dq2), lambda i, hp, n: (i, n, hp)),
                  pl.BlockSpec((1, t, dq2), lambda i, hp, n: (i, 0, hp)),
                  pl.BlockSpec((1, t, dvp2), lambda i, hp, n: (i, 0, hp))],
        out_specs=pl.BlockSpec((1, tq, 2 * dv), lambda i, hp, n: (i, n, hp)),
        out_shape=jax.ShapeDtypeStruct((b, n_q, n_heads * dv), BF16),
        compiler_params=_params("parallel", "parallel", "parallel"),
        name="pair_attention",
    )(q, k, v)


NA_QROWS = 8
NA_KROWS = 4
NA_NEG = -1e30


def na_bias_tables(rpb, rows):
    n_steps = rows // NA_QROWS
    wh, ww = min(NA_WIN_H, rows), NA_WIN_W
    n_kblocks = rows // NA_KROWS
    n_heads = rpb.shape[0]
    width = 2 * GRID_W - 1
    lo = GRID_W - NA_WIN_W
    padded = jnp.full(rpb.shape[:2] + (width,), NA_NEG, F32).at[..., lo:lo + 2 * NA_WIN_W - 1].set(rpb)
    tiles = jnp.stack([padded[..., GRID_W - 1 - qc:width - qc] for qc in range(GRID_W)], axis=2)
    qc, kc = np.arange(GRID_W)[:, None], np.arange(GRID_W)[None, :]
    c0 = np.clip(qc - ww // 2, 0, GRID_W - ww)
    tiles = jnp.where((kc >= c0) & (kc < c0 + ww), tiles, NA_NEG)
    masked = jnp.full((n_heads, GRID_W, GRID_W), NA_NEG, F32)
    tables = []
    for i in (0, min(1, n_steps - 1), n_steps - 1):
        q_tiles = []
        for il in range(NA_QROWS):
            q = NA_QROWS * i + il
            r0 = min(max(q - wh // 2, 0), rows - wh)
            k_tiles = []
            for ka in range(4 * NA_KROWS):
                blk = 2 * i - 1 + ka // NA_KROWS
                kr = NA_KROWS * blk + ka % NA_KROWS
                ok = 0 <= blk < n_kblocks and r0 <= kr < r0 + wh
                k_tiles.append(tiles[:, kr - q + NA_WIN_H - 1] if ok else masked)
            q_tiles.append(jnp.concatenate(k_tiles, axis=-1))
        tables.append(jnp.concatenate(q_tiles, axis=-2))
    return jnp.stack(tables)


def _na_kernel(q_ref, k0_ref, k1_ref, k2_ref, k3_ref, v0_ref, v1_ref, v2_ref, v3_ref,
               kc_ref, vc_ref, bias_ref, o_ref):
    kw = NA_KROWS * GRID_W
    outs = []
    for hh in range(2):
        sl = slice(hh * NA_DH, (hh + 1) * NA_DH)
        q = (q_ref[0][:, sl] * (NA_DH ** -0.5)).astype(BF16)
        ks = [r[0][:, sl].astype(BF16) for r in (k0_ref, k1_ref, k2_ref, k3_ref, kc_ref)]
        vs = [r[0][:, sl].astype(BF16) for r in (v0_ref, v1_ref, v2_ref, v3_ref, vc_ref)]
        ss = [lax.dot_general(q, kk, (((1,), (1,)), ((), ())), preferred_element_type=F32) for kk in ks]
        ss = [ss[j] + bias_ref[0, 0, hh, :, j * kw:(j + 1) * kw] for j in range(4)] + [ss[4]]
        m = functools.reduce(jnp.maximum, [jnp.max(t, axis=-1, keepdims=True) for t in ss])
        ps = [jnp.exp(t - m) for t in ss]
        l = functools.reduce(jnp.add, [jnp.sum(t, axis=-1, keepdims=True) for t in ps])
        o = functools.reduce(jnp.add, [jnp.dot(p.astype(BF16), vv, preferred_element_type=F32)
                                       for p, vv in zip(ps, vs)])
        outs.append(o / l)
    o_ref[0] = jnp.concatenate(outs, axis=-1).astype(o_ref.dtype)


def neighborhood_attention(p3, rpb, n_lat):
    b, t, _ = p3.shape
    c_len = t - n_lat
    rows = n_lat // GRID_W
    tq, tkv = NA_QROWS * GRID_W, NA_KROWS * GRID_W
    assert rows % NA_QROWS == 0 and rows >= 2 * NA_QROWS and n_lat % c_len == 0
    n_steps = rows // NA_QROWS
    n_kblocks = rows // NA_KROWS
    hw = 2 * NA_DH
    n_pairs = NA_HEADS // 2
    bias = na_bias_tables(rpb, rows)
    bias = bias.reshape(3, n_pairs, 2, *bias.shape[2:])

    def view(j, col0):
        return pl.BlockSpec((1, tkv, hw),
                            lambda hp, i, n: (i, jnp.clip(2 * n - 1 + j, 0, n_kblocks - 1), col0 + hp))

    ctx_spec = lambda col0: pl.BlockSpec((1, c_len, hw), lambda hp, i, n: (i, n_lat // c_len, col0 + hp))
    bias_spec = pl.BlockSpec(
        (1, 1, 2, tq, 4 * tkv),
        lambda hp, i, n: (jnp.where(n == 0, 0, jnp.where(n == n_steps - 1, 2, 1)), hp, 0, 0, 0))
    return pl.pallas_call(
        _na_kernel,
        grid=(n_pairs, b, n_steps),
        in_specs=[pl.BlockSpec((1, tq, hw), lambda hp, i, n: (i, n, hp))]
                 + [view(j, n_pairs) for j in range(4)] + [view(j, 2 * n_pairs) for j in range(4)]
                 + [ctx_spec(n_pairs), ctx_spec(2 * n_pairs), bias_spec],
        out_specs=pl.BlockSpec((1, tq, hw), lambda hp, i, n: (i, n, hp)),
        out_shape=jax.ShapeDtypeStruct((b, n_lat, NA_HEADS * NA_DH), BF16),
        compiler_params=_params("parallel", "parallel", "arbitrary"),
        name="neighborhood_attention",
    )(*([p3] * 11), bias)


def rope_tables(n_lat, n_ctx):
    half = MLA_ROPE // 2
    tpos = np.arange(n_lat)
    inv = ROPE_BASE ** (-jnp.arange(0, half, 2, dtype=F32) / half)
    pos = jnp.stack([tpos // GRID_W, tpos % GRID_W], axis=1).astype(F32)
    ang = pos[:, :, None] * inv
    ang = jnp.concatenate([ang, ang], axis=-1).reshape(n_lat, MLA_ROPE)
    pad = ((0, n_ctx), (MLA_NOPE, LANES - MLA_NOPE - MLA_ROPE))
    return (jnp.pad(jnp.cos(ang), pad, constant_values=1.0), jnp.pad(jnp.sin(ang), pad))


def _rope_swap():
    half = MLA_ROPE // 2
    quarter = half // 2
    r = np.zeros((MLA_ROPE, MLA_ROPE), np.float32)
    for j in range(MLA_ROPE):
        if j % half < quarter:
            r[j + quarter, j] = -1.0
        else:
            r[j - quarter, j] = 1.0
    return r


def mla_weights(w_uq, w_ukv):
    dqh = MLA_NOPE + MLA_ROPE
    lead = w_uq.shape[:-1]
    wq = w_uq.reshape(*lead, MLA_HEADS, dqh)
    padh = lambda a: jnp.pad(a, ((0, 0),) * (a.ndim - 1) + ((0, LANES - a.shape[-1]),))
    wq_p = padh(wq).reshape(*lead, MLA_HEADS * LANES)
    rot = jnp.einsum('...hr,rs->...hs', wq[..., MLA_NOPE:], jnp.asarray(_rope_swap()))
    wqr_p = jnp.pad(rot, ((0, 0),) * (rot.ndim - 1) + ((MLA_NOPE, LANES - dqh),)).reshape(*lead, MLA_HEADS * LANES)
    lead = w_ukv.shape[:-1]
    wkv = w_ukv.reshape(*lead, MLA_HEADS, MLA_NOPE + MLA_V)
    wk_p = padh(wkv[..., :MLA_NOPE]).reshape(*lead, MLA_HEADS * LANES)
    wv_p = padh(wkv[..., MLA_NOPE:]).reshape(*lead, MLA_HEADS * LANES)
    return wq_p.astype(BF16), wqr_p.astype(BF16), jnp.concatenate([wk_p, wv_p], axis=-1).astype(BF16)


def _mla_prep_kernel(p_ref, cos_ref, sin_ref, qg_ref, kvg_ref, wq_ref, wqr_ref, wkv_ref, e_ref, er_ref,
                     q_out, k_out, v_out):
    blk = p_ref[0]
    nq = _rms(blk[:, :MLA_Q_RANK]) * qg_ref[...]
    nkv = _rms(blk[:, MLA_Q_RANK:MLA_Q_RANK + MLA_KV_RANK]) * kvg_ref[...]
    kpe = blk[:, MLA_Q_RANK + MLA_KV_RANK:MLA_Q_RANK + MLA_KV_RANK + MLA_ROPE]
    cos, sin = cos_ref[...], sin_ref[...]
    qf, qr = _dot(nq, wq_ref[...]), _dot(nq, wqr_ref[...])
    kv = _dot(nkv, wkv_ref[...])
    kpe_tile = _dot(kpe, e_ref[...], precise=True) * cos + _dot(kpe, er_ref[...], precise=True) * sin
    scale = (MLA_NOPE + MLA_ROPE) ** -0.5
    for h in range(MLA_HEADS):
        sl = slice(h * LANES, (h + 1) * LANES)
        q_out[0, :, sl] = ((qf[:, sl] * cos + qr[:, sl] * sin) * scale).astype(q_out.dtype)
        k_out[0, :, sl] = (kv[:, sl] + kpe_tile).astype(k_out.dtype)
        ones_col = (lax.broadcasted_iota(jnp.int32, (1, LANES), 1) == MLA_V).astype(F32)
        v_out[0, :, sl] = (kv[:, MLA_HEADS * LANES + h * LANES:MLA_HEADS * LANES + (h + 1) * LANES]
                           + ones_col).astype(v_out.dtype)


def mla_prep(p3, cos, sin, q_norm_g, kv_norm_g, wq, wqr, wkv, tb=256):
    b, t, _ = p3.shape
    tb = next(c for c in (tb, 128) if t % c == 0)
    swap = _rope_swap()
    place = np.zeros((MLA_ROPE, LANES), np.float32)
    place[np.arange(MLA_ROPE), MLA_NOPE + np.arange(MLA_ROPE)] = 1.0
    e, er = jnp.asarray(place), jnp.asarray(swap @ place)
    qg, kvg = q_norm_g.reshape(1, -1), kv_norm_g.reshape(1, -1)
    row = lambda w: pl.BlockSpec((1, tb, w), lambda i, j: (i, j, 0))
    tab = pl.BlockSpec((tb, LANES), lambda i, j: (j, 0))
    return pl.pallas_call(
        _mla_prep_kernel,
        grid=(b, t // tb),
        in_specs=[pl.BlockSpec((1, tb, MLA_BLOCK), lambda i, j: (i, j, COL_MLA // MLA_BLOCK)), tab, tab]
                 + [_full(a) for a in (qg, kvg, wq, wqr, wkv, e, er)],
        out_specs=[row(MLA_HEADS * LANES)] * 3,
        out_shape=[jax.ShapeDtypeStruct((b, t, MLA_HEADS * LANES), BF16),
                   jax.ShapeDtypeStruct((b, t, MLA_HEADS * LANES), BF16),
                   jax.ShapeDtypeStruct((b, t, MLA_HEADS * LANES), BF16)],
        compiler_params=_params("parallel", "parallel"),
        name="mla_prep",
    )(p3, cos, sin, qg, kvg, wq, wqr, wkv, e, er)


def _gdn_prep_kernel(x_ref, prev_ref, next_ref, w_ref, o_ref, *, n_lat, n_tok):
    tb = x_ref.shape[1]
    start = pl.program_id(1) * tb
    x = x_ref[0]
    seg_first = jnp.logical_or(start == 0, start == n_lat)
    seg_last = jnp.logical_or(start + tb == n_lat, start + tb == n_tok)
    prev_row = jnp.where(seg_first, 0.0, prev_ref[0][7:8, :])
    next_row = jnp.where(seg_last, 0.0, next_ref[0][0:1, :])
    rows = lax.broadcasted_iota(jnp.int32, (tb, 1), 0)
    x_m1 = jnp.where(rows == 0, prev_row, pltpu.roll(x, 1, axis=0))
    x_p1 = jnp.where(rows == tb - 1, next_row, pltpu.roll(x, tb - 1, axis=0))
    y = x_m1 * w_ref[0:1, :] + x * w_ref[1:2, :] + x_p1 * w_ref[2:3, :]
    y = y * jax.nn.sigmoid(y)
    for g in range(2 * GDN_HEADS):
        sl = slice(g * GDN_DK, (g + 1) * GDN_DK)
        yh = y[:, sl]
        yh = yh * lax.rsqrt(jnp.sum(yh * yh, axis=-1, keepdims=True) + EPS)
        o_ref[0, :, sl] = yh * (GDN_DK ** -0.5) if g < GDN_HEADS else yh
    o_ref[0, :, 2 * GDN_HEADS * GDN_DK:] = y[:, 2 * GDN_HEADS * GDN_DK:]


def gdn_prep(p3, conv_w, n_lat, tb=256):
    b, t, _ = p3.shape
    w = GDN_HEADS * (2 * GDN_DK + GDN_DV)
    tb = min(tb, t - n_lat)
    assert t % tb == 0 and n_lat % tb == 0 and tb % 8 == 0 and COL_GDN % w == 0
    col = COL_GDN // w
    sub = tb // 8
    return pl.pallas_call(
        functools.partial(_gdn_prep_kernel, n_lat=n_lat, n_tok=t),
        grid=(b, t // tb),
        in_specs=[pl.BlockSpec((1, tb, w), lambda i, j: (i, j, col)),
                  pl.BlockSpec((1, 8, w), lambda i, j: (i, jnp.maximum(j * sub - 1, 0), col)),
                  pl.BlockSpec((1, 8, w), lambda i, j: (i, jnp.minimum((j + 1) * sub, t // 8 - 1), col)),
                  _full(conv_w)],
        out_specs=pl.BlockSpec((1, tb, w), lambda i, j: (i, j, 0)),
        out_shape=jax.ShapeDtypeStruct((b, t, w), F32),
        compiler_params=_params("parallel", "parallel"),
        name="gdn_prep",
    )(p3, p3, p3, conv_w)


def _order_masks(length, reverse):
    r = lax.broadcasted_iota(jnp.int32, (length, length), 0)
    c = lax.broadcasted_iota(jnp.int32, (length, length), 1)
    if reverse:
        return r <= c, r < c, r >= c
    return r >= c, r > c, r <= c


def _as_row(col, mask):
    return jnp.sum(jnp.where(mask, col, 0.0), axis=0, keepdims=True)


def _gdn_kernel(xf_ref, sf_ref, xb_ref, sb_ref, alog_ref, dtb_ref, of_ref, ob_ref, s_ref):
    @pl.when(pl.program_id(1) == 0)
    def _():
        s_ref[...] = jnp.zeros_like(s_ref)

    nh, dk, dv, length = GDN_HEADS, GDN_DK, GDN_DV, GDN_CHUNK
    n_sub = xf_ref.shape[1] // length
    r = lax.broadcasted_iota(jnp.int32, (length, length), 0)
    c = lax.broadcasted_iota(jnp.int32, (length, length), 1)
    eye = (r == c).astype(F32)
    chunks = []
    for sub in range(n_sub):
        seqs = []
        for d, (x_ref, sm_ref, o_ref) in enumerate(((xf_ref, sf_ref, of_ref), (xb_ref, sb_ref, ob_ref))):
            r0 = (n_sub - 1 - sub if d == 1 else sub) * length
            rows = slice(r0, r0 + length)
            incl, strict, incl_t = _order_masks(length, d == 1)
            small = sm_ref[0, rows, :]
            hs = slice(d * nh, (d + 1) * nh)
            la_col = -jnp.exp(alog_ref[:, hs]) * _softplus(small[:, hs] + dtb_ref[:, hs])
            be_col = jax.nn.sigmoid(small[:, 2 * nh + d * nh:2 * nh + (d + 1) * nh])
            g_cols = _dot(incl.astype(F32), la_col, precise=True)
            g_ends = jnp.sum(la_col, axis=0, keepdims=True)
            for h in range(nh):
                q = x_ref[0, rows, h * dk:(h + 1) * dk]
                k = x_ref[0, rows, (nh + h) * dk:(nh + h + 1) * dk]
                v = x_ref[0, rows, 2 * nh * dk + h * dv:2 * nh * dk + (h + 1) * dv]
                gc, ge, bc = g_cols[:, h:h + 1], g_ends[:, h:h + 1], be_col[:, h:h + 1]
                gr = _as_row(la_col[:, h:h + 1], incl_t)
                decay = jnp.exp(jnp.where(incl, gc - gr, -jnp.inf))
                qk = decay * _dot_nt(q, k)
                p = -(bc * jnp.where(strict, decay, 0.0) * _dot_nt(k, k))
                eg = jnp.exp(gc)
                seqs.append(dict(o_ref=o_ref, rows=rows, h=h, idx=d * nh + h, q_dec=eg * q, qk=qk, p=p, t=eye + p,
                                 rhs=jnp.concatenate([bc * v, (bc * eg) * k], axis=-1),
                                 k_end=jnp.exp(ge - gc) * k, e_end=jnp.exp(ge)))
        for _ in range(int(math.log2(length)) - 1):
            for s in seqs:
                s['p'] = _dot(s['p'], s['p'])
            for s in seqs:
                s['t'] = s['t'] + _dot(s['t'], s['p'])
        for s in seqs:
            s['sol'] = _dot(s['t'], s['rhs'])
        chunks.append(seqs)
    states = [s_ref[i] for i in range(2 * nh)]
    for seqs in chunks:
        for s in seqs:
            s['u'] = s['sol'][:, :dv] - _dot_nt(s['sol'][:, dv:], states[s['idx']])
        for s in seqs:
            h, s_old = s['h'], states[s['idx']]
            s['o_ref'][0, s['rows'], h * dv:(h + 1) * dv] = _dot_nt(s['q_dec'], s_old) + _dot(s['qk'], s['u'])
            states[s['idx']] = s['e_end'] * s_old + _dot_tn(s['u'], s['k_end'])
    for i, st in enumerate(states):
        s_ref[i] = st


def _mlstm_kernel(xf_ref, sf_ref, xb_ref, sb_ref, bi_ref, bf_ref, of_ref, ob_ref, c_ref, n_ref, m_ref):
    @pl.when(pl.program_id(1) == 0)
    def _():
        c_ref[...] = jnp.zeros_like(c_ref)
        n_ref[...] = jnp.zeros_like(n_ref)
        m_ref[...] = jnp.zeros_like(m_ref)

    nh, dk, dv, length = ML_HEADS, ML_DK, ML_DV, ML_CHUNK
    n_sub = xf_ref.shape[1] // length
    r = lax.broadcasted_iota(jnp.int32, (length, length), 0)
    c = lax.broadcasted_iota(jnp.int32, (length, length), 1)
    eye = r == c
    col_i, col_f = 4 * GDN_HEADS, 4 * GDN_HEADS + 2 * nh
    chunks = []
    for sub in range(n_sub):
        seqs = []
        for d, (x_ref, sm_ref, o_ref) in enumerate(((xf_ref, sf_ref, of_ref), (xb_ref, sb_ref, ob_ref))):
            r0 = (n_sub - 1 - sub if d == 1 else sub) * length
            rows = slice(r0, r0 + length)
            incl, _, incl_t = _order_masks(length, d == 1)
            small = sm_ref[0, rows, :]
            hs = slice(d * nh, (d + 1) * nh)
            i_col = small[:, col_i + d * nh:col_i + (d + 1) * nh] + bi_ref[:, hs]
            f_col = -_softplus(-(small[:, col_f + d * nh:col_f + (d + 1) * nh] + bf_ref[:, hs]))
            fc_cols = _dot(incl.astype(F32), f_col, precise=True)
            f_ends = jnp.sum(f_col, axis=0, keepdims=True)
            for h in range(nh):
                q = x_ref[0, rows, h * dk:(h + 1) * dk] * (dk ** -0.5)
                k = x_ref[0, rows, (nh + h) * dk:(nh + h + 1) * dk]
                v = x_ref[0, rows, 2 * nh * dk + h * dv:2 * nh * dk + (h + 1) * dv]
                fc, fe, ic = fc_cols[:, h:h + 1], f_ends[:, h:h + 1], i_col[:, h:h + 1]
                fr, ir = _as_row(f_col[:, h:h + 1], incl_t), _as_row(ic, eye)
                d_log = jnp.where(incl, fc - fr + ir, -jnp.inf)
                d_max = jnp.max(d_log, axis=-1, keepdims=True)
                w_log = fe - fc + ic
                w_max = jnp.max(w_log, axis=0, keepdims=True)
                seqs.append(dict(o_ref=o_ref, rows=rows, h=h, idx=d * nh + h, q=q, k=k, v=v, fc=fc, fe=fe,
                                 d_max=d_max, w_max=w_max, decay=jnp.exp(d_log - d_max), wk=jnp.exp(w_log - w_max)))
        for s in seqs:
            s['intra'] = s['decay'] * _dot_nt(s['q'], s['k'])
        for s in seqs:
            s['intra_v'] = _dot(s['intra'], s['v'])
            s['vk'] = _dot_tn(s['wk'] * s['v'], s['k'])
        for s in seqs:
            s['intra_sum'] = jnp.sum(s['intra'], axis=-1, keepdims=True)
            s['k_sum'] = jnp.sum(s['wk'] * s['k'], axis=0, keepdims=True)
        chunks.append(seqs)
    n_seq = 2 * nh
    cs = [c_ref[i] for i in range(n_seq)]
    ns = [n_ref[i] for i in range(n_seq)]
    ms = [m_ref[i][:, 0:1] for i in range(n_seq)]
    for seqs in chunks:
        for s in seqs:
            m_old = ms[s['idx']]
            s['m_q'] = jnp.maximum(s['fc'] + m_old, s['d_max'])
            s['inter'], s['scale'] = jnp.exp(s['fc'] + m_old - s['m_q']), jnp.exp(s['d_max'] - s['m_q'])
            s['m_new'] = jnp.maximum(s['fe'] + m_old, s['w_max'])
            s['dec'], s['w_scale'] = jnp.exp(s['fe'] + m_old - s['m_new']), jnp.exp(s['w_max'] - s['m_new'])
        for s in seqs:
            s['qc'] = _dot_nt(s['q'], cs[s['idx']])
        for s in seqs:
            h, idx = s['h'], s['idx']
            num = s['inter'] * s['qc'] + s['scale'] * s['intra_v']
            den = (s['inter'] * jnp.sum(s['q'] * ns[idx], axis=-1, keepdims=True) + s['scale'] * s['intra_sum'])
            s['o_ref'][0, s['rows'], h * dv:(h + 1) * dv] = num / jnp.maximum(jnp.abs(den), jnp.exp(-s['m_q']))
            cs[idx] = s['dec'] * cs[idx] + s['w_scale'] * s['vk']
            ns[idx] = s['dec'] * ns[idx] + s['w_scale'] * s['k_sum']
            ms[idx] = s['m_new']
    for i in range(n_seq):
        c_ref[i], n_ref[i] = cs[i], ns[i]
        m_ref[i] = jnp.broadcast_to(ms[i], m_ref.shape[1:])


SCAN_BLOCK = 128


def bidirectional_scan(kernel_fn, x, x_col, x_w, p3, params, n_lat, out_w, scratch_shapes, name):
    b, t, _ = p3.shape
    length = SCAN_BLOCK
    nl, nc = n_lat // length, t // length
    ncx = nc - nl
    assert t % length == 0 and n_lat % length == 0
    fwd = lambda n: jnp.where(n < ncx, nl + n, n - ncx)
    bwd = lambda n: jnp.where(n < ncx, nc - 1 - n, nl - 1 - (n - ncx))
    specs = lambda f: [pl.BlockSpec((1, length, x_w), lambda i, n: (i, f(n), x_col)),
                       pl.BlockSpec((1, length, LANES), lambda i, n: (i, f(n), COL_SMALL // LANES))]
    out = lambda f: pl.BlockSpec((1, length, out_w), lambda i, n: (i, f(n), 0))
    return pl.pallas_call(
        kernel_fn,
        grid=(b, nc),
        in_specs=specs(fwd) + specs(bwd) + [_full(a) for a in params],
        out_specs=[out(fwd), out(bwd)],
        out_shape=[jax.ShapeDtypeStruct((b, t, out_w), F32)] * 2,
        scratch_shapes=scratch_shapes,
        compiler_params=_params("parallel", "arbitrary"),
        name=name,
    )(x, p3, x, p3, *params)


def gdn_scan(xg, p3, a_log, dt_bias, n_lat):
    state = pltpu.VMEM((2 * GDN_HEADS, GDN_DV, GDN_DK), F32)
    params = (a_log.reshape(1, -1), dt_bias.reshape(1, -1))
    return bidirectional_scan(_gdn_kernel, xg, 0, xg.shape[-1], p3, params, n_lat,
                              GDN_HEADS * GDN_DV, [state], "gdn_scan")


def mlstm_scan(p3, b_if, n_lat):
    n_seq = 2 * ML_HEADS
    w = ML_HEADS * (2 * ML_DK + ML_DV)
    scratch = [pltpu.VMEM((n_seq, ML_DV, ML_DK), F32), pltpu.VMEM((n_seq, 1, ML_DK), F32),
               pltpu.VMEM((n_seq, 1, LANES), F32)]
    params = (b_if[:, 0].reshape(1, -1), b_if[:, 1].reshape(1, -1))
    return bidirectional_scan(_mlstm_kernel, p3, COL_ML // w, w, p3, params, n_lat,
                              ML_HEADS * ML_DV, scratch, "mlstm_scan")


def _gated_head_norm(of_ref, ob_ref, gate, g_ref, n_heads):
    o = of_ref[0] + ob_ref[0]
    d = o.shape[-1] // n_heads
    parts = [_rms(o[:, h * d:(h + 1) * d]) * g_ref[...] for h in range(n_heads)]
    return (jnp.concatenate(parts, axis=-1) * gate).astype(BF16)


def _merge_kernel(h_ref, ya_ref, yc_ref, gof_ref, gob_ref, gg_ref, mof_ref, mob_ref, mg_ref, x_ref, g1_ref,
                  gn_ref, mn_ref, wmg_ref, bmg_ref, wbo_ref, wout_ref, o_ref):
    d = o_ref.shape[-1]
    h = h_ref[0]
    gg = gg_ref[0]
    yb = _gated_head_norm(gof_ref, gob_ref, gg * jax.nn.sigmoid(gg), gn_ref, GDN_HEADS)
    yd = _gated_head_norm(mof_ref, mob_ref, jax.nn.sigmoid(mg_ref[0]), mn_ref, ML_HEADS)
    acc = jnp.zeros((h.shape[0], d), F32)
    for n, y in enumerate((ya_ref[0], yb, yc_ref[0], yd)):
        z = jnp.dot(y, wbo_ref[n], preferred_element_type=F32)
        gate = jnp.dot(h, wmg_ref[:, n * d:(n + 1) * d], preferred_element_type=F32) + bmg_ref[:, n * d:(n + 1) * d]
        acc = acc + jax.nn.sigmoid(gate) * z
    m = jnp.dot(acc.astype(BF16), wout_ref[...], preferred_element_type=F32)
    o_ref[0] = x_ref[0] + g1_ref[0, 0] * m


def merge_residual(h, ya, yc, gdn_o, ml_o, p3, x, mods, n_lat, n_rows, gdn_norm_g, ml_norm_g,
                   w_mg, b_mg, w_bo, w_out, ts=256):
    b, _, d = x.shape
    ts = min(ts, n_rows - n_lat) if n_rows > n_lat else ts
    assert n_rows % ts == 0 and n_lat % ts == 0
    mod_row = _mod_row(n_lat // ts, b)
    row = lambda w, col=0: pl.BlockSpec((1, ts, w), lambda i, j: (i, j, col))
    b_mg = b_mg.reshape(1, -1)
    gn, mn = gdn_norm_g.reshape(1, -1), ml_norm_g.reshape(1, -1)
    return pl.pallas_call(
        _merge_kernel,
        grid=(b, n_rows // ts),
        in_specs=[row(d), row(BRANCH_W), row(BRANCH_W),
                  row(BRANCH_W), row(BRANCH_W), row(BRANCH_W, COL_GDN_GATE // BRANCH_W),
                  row(BRANCH_W), row(BRANCH_W), row(BRANCH_W, COL_ML_GATE // BRANCH_W),
                  row(d), pl.BlockSpec((1, 1, 1, d), lambda i, j: (mod_row(i, j), G1, 0, 0))]
                 + [_full(a) for a in (gn, mn, w_mg, b_mg, w_bo, w_out)],
        out_specs=row(d),
        out_shape=jax.ShapeDtypeStruct((b, n_rows, d), F32),
        compiler_params=_params("parallel", "parallel"),
        name="merge_residual",
    )(h, ya, yc, gdn_o[0], gdn_o[1], p3, ml_o[0], ml_o[1], p3, x, mods, gn, mn, w_mg, b_mg, w_bo, w_out)


def _expert_kernel(be_ref, x_ref, wgu_ref, bgu_ref, wdn_ref, bdn_ref, o_ref, wgu_s, wdn_s):
    i = pl.program_id(0)
    fresh = jnp.logical_or(i == 0, be_ref[i] != be_ref[jnp.maximum(i - 1, 0)])

    @pl.when(fresh)
    def _():
        wgu_s[...] = wgu_ref[0].astype(BF16)
        wdn_s[...] = wdn_ref[0].astype(BF16)

    gu = jnp.dot(x_ref[...], wgu_s[...], preferred_element_type=F32) + bgu_ref[0]
    gate = jnp.minimum(gu[:, :D_FF], SWIGLU_LIMIT)
    up = jnp.clip(gu[:, D_FF:], -SWIGLU_LIMIT, SWIGLU_LIMIT)
    act = (up + 1.0) * gate * jax.nn.sigmoid(SWIGLU_ALPHA * gate)
    y = jnp.dot(act.astype(BF16), wdn_s[...], preferred_element_type=F32) + bdn_ref[0]
    o_ref[...] = y.astype(o_ref.dtype)


def expert_blocks(buf, block_e, w_gu, b_gu, w_dn, b_dn):
    rows, d = buf.shape
    n_blocks = rows // MOE_ROWS
    e = w_gu.shape[0]
    grid_spec = pltpu.PrefetchScalarGridSpec(
        num_scalar_prefetch=1,
        grid=(n_blocks,),
        in_specs=[pl.BlockSpec((MOE_ROWS, d), lambda i, be: (i, 0)),
                  pl.BlockSpec((1, d, 2 * D_FF), lambda i, be: (be[i], 0, 0)),
                  pl.BlockSpec((1, 1, 2 * D_FF), lambda i, be: (be[i], 0, 0)),
                  pl.BlockSpec((1, D_FF, d), lambda i, be: (be[i], 0, 0)),
                  pl.BlockSpec((1, 1, d), lambda i, be: (be[i], 0, 0))],
        out_specs=pl.BlockSpec((MOE_ROWS, d), lambda i, be: (i, 0)),
        scratch_shapes=[pltpu.VMEM((d, 2 * D_FF), BF16), pltpu.VMEM((D_FF, d), BF16)],
    )
    return pl.pallas_call(
        _expert_kernel,
        grid_spec=grid_spec,
        out_shape=jax.ShapeDtypeStruct((rows, d), BF16),
        compiler_params=_params("arbitrary"),
        name="expert_blocks",
    )(block_e, buf, w_gu, b_gu.reshape(e, 1, -1), w_dn, b_dn.reshape(e, 1, -1))


def moe_ffn(t, logits, w_gu, b_gu, w_dn, b_dn):
    n, d = t.shape
    top_v, top_i = lax.top_k(logits, TOP_K)
    top_w = jax.nn.softmax(top_v, axis=-1)
    nk = n * TOP_K
    e_flat = top_i.reshape(nk)
    onehot = (e_flat[:, None] == jnp.arange(N_EXPERTS)[None, :]).astype(jnp.int32)
    csum = jnp.cumsum(onehot, axis=0)
    rank = jnp.take_along_axis(csum, e_flat[:, None], axis=1)[:, 0] - 1
    counts = csum[-1]
    padded = (counts + MOE_ROWS - 1) // MOE_ROWS * MOE_ROWS
    pad_end = jnp.cumsum(padded)
    dest = (pad_end - padded)[e_flat] + rank
    n_blocks = -(-nk // MOE_ROWS) + N_EXPERTS
    tok = jnp.arange(nk, dtype=jnp.int32) // TOP_K
    src = jnp.full((n_blocks * MOE_ROWS,), n, jnp.int32).at[dest].set(tok)
    buf = jnp.concatenate([t, jnp.zeros((1, d), t.dtype)], axis=0)[src]
    block_start = jnp.arange(n_blocks, dtype=pad_end.dtype) * MOE_ROWS
    block_e = jnp.minimum(jnp.sum(pad_end[None, :] <= block_start[:, None], axis=1), N_EXPERTS - 1).astype(jnp.int32)
    y = expert_blocks(buf, block_e, w_gu, b_gu, w_dn, b_dn)
    return y[dest.reshape(n, TOP_K).T.reshape(nk)].reshape(TOP_K, n, d), top_w


def kernel(x, c, ctx, c_ctx, w_ada, b_ada, norm1_g, norm2_g, w_in, na_rpb, gdn_conv_w, gdn_a_log, gdn_dt_bias, gdn_norm_g, mla_q_norm_g, mla_w_uq, mla_kv_norm_g, mla_w_ukv, ml_b_if, ml_norm_g, w_mg, b_mg, w_bo, w_out, router_w, router_b, w_gu, b_gu, w_dn, b_dn, final_g):
    depth = w_ada.shape[0]
    bsz, n_lat, d = x.shape
    n_ctx = ctx.shape[1]
    cond = jnp.concatenate([jax.nn.silu(c), jax.nn.silu(c_ctx)[None]], axis=0)
    cond = jnp.pad(cond, ((0, (-cond.shape[0]) % 8), (0, 0)))
    w_in_r = reorder_w_in(w_in.astype(BF16))
    w_mg_b, w_bo_b, w_out_b = (t.astype(BF16) for t in (w_mg, w_bo, w_out))
    wq, wqr, wkv = mla_weights(mla_w_uq, mla_w_ukv)
    cos, sin = rope_tables(n_lat, n_ctx)
    xs = jnp.concatenate([x, ctx], axis=1)
    n_tok = n_lat + n_ctx
    for l in range(depth):
        last = l == depth - 1
        n_rows = n_lat if last else n_tok
        mods = (pmm(cond, w_ada[l], tn=1024) + b_ada[l]).reshape(cond.shape[0], 6, 1, d)
        h = norm_mod(xs, norm1_g[l], mods, SC1, SH1, n_lat)
        p3 = pmm(h.reshape(bsz * n_tok, d), w_in_r[l], tn=640).reshape(bsz, n_tok, P_WIDTH)
        ya = neighborhood_attention(p3, na_rpb[l], n_lat)
        gdn_o = gdn_scan(gdn_prep(p3, gdn_conv_w[l], n_lat), p3, gdn_a_log[l], gdn_dt_bias[l], n_lat)
        ml_o = mlstm_scan(p3, ml_b_if[l], n_lat)
        q, k, v = mla_prep(p3, cos, sin, mla_q_norm_g[l], mla_kv_norm_g[l], wq[l], wqr[l], wkv[l])
        yc = pair_attention(q, k, v, n_lat, MLA_HEADS, MLA_V)
        if not last:
            na_ctx = p3[:, n_lat:, :3 * NA_HEADS * NA_DH]
            w = NA_HEADS * NA_DH
            ya_ctx = pair_attention((na_ctx[..., :w] * (NA_DH ** -0.5)).astype(BF16), na_ctx[..., w:2 * w].astype(BF16),
                                    na_ctx[..., 2 * w:].astype(BF16), n_ctx, NA_HEADS, NA_DH)
            yc_ctx = pair_attention(q[:, n_lat:], k[:, n_lat:], v[:, n_lat:], n_ctx, MLA_HEADS, MLA_V)
            ya = jnp.concatenate([ya, ya_ctx], axis=1)
            yc = jnp.concatenate([yc, yc_ctx], axis=1)
        xs = merge_residual(h, ya, yc, gdn_o, ml_o, p3, xs, mods, n_lat, n_rows, gdn_norm_g[l], ml_norm_g[l],
                            w_mg_b[l], b_mg[l], w_bo_b[l], w_out_b[l])
        h2, logits = norm_router(xs, norm2_g[l], mods, n_lat, router_w[l], router_b[l])
        y, top_w = moe_ffn(h2.reshape(bsz * n_rows, d), logits.reshape(bsz * n_rows, N_EXPERTS),
                           w_gu[l], b_gu[l], w_dn[l], b_dn[l])
        xs = combine_residual(xs, y.reshape(TOP_K, bsz, n_rows, d), top_w.reshape(bsz, n_rows, TOP_K),
                              mods, G2, n_lat)
    return norm_mod(xs, final_g, jnp.zeros((cond.shape[0], 6, 1, d), F32), SC1, SH1, n_lat, out_dtype=F32)
```

```python
import functools
import math

import numpy as np
import jax
import jax.numpy as jnp
from jax import lax
from jax.experimental import pallas as pl
from jax.experimental.pallas import tpu as pltpu

F32 = jnp.float32
BF16 = jnp.bfloat16

GRID_W = 64
EPS = 1e-6
ROPE_BASE = 10000.0
NA_HEADS, NA_DH, NA_WIN_H, NA_WIN_W = 8, 64, 8, 16
GDN_HEADS, GDN_DK, GDN_DV, GDN_CHUNK = 4, 128, 128, 64
MLA_HEADS, MLA_NOPE, MLA_ROPE, MLA_V, MLA_Q_RANK, MLA_KV_RANK = 8, 64, 32, 64, 256, 128
ML_HEADS, ML_DK, ML_DV, ML_CHUNK = 4, 64, 128, 64
N_BRANCH, BRANCH_W = 4, 512
N_EXPERTS, TOP_K, D_FF = 32, 4, 1024
SWIGLU_ALPHA, SWIGLU_LIMIT = 1.702, 7.0
MOE_ROWS = 256
LANES = 128

IN_SPLITS = (
    NA_HEADS * NA_DH, NA_HEADS * NA_DH, NA_HEADS * NA_DH,
    GDN_HEADS * (2 * GDN_DK + GDN_DV), GDN_HEADS * GDN_DV,
    2 * GDN_HEADS, 2 * GDN_HEADS,
    MLA_Q_RANK, MLA_KV_RANK + MLA_ROPE,
    ML_HEADS * ML_DK, ML_HEADS * ML_DK, ML_HEADS * ML_DV, ML_HEADS * ML_DV,
    2 * ML_HEADS, 2 * ML_HEADS,
)
COL_NA = 0
COL_GDN = 1536
COL_GDN_GATE = 3072
COL_MLA = 3584
COL_ML = 4096
COL_ML_GATE = 5120
COL_SMALL = 5632
P_WIDTH = 6144
MLA_BLOCK = COL_ML - COL_MLA
SH1, SC1, G1, SH2, SC2, G2 = range(6)

VMEM_LIMIT_BYTES = 56 * 1024 * 1024


def _params(*sem):
    return pltpu.CompilerParams(dimension_semantics=sem, vmem_limit_bytes=VMEM_LIMIT_BYTES)


def _full(a):
    return pl.BlockSpec(a.shape, lambda *_: (0,) * a.ndim)


def _dot(a, b, precise=False):
    if precise:
        return jnp.dot(a, b, preferred_element_type=F32, precision=lax.Precision.HIGHEST)
    return jnp.dot(a.astype(BF16), b.astype(BF16), preferred_element_type=F32)


def _dot_nt(a, b):
    return lax.dot_general(a.astype(BF16), b.astype(BF16), (((1,), (1,)), ((), ())), preferred_element_type=F32)


def _dot_tn(a, b):
    return lax.dot_general(a.astype(BF16), b.astype(BF16), (((0,), (0,)), ((), ())), preferred_element_type=F32)


def _softplus(x):
    return jnp.maximum(x, 0.0) + jnp.log(1.0 + jnp.exp(-jnp.abs(x)))


def _rms(x):
    return x * lax.rsqrt(jnp.mean(x * x, axis=-1, keepdims=True) + EPS)


def _mm_kernel(x_ref, w_ref, o_ref):
    o_ref[...] = _dot(x_ref[...], w_ref[...]).astype(o_ref.dtype)


def pmm(x, w, out_dtype=F32, tm=512, tn=512):
    m, k = x.shape
    n = w.shape[1]
    tm, tn = math.gcd(tm, m), min(tn, n)
    if n % tn:
        tn = n
    assert m % tm == 0 and n % tn == 0, (m, n, tm, tn)
    return pl.pallas_call(
        _mm_kernel,
        grid=(m // tm, n // tn),
        in_specs=[pl.BlockSpec((tm, k), lambda i, j: (i, 0)),
                  pl.BlockSpec((k, tn), lambda i, j: (0, j))],
        out_specs=pl.BlockSpec((tm, tn), lambda i, j: (i, j)),
        out_shape=jax.ShapeDtypeStruct((m, n), out_dtype),
        compiler_params=_params("parallel", "parallel"),
        name="pmm",
    )(x, w)


def _cast_kernel(x_ref, o_ref):
    o_ref[...] = x_ref[...].astype(o_ref.dtype)


def cast_bf16(w, rows=256):
    w2 = w.reshape(-1, w.shape[-1])
    m, n = w2.shape
    rows = math.gcd(rows, m)
    blk = pl.BlockSpec((rows, n), lambda i: (i, 0))
    out = pl.pallas_call(_cast_kernel, grid=(m // rows,), in_specs=[blk], out_specs=blk,
                         out_shape=jax.ShapeDtypeStruct((m, n), BF16),
                         compiler_params=_params("parallel"), name="cast_bf16")(w2)
    return out.reshape(w.shape)


W_IN_BLOCK = 512


def _w_in_kernel(head_ref, tail_ref, o_ref, *, n_head_blocks):
    @pl.when(pl.program_id(1) < n_head_blocks)
    def _():
        o_ref[...] = head_ref[...].astype(o_ref.dtype)

    @pl.when(pl.program_id(1) >= n_head_blocks)
    def _():
        o_ref[...] = tail_ref[...].astype(o_ref.dtype)


def reorder_w_in(w_in):
    depth, d, _ = w_in.shape
    o = np.cumsum((0,) + IN_SPLITS)
    assert o[5] == COL_MLA and COL_MLA % W_IN_BLOCK == 0 and P_WIDTH % W_IN_BLOCK == 0
    piece = lambda i: w_in[..., o[i]:o[i + 1]]
    zeros = lambda n: jnp.zeros(w_in.shape[:-1] + (n,), w_in.dtype)
    tail = jnp.concatenate([piece(7), piece(8), zeros(MLA_BLOCK - IN_SPLITS[7] - IN_SPLITS[8]),
                            piece(9), piece(10), piece(11), piece(12),
                            piece(5), piece(6), piece(13), piece(14), zeros(P_WIDTH - COL_SMALL - 8 * GDN_HEADS)],
                           axis=-1)
    assert tail.shape[-1] == P_WIDTH - COL_MLA
    nh_blocks, n_blocks = COL_MLA // W_IN_BLOCK, P_WIDTH // W_IN_BLOCK
    return pl.pallas_call(
        functools.partial(_w_in_kernel, n_head_blocks=nh_blocks),
        grid=(depth, n_blocks),
        in_specs=[pl.BlockSpec((1, d, W_IN_BLOCK), lambda l, j: (l, 0, jnp.minimum(j, nh_blocks - 1))),
                  pl.BlockSpec((1, d, W_IN_BLOCK), lambda l, j: (l, 0, jnp.maximum(j - nh_blocks, 0)))],
        out_specs=pl.BlockSpec((1, d, W_IN_BLOCK), lambda l, j: (l, 0, j)),
        out_shape=jax.ShapeDtypeStruct((depth, d, P_WIDTH), BF16),
        compiler_params=_params("parallel", "arbitrary"),
        name="reorder_w_in",
    )(w_in, tail)


def _mod_row(n_lat_blocks, ctx_row):
    return lambda i, j: jnp.where(j >= n_lat_blocks, ctx_row, i)


def _norm_mod_kernel(x_ref, g_ref, sc_ref, sh_ref, o_ref):
    y = _rms(x_ref[0]) * g_ref[...]
    o_ref[0] = (y * (1.0 + sc_ref[0, 0]) + sh_ref[0, 0]).astype(o_ref.dtype)


def norm_mod(x, g, mods, i_sc, i_sh, n_lat, out_dtype=BF16, ts=256):
    b, t, d = x.shape
    ts = min(ts, t - n_lat) if t > n_lat else ts
    assert t % ts == 0 and n_lat % ts == 0
    row = _mod_row(n_lat // ts, b)
    return pl.pallas_call(
        _norm_mod_kernel,
        grid=(b, t // ts),
        in_specs=[pl.BlockSpec((1, ts, d), lambda i, j: (i, j, 0)),
                  pl.BlockSpec((1, d), lambda i, j: (0, 0)),
                  pl.BlockSpec((1, 1, 1, d), lambda i, j: (row(i, j), i_sc, 0, 0)),
                  pl.BlockSpec((1, 1, 1, d), lambda i, j: (row(i, j), i_sh, 0, 0))],
        out_specs=pl.BlockSpec((1, ts, d), lambda i, j: (i, j, 0)),
        out_shape=jax.ShapeDtypeStruct((b, t, d), out_dtype),
        compiler_params=_params("parallel", "parallel"),
        name="norm_mod",
    )(x, g.reshape(1, d), mods, mods)


def _norm_router_kernel(x_ref, g_ref, sc_ref, sh_ref, rw_ref, rb_ref, h_ref, lg_ref):
    h = (_rms(x_ref[0]) * g_ref[...]) * (1.0 + sc_ref[0, 0]) + sh_ref[0, 0]
    h_ref[0] = h.astype(h_ref.dtype)
    lg_ref[0] = _dot(h, rw_ref[...], precise=True) + rb_ref[...]


def norm_router(x, g, mods, n_lat, router_w, router_b, ts=256):
    b, t, d = x.shape
    ts = min(ts, t - n_lat) if t > n_lat else ts
    assert t % ts == 0 and n_lat % ts == 0
    row = _mod_row(n_lat // ts, b)
    e = router_w.shape[1]
    rb = router_b.reshape(1, e)
    return pl.pallas_call(
        _norm_router_kernel,
        grid=(b, t // ts),
        in_specs=[pl.BlockSpec((1, ts, d), lambda i, j: (i, j, 0)),
                  pl.BlockSpec((1, d), lambda i, j: (0, 0)),
                  pl.BlockSpec((1, 1, 1, d), lambda i, j: (row(i, j), SC2, 0, 0)),
                  pl.BlockSpec((1, 1, 1, d), lambda i, j: (row(i, j), SH2, 0, 0)),
                  _full(router_w), _full(rb)],
        out_specs=[pl.BlockSpec((1, ts, d), lambda i, j: (i, j, 0)),
                   pl.BlockSpec((1, ts, e), lambda i, j: (i, j, 0))],
        out_shape=[jax.ShapeDtypeStruct((b, t, d), BF16), jax.ShapeDtypeStruct((b, t, e), F32)],
        compiler_params=_params("parallel", "parallel"),
        name="norm_router",
    )(x, g.reshape(1, d), mods, mods, router_w, rb)


def _combine_residual_kernel(x_ref, y_ref, w_ref, g_ref, o_ref):
    w = w_ref[0]
    f = functools.reduce(jnp.add, [w[:, k:k + 1] * y_ref[k, 0].astype(F32) for k in range(y_ref.shape[0])])
    o_ref[0] = x_ref[0] + g_ref[0, 0] * f


def combine_residual(x, y, w, mods, i_gate, n_lat, ts=256):
    b, t, d = x.shape
    n_slots = y.shape[0]
    ts = min(ts, t - n_lat) if t > n_lat else ts
    row = _mod_row(n_lat // ts, b)
    blk = pl.BlockSpec((1, ts, d), lambda i, j: (i, j, 0))
    return pl.pallas_call(
        _combine_residual_kernel,
        grid=(b, t // ts),
        in_specs=[blk, pl.BlockSpec((n_slots, 1, ts, d), lambda i, j: (0, i, j, 0)),
                  pl.BlockSpec((1, ts, n_slots), lambda i, j: (i, j, 0)),
                  pl.BlockSpec((1, 1, 1, d), lambda i, j: (row(i, j), i_gate, 0, 0))],
        out_specs=blk,
        out_shape=jax.ShapeDtypeStruct((b, t, d), F32),
        compiler_params=_params("parallel", "parallel"),
        name="combine_residual",
    )(x, y, w, mods)


def _pair_attn_kernel(q_ref, k_ref, v_ref, o_ref, *, tk, dv):
    dq, dvp = q_ref.shape[-1] // 2, v_ref.shape[-1] // 2
    tq = q_ref.shape[1]
    sum_on_mxu = dvp > dv
    outs = []
    for hh in range(2):
        q = q_ref[0][:, hh * dq:(hh + 1) * dq]
        m = jnp.full((tq, 1), -jnp.inf, F32)
        l = jnp.zeros((tq, 1), F32)
        acc = jnp.zeros((tq, dvp), F32)
        for c in range(k_ref.shape[1] // tk):
            k = k_ref[0, c * tk:(c + 1) * tk, hh * dq:(hh + 1) * dq]
            v = v_ref[0, c * tk:(c + 1) * tk, hh * dvp:(hh + 1) * dvp]
            s = lax.dot_general(q, k, (((1,), (1,)), ((), ())), preferred_element_type=F32)
            m_new = jnp.maximum(m, jnp.max(s, axis=-1, keepdims=True))
            alpha = jnp.exp(m - m_new)
            p = jnp.exp(s - m_new)
            if not sum_on_mxu:
                l = alpha * l + jnp.sum(p, axis=-1, keepdims=True)
            acc = alpha * acc + jnp.dot(p.astype(BF16), v, preferred_element_type=F32)
            m = m_new
        outs.append(acc[:, :dv] / (acc[:, dv:dv + 1] if sum_on_mxu else l))
    o_ref[0] = jnp.concatenate(outs, axis=-1).astype(o_ref.dtype)


def pair_attention(q, k, v, n_q, n_heads, dv, tq=512):
    b, t, hq = k.shape
    dq2, dvp2 = 2 * hq // n_heads, 2 * v.shape[-1] // n_heads
    tq = min(tq, n_q)
    tk = next(c for c in (768, 512, 384, 256, 128) if t % c == 0)
    assert n_q % tq == 0 and dq2 % LANES == 0 and dvp2 % LANES == 0 and (2 * dv) % LANES == 0
    return pl.pallas_call(
        functools.partial(_pair_attn_kernel, tk=tk, dv=dv),
        grid=(b, n_heads // 2, n_q // tq),
        in_specs=[pl.BlockSpec((1, tq, dq2), lambda i, hp, n: (i, n, hp)),
                  pl.BlockSpec((1, t, dq2), lambda i, hp, n: (i, 0, hp)),
                  pl.BlockSpec((1, t, dvp2), lambda i, hp, n: (i, 0, hp))],
        out_specs=pl.BlockSpec((1, tq, 2 * dv), lambda i, hp, n: (i, n, hp)),
        out_shape=jax.ShapeDtypeStruct((b, n_q, n_heads * dv), BF16),
        compiler_params=_params("parallel", "parallel", "parallel"),
        name="pair_attention",
    )(q, k, v)


NA_QROWS = 8
NA_KROWS = 4
NA_NEG = -1e30


def na_bias_tables(rpb, rows):
    n_steps = rows // NA_QROWS
    wh, ww = min(NA_WIN_H, rows), NA_WIN_W
    n_kblocks = rows // NA_KROWS
    n_heads = rpb.shape[0]
    width = 2 * GRID_W - 1
    lo = GRID_W - NA_WIN_W
    padded = jnp.full(rpb.shape[:2] + (width,), NA_NEG, F32).at[..., lo:lo + 2 * NA_WIN_W - 1].set(rpb)
    tiles = jnp.stack([padded[..., GRID_W - 1 - qc:width - qc] for qc in range(GRID_W)], axis=2)
    qc, kc = np.arange(GRID_W)[:, None], np.arange(GRID_W)[None, :]
    c0 = np.clip(qc - ww // 2, 0, GRID_W - ww)
    tiles = jnp.where((kc >= c0) & (kc < c0 + ww), tiles, NA_NEG)
    masked = jnp.full((n_heads, GRID_W, GRID_W), NA_NEG, F32)
    tables = []
    for i in (0, min(1, n_steps - 1), n_steps - 1):
        q_tiles = []
        for il in range(NA_QROWS):
            q = NA_QROWS * i + il
            r0 = min(max(q - wh // 2, 0), rows - wh)
            k_tiles = []
            for ka in range(4 * NA_KROWS):
                blk = 2 * i - 1 + ka // NA_KROWS
                kr = NA_KROWS * blk + ka % NA_KROWS
                ok = 0 <= blk < n_kblocks and r0 <= kr < r0 + wh
                k_tiles.append(tiles[:, kr - q + NA_WIN_H - 1] if ok else masked)
            q_tiles.append(jnp.concatenate(k_tiles, axis=-1))
        tables.append(jnp.concatenate(q_tiles, axis=-2))
    return jnp.stack(tables)


def _na_kernel(q_ref, k0_ref, k1_ref, k2_ref, k3_ref, v0_ref, v1_ref, v2_ref, v3_ref,
               kc_ref, vc_ref, bias_ref, o_ref):
    kw = NA_KROWS * GRID_W
    outs = []
    for hh in range(2):
        sl = slice(hh * NA_DH, (hh + 1) * NA_DH)
        q = (q_ref[0][:, sl] * (NA_DH ** -0.5)).astype(BF16)
        ks = [r[0][:, sl].astype(BF16) for r in (k0_ref, k1_ref, k2_ref, k3_ref, kc_ref)]
        vs = [r[0][:, sl].astype(BF16) for r in (v0_ref, v1_ref, v2_ref, v3_ref, vc_ref)]
        ss = [lax.dot_general(q, kk, (((1,), (1,)), ((), ())), preferred_element_type=F32) for kk in ks]
        ss = [ss[j] + bias_ref[0, 0, hh, :, j * kw:(j + 1) * kw] for j in range(4)] + [ss[4]]
        m = functools.reduce(jnp.maximum, [jnp.max(t, axis=-1, keepdims=True) for t in ss])
        ps = [jnp.exp(t - m) for t in ss]
        l = functools.reduce(jnp.add, [jnp.sum(t, axis=-1, keepdims=True) for t in ps])
        o = functools.reduce(jnp.add, [jnp.dot(p.astype(BF16), vv, preferred_element_type=F32)
                                       for p, vv in zip(ps, vs)])
        outs.append(o / l)
    o_ref[0] = jnp.concatenate(outs, axis=-1).astype(o_ref.dtype)


def neighborhood_attention(p3, rpb, n_lat):
    b, t, _ = p3.shape
    c_len = t - n_lat
    rows = n_lat // GRID_W
    tq, tkv = NA_QROWS * GRID_W, NA_KROWS * GRID_W
    assert rows % NA_QROWS == 0 and rows >= 2 * NA_QROWS and n_lat % c_len == 0
    n_steps = rows // NA_QROWS
    n_kblocks = rows // NA_KROWS
    hw = 2 * NA_DH
    n_pairs = NA_HEADS // 2
    bias = na_bias_tables(rpb, rows)
    bias = bias.reshape(3, n_pairs, 2, *bias.shape[2:])

    def view(j, col0):
        return pl.BlockSpec((1, tkv, hw),
                            lambda hp, i, n: (i, jnp.clip(2 * n - 1 + j, 0, n_kblocks - 1), col0 + hp))

    ctx_spec = lambda col0: pl.BlockSpec((1, c_len, hw), lambda hp, i, n: (i, n_lat // c_len, col0 + hp))
    bias_spec = pl.BlockSpec(
        (1, 1, 2, tq, 4 * tkv),
        lambda hp, i, n: (jnp.where(n == 0, 0, jnp.where(n == n_steps - 1, 2, 1)), hp, 0, 0, 0))
    return pl.pallas_call(
        _na_kernel,
        grid=(n_pairs, b, n_steps),
        in_specs=[pl.BlockSpec((1, tq, hw), lambda hp, i, n: (i, n, hp))]
                 + [view(j, n_pairs) for j in range(4)] + [view(j, 2 * n_pairs) for j in range(4)]
                 + [ctx_spec(n_pairs), ctx_spec(2 * n_pairs), bias_spec],
        out_specs=pl.BlockSpec((1, tq, hw), lambda hp, i, n: (i, n, hp)),
        out_shape=jax.ShapeDtypeStruct((b, n_lat, NA_HEADS * NA_DH), BF16),
        compiler_params=_params("parallel", "parallel", "arbitrary"),
        name="neighborhood_attention",
    )(*([p3] * 11), bias)


def rope_tables(n_lat, n_ctx):
    half = MLA_ROPE // 2
    tpos = np.arange(n_lat)
    inv = ROPE_BASE ** (-jnp.arange(0, half, 2, dtype=F32) / half)
    pos = jnp.stack([tpos // GRID_W, tpos % GRID_W], axis=1).astype(F32)
    ang = pos[:, :, None] * inv
    ang = jnp.concatenate([ang, ang], axis=-1).reshape(n_lat, MLA_ROPE)
    pad = ((0, n_ctx), (MLA_NOPE, LANES - MLA_NOPE - MLA_ROPE))
    return (jnp.pad(jnp.cos(ang), pad, constant_values=1.0), jnp.pad(jnp.sin(ang), pad))


def _rope_swap():
    half = MLA_ROPE // 2
    quarter = half // 2
    r = np.zeros((MLA_ROPE, MLA_ROPE), np.float32)
    for j in range(MLA_ROPE):
        if j % half < quarter:
            r[j + quarter, j] = -1.0
        else:
            r[j - quarter, j] = 1.0
    return r


def mla_weights(w_uq, w_ukv):
    dqh = MLA_NOPE + MLA_ROPE
    lead = w_uq.shape[:-1]
    wq = w_uq.reshape(*lead, MLA_HEADS, dqh)
    padh = lambda a: jnp.pad(a, ((0, 0),) * (a.ndim - 1) + ((0, LANES - a.shape[-1]),))
    wq_p = padh(wq).reshape(*lead, MLA_HEADS * LANES)
    rot = jnp.einsum('...hr,rs->...hs', wq[..., MLA_NOPE:], jnp.asarray(_rope_swap()))
    wqr_p = jnp.pad(rot, ((0, 0),) * (rot.ndim - 1) + ((MLA_NOPE, LANES - dqh),)).reshape(*lead, MLA_HEADS * LANES)
    lead = w_ukv.shape[:-1]
    wkv = w_ukv.reshape(*lead, MLA_HEADS, MLA_NOPE + MLA_V)
    wk_p = padh(wkv[..., :MLA_NOPE]).reshape(*lead, MLA_HEADS * LANES)
    wv_p = padh(wkv[..., MLA_NOPE:]).reshape(*lead, MLA_HEADS * LANES)
    return wq_p.astype(BF16), wqr_p.astype(BF16), jnp.concatenate([wk_p, wv_p], axis=-1).astype(BF16)


def _mla_prep_kernel(p_ref, cos_ref, sin_ref, qg_ref, kvg_ref, wq_ref, wqr_ref, wkv_ref, e_ref, er_ref,
                     q_out, k_out, v_out):
    blk = p_ref[0]
    nq = _rms(blk[:, :MLA_Q_RANK]) * qg_ref[...]
    nkv = _rms(blk[:, MLA_Q_RANK:MLA_Q_RANK + MLA_KV_RANK]) * kvg_ref[...]
    kpe = blk[:, MLA_Q_RANK + MLA_KV_RANK:MLA_Q_RANK + MLA_KV_RANK + MLA_ROPE]
    cos, sin = cos_ref[...], sin_ref[...]
    qf, qr = _dot(nq, wq_ref[...]), _dot(nq, wqr_ref[...])
    kv = _dot(nkv, wkv_ref[...])
    kpe_tile = _dot(kpe, e_ref[...], precise=True) * cos + _dot(kpe, er_ref[...], precise=True) * sin
    scale = (MLA_NOPE + MLA_ROPE) ** -0.5
    for h in range(MLA_HEADS):
        sl = slice(h * LANES, (h + 1) * LANES)
        q_out[0, :, sl] = ((qf[:, sl] * cos + qr[:, sl] * sin) * scale).astype(q_out.dtype)
        k_out[0, :, sl] = (kv[:, sl] + kpe_tile).astype(k_out.dtype)
        ones_col = (lax.broadcasted_iota(jnp.int32, (1, LANES), 1) == MLA_V).astype(F32)
        v_out[0, :, sl] = (kv[:, MLA_HEADS * LANES + h * LANES:MLA_HEADS * LANES + (h + 1) * LANES]
                           + ones_col).astype(v_out.dtype)


def mla_prep(p3, cos, sin, q_norm_g, kv_norm_g, wq, wqr, wkv, tb=256):
    b, t, _ = p3.shape
    tb = next(c for c in (tb, 128) if t % c == 0)
    swap = _rope_swap()
    place = np.zeros((MLA_ROPE, LANES), np.float32)
    place[np.arange(MLA_ROPE), MLA_NOPE + np.arange(MLA_ROPE)] = 1.0
    e, er = jnp.asarray(place), jnp.asarray(swap @ place)
    qg, kvg = q_norm_g.reshape(1, -1), kv_norm_g.reshape(1, -1)
    row = lambda w: pl.BlockSpec((1, tb, w), lambda i, j: (i, j, 0))
    tab = pl.BlockSpec((tb, LANES), lambda i, j: (j, 0))
    return pl.pallas_call(
        _mla_prep_kernel,
        grid=(b, t // tb),
        in_specs=[pl.BlockSpec((1, tb, MLA_BLOCK), lambda i, j: (i, j, COL_MLA // MLA_BLOCK)), tab, tab]
                 + [_full(a) for a in (qg, kvg, wq, wqr, wkv, e, er)],
        out_specs=[row(MLA_HEADS * LANES)] * 3,
        out_shape=[jax.ShapeDtypeStruct((b, t, MLA_HEADS * LANES), BF16),
                   jax.ShapeDtypeStruct((b, t, MLA_HEADS * LANES), BF16),
                   jax.ShapeDtypeStruct((b, t, MLA_HEADS * LANES), BF16)],
        compiler_params=_params("parallel", "parallel"),
        name="mla_prep",
    )(p3, cos, sin, qg, kvg, wq, wqr, wkv, e, er)


def _gdn_prep_kernel(x_ref, prev_ref, next_ref, w_ref, o_ref, *, n_lat, n_tok):
    tb = x_ref.shape[1]
    start = pl.program_id(1) * tb
    x = x_ref[0]
    seg_first = jnp.logical_or(start == 0, start == n_lat)
    seg_last = jnp.logical_or(start + tb == n_lat, start + tb == n_tok)
    prev_row = jnp.where(seg_first, 0.0, prev_ref[0][7:8, :])
    next_row = jnp.where(seg_last, 0.0, next_ref[0][0:1, :])
    rows = lax.broadcasted_iota(jnp.int32, (tb, 1), 0)
    x_m1 = jnp.where(rows == 0, prev_row, pltpu.roll(x, 1, axis=0))
    x_p1 = jnp.where(rows == tb - 1, next_row, pltpu.roll(x, tb - 1, axis=0))
    y = x_m1 * w_ref[0:1, :] + x * w_ref[1:2, :] + x_p1 * w_ref[2:3, :]
    y = y * jax.nn.sigmoid(y)
    for g in range(2 * GDN_HEADS):
        sl = slice(g * GDN_DK, (g + 1) * GDN_DK)
        yh = y[:, sl]
        yh = yh * lax.rsqrt(jnp.sum(yh * yh, axis=-1, keepdims=True) + EPS)
        o_ref[0, :, sl] = yh * (GDN_DK ** -0.5) if g < GDN_HEADS else yh
    o_ref[0, :, 2 * GDN_HEADS * GDN_DK:] = y[:, 2 * GDN_HEADS * GDN_DK:]


def gdn_prep(p3, conv_w, n_lat, tb=256):
    b, t, _ = p3.shape
    w = GDN_HEADS * (2 * GDN_DK + GDN_DV)
    tb = min(tb, t - n_lat)
    assert t % tb == 0 and n_lat % tb == 0 and tb % 8 == 0 and COL_GDN % w == 0
    col = COL_GDN // w
    sub = tb // 8
    return pl.pallas_call(
        functools.partial(_gdn_prep_kernel, n_lat=n_lat, n_tok=t),
        grid=(b, t // tb),
        in_specs=[pl.BlockSpec((1, tb, w), lambda i, j: (i, j, col)),
                  pl.BlockSpec((1, 8, w), lambda i, j: (i, jnp.maximum(j * sub - 1, 0), col)),
                  pl.BlockSpec((1, 8, w), lambda i, j: (i, jnp.minimum((j + 1) * sub, t // 8 - 1), col)),
                  _full(conv_w)],
        out_specs=pl.BlockSpec((1, tb, w), lambda i, j: (i, j, 0)),
        out_shape=jax.ShapeDtypeStruct((b, t, w), F32),
        compiler_params=_params("parallel", "parallel"),
        name="gdn_prep",
    )(p3, p3, p3, conv_w)


def _order_masks(length, reverse):
    r = lax.broadcasted_iota(jnp.int32, (length, length), 0)
    c = lax.broadcasted_iota(jnp.int32, (length, length), 1)
    if reverse:
        return r <= c, r < c, r >= c
    return r >= c, r > c, r <= c


def _as_row(col, mask):
    return jnp.sum(jnp.where(mask, col, 0.0), axis=0, keepdims=True)


def _gdn_kernel(xf_ref, sf_ref, xb_ref, sb_ref, alog_ref, dtb_ref, of_ref, ob_ref, s_ref):
    @pl.when(pl.program_id(1) == 0)
    def _():
        s_ref[...] = jnp.zeros_like(s_ref)

    nh, dk, dv, length = GDN_HEADS, GDN_DK, GDN_DV, GDN_CHUNK
    n_sub = xf_ref.shape[1] // length
    r = lax.broadcasted_iota(jnp.int32, (length, length), 0)
    c = lax.broadcasted_iota(jnp.int32, (length, length), 1)
    eye = (r == c).astype(F32)
    chunks = []
    for sub in range(n_sub):
        seqs = []
        for d, (x_ref, sm_ref, o_ref) in enumerate(((xf_ref, sf_ref, of_ref), (xb_ref, sb_ref, ob_ref))):
            r0 = (n_sub - 1 - sub if d == 1 else sub) * length
            rows = slice(r0, r0 + length)
            incl, strict, incl_t = _order_masks(length, d == 1)
            small = sm_ref[0, rows, :]
            hs = slice(d * nh, (d + 1) * nh)
            la_col = -jnp.exp(alog_ref[:, hs]) * _softplus(small[:, hs] + dtb_ref[:, hs])
            be_col = jax.nn.sigmoid(small[:, 2 * nh + d * nh:2 * nh + (d + 1) * nh])
            g_cols = _dot(incl.astype(F32), la_col, precise=True)
            g_ends = jnp.sum(la_col, axis=0, keepdims=True)
            for h in range(nh):
                q = x_ref[0, rows, h * dk:(h + 1) * dk]
                k = x_ref[0, rows, (nh + h) * dk:(nh + h + 1) * dk]
                v = x_ref[0, rows, 2 * nh * dk + h * dv:2 * nh * dk + (h + 1) * dv]
                gc, ge, bc = g_cols[:, h:h + 1], g_ends[:, h:h + 1], be_col[:, h:h + 1]
                gr = _as_row(la_col[:, h:h + 1], incl_t)
                decay = jnp.exp(jnp.where(incl, gc - gr, -jnp.inf))
                qk = decay * _dot_nt(q, k)
                p = -(bc * jnp.where(strict, decay, 0.0) * _dot_nt(k, k))
                eg = jnp.exp(gc)
                seqs.append(dict(o_ref=o_ref, rows=rows, h=h, idx=d * nh + h, q_dec=eg * q, qk=qk, p=p, t=eye + p,
                                 rhs=jnp.concatenate([bc * v, (bc * eg) * k], axis=-1),
                                 k_end=jnp.exp(ge - gc) * k, e_end=jnp.exp(ge)))
        for _ in range(int(math.log2(length)) - 1):
            for s in seqs:
                s['p'] = _dot(s['p'], s['p'])
            for s in seqs:
                s['t'] = s['t'] + _dot(s['t'], s['p'])
        for s in seqs:
            s['sol'] = _dot(s['t'], s['rhs'])
        chunks.append(seqs)
    states = [s_ref[i] for i in range(2 * nh)]
    for seqs in chunks:
        for s in seqs:
            s['u'] = s['sol'][:, :dv] - _dot_nt(s['sol'][:, dv:], states[s['idx']])
        for s in seqs:
            h, s_old = s['h'], states[s['idx']]
            s['o_ref'][0, s['rows'], h * dv:(h + 1) * dv] = _dot_nt(s['q_dec'], s_old) + _dot(s['qk'], s['u'])
            states[s['idx']] = s['e_end'] * s_old + _dot_tn(s['u'], s['k_end'])
    for i, st in enumerate(states):
        s_ref[i] = st


def _mlstm_kernel(xf_ref, sf_ref, xb_ref, sb_ref, bi_ref, bf_ref, of_ref, ob_ref, c_ref, n_ref, m_ref):
    @pl.when(pl.program_id(1) == 0)
    def _():
        c_ref[...] = jnp.zeros_like(c_ref)
        n_ref[...] = jnp.zeros_like(n_ref)
        m_ref[...] = jnp.zeros_like(m_ref)

    nh, dk, dv, length = ML_HEADS, ML_DK, ML_DV, ML_CHUNK
    n_sub = xf_ref.shape[1] // length
    r = lax.broadcasted_iota(jnp.int32, (length, length), 0)
    c = lax.broadcasted_iota(jnp.int32, (length, length), 1)
    eye = r == c
    col_i, col_f = 4 * GDN_HEADS, 4 * GDN_HEADS + 2 * nh
    chunks = []
    for sub in range(n_sub):
        seqs = []
        for d, (x_ref, sm_ref, o_ref) in enumerate(((xf_ref, sf_ref, of_ref), (xb_ref, sb_ref, ob_ref))):
            r0 = (n_sub - 1 - sub if d == 1 else sub) * length
            rows = slice(r0, r0 + length)
            incl, _, incl_t = _order_masks(length, d == 1)
            small = sm_ref[0, rows, :]
            hs = slice(d * nh, (d + 1) * nh)
            i_col = small[:, col_i + d * nh:col_i + (d + 1) * nh] + bi_ref[:, hs]
            f_col = -_softplus(-(small[:, col_f + d * nh:col_f + (d + 1) * nh] + bf_ref[:, hs]))
            fc_cols = _dot(incl.astype(F32), f_col, precise=True)
            f_ends = jnp.sum(f_col, axis=0, keepdims=True)
            for h in range(nh):
                q = x_ref[0, rows, h * dk:(h + 1) * dk] * (dk ** -0.5)
                k = x_ref[0, rows, (nh + h) * dk:(nh + h + 1) * dk]
                v = x_ref[0, rows, 2 * nh * dk + h * dv:2 * nh * dk + (h + 1) * dv]
                fc, fe, ic = fc_cols[:, h:h + 1], f_ends[:, h:h + 1], i_col[:, h:h + 1]
                fr, ir = _as_row(f_col[:, h:h + 1], incl_t), _as_row(ic, eye)
                d_log = jnp.where(incl, fc - fr + ir, -jnp.inf)
                d_max = jnp.max(d_log, axis=-1, keepdims=True)
                w_log = fe - fc + ic
                w_max = jnp.max(w_log, axis=0, keepdims=True)
                seqs.append(dict(o_ref=o_ref, rows=rows, h=h, idx=d * nh + h, q=q, k=k, v=v, fc=fc, fe=fe,
                                 d_max=d_max, w_max=w_max, decay=jnp.exp(d_log - d_max), wk=jnp.exp(w_log - w_max)))
        for s in seqs:
            s['intra'] = s['decay'] * _dot_nt(s['q'], s['k'])
        for s in seqs:
            s['intra_v'] = _dot(s['intra'], s['v'])
            s['vk'] = _dot_tn(s['wk'] * s['v'], s['k'])
        for s in seqs:
            s['intra_sum'] = jnp.sum(s['intra'], axis=-1, keepdims=True)
            s['k_sum'] = jnp.sum(s['wk'] * s['k'], axis=0, keepdims=True)
        chunks.append(seqs)
    n_seq = 2 * nh
    cs = [c_ref[i] for i in range(n_seq)]
    ns = [n_ref[i] for i in range(n_seq)]
    ms = [m_ref[i][:, 0:1] for i in range(n_seq)]
    for seqs in chunks:
        for s in seqs:
            m_old = ms[s['idx']]
            s['m_q'] = jnp.maximum(s['fc'] + m_old, s['d_max'])
            s['inter'], s['scale'] = jnp.exp(s['fc'] + m_old - s['m_q']), jnp.exp(s['d_max'] - s['m_q'])
            s['m_new'] = jnp.maximum(s['fe'] + m_old, s['w_max'])
            s['dec'], s['w_scale'] = jnp.exp(s['fe'] + m_old - s['m_new']), jnp.exp(s['w_max'] - s['m_new'])
        for s in seqs:
            s['qc'] = _dot_nt(s['q'], cs[s['idx']])
        for s in seqs:
            h, idx = s['h'], s['idx']
            num = s['inter'] * s['qc'] + s['scale'] * s['intra_v']
            den = (s['inter'] * jnp.sum(s['q'] * ns[idx], axis=-1, keepdims=True) + s['scale'] * s['intra_sum'])
            s['o_ref'][0, s['rows'], h * dv:(h + 1) * dv] = num / jnp.maximum(jnp.abs(den), jnp.exp(-s['m_q']))
            cs[idx] = s['dec'] * cs[idx] + s['w_scale'] * s['vk']
            ns[idx] = s['dec'] * ns[idx] + s['w_scale'] * s['k_sum']
            ms[idx] = s['m_new']
    for i in range(n_seq):
        c_ref[i], n_ref[i] = cs[i], ns[i]
        m_ref[i] = jnp.broadcast_to(ms[i], m_ref.shape[1:])


SCAN_BLOCK = 128


def bidirectional_scan(kernel_fn, x, x_col, x_w, p3, params, n_lat, out_w, scratch_shapes, name):
    b, t, _ = p3.shape
    length = SCAN_BLOCK
    nl, nc = n_lat // length, t // length
    ncx = nc - nl
    assert t % length == 0 and n_lat % length == 0
    fwd = lambda n: jnp.where(n < ncx, nl + n, n - ncx)
    bwd = lambda n: jnp.where(n < ncx, nc - 1 - n, nl - 1 - (n - ncx))
    specs = lambda f: [pl.BlockSpec((1, length, x_w), lambda i, n: (i, f(n), x_col)),
                       pl.BlockSpec((1, length, LANES), lambda i, n: (i, f(n), COL_SMALL // LANES))]
    out = lambda f: pl.BlockSpec((1, length, out_w), lambda i, n: (i, f(n), 0))
    return pl.pallas_call(
        kernel_fn,
        grid=(b, nc),
        in_specs=specs(fwd) + specs(bwd) + [_full(a) for a in params],
        out_specs=[out(fwd), out(bwd)],
        out_shape=[jax.ShapeDtypeStruct((b, t, out_w), F32)] * 2,
        scratch_shapes=scratch_shapes,
        compiler_params=_params("parallel", "arbitrary"),
        name=name,
    )(x, p3, x, p3, *params)


def gdn_scan(xg, p3, a_log, dt_bias, n_lat):
    state = pltpu.VMEM((2 * GDN_HEADS, GDN_DV, GDN_DK), F32)
    params = (a_log.reshape(1, -1), dt_bias.reshape(1, -1))
    return bidirectional_scan(_gdn_kernel, xg, 0, xg.shape[-1], p3, params, n_lat,
                              GDN_HEADS * GDN_DV, [state], "gdn_scan")


def mlstm_scan(p3, b_if, n_lat):
    n_seq = 2 * ML_HEADS
    w = ML_HEADS * (2 * ML_DK + ML_DV)
    scratch = [pltpu.VMEM((n_seq, ML_DV, ML_DK), F32), pltpu.VMEM((n_seq, 1, ML_DK), F32),
               pltpu.VMEM((n_seq, 1, LANES), F32)]
    params = (b_if[:, 0].reshape(1, -1), b_if[:, 1].reshape(1, -1))
    return bidirectional_scan(_mlstm_kernel, p3, COL_ML // w, w, p3, params, n_lat,
                              ML_HEADS * ML_DV, scratch, "mlstm_scan")


def _gated_head_norm(of_ref, ob_ref, gate, g_ref, n_heads):
    o = of_ref[0] + ob_ref[0]
    d = o.shape[-1] // n_heads
    parts = [_rms(o[:, h * d:(h + 1) * d]) * g_ref[...] for h in range(n_heads)]
    return (jnp.concatenate(parts, axis=-1) * gate).astype(BF16)


def _merge_kernel(h_ref, ya_ref, yc_ref, gof_ref, gob_ref, gg_ref, mof_ref, mob_ref, mg_ref, x_ref, g1_ref,
                  gn_ref, mn_ref, wmg_ref, bmg_ref, wbo_ref, wout_ref, o_ref):
    d = o_ref.shape[-1]
    h = h_ref[0]
    gg = gg_ref[0]
    yb = _gated_head_norm(gof_ref, gob_ref, gg * jax.nn.sigmoid(gg), gn_ref, GDN_HEADS)
    yd = _gated_head_norm(mof_ref, mob_ref, jax.nn.sigmoid(mg_ref[0]), mn_ref, ML_HEADS)
    acc = jnp.zeros((h.shape[0], d), F32)
    for n, y in enumerate((ya_ref[0], yb, yc_ref[0], yd)):
        z = jnp.dot(y, wbo_ref[n], preferred_element_type=F32)
        gate = jnp.dot(h, wmg_ref[:, n * d:(n + 1) * d], preferred_element_type=F32) + bmg_ref[:, n * d:(n + 1) * d]
        acc = acc + jax.nn.sigmoid(gate) * z
    m = jnp.dot(acc.astype(BF16), wout_ref[...], preferred_element_type=F32)
    o_ref[0] = x_ref[0] + g1_ref[0, 0] * m


def merge_residual(h, ya, yc, gdn_o, ml_o, p3, x, mods, n_lat, n_rows, gdn_norm_g, ml_norm_g,
                   w_mg, b_mg, w_bo, w_out, ts=256):
    b, _, d = x.shape
    ts = min(ts, n_rows - n_lat) if n_rows > n_lat else ts
    assert n_rows % ts == 0 and n_lat % ts == 0
    mod_row = _mod_row(n_lat // ts, b)
    row = lambda w, col=0: pl.BlockSpec((1, ts, w), lambda i, j: (i, j, col))
    b_mg = b_mg.reshape(1, -1)
    gn, mn = gdn_norm_g.reshape(1, -1), ml_norm_g.reshape(1, -1)
    return pl.pallas_call(
        _merge_kernel,
        grid=(b, n_rows // ts),
        in_specs=[row(d), row(BRANCH_W), row(BRANCH_W),
                  row(BRANCH_W), row(BRANCH_W), row(BRANCH_W, COL_GDN_GATE // BRANCH_W),
                  row(BRANCH_W), row(BRANCH_W), row(BRANCH_W, COL_ML_GATE // BRANCH_W),
                  row(d), pl.BlockSpec((1, 1, 1, d), lambda i, j: (mod_row(i, j), G1, 0, 0))]
                 + [_full(a) for a in (gn, mn, w_mg, b_mg, w_bo, w_out)],
        out_specs=row(d),
        out_shape=jax.ShapeDtypeStruct((b, n_rows, d), F32),
        compiler_params=_params("parallel", "parallel"),
        name="merge_residual",
    )(h, ya, yc, gdn_o[0], gdn_o[1], p3, ml_o[0], ml_o[1], p3, x, mods, gn, mn, w_mg, b_mg, w_bo, w_out)


def _expert_kernel(be_ref, x_ref, wgu_ref, bgu_ref, wdn_ref, bdn_ref, o_ref, wgu_s, wdn_s):
    i = pl.program_id(0)
    fresh = jnp.logical_or(i == 0, be_ref[i] != be_ref[jnp.maximum(i - 1, 0)])

    @pl.when(fresh)
    def _():
        wgu_s[...] = wgu_ref[0].astype(BF16)
        wdn_s[...] = wdn_ref[0].astype(BF16)

    gu = jnp.dot(x_ref[...], wgu_s[...], preferred_element_type=F32) + bgu_ref[0]
    gate = jnp.minimum(gu[:, :D_FF], SWIGLU_LIMIT)
    up = jnp.clip(gu[:, D_FF:], -SWIGLU_LIMIT, SWIGLU_LIMIT)
    act = (up + 1.0) * gate * jax.nn.sigmoid(SWIGLU_ALPHA * gate)
    y = jnp.dot(act.astype(BF16), wdn_s[...], preferred_element_type=F32) + bdn_ref[0]
    o_ref[...] = y.astype(o_ref.dtype)


def expert_blocks(buf, block_e, w_gu, b_gu, w_dn, b_dn):
    rows, d = buf.shape
    n_blocks = rows // MOE_ROWS
    e = w_gu.shape[0]
    grid_spec = pltpu.PrefetchScalarGridSpec(
        num_scalar_prefetch=1,
        grid=(n_blocks,),
        in_specs=[pl.BlockSpec((MOE_ROWS, d), lambda i, be: (i, 0)),
                  pl.BlockSpec((1, d, 2 * D_FF), lambda i, be: (be[i], 0, 0)),
                  pl.BlockSpec((1, 1, 2 * D_FF), lambda i, be: (be[i], 0, 0)),
                  pl.BlockSpec((1, D_FF, d), lambda i, be: (be[i], 0, 0)),
                  pl.BlockSpec((1, 1, d), lambda i, be: (be[i], 0, 0))],
        out_specs=pl.BlockSpec((MOE_ROWS, d), lambda i, be: (i, 0)),
        scratch_shapes=[pltpu.VMEM((d, 2 * D_FF), BF16), pltpu.VMEM((D_FF, d), BF16)],
    )
    return pl.pallas_call(
        _expert_kernel,
        grid_spec=grid_spec,
        out_shape=jax.ShapeDtypeStruct((rows, d), BF16),
        compiler_params=_params("arbitrary"),
        name="expert_blocks",
    )(block_e, buf, w_gu, b_gu.reshape(e, 1, -1), w_dn, b_dn.reshape(e, 1, -1))


def moe_ffn(t, logits, w_gu, b_gu, w_dn, b_dn):
    n, d = t.shape
    top_v, top_i = lax.top_k(logits, TOP_K)
    top_w = jax.nn.softmax(top_v, axis=-1)
    nk = n * TOP_K
    e_flat = top_i.reshape(nk)
    onehot = (e_flat[:, None] == jnp.arange(N_EXPERTS)[None, :]).astype(jnp.int32)
    csum = jnp.cumsum(onehot, axis=0)
    rank = jnp.take_along_axis(csum, e_flat[:, None], axis=1)[:, 0] - 1
    counts = csum[-1]
    padded = (counts + MOE_ROWS - 1) // MOE_ROWS * MOE_ROWS
    pad_end = jnp.cumsum(padded)
    dest = (pad_end - padded)[e_flat] + rank
    n_blocks = -(-nk // MOE_ROWS) + N_EXPERTS
    tok = jnp.arange(nk, dtype=jnp.int32) // TOP_K
    src = jnp.full((n_blocks * MOE_ROWS,), n, jnp.int32).at[dest].set(tok)
    buf = jnp.concatenate([t, jnp.zeros((1, d), t.dtype)], axis=0)[src]
    block_start = jnp.arange(n_blocks, dtype=pad_end.dtype) * MOE_ROWS
    block_e = jnp.minimum(jnp.sum(pad_end[None, :] <= block_start[:, None], axis=1), N_EXPERTS - 1).astype(jnp.int32)
    y = expert_blocks(buf, block_e, w_gu, b_gu, w_dn, b_dn)
    return y[dest.reshape(n, TOP_K).T.reshape(nk)].reshape(TOP_K, n, d), top_w


def kernel(x, c, ctx, c_ctx, w_ada, b_ada, norm1_g, norm2_g, w_in, na_rpb, gdn_conv_w, gdn_a_log, gdn_dt_bias, gdn_norm_g, mla_q_norm_g, mla_w_uq, mla_kv_norm_g, mla_w_ukv, ml_b_if, ml_norm_g, w_mg, b_mg, w_bo, w_out, router_w, router_b, w_gu, b_gu, w_dn, b_dn, final_g):
    depth = w_ada.shape[0]
    bsz, n_lat, d = x.shape
    n_ctx = ctx.shape[1]
    cond = jnp.concatenate([jax.nn.silu(c), jax.nn.silu(c_ctx)[None]], axis=0)
    cond = jnp.pad(cond, ((0, (-cond.shape[0]) % 8), (0, 0)))
    w_in_r = reorder_w_in(w_in)
    w_mg_b, w_bo_b, w_out_b = (cast_bf16(t) for t in (w_mg, w_bo, w_out))
    wq, wqr, wkv = mla_weights(mla_w_uq, mla_w_ukv)
    cos, sin = rope_tables(n_lat, n_ctx)
    xs = jnp.concatenate([x, ctx], axis=1)
    n_tok = n_lat + n_ctx
    for l in range(depth):
        last = l == depth - 1
        n_rows = n_lat if last else n_tok
        mods = (pmm(cond, w_ada[l], tn=1024) + b_ada[l]).reshape(cond.shape[0], 6, 1, d)
        h = norm_mod(xs, norm1_g[l], mods, SC1, SH1, n_lat)
        p3 = pmm(h.reshape(bsz * n_tok, d), w_in_r[l], tm=1024, tn=768).reshape(bsz, n_tok, P_WIDTH)
        ya = neighborhood_attention(p3, na_rpb[l], n_lat)
        gdn_o = gdn_scan(gdn_prep(p3, gdn_conv_w[l], n_lat), p3, gdn_a_log[l], gdn_dt_bias[l], n_lat)
        ml_o = mlstm_scan(p3, ml_b_if[l], n_lat)
        q, k, v = mla_prep(p3, cos, sin, mla_q_norm_g[l], mla_kv_norm_g[l], wq[l], wqr[l], wkv[l])
        yc = pair_attention(q, k, v, n_lat, MLA_HEADS, MLA_V)
        if not last:
            na_ctx = p3[:, n_lat:, :3 * NA_HEADS * NA_DH]
            w = NA_HEADS * NA_DH
            ya_ctx = pair_attention((na_ctx[..., :w] * (NA_DH ** -0.5)).astype(BF16), na_ctx[..., w:2 * w].astype(BF16),
                                    na_ctx[..., 2 * w:].astype(BF16), n_ctx, NA_HEADS, NA_DH)
            yc_ctx = pair_attention(q[:, n_lat:], k[:, n_lat:], v[:, n_lat:], n_ctx, MLA_HEADS, MLA_V)
            ya = jnp.concatenate([ya, ya_ctx], axis=1)
            yc = jnp.concatenate([yc, yc_ctx], axis=1)
        xs = merge_residual(h, ya, yc, gdn_o, ml_o, p3, xs, mods, n_lat, n_rows, gdn_norm_g[l], ml_norm_g[l],
                            w_mg_b[l], b_mg[l], w_bo_b[l], w_out_b[l])
        h2, logits = norm_router(xs, norm2_g[l], mods, n_lat, router_w[l], router_b[l])
        y, top_w = moe_ffn(h2.reshape(bsz * n_rows, d), logits.reshape(bsz * n_rows, N_EXPERTS),
                           w_gu[l], b_gu[l], w_dn[l], b_dn[l])
        xs = combine_residual(xs, y.reshape(TOP_K, bsz, n_rows, d), top_w.reshape(bsz, n_rows, TOP_K),
                              mods, G2, n_lat)
    return norm_mod(xs, final_g, jnp.zeros((cond.shape[0], 6, 1, d), F32), SC1, SH1, n_lat, out_dtype=F32)
```

```python
import functools
import math

import numpy as np
import jax
import jax.numpy as jnp
from jax import lax
from jax.experimental import pallas as pl
from jax.experimental.pallas import tpu as pltpu

F32 = jnp.float32
BF16 = jnp.bfloat16

GRID_W = 64
EPS = 1e-6
ROPE_BASE = 10000.0
NA_HEADS, NA_DH, NA_WIN_H, NA_WIN_W = 8, 64, 8, 16
GDN_HEADS, GDN_DK, GDN_DV, GDN_CHUNK = 4, 128, 128, 64
MLA_HEADS, MLA_NOPE, MLA_ROPE, MLA_V, MLA_Q_RANK, MLA_KV_RANK = 8, 64, 32, 64, 256, 128
ML_HEADS, ML_DK, ML_DV, ML_CHUNK = 4, 64, 128, 64
N_BRANCH, BRANCH_W = 4, 512
N_EXPERTS, TOP_K, D_FF = 32, 4, 1024
SWIGLU_ALPHA, SWIGLU_LIMIT = 1.702, 7.0
MOE_ROWS = 256
LANES = 128

IN_SPLITS = (
    NA_HEADS * NA_DH, NA_HEADS * NA_DH, NA_HEADS * NA_DH,
    GDN_HEADS * (2 * GDN_DK + GDN_DV), GDN_HEADS * GDN_DV,
    2 * GDN_HEADS, 2 * GDN_HEADS,
    MLA_Q_RANK, MLA_KV_RANK + MLA_ROPE,
    ML_HEADS * ML_DK, ML_HEADS * ML_DK, ML_HEADS * ML_DV, ML_HEADS * ML_DV,
    2 * ML_HEADS, 2 * ML_HEADS,
)
COL_NA = 0
COL_GDN = 1536
COL_GDN_GATE = 3072
COL_MLA = 3584
COL_ML = 4096
COL_ML_GATE = 5120
COL_SMALL = 5632
P_WIDTH = 6144
MLA_BLOCK = COL_ML - COL_MLA
SH1, SC1, G1, SH2, SC2, G2 = range(6)

VMEM_LIMIT_BYTES = 56 * 1024 * 1024


def _params(*sem):
    return pltpu.CompilerParams(dimension_semantics=sem, vmem_limit_bytes=VMEM_LIMIT_BYTES)


def _full(a):
    return pl.BlockSpec(a.shape, lambda *_: (0,) * a.ndim)


def _layer(a, layer):
    return pl.BlockSpec((None,) + a.shape[1:], lambda *_: (layer,) + (0,) * (a.ndim - 1))


def _dot(a, b, precise=False):
    if precise:
        return jnp.dot(a, b, preferred_element_type=F32, precision=lax.Precision.HIGHEST)
    return jnp.dot(a.astype(BF16), b.astype(BF16), preferred_element_type=F32)


def _dot_nt(a, b):
    return lax.dot_general(a.astype(BF16), b.astype(BF16), (((1,), (1,)), ((), ())), preferred_element_type=F32)


def _dot_tn(a, b):
    return lax.dot_general(a.astype(BF16), b.astype(BF16), (((0,), (0,)), ((), ())), preferred_element_type=F32)


def _softplus(x):
    return jnp.maximum(x, 0.0) + jnp.log(1.0 + jnp.exp(-jnp.abs(x)))


def _rms(x):
    return x * lax.rsqrt(jnp.mean(x * x, axis=-1, keepdims=True) + EPS)


def _mm_kernel(x_ref, w_ref, o_ref):
    o_ref[...] = _dot(x_ref[...], w_ref[...]).astype(o_ref.dtype)


def pmm(x, w, layer, out_dtype=F32, tm=512, tn=512):
    m, k = x.shape
    n = w.shape[-1]
    tm, tn = math.gcd(tm, m), min(tn, n)
    if n % tn:
        tn = n
    assert m % tm == 0 and n % tn == 0, (m, n, tm, tn)
    return pl.pallas_call(
        _mm_kernel,
        grid=(m // tm, n // tn),
        in_specs=[pl.BlockSpec((tm, k), lambda i, j: (i, 0)),
                  pl.BlockSpec((None, k, tn), lambda i, j: (layer, 0, j))],
        out_specs=pl.BlockSpec((tm, tn), lambda i, j: (i, j)),
        out_shape=jax.ShapeDtypeStruct((m, n), out_dtype),
        compiler_params=_params("parallel", "parallel"),
        name="pmm",
    )(x, w)


def _cast_kernel(x_ref, o_ref):
    o_ref[...] = x_ref[...].astype(o_ref.dtype)


def cast_bf16(w, rows=256):
    w2 = w.reshape(-1, w.shape[-1])
    m, n = w2.shape
    rows = math.gcd(rows, m)
    blk = pl.BlockSpec((rows, n), lambda i: (i, 0))
    out = pl.pallas_call(_cast_kernel, grid=(m // rows,), in_specs=[blk], out_specs=blk,
                         out_shape=jax.ShapeDtypeStruct((m, n), BF16),
                         compiler_params=_params("parallel"), name="cast_bf16")(w2)
    return out.reshape(w.shape)


W_IN_BLOCK = 512


def _w_in_kernel(head_ref, tail_ref, o_ref, *, n_head_blocks):
    @pl.when(pl.program_id(1) < n_head_blocks)
    def _():
        o_ref[...] = head_ref[...].astype(o_ref.dtype)

    @pl.when(pl.program_id(1) >= n_head_blocks)
    def _():
        o_ref[...] = tail_ref[...].astype(o_ref.dtype)


def reorder_w_in(w_in):
    depth, d, _ = w_in.shape
    o = np.cumsum((0,) + IN_SPLITS)
    assert o[5] == COL_MLA and COL_MLA % W_IN_BLOCK == 0 and P_WIDTH % W_IN_BLOCK == 0
    piece = lambda i: w_in[..., o[i]:o[i + 1]]
    zeros = lambda n: jnp.zeros(w_in.shape[:-1] + (n,), w_in.dtype)
    tail = jnp.concatenate([piece(7), piece(8), zeros(MLA_BLOCK - IN_SPLITS[7] - IN_SPLITS[8]),
                            piece(9), piece(10), piece(11), piece(12),
                            piece(5), piece(6), piece(13), piece(14), zeros(P_WIDTH - COL_SMALL - 8 * GDN_HEADS)],
                           axis=-1)
    assert tail.shape[-1] == P_WIDTH - COL_MLA
    nh_blocks, n_blocks = COL_MLA // W_IN_BLOCK, P_WIDTH // W_IN_BLOCK
    return pl.pallas_call(
        functools.partial(_w_in_kernel, n_head_blocks=nh_blocks),
        grid=(depth, n_blocks),
        in_specs=[pl.BlockSpec((1, d, W_IN_BLOCK), lambda l, j: (l, 0, jnp.minimum(j, nh_blocks - 1))),
                  pl.BlockSpec((1, d, W_IN_BLOCK), lambda l, j: (l, 0, jnp.maximum(j - nh_blocks, 0)))],
        out_specs=pl.BlockSpec((1, d, W_IN_BLOCK), lambda l, j: (l, 0, j)),
        out_shape=jax.ShapeDtypeStruct((depth, d, P_WIDTH), BF16),
        compiler_params=_params("parallel", "arbitrary"),
        name="reorder_w_in",
    )(w_in, tail)


def _mod_row(n_lat_blocks, ctx_row):
    return lambda i, j: jnp.where(j >= n_lat_blocks, ctx_row, i)


def _norm_mod_kernel(x_ref, g_ref, sc_ref, sh_ref, o_ref):
    y = _rms(x_ref[0]) * g_ref[...]
    o_ref[0] = (y * (1.0 + sc_ref[0, 0]) + sh_ref[0, 0]).astype(o_ref.dtype)


def norm_mod(x, g, mods, i_sc, i_sh, n_lat, out_dtype=BF16, ts=256):
    b, t, d = x.shape
    ts = min(ts, t - n_lat) if t > n_lat else ts
    assert t % ts == 0 and n_lat % ts == 0
    row = _mod_row(n_lat // ts, b)
    return pl.pallas_call(
        _norm_mod_kernel,
        grid=(b, t // ts),
        in_specs=[pl.BlockSpec((1, ts, d), lambda i, j: (i, j, 0)),
                  pl.BlockSpec((1, d), lambda i, j: (0, 0)),
                  pl.BlockSpec((1, 1, 1, d), lambda i, j: (row(i, j), i_sc, 0, 0)),
                  pl.BlockSpec((1, 1, 1, d), lambda i, j: (row(i, j), i_sh, 0, 0))],
        out_specs=pl.BlockSpec((1, ts, d), lambda i, j: (i, j, 0)),
        out_shape=jax.ShapeDtypeStruct((b, t, d), out_dtype),
        compiler_params=_params("parallel", "parallel"),
        name="norm_mod",
    )(x, g.reshape(1, d), mods, mods)


def _norm_router_kernel(x_ref, g_ref, sc_ref, sh_ref, rw_ref, rb_ref, h_ref, lg_ref):
    h = (_rms(x_ref[0]) * g_ref[...]) * (1.0 + sc_ref[0, 0]) + sh_ref[0, 0]
    h_ref[0] = h.astype(h_ref.dtype)
    lg_ref[0] = _dot(h, rw_ref[...], precise=True) + rb_ref[...]


def norm_router(x, g, mods, n_lat, router_w, router_b, ts=256):
    b, t, d = x.shape
    ts = min(ts, t - n_lat) if t > n_lat else ts
    assert t % ts == 0 and n_lat % ts == 0
    row = _mod_row(n_lat // ts, b)
    e = router_w.shape[1]
    rb = router_b.reshape(1, e)
    return pl.pallas_call(
        _norm_router_kernel,
        grid=(b, t // ts),
        in_specs=[pl.BlockSpec((1, ts, d), lambda i, j: (i, j, 0)),
                  pl.BlockSpec((1, d), lambda i, j: (0, 0)),
                  pl.BlockSpec((1, 1, 1, d), lambda i, j: (row(i, j), SC2, 0, 0)),
                  pl.BlockSpec((1, 1, 1, d), lambda i, j: (row(i, j), SH2, 0, 0)),
                  _full(router_w), _full(rb)],
        out_specs=[pl.BlockSpec((1, ts, d), lambda i, j: (i, j, 0)),
                   pl.BlockSpec((1, ts, e), lambda i, j: (i, j, 0))],
        out_shape=[jax.ShapeDtypeStruct((b, t, d), BF16), jax.ShapeDtypeStruct((b, t, e), F32)],
        compiler_params=_params("parallel", "parallel"),
        name="norm_router",
    )(x, g.reshape(1, d), mods, mods, router_w, rb)


def _combine_residual_kernel(x_ref, y_ref, w_ref, g_ref, o_ref):
    w = w_ref[0]
    f = functools.reduce(jnp.add, [w[:, k:k + 1] * y_ref[k, 0].astype(F32) for k in range(y_ref.shape[0])])
    o_ref[0] = x_ref[0] + g_ref[0, 0] * f


def combine_residual(x, y, w, mods, i_gate, n_lat, ts=256):
    b, t, d = x.shape
    n_slots = y.shape[0]
    ts = min(ts, t - n_lat) if t > n_lat else ts
    row = _mod_row(n_lat // ts, b)
    blk = pl.BlockSpec((1, ts, d), lambda i, j: (i, j, 0))
    return pl.pallas_call(
        _combine_residual_kernel,
        grid=(b, t // ts),
        in_specs=[blk, pl.BlockSpec((n_slots, 1, ts, d), lambda i, j: (0, i, j, 0)),
                  pl.BlockSpec((1, ts, n_slots), lambda i, j: (i, j, 0)),
                  pl.BlockSpec((1, 1, 1, d), lambda i, j: (row(i, j), i_gate, 0, 0))],
        out_specs=blk,
        out_shape=jax.ShapeDtypeStruct((b, t, d), F32),
        compiler_params=_params("parallel", "parallel"),
        name="combine_residual",
    )(x, y, w, mods)


def _pair_attn_kernel(q_ref, k_ref, v_ref, o_ref, *, tk, dv, scale):
    dq, dvp = q_ref.shape[-1] // 2, v_ref.shape[-1] // 2
    tq = q_ref.shape[1]
    sum_on_mxu = dvp > dv
    outs = []
    for hh in range(2):
        q = q_ref[0][:, hh * dq:(hh + 1) * dq]
        q = (q if scale is None else q * scale).astype(BF16)
        m = jnp.full((tq, 1), -jnp.inf, F32)
        l = jnp.zeros((tq, 1), F32)
        acc = jnp.zeros((tq, dvp), F32)
        for c in range(k_ref.shape[1] // tk):
            k = k_ref[0, c * tk:(c + 1) * tk, hh * dq:(hh + 1) * dq].astype(BF16)
            v = v_ref[0, c * tk:(c + 1) * tk, hh * dvp:(hh + 1) * dvp].astype(BF16)
            s = lax.dot_general(q, k, (((1,), (1,)), ((), ())), preferred_element_type=F32)
            m_new = jnp.maximum(m, jnp.max(s, axis=-1, keepdims=True))
            alpha = jnp.exp(m - m_new)
            p = jnp.exp(s - m_new)
            if not sum_on_mxu:
                l = alpha * l + jnp.sum(p, axis=-1, keepdims=True)
            acc = alpha * acc + jnp.dot(p.astype(BF16), v, preferred_element_type=F32)
            m = m_new
        outs.append(acc[:, :dv] / (acc[:, dv:dv + 1] if sum_on_mxu else l))
    o_ref[0] = jnp.concatenate(outs, axis=-1).astype(o_ref.dtype)


def pair_attention(q, k, v, n_heads, dq, dvp, dv, q_rows, kv_rows, cols=(0, 0, 0), scale=None, tq=512):
    b = q.shape[0]
    (q0, n_q), (kv0, t) = q_rows, kv_rows
    dq2, dvp2 = 2 * dq, 2 * dvp
    tq = min(tq, n_q)
    tk = next(c for c in (768, 512, 384, 256, 128) if t % c == 0)
    assert n_q % tq == 0 and q0 % tq == 0 and kv0 % t == 0
    assert dq2 % LANES == 0 and dvp2 % LANES == 0 and (2 * dv) % LANES == 0
    assert all(c % w == 0 for c, w in zip(cols, (dq2, dq2, dvp2)))
    qc, kc, vc = (c // w for c, w in zip(cols, (dq2, dq2, dvp2)))
    return pl.pallas_call(
        functools.partial(_pair_attn_kernel, tk=tk, dv=dv, scale=scale),
        grid=(b, n_heads // 2, n_q // tq),
        in_specs=[pl.BlockSpec((1, tq, dq2), lambda i, hp, n: (i, q0 // tq + n, qc + hp)),
                  pl.BlockSpec((1, t, dq2), lambda i, hp, n: (i, kv0 // t, kc + hp)),
                  pl.BlockSpec((1, t, dvp2), lambda i, hp, n: (i, kv0 // t, vc + hp))],
        out_specs=pl.BlockSpec((1, tq, 2 * dv), lambda i, hp, n: (i, n, hp)),
        out_shape=jax.ShapeDtypeStruct((b, n_q, n_heads * dv), BF16),
        compiler_params=_params("parallel", "parallel", "parallel"),
        name="pair_attention",
    )(q, k, v)


NA_QROWS = 8
NA_KROWS = 4
NA_NEG = -1e30


def na_bias_tables(rpb, rows):
    n_steps = rows // NA_QROWS
    wh, ww = min(NA_WIN_H, rows), NA_WIN_W
    n_kblocks = rows // NA_KROWS
    n_heads = rpb.shape[0]
    width = 2 * GRID_W - 1
    lo = GRID_W - NA_WIN_W
    padded = jnp.full(rpb.shape[:2] + (width,), NA_NEG, F32).at[..., lo:lo + 2 * NA_WIN_W - 1].set(rpb)
    tiles = jnp.stack([padded[..., GRID_W - 1 - qc:width - qc] for qc in range(GRID_W)], axis=2)
    qc, kc = np.arange(GRID_W)[:, None], np.arange(GRID_W)[None, :]
    c0 = np.clip(qc - ww // 2, 0, GRID_W - ww)
    tiles = jnp.where((kc >= c0) & (kc < c0 + ww), tiles, NA_NEG)
    masked = jnp.full((n_heads, GRID_W, GRID_W), NA_NEG, F32)
    tables = []
    for i in (0, min(1, n_steps - 1), n_steps - 1):
        q_tiles = []
        for il in range(NA_QROWS):
            q = NA_QROWS * i + il
            r0 = min(max(q - wh // 2, 0), rows - wh)
            k_tiles = []
            for ka in range(4 * NA_KROWS):
                blk = 2 * i - 1 + ka // NA_KROWS
                kr = NA_KROWS * blk + ka % NA_KROWS
                ok = 0 <= blk < n_kblocks and r0 <= kr < r0 + wh
                k_tiles.append(tiles[:, kr - q + NA_WIN_H - 1] if ok else masked)
            q_tiles.append(jnp.concatenate(k_tiles, axis=-1))
        tables.append(jnp.concatenate(q_tiles, axis=-2))
    return jnp.stack(tables)


def _na_kernel(q_ref, k0_ref, k1_ref, k2_ref, k3_ref, v0_ref, v1_ref, v2_ref, v3_ref,
               kc_ref, vc_ref, bias_ref, o_ref):
    kw = NA_KROWS * GRID_W
    outs = []
    for hh in range(2):
        sl = slice(hh * NA_DH, (hh + 1) * NA_DH)
        q = (q_ref[0][:, sl] * (NA_DH ** -0.5)).astype(BF16)
        ks = [r[0][:, sl].astype(BF16) for r in (k0_ref, k1_ref, k2_ref, k3_ref, kc_ref)]
        vs = [r[0][:, sl].astype(BF16) for r in (v0_ref, v1_ref, v2_ref, v3_ref, vc_ref)]
        ss = [lax.dot_general(q, kk, (((1,), (1,)), ((), ())), preferred_element_type=F32) for kk in ks]
        ss = [ss[j] + bias_ref[0, 0, hh, :, j * kw:(j + 1) * kw] for j in range(4)] + [ss[4]]
        m = functools.reduce(jnp.maximum, [jnp.max(t, axis=-1, keepdims=True) for t in ss])
        ps = [jnp.exp(t - m) for t in ss]
        l = functools.reduce(jnp.add, [jnp.sum(t, axis=-1, keepdims=True) for t in ps])
        o = functools.reduce(jnp.add, [jnp.dot(p.astype(BF16), vv, preferred_element_type=F32)
                                       for p, vv in zip(ps, vs)])
        outs.append(o / l)
    o_ref[0] = jnp.concatenate(outs, axis=-1).astype(o_ref.dtype)


def neighborhood_attention(p3, rpb, n_lat):
    b, t, _ = p3.shape
    c_len = t - n_lat
    rows = n_lat // GRID_W
    tq, tkv = NA_QROWS * GRID_W, NA_KROWS * GRID_W
    assert rows % NA_QROWS == 0 and rows >= 2 * NA_QROWS and n_lat % c_len == 0
    n_steps = rows // NA_QROWS
    n_kblocks = rows // NA_KROWS
    hw = 2 * NA_DH
    n_pairs = NA_HEADS // 2
    bias = na_bias_tables(rpb, rows)
    bias = bias.reshape(3, n_pairs, 2, *bias.shape[2:])

    def view(j, col0):
        return pl.BlockSpec((1, tkv, hw),
                            lambda hp, i, n: (i, jnp.clip(2 * n - 1 + j, 0, n_kblocks - 1), col0 + hp))

    ctx_spec = lambda col0: pl.BlockSpec((1, c_len, hw), lambda hp, i, n: (i, n_lat // c_len, col0 + hp))
    bias_spec = pl.BlockSpec(
        (1, 1, 2, tq, 4 * tkv),
        lambda hp, i, n: (jnp.where(n == 0, 0, jnp.where(n == n_steps - 1, 2, 1)), hp, 0, 0, 0))
    return pl.pallas_call(
        _na_kernel,
        grid=(n_pairs, b, n_steps),
        in_specs=[pl.BlockSpec((1, tq, hw), lambda hp, i, n: (i, n, hp))]
                 + [view(j, n_pairs) for j in range(4)] + [view(j, 2 * n_pairs) for j in range(4)]
                 + [ctx_spec(n_pairs), ctx_spec(2 * n_pairs), bias_spec],
        out_specs=pl.BlockSpec((1, tq, hw), lambda hp, i, n: (i, n, hp)),
        out_shape=jax.ShapeDtypeStruct((b, n_lat, NA_HEADS * NA_DH), BF16),
        compiler_params=_params("parallel", "parallel", "arbitrary"),
        name="neighborhood_attention",
    )(*([p3] * 11), bias)


def rope_tables(n_lat, n_ctx):
    half = MLA_ROPE // 2
    tpos = np.arange(n_lat)
    inv = ROPE_BASE ** (-jnp.arange(0, half, 2, dtype=F32) / half)
    pos = jnp.stack([tpos // GRID_W, tpos % GRID_W], axis=1).astype(F32)
    ang = pos[:, :, None] * inv
    ang = jnp.concatenate([ang, ang], axis=-1).reshape(n_lat, MLA_ROPE)
    pad = ((0, n_ctx), (MLA_NOPE, LANES - MLA_NOPE - MLA_ROPE))
    return (jnp.pad(jnp.cos(ang), pad, constant_values=1.0), jnp.pad(jnp.sin(ang), pad))


def _rope_swap():
    half = MLA_ROPE // 2
    quarter = half // 2
    r = np.zeros((MLA_ROPE, MLA_ROPE), np.float32)
    for j in range(MLA_ROPE):
        if j % half < quarter:
            r[j + quarter, j] = -1.0
        else:
            r[j - quarter, j] = 1.0
    return r


def mla_weights(w_uq, w_ukv):
    dqh = MLA_NOPE + MLA_ROPE
    lead = w_uq.shape[:-1]
    wq = w_uq.reshape(*lead, MLA_HEADS, dqh)
    padh = lambda a: jnp.pad(a, ((0, 0),) * (a.ndim - 1) + ((0, LANES - a.shape[-1]),))
    wq_p = padh(wq).reshape(*lead, MLA_HEADS * LANES)
    rot = jnp.einsum('...hr,rs->...hs', wq[..., MLA_NOPE:], jnp.asarray(_rope_swap()))
    wqr_p = jnp.pad(rot, ((0, 0),) * (rot.ndim - 1) + ((MLA_NOPE, LANES - dqh),)).reshape(*lead, MLA_HEADS * LANES)
    lead = w_ukv.shape[:-1]
    wkv = w_ukv.reshape(*lead, MLA_HEADS, MLA_NOPE + MLA_V)
    wk_p = padh(wkv[..., :MLA_NOPE]).reshape(*lead, MLA_HEADS * LANES)
    wv_p = padh(wkv[..., MLA_NOPE:]).reshape(*lead, MLA_HEADS * LANES)
    return wq_p.astype(BF16), wqr_p.astype(BF16), jnp.concatenate([wk_p, wv_p], axis=-1).astype(BF16)


def _mla_prep_kernel(p_ref, cos_ref, sin_ref, qg_ref, kvg_ref, wq_ref, wqr_ref, wkv_ref, e_ref, er_ref,
                     q_out, k_out, v_out):
    blk = p_ref[0]
    nq = _rms(blk[:, :MLA_Q_RANK]) * qg_ref[...]
    nkv = _rms(blk[:, MLA_Q_RANK:MLA_Q_RANK + MLA_KV_RANK]) * kvg_ref[...]
    kpe = blk[:, MLA_Q_RANK + MLA_KV_RANK:MLA_Q_RANK + MLA_KV_RANK + MLA_ROPE]
    cos, sin = cos_ref[...], sin_ref[...]
    qf, qr = _dot(nq, wq_ref[...]), _dot(nq, wqr_ref[...])
    kv = _dot(nkv, wkv_ref[...])
    kpe_tile = _dot(kpe, e_ref[...], precise=True) * cos + _dot(kpe, er_ref[...], precise=True) * sin
    scale = (MLA_NOPE + MLA_ROPE) ** -0.5
    for h in range(MLA_HEADS):
        sl = slice(h * LANES, (h + 1) * LANES)
        q_out[0, :, sl] = ((qf[:, sl] * cos + qr[:, sl] * sin) * scale).astype(q_out.dtype)
        k_out[0, :, sl] = (kv[:, sl] + kpe_tile).astype(k_out.dtype)
        ones_col = (lax.broadcasted_iota(jnp.int32, (1, LANES), 1) == MLA_V).astype(F32)
        v_out[0, :, sl] = (kv[:, MLA_HEADS * LANES + h * LANES:MLA_HEADS * LANES + (h + 1) * LANES]
                           + ones_col).astype(v_out.dtype)


def mla_prep(p3, cos, sin, q_norm_g, kv_norm_g, wq, wqr, wkv, tb=256):
    b, t, _ = p3.shape
    tb = next(c for c in (tb, 128) if t % c == 0)
    swap = _rope_swap()
    place = np.zeros((MLA_ROPE, LANES), np.float32)
    place[np.arange(MLA_ROPE), MLA_NOPE + np.arange(MLA_ROPE)] = 1.0
    e, er = jnp.asarray(place), jnp.asarray(swap @ place)
    qg, kvg = q_norm_g.reshape(1, -1), kv_norm_g.reshape(1, -1)
    row = lambda w: pl.BlockSpec((1, tb, w), lambda i, j: (i, j, 0))
    tab = pl.BlockSpec((tb, LANES), lambda i, j: (j, 0))
    return pl.pallas_call(
        _mla_prep_kernel,
        grid=(b, t // tb),
        in_specs=[pl.BlockSpec((1, tb, MLA_BLOCK), lambda i, j: (i, j, COL_MLA // MLA_BLOCK)), tab, tab]
                 + [_full(a) for a in (qg, kvg, wq, wqr, wkv, e, er)],
        out_specs=[row(MLA_HEADS * LANES)] * 3,
        out_shape=[jax.ShapeDtypeStruct((b, t, MLA_HEADS * LANES), BF16),
                   jax.ShapeDtypeStruct((b, t, MLA_HEADS * LANES), BF16),
                   jax.ShapeDtypeStruct((b, t, MLA_HEADS * LANES), BF16)],
        compiler_params=_params("parallel", "parallel"),
        name="mla_prep",
    )(p3, cos, sin, qg, kvg, wq, wqr, wkv, e, er)


def _gdn_prep_kernel(x_ref, prev_ref, next_ref, w_ref, o_ref, *, n_lat, n_tok):
    tb = x_ref.shape[1]
    start = pl.program_id(1) * tb
    x = x_ref[0]
    seg_first = jnp.logical_or(start == 0, start == n_lat)
    seg_last = jnp.logical_or(start + tb == n_lat, start + tb == n_tok)
    prev_row = jnp.where(seg_first, 0.0, prev_ref[0][7:8, :])
    next_row = jnp.where(seg_last, 0.0, next_ref[0][0:1, :])
    rows = lax.broadcasted_iota(jnp.int32, (tb, 1), 0)
    x_m1 = jnp.where(rows == 0, prev_row, pltpu.roll(x, 1, axis=0))
    x_p1 = jnp.where(rows == tb - 1, next_row, pltpu.roll(x, tb - 1, axis=0))
    y = x_m1 * w_ref[0:1, :] + x * w_ref[1:2, :] + x_p1 * w_ref[2:3, :]
    y = y * jax.nn.sigmoid(y)
    for g in range(2 * GDN_HEADS):
        sl = slice(g * GDN_DK, (g + 1) * GDN_DK)
        yh = y[:, sl]
        yh = yh * lax.rsqrt(jnp.sum(yh * yh, axis=-1, keepdims=True) + EPS)
        o_ref[0, :, sl] = yh * (GDN_DK ** -0.5) if g < GDN_HEADS else yh
    o_ref[0, :, 2 * GDN_HEADS * GDN_DK:] = y[:, 2 * GDN_HEADS * GDN_DK:]


def gdn_prep(p3, conv_w, n_lat, tb=256):
    b, t, _ = p3.shape
    w = GDN_HEADS * (2 * GDN_DK + GDN_DV)
    tb = min(tb, t - n_lat)
    assert t % tb == 0 and n_lat % tb == 0 and tb % 8 == 0 and COL_GDN % w == 0
    col = COL_GDN // w
    sub = tb // 8
    return pl.pallas_call(
        functools.partial(_gdn_prep_kernel, n_lat=n_lat, n_tok=t),
        grid=(b, t // tb),
        in_specs=[pl.BlockSpec((1, tb, w), lambda i, j: (i, j, col)),
                  pl.BlockSpec((1, 8, w), lambda i, j: (i, jnp.maximum(j * sub - 1, 0), col)),
                  pl.BlockSpec((1, 8, w), lambda i, j: (i, jnp.minimum((j + 1) * sub, t // 8 - 1), col)),
                  _full(conv_w)],
        out_specs=pl.BlockSpec((1, tb, w), lambda i, j: (i, j, 0)),
        out_shape=jax.ShapeDtypeStruct((b, t, w), F32),
        compiler_params=_params("parallel", "parallel"),
        name="gdn_prep",
    )(p3, p3, p3, conv_w)


def _order_masks(length, reverse):
    r = lax.broadcasted_iota(jnp.int32, (length, length), 0)
    c = lax.broadcasted_iota(jnp.int32, (length, length), 1)
    if reverse:
        return r <= c, r < c, r >= c
    return r >= c, r > c, r <= c


def _as_row(col, mask):
    return jnp.sum(jnp.where(mask, col, 0.0), axis=0, keepdims=True)


def _gdn_kernel(xf_ref, sf_ref, xb_ref, sb_ref, alog_ref, dtb_ref, of_ref, ob_ref, s_ref):
    @pl.when(pl.program_id(1) == 0)
    def _():
        s_ref[...] = jnp.zeros_like(s_ref)

    nh, dk, dv, length = GDN_HEADS, GDN_DK, GDN_DV, GDN_CHUNK
    n_sub = xf_ref.shape[1] // length
    r = lax.broadcasted_iota(jnp.int32, (length, length), 0)
    c = lax.broadcasted_iota(jnp.int32, (length, length), 1)
    eye = (r == c).astype(F32)
    chunks = []
    for sub in range(n_sub):
        seqs = []
        for d, (x_ref, sm_ref, o_ref) in enumerate(((xf_ref, sf_ref, of_ref), (xb_ref, sb_ref, ob_ref))):
            r0 = (n_sub - 1 - sub if d == 1 else sub) * length
            rows = slice(r0, r0 + length)
            incl, strict, incl_t = _order_masks(length, d == 1)
            small = sm_ref[0, rows, :]
            hs = slice(d * nh, (d + 1) * nh)
            la_col = -jnp.exp(alog_ref[:, hs]) * _softplus(small[:, hs] + dtb_ref[:, hs])
            be_col = jax.nn.sigmoid(small[:, 2 * nh + d * nh:2 * nh + (d + 1) * nh])
            g_cols = _dot(incl.astype(F32), la_col, precise=True)
            g_ends = jnp.sum(la_col, axis=0, keepdims=True)
            for h in range(nh):
                q = x_ref[0, rows, h * dk:(h + 1) * dk]
                k = x_ref[0, rows, (nh + h) * dk:(nh + h + 1) * dk]
                v = x_ref[0, rows, 2 * nh * dk + h * dv:2 * nh * dk + (h + 1) * dv]
                gc, ge, bc = g_cols[:, h:h + 1], g_ends[:, h:h + 1], be_col[:, h:h + 1]
                gr = _as_row(la_col[:, h:h + 1], incl_t)
                decay = jnp.exp(jnp.where(incl, gc - gr, -jnp.inf))
                qk = decay * _dot_nt(q, k)
                p = -(bc * jnp.where(strict, decay, 0.0) * _dot_nt(k, k))
                eg = jnp.exp(gc)
                seqs.append(dict(o_ref=o_ref, rows=rows, h=h, idx=d * nh + h, q_dec=eg * q, qk=qk, p=p, t=eye + p,
                                 rhs=jnp.concatenate([bc * v, (bc * eg) * k], axis=-1),
                                 k_end=jnp.exp(ge - gc) * k, e_end=jnp.exp(ge)))
        for _ in range(int(math.log2(length)) - 1):
            for s in seqs:
                s['p'] = _dot(s['p'], s['p'])
            for s in seqs:
                s['t'] = s['t'] + _dot(s['t'], s['p'])
        for s in seqs:
            s['sol'] = _dot(s['t'], s['rhs'])
        chunks.append(seqs)
    states = [s_ref[i] for i in range(2 * nh)]
    for seqs in chunks:
        for s in seqs:
            s['u'] = s['sol'][:, :dv] - _dot_nt(s['sol'][:, dv:], states[s['idx']])
        for s in seqs:
            h, s_old = s['h'], states[s['idx']]
            s['o_ref'][0, s['rows'], h * dv:(h + 1) * dv] = _dot_nt(s['q_dec'], s_old) + _dot(s['qk'], s['u'])
            states[s['idx']] = s['e_end'] * s_old + _dot_tn(s['u'], s['k_end'])
    for i, st in enumerate(states):
        s_ref[i] = st


def _mlstm_kernel(xf_ref, sf_ref, xb_ref, sb_ref, bi_ref, bf_ref, of_ref, ob_ref, c_ref, n_ref, m_ref):
    @pl.when(pl.program_id(1) == 0)
    def _():
        c_ref[...] = jnp.zeros_like(c_ref)
        n_ref[...] = jnp.zeros_like(n_ref)
        m_ref[...] = jnp.zeros_like(m_ref)

    nh, dk, dv, length = ML_HEADS, ML_DK, ML_DV, ML_CHUNK
    n_sub = xf_ref.shape[1] // length
    r = lax.broadcasted_iota(jnp.int32, (length, length), 0)
    c = lax.broadcasted_iota(jnp.int32, (length, length), 1)
    eye = r == c
    col_i, col_f = 4 * GDN_HEADS, 4 * GDN_HEADS + 2 * nh
    chunks = []
    for sub in range(n_sub):
        seqs = []
        for d, (x_ref, sm_ref, o_ref) in enumerate(((xf_ref, sf_ref, of_ref), (xb_ref, sb_ref, ob_ref))):
            r0 = (n_sub - 1 - sub if d == 1 else sub) * length
            rows = slice(r0, r0 + length)
            incl, _, incl_t = _order_masks(length, d == 1)
            small = sm_ref[0, rows, :]
            hs = slice(d * nh, (d + 1) * nh)
            i_col = small[:, col_i + d * nh:col_i + (d + 1) * nh] + bi_ref[:, hs]
            f_col = -_softplus(-(small[:, col_f + d * nh:col_f + (d + 1) * nh] + bf_ref[:, hs]))
            fc_cols = _dot(incl.astype(F32), f_col, precise=True)
            f_ends = jnp.sum(f_col, axis=0, keepdims=True)
            for h in range(nh):
                q = x_ref[0, rows, h * dk:(h + 1) * dk] * (dk ** -0.5)
                k = x_ref[0, rows, (nh + h) * dk:(nh + h + 1) * dk]
                v = x_ref[0, rows, 2 * nh * dk + h * dv:2 * nh * dk + (h + 1) * dv]
                fc, fe, ic = fc_cols[:, h:h + 1], f_ends[:, h:h + 1], i_col[:, h:h + 1]
                fr, ir = _as_row(f_col[:, h:h + 1], incl_t), _as_row(ic, eye)
                d_log = jnp.where(incl, fc - fr + ir, -jnp.inf)
                d_max = jnp.max(d_log, axis=-1, keepdims=True)
                w_log = fe - fc + ic
                w_max = jnp.max(w_log, axis=0, keepdims=True)
                seqs.append(dict(o_ref=o_ref, rows=rows, h=h, idx=d * nh + h, q=q, k=k, v=v, fc=fc, fe=fe,
                                 d_max=d_max, w_max=w_max, decay=jnp.exp(d_log - d_max), wk=jnp.exp(w_log - w_max)))
        for s in seqs:
            s['intra'] = s['decay'] * _dot_nt(s['q'], s['k'])
        for s in seqs:
            s['intra_v'] = _dot(s['intra'], s['v'])
            s['vk'] = _dot_tn(s['wk'] * s['v'], s['k'])
        for s in seqs:
            s['intra_sum'] = jnp.sum(s['intra'], axis=-1, keepdims=True)
            s['k_sum'] = jnp.sum(s['wk'] * s['k'], axis=0, keepdims=True)
        chunks.append(seqs)
    n_seq = 2 * nh
    cs = [c_ref[i] for i in range(n_seq)]
    ns = [n_ref[i] for i in range(n_seq)]
    ms = [m_ref[i][:, 0:1] for i in range(n_seq)]
    for seqs in chunks:
        for s in seqs:
            m_old = ms[s['idx']]
            s['m_q'] = jnp.maximum(s['fc'] + m_old, s['d_max'])
            s['inter'], s['scale'] = jnp.exp(s['fc'] + m_old - s['m_q']), jnp.exp(s['d_max'] - s['m_q'])
            s['m_new'] = jnp.maximum(s['fe'] + m_old, s['w_max'])
            s['dec'], s['w_scale'] = jnp.exp(s['fe'] + m_old - s['m_new']), jnp.exp(s['w_max'] - s['m_new'])
        for s in seqs:
            s['qc'] = _dot_nt(s['q'], cs[s['idx']])
        for s in seqs:
            h, idx = s['h'], s['idx']
            num = s['inter'] * s['qc'] + s['scale'] * s['intra_v']
            den = (s['inter'] * jnp.sum(s['q'] * ns[idx], axis=-1, keepdims=True) + s['scale'] * s['intra_sum'])
            s['o_ref'][0, s['rows'], h * dv:(h + 1) * dv] = num / jnp.maximum(jnp.abs(den), jnp.exp(-s['m_q']))
            cs[idx] = s['dec'] * cs[idx] + s['w_scale'] * s['vk']
            ns[idx] = s['dec'] * ns[idx] + s['w_scale'] * s['k_sum']
            ms[idx] = s['m_new']
    for i in range(n_seq):
        c_ref[i], n_ref[i] = cs[i], ns[i]
        m_ref[i] = jnp.broadcast_to(ms[i], m_ref.shape[1:])


SCAN_BLOCK = 128


def bidirectional_scan(kernel_fn, x, x_col, x_w, p3, params, n_lat, out_w, scratch_shapes, name):
    b, t, _ = p3.shape
    length = SCAN_BLOCK
    nl, nc = n_lat // length, t // length
    ncx = nc - nl
    assert t % length == 0 and n_lat % length == 0
    fwd = lambda n: jnp.where(n < ncx, nl + n, n - ncx)
    bwd = lambda n: jnp.where(n < ncx, nc - 1 - n, nl - 1 - (n - ncx))
    specs = lambda f: [pl.BlockSpec((1, length, x_w), lambda i, n: (i, f(n), x_col)),
                       pl.BlockSpec((1, length, LANES), lambda i, n: (i, f(n), COL_SMALL // LANES))]
    out = lambda f: pl.BlockSpec((1, length, out_w), lambda i, n: (i, f(n), 0))
    return pl.pallas_call(
        kernel_fn,
        grid=(b, nc),
        in_specs=specs(fwd) + specs(bwd) + [_full(a) for a in params],
        out_specs=[out(fwd), out(bwd)],
        out_shape=[jax.ShapeDtypeStruct((b, t, out_w), F32)] * 2,
        scratch_shapes=scratch_shapes,
        compiler_params=_params("parallel", "arbitrary"),
        name=name,
    )(x, p3, x, p3, *params)


def gdn_scan(xg, p3, a_log, dt_bias, n_lat):
    state = pltpu.VMEM((2 * GDN_HEADS, GDN_DV, GDN_DK), F32)
    params = (a_log.reshape(1, -1), dt_bias.reshape(1, -1))
    return bidirectional_scan(_gdn_kernel, xg, 0, xg.shape[-1], p3, params, n_lat,
                              GDN_HEADS * GDN_DV, [state], "gdn_scan")


def mlstm_scan(p3, b_if, n_lat):
    n_seq = 2 * ML_HEADS
    w = ML_HEADS * (2 * ML_DK + ML_DV)
    scratch = [pltpu.VMEM((n_seq, ML_DV, ML_DK), F32), pltpu.VMEM((n_seq, 1, ML_DK), F32),
               pltpu.VMEM((n_seq, 1, LANES), F32)]
    params = (b_if[:, 0].reshape(1, -1), b_if[:, 1].reshape(1, -1))
    return bidirectional_scan(_mlstm_kernel, p3, COL_ML // w, w, p3, params, n_lat,
                              ML_HEADS * ML_DV, scratch, "mlstm_scan")


def _gated_head_norm(of_ref, ob_ref, gate, g_ref, n_heads):
    o = of_ref[0] + ob_ref[0]
    d = o.shape[-1] // n_heads
    parts = [_rms(o[:, h * d:(h + 1) * d]) * g_ref[...] for h in range(n_heads)]
    return (jnp.concatenate(parts, axis=-1) * gate).astype(BF16)


def _merge_kernel(h_ref, ya_ref, yc_ref, gof_ref, gob_ref, gg_ref, mof_ref, mob_ref, mg_ref, x_ref, g1_ref,
                  gn_ref, mn_ref, wmg_ref, bmg_ref, wbo_ref, wout_ref, o_ref):
    d = o_ref.shape[-1]
    h = h_ref[0]
    gg = gg_ref[0]
    yb = _gated_head_norm(gof_ref, gob_ref, gg * jax.nn.sigmoid(gg), gn_ref, GDN_HEADS)
    yd = _gated_head_norm(mof_ref, mob_ref, jax.nn.sigmoid(mg_ref[0]), mn_ref, ML_HEADS)
    acc = jnp.zeros((h.shape[0], d), F32)
    for n, y in enumerate((ya_ref[0], yb, yc_ref[0], yd)):
        z = jnp.dot(y, wbo_ref[n], preferred_element_type=F32)
        gate = jnp.dot(h, wmg_ref[:, n * d:(n + 1) * d], preferred_element_type=F32) + bmg_ref[:, n * d:(n + 1) * d]
        acc = acc + jax.nn.sigmoid(gate) * z
    m = jnp.dot(acc.astype(BF16), wout_ref[...], preferred_element_type=F32)
    o_ref[0] = x_ref[0] + g1_ref[0, 0] * m


def merge_residual(h, ya, yc, gdn_o, ml_o, p3, x, mods, n_lat, n_rows, layer, gdn_norm_g, ml_norm_g,
                   w_mg, b_mg, w_bo, w_out, ts=256):
    b, _, d = x.shape
    ts = min(ts, n_rows - n_lat) if n_rows > n_lat else ts
    assert n_rows % ts == 0 and n_lat % ts == 0
    mod_row = _mod_row(n_lat // ts, b)
    row = lambda w, col=0: pl.BlockSpec((1, ts, w), lambda i, j: (i, j, col))
    depth = w_mg.shape[0]
    b_mg = b_mg.reshape(depth, 1, -1)
    gn, mn = gdn_norm_g.reshape(depth, 1, -1), ml_norm_g.reshape(depth, 1, -1)
    return pl.pallas_call(
        _merge_kernel,
        grid=(b, n_rows // ts),
        in_specs=[row(d), row(BRANCH_W), row(BRANCH_W),
                  row(BRANCH_W), row(BRANCH_W), row(BRANCH_W, COL_GDN_GATE // BRANCH_W),
                  row(BRANCH_W), row(BRANCH_W), row(BRANCH_W, COL_ML_GATE // BRANCH_W),
                  row(d), pl.BlockSpec((1, 1, 1, d), lambda i, j: (mod_row(i, j), G1, 0, 0))]
                 + [_layer(a, layer) for a in (gn, mn, w_mg, b_mg, w_bo, w_out)],
        out_specs=row(d),
        out_shape=jax.ShapeDtypeStruct((b, n_rows, d), F32),
        compiler_params=_params("parallel", "parallel"),
        name="merge_residual",
    )(h, ya, yc, gdn_o[0], gdn_o[1], p3, ml_o[0], ml_o[1], p3, x, mods, gn, mn, w_mg, b_mg, w_bo, w_out)


def _expert_kernel(be_ref, x_ref, wgu_ref, bgu_ref, wdn_ref, bdn_ref, o_ref, wgu_s, wdn_s):
    i = pl.program_id(0)
    fresh = jnp.logical_or(i == 0, be_ref[i] != be_ref[jnp.maximum(i - 1, 0)])

    @pl.when(fresh)
    def _():
        wgu_s[...] = wgu_ref[0].astype(BF16)
        wdn_s[...] = wdn_ref[0].astype(BF16)

    gu = jnp.dot(x_ref[...], wgu_s[...], preferred_element_type=F32) + bgu_ref[0]
    gate = jnp.minimum(gu[:, :D_FF], SWIGLU_LIMIT)
    up = jnp.clip(gu[:, D_FF:], -SWIGLU_LIMIT, SWIGLU_LIMIT)
    act = (up + 1.0) * gate * jax.nn.sigmoid(SWIGLU_ALPHA * gate)
    y = jnp.dot(act.astype(BF16), wdn_s[...], preferred_element_type=F32) + bdn_ref[0]
    o_ref[...] = y.astype(o_ref.dtype)


def expert_blocks(buf, block_e, layer, w_gu, b_gu, w_dn, b_dn):
    rows, d = buf.shape
    n_blocks = rows // MOE_ROWS
    depth, e = w_gu.shape[:2]
    grid_spec = pltpu.PrefetchScalarGridSpec(
        num_scalar_prefetch=1,
        grid=(n_blocks,),
        in_specs=[pl.BlockSpec((MOE_ROWS, d), lambda i, be: (i, 0)),
                  pl.BlockSpec((None, 1, d, 2 * D_FF), lambda i, be: (layer, be[i], 0, 0)),
                  pl.BlockSpec((None, 1, 1, 2 * D_FF), lambda i, be: (layer, be[i], 0, 0)),
                  pl.BlockSpec((None, 1, D_FF, d), lambda i, be: (layer, be[i], 0, 0)),
                  pl.BlockSpec((None, 1, 1, d), lambda i, be: (layer, be[i], 0, 0))],
        out_specs=pl.BlockSpec((MOE_ROWS, d), lambda i, be: (i, 0)),
        scratch_shapes=[pltpu.VMEM((d, 2 * D_FF), BF16), pltpu.VMEM((D_FF, d), BF16)],
    )
    return pl.pallas_call(
        _expert_kernel,
        grid_spec=grid_spec,
        out_shape=jax.ShapeDtypeStruct((rows, d), BF16),
        compiler_params=_params("arbitrary"),
        name="expert_blocks",
    )(block_e, buf, w_gu, b_gu.reshape(depth, e, 1, -1), w_dn, b_dn.reshape(depth, e, 1, -1))


def moe_ffn(t, logits, layer, w_gu, b_gu, w_dn, b_dn):
    n, d = t.shape
    top_v, top_i = lax.top_k(logits, TOP_K)
    top_w = jax.nn.softmax(top_v, axis=-1)
    nk = n * TOP_K
    e_flat = top_i.reshape(nk)
    onehot = (e_flat[:, None] == jnp.arange(N_EXPERTS)[None, :]).astype(jnp.int32)
    csum = jnp.cumsum(onehot, axis=0)
    rank = jnp.take_along_axis(csum, e_flat[:, None], axis=1)[:, 0] - 1
    counts = csum[-1]
    padded = (counts + MOE_ROWS - 1) // MOE_ROWS * MOE_ROWS
    pad_end = jnp.cumsum(padded)
    dest = (pad_end - padded)[e_flat] + rank
    n_blocks = -(-nk // MOE_ROWS) + N_EXPERTS
    tok = jnp.arange(nk, dtype=jnp.int32) // TOP_K
    src = jnp.full((n_blocks * MOE_ROWS,), n, jnp.int32).at[dest].set(tok)
    buf = jnp.concatenate([t, jnp.zeros((1, d), t.dtype)], axis=0)[src]
    block_start = jnp.arange(n_blocks, dtype=pad_end.dtype) * MOE_ROWS
    block_e = jnp.minimum(jnp.sum(pad_end[None, :] <= block_start[:, None], axis=1), N_EXPERTS - 1).astype(jnp.int32)
    y = expert_blocks(buf, block_e, layer, w_gu, b_gu, w_dn, b_dn)
    return y[dest.reshape(n, TOP_K).T.reshape(nk)].reshape(TOP_K, n, d), top_w


def kernel(x, c, ctx, c_ctx, w_ada, b_ada, norm1_g, norm2_g, w_in, na_rpb, gdn_conv_w, gdn_a_log, gdn_dt_bias, gdn_norm_g, mla_q_norm_g, mla_w_uq, mla_kv_norm_g, mla_w_ukv, ml_b_if, ml_norm_g, w_mg, b_mg, w_bo, w_out, router_w, router_b, w_gu, b_gu, w_dn, b_dn, final_g):
    depth = w_ada.shape[0]
    bsz, n_lat, d = x.shape
    n_ctx = ctx.shape[1]
    cond = jnp.concatenate([jax.nn.silu(c), jax.nn.silu(c_ctx)[None]], axis=0)
    cond = jnp.pad(cond, ((0, (-cond.shape[0]) % 8), (0, 0)))
    w_in_r = reorder_w_in(w_in)
    w_mg_b, w_bo_b, w_out_b = (cast_bf16(t) for t in (w_mg, w_bo, w_out))
    wq, wqr, wkv = mla_weights(mla_w_uq, mla_w_ukv)
    cos, sin = rope_tables(n_lat, n_ctx)
    xs = jnp.concatenate([x, ctx], axis=1)
    n_tok = n_lat + n_ctx
    for l in range(depth):
        last = l == depth - 1
        n_rows = n_lat if last else n_tok
        mods = (pmm(cond, w_ada, l, tn=1024) + b_ada[l]).reshape(cond.shape[0], 6, 1, d)
        h = norm_mod(xs, norm1_g[l], mods, SC1, SH1, n_lat)
        p3 = pmm(h.reshape(bsz * n_tok, d), w_in_r, l, tm=1024, tn=768).reshape(bsz, n_tok, P_WIDTH)
        ya = neighborhood_attention(p3, na_rpb[l], n_lat)
        gdn_o = gdn_scan(gdn_prep(p3, gdn_conv_w[l], n_lat), p3, gdn_a_log[l], gdn_dt_bias[l], n_lat)
        ml_o = mlstm_scan(p3, ml_b_if[l], n_lat)
        q, k, v = mla_prep(p3, cos, sin, mla_q_norm_g[l], mla_kv_norm_g[l], wq[l], wqr[l], wkv[l])
        lat_rows, ctx_rows = (0, n_lat), (n_lat, n_ctx)
        yc = pair_attention(q, k, v, MLA_HEADS, LANES, LANES, MLA_V, lat_rows, (0, n_tok))
        if not last:
            w = NA_HEADS * NA_DH
            ya_ctx = pair_attention(p3, p3, p3, NA_HEADS, NA_DH, NA_DH, NA_DH, ctx_rows, ctx_rows,
                                    cols=(COL_NA, COL_NA + w, COL_NA + 2 * w), scale=NA_DH ** -0.5)
            yc_ctx = pair_attention(q, k, v, MLA_HEADS, LANES, LANES, MLA_V, ctx_rows, ctx_rows)
            ya = jnp.concatenate([ya, ya_ctx], axis=1)
            yc = jnp.concatenate([yc, yc_ctx], axis=1)
        xs = merge_residual(h, ya, yc, gdn_o, ml_o, p3, xs, mods, n_lat, n_rows, l, gdn_norm_g, ml_norm_g,
                            w_mg_b, b_mg, w_bo_b, w_out_b)
        h2, logits = norm_router(xs, norm2_g[l], mods, n_lat, router_w[l], router_b[l])
        y, top_w = moe_ffn(h2.reshape(bsz * n_rows, d), logits.reshape(bsz * n_rows, N_EXPERTS),
                           l, w_gu, b_gu, w_dn, b_dn)
        xs = combine_residual(xs, y.reshape(TOP_K, bsz, n_rows, d), top_w.reshape(bsz, n_rows, TOP_K),
                              mods, G2, n_lat)
    return norm_mod(xs, final_g, jnp.zeros((cond.shape[0], 6, 1, d), F32), SC1, SH1, n_lat, out_dtype=F32)
```

```python
import functools
import math

import numpy as np
import jax
import jax.numpy as jnp
from jax import lax
from jax.experimental import pallas as pl
from jax.experimental.pallas import tpu as pltpu

F32 = jnp.float32
BF16 = jnp.bfloat16

GRID_W = 64
EPS = 1e-6
ROPE_BASE = 10000.0
NA_HEADS, NA_DH, NA_WIN_H, NA_WIN_W = 8, 64, 8, 16
GDN_HEADS, GDN_DK, GDN_DV, GDN_CHUNK = 4, 128, 128, 64
MLA_HEADS, MLA_NOPE, MLA_ROPE, MLA_V, MLA_Q_RANK, MLA_KV_RANK = 8, 64, 32, 64, 256, 128
ML_HEADS, ML_DK, ML_DV, ML_CHUNK = 4, 64, 128, 64
N_BRANCH, BRANCH_W = 4, 512
N_EXPERTS, TOP_K, D_FF = 32, 4, 1024
SWIGLU_ALPHA, SWIGLU_LIMIT = 1.702, 7.0
MOE_ROWS = 256
LANES = 128

IN_SPLITS = (
    NA_HEADS * NA_DH, NA_HEADS * NA_DH, NA_HEADS * NA_DH,
    GDN_HEADS * (2 * GDN_DK + GDN_DV), GDN_HEADS * GDN_DV,
    2 * GDN_HEADS, 2 * GDN_HEADS,
    MLA_Q_RANK, MLA_KV_RANK + MLA_ROPE,
    ML_HEADS * ML_DK, ML_HEADS * ML_DK, ML_HEADS * ML_DV, ML_HEADS * ML_DV,
    2 * ML_HEADS, 2 * ML_HEADS,
)
COL_NA = 0
COL_GDN = 1536
COL_GDN_GATE = 3072
COL_MLA = 3584
COL_ML = 4096
COL_ML_GATE = 5120
COL_SMALL = 5632
P_WIDTH = 6144
MLA_BLOCK = COL_ML - COL_MLA
SH1, SC1, G1, SH2, SC2, G2 = range(6)

VMEM_LIMIT_BYTES = 56 * 1024 * 1024


def _params(*sem):
    return pltpu.CompilerParams(dimension_semantics=sem, vmem_limit_bytes=VMEM_LIMIT_BYTES)


def _full(a):
    return pl.BlockSpec(a.shape, lambda *_: (0,) * a.ndim)


def _layer(a, layer):
    return pl.BlockSpec((None,) + a.shape[1:], lambda *_: (layer,) + (0,) * (a.ndim - 1))


def _dot(a, b, precise=False):
    if precise:
        return jnp.dot(a, b, preferred_element_type=F32, precision=lax.Precision.HIGHEST)
    return jnp.dot(a.astype(BF16), b.astype(BF16), preferred_element_type=F32)


def _dot_nt(a, b):
    return lax.dot_general(a.astype(BF16), b.astype(BF16), (((1,), (1,)), ((), ())), preferred_element_type=F32)


def _dot_tn(a, b):
    return lax.dot_general(a.astype(BF16), b.astype(BF16), (((0,), (0,)), ((), ())), preferred_element_type=F32)


def _softplus(x):
    return jnp.maximum(x, 0.0) + jnp.log(1.0 + jnp.exp(-jnp.abs(x)))


def _rms(x):
    return x * lax.rsqrt(jnp.mean(x * x, axis=-1, keepdims=True) + EPS)


def _mm_kernel(x_ref, w_ref, o_ref):
    o_ref[...] = _dot(x_ref[...], w_ref[...]).astype(o_ref.dtype)


def pmm(x, w, layer, out_dtype=F32, tm=512, tn=512):
    m, k = x.shape
    n = w.shape[-1]
    tm, tn = math.gcd(tm, m), min(tn, n)
    if n % tn:
        tn = n
    assert m % tm == 0 and n % tn == 0, (m, n, tm, tn)
    return pl.pallas_call(
        _mm_kernel,
        grid=(m // tm, n // tn),
        in_specs=[pl.BlockSpec((tm, k), lambda i, j: (i, 0)),
                  pl.BlockSpec((None, k, tn), lambda i, j: (layer, 0, j))],
        out_specs=pl.BlockSpec((tm, tn), lambda i, j: (i, j)),
        out_shape=jax.ShapeDtypeStruct((m, n), out_dtype),
        compiler_params=_params("parallel", "parallel"),
        name="pmm",
    )(x, w)


def _cast_kernel(x_ref, o_ref):
    o_ref[...] = x_ref[...].astype(o_ref.dtype)


def cast_bf16(w, rows=256):
    w2 = w.reshape(-1, w.shape[-1])
    m, n = w2.shape
    rows = math.gcd(rows, m)
    blk = pl.BlockSpec((rows, n), lambda i: (i, 0))
    out = pl.pallas_call(_cast_kernel, grid=(m // rows,), in_specs=[blk], out_specs=blk,
                         out_shape=jax.ShapeDtypeStruct((m, n), BF16),
                         compiler_params=_params("parallel"), name="cast_bf16")(w2)
    return out.reshape(w.shape)


W_IN_BLOCK = 512


def _w_in_kernel(head_ref, tail_ref, o_ref, *, n_head_blocks):
    @pl.when(pl.program_id(1) < n_head_blocks)
    def _():
        o_ref[...] = head_ref[...].astype(o_ref.dtype)

    @pl.when(pl.program_id(1) >= n_head_blocks)
    def _():
        o_ref[...] = tail_ref[...].astype(o_ref.dtype)


def reorder_w_in(w_in):
    depth, d, _ = w_in.shape
    o = np.cumsum((0,) + IN_SPLITS)
    assert o[5] == COL_MLA and COL_MLA % W_IN_BLOCK == 0 and P_WIDTH % W_IN_BLOCK == 0
    piece = lambda i: w_in[..., o[i]:o[i + 1]]
    zeros = lambda n: jnp.zeros(w_in.shape[:-1] + (n,), w_in.dtype)
    tail = jnp.concatenate([piece(7), piece(8), zeros(MLA_BLOCK - IN_SPLITS[7] - IN_SPLITS[8]),
                            piece(9), piece(10), piece(11), piece(12),
                            piece(5), piece(6), piece(13), piece(14), zeros(P_WIDTH - COL_SMALL - 8 * GDN_HEADS)],
                           axis=-1)
    assert tail.shape[-1] == P_WIDTH - COL_MLA
    nh_blocks, n_blocks = COL_MLA // W_IN_BLOCK, P_WIDTH // W_IN_BLOCK
    return pl.pallas_call(
        functools.partial(_w_in_kernel, n_head_blocks=nh_blocks),
        grid=(depth, n_blocks),
        in_specs=[pl.BlockSpec((1, d, W_IN_BLOCK), lambda l, j: (l, 0, jnp.minimum(j, nh_blocks - 1))),
                  pl.BlockSpec((1, d, W_IN_BLOCK), lambda l, j: (l, 0, jnp.maximum(j - nh_blocks, 0)))],
        out_specs=pl.BlockSpec((1, d, W_IN_BLOCK), lambda l, j: (l, 0, j)),
        out_shape=jax.ShapeDtypeStruct((depth, d, P_WIDTH), BF16),
        compiler_params=_params("parallel", "arbitrary"),
        name="reorder_w_in",
    )(w_in, tail)


def _mod_row(n_lat_blocks, ctx_row):
    return lambda i, j: jnp.where(j >= n_lat_blocks, ctx_row, i)


def _norm_mod_kernel(x_ref, g_ref, sc_ref, sh_ref, o_ref):
    y = _rms(x_ref[0]) * g_ref[...]
    o_ref[0] = (y * (1.0 + sc_ref[0, 0]) + sh_ref[0, 0]).astype(o_ref.dtype)


def norm_mod(x, g, mods, i_sc, i_sh, n_lat, out_dtype=BF16, ts=256):
    b, t, d = x.shape
    ts = min(ts, t - n_lat) if t > n_lat else ts
    assert t % ts == 0 and n_lat % ts == 0
    row = _mod_row(n_lat // ts, b)
    return pl.pallas_call(
        _norm_mod_kernel,
        grid=(b, t // ts),
        in_specs=[pl.BlockSpec((1, ts, d), lambda i, j: (i, j, 0)),
                  pl.BlockSpec((1, d), lambda i, j: (0, 0)),
                  pl.BlockSpec((1, 1, 1, d), lambda i, j: (row(i, j), i_sc, 0, 0)),
                  pl.BlockSpec((1, 1, 1, d), lambda i, j: (row(i, j), i_sh, 0, 0))],
        out_specs=pl.BlockSpec((1, ts, d), lambda i, j: (i, j, 0)),
        out_shape=jax.ShapeDtypeStruct((b, t, d), out_dtype),
        compiler_params=_params("parallel", "parallel"),
        name="norm_mod",
    )(x, g.reshape(1, d), mods, mods)


def _norm_router_kernel(x_ref, g_ref, sc_ref, sh_ref, rw_ref, rb_ref, h_ref, lg_ref):
    h = (_rms(x_ref[0]) * g_ref[...]) * (1.0 + sc_ref[0, 0]) + sh_ref[0, 0]
    h_ref[0] = h.astype(h_ref.dtype)
    lg_ref[0] = _dot(h, rw_ref[...], precise=True) + rb_ref[...]


def norm_router(x, g, mods, n_lat, router_w, router_b, ts=256):
    b, t, d = x.shape
    ts = min(ts, t - n_lat) if t > n_lat else ts
    assert t % ts == 0 and n_lat % ts == 0
    row = _mod_row(n_lat // ts, b)
    e = router_w.shape[1]
    rb = router_b.reshape(1, e)
    return pl.pallas_call(
        _norm_router_kernel,
        grid=(b, t // ts),
        in_specs=[pl.BlockSpec((1, ts, d), lambda i, j: (i, j, 0)),
                  pl.BlockSpec((1, d), lambda i, j: (0, 0)),
                  pl.BlockSpec((1, 1, 1, d), lambda i, j: (row(i, j), SC2, 0, 0)),
                  pl.BlockSpec((1, 1, 1, d), lambda i, j: (row(i, j), SH2, 0, 0)),
                  _full(router_w), _full(rb)],
        out_specs=[pl.BlockSpec((1, ts, d), lambda i, j: (i, j, 0)),
                   pl.BlockSpec((1, ts, e), lambda i, j: (i, j, 0))],
        out_shape=[jax.ShapeDtypeStruct((b, t, d), BF16), jax.ShapeDtypeStruct((b, t, e), F32)],
        compiler_params=_params("parallel", "parallel"),
        name="norm_router",
    )(x, g.reshape(1, d), mods, mods, router_w, rb)


def _combine_residual_kernel(x_ref, y_ref, w_ref, g_ref, o_ref):
    w = w_ref[0]
    f = functools.reduce(jnp.add, [w[:, k:k + 1] * y_ref[k, 0].astype(F32) for k in range(y_ref.shape[0])])
    o_ref[0] = x_ref[0] + g_ref[0, 0] * f


def combine_residual(x, y, w, mods, i_gate, n_lat, ts=256):
    b, t, d = x.shape
    n_slots = y.shape[0]
    ts = min(ts, t - n_lat) if t > n_lat else ts
    row = _mod_row(n_lat // ts, b)
    blk = pl.BlockSpec((1, ts, d), lambda i, j: (i, j, 0))
    return pl.pallas_call(
        _combine_residual_kernel,
        grid=(b, t // ts),
        in_specs=[blk, pl.BlockSpec((n_slots, 1, ts, d), lambda i, j: (0, i, j, 0)),
                  pl.BlockSpec((1, ts, n_slots), lambda i, j: (i, j, 0)),
                  pl.BlockSpec((1, 1, 1, d), lambda i, j: (row(i, j), i_gate, 0, 0))],
        out_specs=blk,
        out_shape=jax.ShapeDtypeStruct((b, t, d), F32),
        compiler_params=_params("parallel", "parallel"),
        name="combine_residual",
    )(x, y, w, mods)


def _pair_attn_kernel(q_ref, k_ref, v_ref, o_ref, *, tk, dv, scale):
    dq, dvp = q_ref.shape[-1] // 2, v_ref.shape[-1] // 2
    tq = q_ref.shape[1]
    sum_on_mxu = dvp > dv
    outs = []
    for hh in range(2):
        q = q_ref[0][:, hh * dq:(hh + 1) * dq]
        q = (q if scale is None else q * scale).astype(BF16)
        m = jnp.full((tq, 1), -jnp.inf, F32)
        l = jnp.zeros((tq, 1), F32)
        acc = jnp.zeros((tq, dvp), F32)
        for c in range(k_ref.shape[1] // tk):
            k = k_ref[0, c * tk:(c + 1) * tk, hh * dq:(hh + 1) * dq].astype(BF16)
            v = v_ref[0, c * tk:(c + 1) * tk, hh * dvp:(hh + 1) * dvp].astype(BF16)
            s = lax.dot_general(q, k, (((1,), (1,)), ((), ())), preferred_element_type=F32)
            m_new = jnp.maximum(m, jnp.max(s, axis=-1, keepdims=True))
            alpha = jnp.exp(m - m_new)
            p = jnp.exp(s - m_new)
            if not sum_on_mxu:
                l = alpha * l + jnp.sum(p, axis=-1, keepdims=True)
            acc = alpha * acc + jnp.dot(p.astype(BF16), v, preferred_element_type=F32)
            m = m_new
        outs.append(acc[:, :dv] / (acc[:, dv:dv + 1] if sum_on_mxu else l))
    o_ref[0] = jnp.concatenate(outs, axis=-1).astype(o_ref.dtype)


def pair_attention(q, k, v, n_heads, dq, dvp, dv, q_rows, kv_rows, cols=(0, 0, 0), scale=None, tq=512):
    b = q.shape[0]
    (q0, n_q), (kv0, t) = q_rows, kv_rows
    dq2, dvp2 = 2 * dq, 2 * dvp
    tq = min(tq, n_q)
    tk = next(c for c in (768, 512, 384, 256, 128) if t % c == 0)
    assert n_q % tq == 0 and q0 % tq == 0 and kv0 % t == 0
    assert dq2 % LANES == 0 and dvp2 % LANES == 0 and (2 * dv) % LANES == 0
    assert all(c % w == 0 for c, w in zip(cols, (dq2, dq2, dvp2)))
    qc, kc, vc = (c // w for c, w in zip(cols, (dq2, dq2, dvp2)))
    return pl.pallas_call(
        functools.partial(_pair_attn_kernel, tk=tk, dv=dv, scale=scale),
        grid=(b, n_heads // 2, n_q // tq),
        in_specs=[pl.BlockSpec((1, tq, dq2), lambda i, hp, n: (i, q0 // tq + n, qc + hp)),
                  pl.BlockSpec((1, t, dq2), lambda i, hp, n: (i, kv0 // t, kc + hp)),
                  pl.BlockSpec((1, t, dvp2), lambda i, hp, n: (i, kv0 // t, vc + hp))],
        out_specs=pl.BlockSpec((1, tq, 2 * dv), lambda i, hp, n: (i, n, hp)),
        out_shape=jax.ShapeDtypeStruct((b, n_q, n_heads * dv), BF16),
        compiler_params=_params("parallel", "parallel", "parallel"),
        name="pair_attention",
    )(q, k, v)


NA_QROWS = 8
NA_KROWS = 4
NA_NEG = -1e30


def na_bias_tables(rpb, rows):
    n_steps = rows // NA_QROWS
    wh, ww = min(NA_WIN_H, rows), NA_WIN_W
    n_kblocks = rows // NA_KROWS
    n_heads = rpb.shape[0]
    width = 2 * GRID_W - 1
    lo = GRID_W - NA_WIN_W
    padded = jnp.full(rpb.shape[:2] + (width,), NA_NEG, F32).at[..., lo:lo + 2 * NA_WIN_W - 1].set(rpb)
    tiles = jnp.stack([padded[..., GRID_W - 1 - qc:width - qc] for qc in range(GRID_W)], axis=2)
    qc, kc = np.arange(GRID_W)[:, None], np.arange(GRID_W)[None, :]
    c0 = np.clip(qc - ww // 2, 0, GRID_W - ww)
    tiles = jnp.where((kc >= c0) & (kc < c0 + ww), tiles, NA_NEG)
    masked = jnp.full((n_heads, GRID_W, GRID_W), NA_NEG, F32)
    tables = []
    for i in (0, min(1, n_steps - 1), n_steps - 1):
        q_tiles = []
        for il in range(NA_QROWS):
            q = NA_QROWS * i + il
            r0 = min(max(q - wh // 2, 0), rows - wh)
            k_tiles = []
            for ka in range(4 * NA_KROWS):
                blk = 2 * i - 1 + ka // NA_KROWS
                kr = NA_KROWS * blk + ka % NA_KROWS
                ok = 0 <= blk < n_kblocks and r0 <= kr < r0 + wh
                k_tiles.append(tiles[:, kr - q + NA_WIN_H - 1] if ok else masked)
            q_tiles.append(jnp.concatenate(k_tiles, axis=-1))
        tables.append(jnp.concatenate(q_tiles, axis=-2))
    return jnp.stack(tables)


def _na_kernel(q_ref, k0_ref, k1_ref, k2_ref, k3_ref, v0_ref, v1_ref, v2_ref, v3_ref,
               kc_ref, vc_ref, bias_ref, o_ref):
    kw = NA_KROWS * GRID_W
    outs = []
    for hh in range(2):
        sl = slice(hh * NA_DH, (hh + 1) * NA_DH)
        q = (q_ref[0][:, sl] * (NA_DH ** -0.5)).astype(BF16)
        ks = [r[0][:, sl].astype(BF16) for r in (k0_ref, k1_ref, k2_ref, k3_ref, kc_ref)]
        vs = [r[0][:, sl].astype(BF16) for r in (v0_ref, v1_ref, v2_ref, v3_ref, vc_ref)]
        ss = [lax.dot_general(q, kk, (((1,), (1,)), ((), ())), preferred_element_type=F32) for kk in ks]
        ss = [ss[j] + bias_ref[0, 0, hh, :, j * kw:(j + 1) * kw] for j in range(4)] + [ss[4]]
        m = functools.reduce(jnp.maximum, [jnp.max(t, axis=-1, keepdims=True) for t in ss])
        ps = [jnp.exp(t - m) for t in ss]
        l = functools.reduce(jnp.add, [jnp.sum(t, axis=-1, keepdims=True) for t in ps])
        o = functools.reduce(jnp.add, [jnp.dot(p.astype(BF16), vv, preferred_element_type=F32)
                                       for p, vv in zip(ps, vs)])
        outs.append(o / l)
    o_ref[0] = jnp.concatenate(outs, axis=-1).astype(o_ref.dtype)


def neighborhood_attention(p3, rpb, n_lat):
    b, t, _ = p3.shape
    c_len = t - n_lat
    rows = n_lat // GRID_W
    tq, tkv = NA_QROWS * GRID_W, NA_KROWS * GRID_W
    assert rows % NA_QROWS == 0 and rows >= 2 * NA_QROWS and n_lat % c_len == 0
    n_steps = rows // NA_QROWS
    n_kblocks = rows // NA_KROWS
    hw = 2 * NA_DH
    n_pairs = NA_HEADS // 2
    bias = na_bias_tables(rpb, rows)
    bias = bias.reshape(3, n_pairs, 2, *bias.shape[2:])

    def view(j, col0):
        return pl.BlockSpec((1, tkv, hw),
                            lambda hp, i, n: (i, jnp.clip(2 * n - 1 + j, 0, n_kblocks - 1), col0 + hp))

    ctx_spec = lambda col0: pl.BlockSpec((1, c_len, hw), lambda hp, i, n: (i, n_lat // c_len, col0 + hp))
    bias_spec = pl.BlockSpec(
        (1, 1, 2, tq, 4 * tkv),
        lambda hp, i, n: (jnp.where(n == 0, 0, jnp.where(n == n_steps - 1, 2, 1)), hp, 0, 0, 0))
    return pl.pallas_call(
        _na_kernel,
        grid=(n_pairs, b, n_steps),
        in_specs=[pl.BlockSpec((1, tq, hw), lambda hp, i, n: (i, n, hp))]
                 + [view(j, n_pairs) for j in range(4)] + [view(j, 2 * n_pairs) for j in range(4)]
                 + [ctx_spec(n_pairs), ctx_spec(2 * n_pairs), bias_spec],
        out_specs=pl.BlockSpec((1, tq, hw), lambda hp, i, n: (i, n, hp)),
        out_shape=jax.ShapeDtypeStruct((b, n_lat, NA_HEADS * NA_DH), BF16),
        compiler_params=_params("parallel", "parallel", "arbitrary"),
        name="neighborhood_attention",
    )(*([p3] * 11), bias)


def rope_tables(n_lat, n_ctx):
    half = MLA_ROPE // 2
    tpos = np.arange(n_lat)
    inv = ROPE_BASE ** (-jnp.arange(0, half, 2, dtype=F32) / half)
    pos = jnp.stack([tpos // GRID_W, tpos % GRID_W], axis=1).astype(F32)
    ang = pos[:, :, None] * inv
    ang = jnp.concatenate([ang, ang], axis=-1).reshape(n_lat, MLA_ROPE)
    pad = ((0, n_ctx), (MLA_NOPE, LANES - MLA_NOPE - MLA_ROPE))
    return (jnp.pad(jnp.cos(ang), pad, constant_values=1.0), jnp.pad(jnp.sin(ang), pad))


def _rope_swap():
    half = MLA_ROPE // 2
    quarter = half // 2
    r = np.zeros((MLA_ROPE, MLA_ROPE), np.float32)
    for j in range(MLA_ROPE):
        if j % half < quarter:
            r[j + quarter, j] = -1.0
        else:
            r[j - quarter, j] = 1.0
    return r


def mla_weights(w_uq, w_ukv):
    dqh = MLA_NOPE + MLA_ROPE
    lead = w_uq.shape[:-1]
    wq = w_uq.reshape(*lead, MLA_HEADS, dqh)
    padh = lambda a: jnp.pad(a, ((0, 0),) * (a.ndim - 1) + ((0, LANES - a.shape[-1]),))
    wq_p = padh(wq).reshape(*lead, MLA_HEADS * LANES)
    rot = jnp.einsum('...hr,rs->...hs', wq[..., MLA_NOPE:], jnp.asarray(_rope_swap()))
    wqr_p = jnp.pad(rot, ((0, 0),) * (rot.ndim - 1) + ((MLA_NOPE, LANES - dqh),)).reshape(*lead, MLA_HEADS * LANES)
    lead = w_ukv.shape[:-1]
    wkv = w_ukv.reshape(*lead, MLA_HEADS, MLA_NOPE + MLA_V)
    wk_p = padh(wkv[..., :MLA_NOPE]).reshape(*lead, MLA_HEADS * LANES)
    wv_p = padh(wkv[..., MLA_NOPE:]).reshape(*lead, MLA_HEADS * LANES)
    return wq_p.astype(BF16), wqr_p.astype(BF16), jnp.concatenate([wk_p, wv_p], axis=-1).astype(BF16)


def _mla_prep_kernel(p_ref, cos_ref, sin_ref, qg_ref, kvg_ref, wq_ref, wqr_ref, wkv_ref, e_ref, er_ref,
                     q_out, k_out, v_out):
    blk = p_ref[0]
    nq = _rms(blk[:, :MLA_Q_RANK]) * qg_ref[...]
    nkv = _rms(blk[:, MLA_Q_RANK:MLA_Q_RANK + MLA_KV_RANK]) * kvg_ref[...]
    kpe = blk[:, MLA_Q_RANK + MLA_KV_RANK:MLA_Q_RANK + MLA_KV_RANK + MLA_ROPE]
    cos, sin = cos_ref[...], sin_ref[...]
    qf, qr = _dot(nq, wq_ref[...]), _dot(nq, wqr_ref[...])
    kv = _dot(nkv, wkv_ref[...])
    kpe_tile = _dot(kpe, e_ref[...], precise=True) * cos + _dot(kpe, er_ref[...], precise=True) * sin
    scale = (MLA_NOPE + MLA_ROPE) ** -0.5
    for h in range(MLA_HEADS):
        sl = slice(h * LANES, (h + 1) * LANES)
        q_out[0, :, sl] = ((qf[:, sl] * cos + qr[:, sl] * sin) * scale).astype(q_out.dtype)
        k_out[0, :, sl] = (kv[:, sl] + kpe_tile).astype(k_out.dtype)
        ones_col = (lax.broadcasted_iota(jnp.int32, (1, LANES), 1) == MLA_V).astype(F32)
        v_out[0, :, sl] = (kv[:, MLA_HEADS * LANES + h * LANES:MLA_HEADS * LANES + (h + 1) * LANES]
                           + ones_col).astype(v_out.dtype)


def mla_prep(p3, cos, sin, q_norm_g, kv_norm_g, wq, wqr, wkv, tb=256):
    b, t, _ = p3.shape
    tb = next(c for c in (tb, 128) if t % c == 0)
    swap = _rope_swap()
    place = np.zeros((MLA_ROPE, LANES), np.float32)
    place[np.arange(MLA_ROPE), MLA_NOPE + np.arange(MLA_ROPE)] = 1.0
    e, er = jnp.asarray(place), jnp.asarray(swap @ place)
    qg, kvg = q_norm_g.reshape(1, -1), kv_norm_g.reshape(1, -1)
    row = lambda w: pl.BlockSpec((1, tb, w), lambda i, j: (i, j, 0))
    tab = pl.BlockSpec((tb, LANES), lambda i, j: (j, 0))
    return pl.pallas_call(
        _mla_prep_kernel,
        grid=(b, t // tb),
        in_specs=[pl.BlockSpec((1, tb, MLA_BLOCK), lambda i, j: (i, j, COL_MLA // MLA_BLOCK)), tab, tab]
                 + [_full(a) for a in (qg, kvg, wq, wqr, wkv, e, er)],
        out_specs=[row(MLA_HEADS * LANES)] * 3,
        out_shape=[jax.ShapeDtypeStruct((b, t, MLA_HEADS * LANES), BF16),
                   jax.ShapeDtypeStruct((b, t, MLA_HEADS * LANES), BF16),
                   jax.ShapeDtypeStruct((b, t, MLA_HEADS * LANES), BF16)],
        compiler_params=_params("parallel", "parallel"),
        name="mla_prep",
    )(p3, cos, sin, qg, kvg, wq, wqr, wkv, e, er)


def _gdn_prep_kernel(x_ref, prev_ref, next_ref, w_ref, o_ref, *, n_lat, n_tok):
    tb = x_ref.shape[1]
    start = pl.program_id(1) * tb
    x = x_ref[0]
    seg_first = jnp.logical_or(start == 0, start == n_lat)
    seg_last = jnp.logical_or(start + tb == n_lat, start + tb == n_tok)
    prev_row = jnp.where(seg_first, 0.0, prev_ref[0][7:8, :])
    next_row = jnp.where(seg_last, 0.0, next_ref[0][0:1, :])
    rows = lax.broadcasted_iota(jnp.int32, (tb, 1), 0)
    x_m1 = jnp.where(rows == 0, prev_row, pltpu.roll(x, 1, axis=0))
    x_p1 = jnp.where(rows == tb - 1, next_row, pltpu.roll(x, tb - 1, axis=0))
    y = x_m1 * w_ref[0:1, :] + x * w_ref[1:2, :] + x_p1 * w_ref[2:3, :]
    y = y * jax.nn.sigmoid(y)
    for g in range(2 * GDN_HEADS):
        sl = slice(g * GDN_DK, (g + 1) * GDN_DK)
        yh = y[:, sl]
        yh = yh * lax.rsqrt(jnp.sum(yh * yh, axis=-1, keepdims=True) + EPS)
        o_ref[0, :, sl] = yh * (GDN_DK ** -0.5) if g < GDN_HEADS else yh
    o_ref[0, :, 2 * GDN_HEADS * GDN_DK:] = y[:, 2 * GDN_HEADS * GDN_DK:]


def gdn_prep(p3, conv_w, n_lat, tb=256):
    b, t, _ = p3.shape
    w = GDN_HEADS * (2 * GDN_DK + GDN_DV)
    tb = min(tb, t - n_lat)
    assert t % tb == 0 and n_lat % tb == 0 and tb % 8 == 0 and COL_GDN % w == 0
    col = COL_GDN // w
    sub = tb // 8
    return pl.pallas_call(
        functools.partial(_gdn_prep_kernel, n_lat=n_lat, n_tok=t),
        grid=(b, t // tb),
        in_specs=[pl.BlockSpec((1, tb, w), lambda i, j: (i, j, col)),
                  pl.BlockSpec((1, 8, w), lambda i, j: (i, jnp.maximum(j * sub - 1, 0), col)),
                  pl.BlockSpec((1, 8, w), lambda i, j: (i, jnp.minimum((j + 1) * sub, t // 8 - 1), col)),
                  _full(conv_w)],
        out_specs=pl.BlockSpec((1, tb, w), lambda i, j: (i, j, 0)),
        out_shape=jax.ShapeDtypeStruct((b, t, w), F32),
        compiler_params=_params("parallel", "parallel"),
        name="gdn_prep",
    )(p3, p3, p3, conv_w)


def _order_masks(length, reverse):
    r = lax.broadcasted_iota(jnp.int32, (length, length), 0)
    c = lax.broadcasted_iota(jnp.int32, (length, length), 1)
    if reverse:
        return r <= c, r < c, r >= c
    return r >= c, r > c, r <= c


def _as_row(col, mask):
    return jnp.sum(jnp.where(mask, col, 0.0), axis=0, keepdims=True)


def _gdn_kernel(xf_ref, sf_ref, xb_ref, sb_ref, alog_ref, dtb_ref, of_ref, ob_ref, s_ref):
    @pl.when(pl.program_id(1) == 0)
    def _():
        s_ref[...] = jnp.zeros_like(s_ref)

    nh, dk, dv, length = GDN_HEADS, GDN_DK, GDN_DV, GDN_CHUNK
    n_sub = xf_ref.shape[1] // length
    r = lax.broadcasted_iota(jnp.int32, (length, length), 0)
    c = lax.broadcasted_iota(jnp.int32, (length, length), 1)
    eye = (r == c).astype(F32)
    chunks = []
    for sub in range(n_sub):
        seqs = []
        for d, (x_ref, sm_ref, o_ref) in enumerate(((xf_ref, sf_ref, of_ref), (xb_ref, sb_ref, ob_ref))):
            r0 = (n_sub - 1 - sub if d == 1 else sub) * length
            rows = slice(r0, r0 + length)
            incl, strict, incl_t = _order_masks(length, d == 1)
            small = sm_ref[0, rows, :]
            hs = slice(d * nh, (d + 1) * nh)
            la_col = -jnp.exp(alog_ref[:, hs]) * _softplus(small[:, hs] + dtb_ref[:, hs])
            be_col = jax.nn.sigmoid(small[:, 2 * nh + d * nh:2 * nh + (d + 1) * nh])
            g_cols = _dot(incl.astype(F32), la_col, precise=True)
            g_ends = jnp.sum(la_col, axis=0, keepdims=True)
            for h in range(nh):
                q = x_ref[0, rows, h * dk:(h + 1) * dk]
                k = x_ref[0, rows, (nh + h) * dk:(nh + h + 1) * dk]
                v = x_ref[0, rows, 2 * nh * dk + h * dv:2 * nh * dk + (h + 1) * dv]
                gc, ge, bc = g_cols[:, h:h + 1], g_ends[:, h:h + 1], be_col[:, h:h + 1]
                gr = _as_row(la_col[:, h:h + 1], incl_t)
                decay = jnp.exp(jnp.where(incl, gc - gr, -jnp.inf))
                qk = decay * _dot_nt(q, k)
                p = -(bc * jnp.where(strict, decay, 0.0) * _dot_nt(k, k))
                eg = jnp.exp(gc)
                seqs.append(dict(o_ref=o_ref, rows=rows, h=h, idx=d * nh + h, q_dec=eg * q, qk=qk, p=p, t=eye + p,
                                 rhs=jnp.concatenate([bc * v, (bc * eg) * k], axis=-1),
                                 k_end=jnp.exp(ge - gc) * k, e_end=jnp.exp(ge)))
        for _ in range(int(math.log2(length)) - 1):
            for s in seqs:
                s['p'] = _dot(s['p'], s['p'])
            for s in seqs:
                s['t'] = s['t'] + _dot(s['t'], s['p'])
        for s in seqs:
            s['sol'] = _dot(s['t'], s['rhs'])
        chunks.append(seqs)
    states = [s_ref[i] for i in range(2 * nh)]
    for seqs in chunks:
        for s in seqs:
            s['u'] = s['sol'][:, :dv] - _dot_nt(s['sol'][:, dv:], states[s['idx']])
        for s in seqs:
            h, s_old = s['h'], states[s['idx']]
            s['o_ref'][0, s['rows'], h * dv:(h + 1) * dv] = _dot_nt(s['q_dec'], s_old) + _dot(s['qk'], s['u'])
            states[s['idx']] = s['e_end'] * s_old + _dot_tn(s['u'], s['k_end'])
    for i, st in enumerate(states):
        s_ref[i] = st


def _mlstm_kernel(xf_ref, sf_ref, xb_ref, sb_ref, bi_ref, bf_ref, of_ref, ob_ref, c_ref, n_ref, m_ref):
    @pl.when(pl.program_id(1) == 0)
    def _():
        c_ref[...] = jnp.zeros_like(c_ref)
        n_ref[...] = jnp.zeros_like(n_ref)
        m_ref[...] = jnp.zeros_like(m_ref)

    nh, dk, dv, length = ML_HEADS, ML_DK, ML_DV, ML_CHUNK
    n_sub = xf_ref.shape[1] // length
    r = lax.broadcasted_iota(jnp.int32, (length, length), 0)
    c = lax.broadcasted_iota(jnp.int32, (length, length), 1)
    eye = r == c
    col_i, col_f = 4 * GDN_HEADS, 4 * GDN_HEADS + 2 * nh
    chunks = []
    for sub in range(n_sub):
        seqs = []
        for d, (x_ref, sm_ref, o_ref) in enumerate(((xf_ref, sf_ref, of_ref), (xb_ref, sb_ref, ob_ref))):
            r0 = (n_sub - 1 - sub if d == 1 else sub) * length
            rows = slice(r0, r0 + length)
            incl, _, incl_t = _order_masks(length, d == 1)
            small = sm_ref[0, rows, :]
            hs = slice(d * nh, (d + 1) * nh)
            i_col = small[:, col_i + d * nh:col_i + (d + 1) * nh] + bi_ref[:, hs]
            f_col = -_softplus(-(small[:, col_f + d * nh:col_f + (d + 1) * nh] + bf_ref[:, hs]))
            fc_cols = _dot(incl.astype(F32), f_col, precise=True)
            f_ends = jnp.sum(f_col, axis=0, keepdims=True)
            for h in range(nh):
                q = x_ref[0, rows, h * dk:(h + 1) * dk] * (dk ** -0.5)
                k = x_ref[0, rows, (nh + h) * dk:(nh + h + 1) * dk]
                v = x_ref[0, rows, 2 * nh * dk + h * dv:2 * nh * dk + (h + 1) * dv]
                fc, fe, ic = fc_cols[:, h:h + 1], f_ends[:, h:h + 1], i_col[:, h:h + 1]
                fr, ir = _as_row(f_col[:, h:h + 1], incl_t), _as_row(ic, eye)
                d_log = jnp.where(incl, fc - fr + ir, -jnp.inf)
                d_max = jnp.max(d_log, axis=-1, keepdims=True)
                w_log = fe - fc + ic
                w_max = jnp.max(w_log, axis=0, keepdims=True)
                seqs.append(dict(o_ref=o_ref, rows=rows, h=h, idx=d * nh + h, q=q, k=k, v=v, fc=fc, fe=fe,
                                 d_max=d_max, w_max=w_max, decay=jnp.exp(d_log - d_max), wk=jnp.exp(w_log - w_max)))
        for s in seqs:
            s['intra'] = s['decay'] * _dot_nt(s['q'], s['k'])
        for s in seqs:
            s['intra_v'] = _dot(s['intra'], s['v'])
            s['vk'] = _dot_tn(s['wk'] * s['v'], s['k'])
        for s in seqs:
            s['intra_sum'] = jnp.sum(s['intra'], axis=-1, keepdims=True)
            s['k_sum'] = jnp.sum(s['wk'] * s['k'], axis=0, keepdims=True)
        chunks.append(seqs)
    n_seq = 2 * nh
    cs = [c_ref[i] for i in range(n_seq)]
    ns = [n_ref[i] for i in range(n_seq)]
    ms = [m_ref[i][:, 0:1] for i in range(n_seq)]
    for seqs in chunks:
        for s in seqs:
            m_old = ms[s['idx']]
            s['m_q'] = jnp.maximum(s['fc'] + m_old, s['d_max'])
            s['inter'], s['scale'] = jnp.exp(s['fc'] + m_old - s['m_q']), jnp.exp(s['d_max'] - s['m_q'])
            s['m_new'] = jnp.maximum(s['fe'] + m_old, s['w_max'])
            s['dec'], s['w_scale'] = jnp.exp(s['fe'] + m_old - s['m_new']), jnp.exp(s['w_max'] - s['m_new'])
        for s in seqs:
            s['qc'] = _dot_nt(s['q'], cs[s['idx']])
        for s in seqs:
            h, idx = s['h'], s['idx']
            num = s['inter'] * s['qc'] + s['scale'] * s['intra_v']
            den = (s['inter'] * jnp.sum(s['q'] * ns[idx], axis=-1, keepdims=True) + s['scale'] * s['intra_sum'])
            s['o_ref'][0, s['rows'], h * dv:(h + 1) * dv] = num / jnp.maximum(jnp.abs(den), jnp.exp(-s['m_q']))
            cs[idx] = s['dec'] * cs[idx] + s['w_scale'] * s['vk']
            ns[idx] = s['dec'] * ns[idx] + s['w_scale'] * s['k_sum']
            ms[idx] = s['m_new']
    for i in range(n_seq):
        c_ref[i], n_ref[i] = cs[i], ns[i]
        m_ref[i] = jnp.broadcast_to(ms[i], m_ref.shape[1:])


SCAN_BLOCK = 128


def bidirectional_scan(kernel_fn, x, x_col, x_w, p3, params, n_lat, out_w, scratch_shapes, name):
    b, t, _ = p3.shape
    length = SCAN_BLOCK
    nl, nc = n_lat // length, t // length
    ncx = nc - nl
    assert t % length == 0 and n_lat % length == 0
    fwd = lambda n: jnp.where(n < ncx, nl + n, n - ncx)
    bwd = lambda n: jnp.where(n < ncx, nc - 1 - n, nl - 1 - (n - ncx))
    specs = lambda f: [pl.BlockSpec((1, length, x_w), lambda i, n: (i, f(n), x_col)),
                       pl.BlockSpec((1, length, LANES), lambda i, n: (i, f(n), COL_SMALL // LANES))]
    out = lambda f: pl.BlockSpec((1, length, out_w), lambda i, n: (i, f(n), 0))
    return pl.pallas_call(
        kernel_fn,
        grid=(b, nc),
        in_specs=specs(fwd) + specs(bwd) + [_full(a) for a in params],
        out_specs=[out(fwd), out(bwd)],
        out_shape=[jax.ShapeDtypeStruct((b, t, out_w), F32)] * 2,
        scratch_shapes=scratch_shapes,
        compiler_params=_params("parallel", "arbitrary"),
        name=name,
    )(x, p3, x, p3, *params)


def gdn_scan(xg, p3, a_log, dt_bias, n_lat):
    state = pltpu.VMEM((2 * GDN_HEADS, GDN_DV, GDN_DK), F32)
    params = (a_log.reshape(1, -1), dt_bias.reshape(1, -1))
    return bidirectional_scan(_gdn_kernel, xg, 0, xg.shape[-1], p3, params, n_lat,
                              GDN_HEADS * GDN_DV, [state], "gdn_scan")


def mlstm_scan(p3, b_if, n_lat):
    n_seq = 2 * ML_HEADS
    w = ML_HEADS * (2 * ML_DK + ML_DV)
    scratch = [pltpu.VMEM((n_seq, ML_DV, ML_DK), F32), pltpu.VMEM((n_seq, 1, ML_DK), F32),
               pltpu.VMEM((n_seq, 1, LANES), F32)]
    params = (b_if[:, 0].reshape(1, -1), b_if[:, 1].reshape(1, -1))
    return bidirectional_scan(_mlstm_kernel, p3, COL_ML // w, w, p3, params, n_lat,
                              ML_HEADS * ML_DV, scratch, "mlstm_scan")


def _gated_head_norm(of_ref, ob_ref, gate, g_ref, n_heads):
    o = of_ref[0] + ob_ref[0]
    d = o.shape[-1] // n_heads
    parts = [_rms(o[:, h * d:(h + 1) * d]) * g_ref[...] for h in range(n_heads)]
    return (jnp.concatenate(parts, axis=-1) * gate).astype(BF16)


def _merge_kernel(h_ref, ya_ref, yc_ref, gof_ref, gob_ref, gg_ref, mof_ref, mob_ref, mg_ref, x_ref, g1_ref,
                  gn_ref, mn_ref, wmg_ref, bmg_ref, wbo_ref, wout_ref, o_ref):
    d = o_ref.shape[-1]
    h = h_ref[0]
    gg = gg_ref[0]
    yb = _gated_head_norm(gof_ref, gob_ref, gg * jax.nn.sigmoid(gg), gn_ref, GDN_HEADS)
    yd = _gated_head_norm(mof_ref, mob_ref, jax.nn.sigmoid(mg_ref[0]), mn_ref, ML_HEADS)
    acc = jnp.zeros((h.shape[0], d), F32)
    for n, y in enumerate((ya_ref[0], yb, yc_ref[0], yd)):
        z = jnp.dot(y, wbo_ref[n], preferred_element_type=F32)
        gate = jnp.dot(h, wmg_ref[:, n * d:(n + 1) * d], preferred_element_type=F32) + bmg_ref[:, n * d:(n + 1) * d]
        acc = acc + jax.nn.sigmoid(gate) * z
    m = jnp.dot(acc.astype(BF16), wout_ref[...], preferred_element_type=F32)
    o_ref[0] = x_ref[0] + g1_ref[0, 0] * m


def merge_residual(h, ya, yc, gdn_o, ml_o, p3, x, mods, n_lat, n_rows, layer, gdn_norm_g, ml_norm_g,
                   w_mg, b_mg, w_bo, w_out, ts=256):
    b, _, d = x.shape
    ts = min(ts, n_rows - n_lat) if n_rows > n_lat else ts
    assert n_rows % ts == 0 and n_lat % ts == 0
    mod_row = _mod_row(n_lat // ts, b)
    row = lambda w, col=0: pl.BlockSpec((1, ts, w), lambda i, j: (i, j, col))
    depth = w_mg.shape[0]
    b_mg = b_mg.reshape(depth, 1, -1)
    gn, mn = gdn_norm_g.reshape(depth, 1, -1), ml_norm_g.reshape(depth, 1, -1)
    return pl.pallas_call(
        _merge_kernel,
        grid=(b, n_rows // ts),
        in_specs=[row(d), row(BRANCH_W), row(BRANCH_W),
                  row(BRANCH_W), row(BRANCH_W), row(BRANCH_W, COL_GDN_GATE // BRANCH_W),
                  row(BRANCH_W), row(BRANCH_W), row(BRANCH_W, COL_ML_GATE // BRANCH_W),
                  row(d), pl.BlockSpec((1, 1, 1, d), lambda i, j: (mod_row(i, j), G1, 0, 0))]
                 + [_layer(a, layer) for a in (gn, mn, w_mg, b_mg, w_bo, w_out)],
        out_specs=row(d),
        out_shape=jax.ShapeDtypeStruct((b, n_rows, d), F32),
        compiler_params=_params("parallel", "parallel"),
        name="merge_residual",
    )(h, ya, yc, gdn_o[0], gdn_o[1], p3, ml_o[0], ml_o[1], p3, x, mods, gn, mn, w_mg, b_mg, w_bo, w_out)


def _expert_kernel(be_ref, x_ref, wgu_ref, bgu_ref, wdn_ref, bdn_ref, o_ref, wgu_s, wdn_s):
    i = pl.program_id(0)
    fresh = jnp.logical_or(i == 0, be_ref[i] != be_ref[jnp.maximum(i - 1, 0)])

    @pl.when(fresh)
    def _():
        wgu_s[...] = wgu_ref[0].astype(BF16)
        wdn_s[...] = wdn_ref[0].astype(BF16)

    gu = jnp.dot(x_ref[...], wgu_s[...], preferred_element_type=F32) + bgu_ref[0]
    gate = jnp.minimum(gu[:, :D_FF], SWIGLU_LIMIT)
    up = jnp.clip(gu[:, D_FF:], -SWIGLU_LIMIT, SWIGLU_LIMIT)
    act = (up + 1.0) * gate * jax.nn.sigmoid(SWIGLU_ALPHA * gate)
    y = jnp.dot(act.astype(BF16), wdn_s[...], preferred_element_type=F32) + bdn_ref[0]
    o_ref[...] = y.astype(o_ref.dtype)


def expert_blocks(buf, block_e, layer, w_gu, b_gu, w_dn, b_dn):
    rows, d = buf.shape
    n_blocks = rows // MOE_ROWS
    depth, e = w_gu.shape[:2]
    grid_spec = pltpu.PrefetchScalarGridSpec(
        num_scalar_prefetch=1,
        grid=(n_blocks,),
        in_specs=[pl.BlockSpec((MOE_ROWS, d), lambda i, be: (i, 0)),
                  pl.BlockSpec((None, 1, d, 2 * D_FF), lambda i, be: (layer, be[i], 0, 0)),
                  pl.BlockSpec((None, 1, 1, 2 * D_FF), lambda i, be: (layer, be[i], 0, 0)),
                  pl.BlockSpec((None, 1, D_FF, d), lambda i, be: (layer, be[i], 0, 0)),
                  pl.BlockSpec((None, 1, 1, d), lambda i, be: (layer, be[i], 0, 0))],
        out_specs=pl.BlockSpec((MOE_ROWS, d), lambda i, be: (i, 0)),
        scratch_shapes=[pltpu.VMEM((d, 2 * D_FF), BF16), pltpu.VMEM((D_FF, d), BF16)],
    )
    return pl.pallas_call(
        _expert_kernel,
        grid_spec=grid_spec,
        out_shape=jax.ShapeDtypeStruct((rows, d), BF16),
        compiler_params=_params("arbitrary"),
        name="expert_blocks",
    )(block_e, buf, w_gu, b_gu.reshape(depth, e, 1, -1), w_dn, b_dn.reshape(depth, e, 1, -1))


def moe_ffn(t, logits, layer, w_gu, b_gu, w_dn, b_dn):
    n, d = t.shape
    top_v, top_i = lax.top_k(logits, TOP_K)
    top_w = jax.nn.softmax(top_v, axis=-1)
    nk = n * TOP_K
    e_flat = top_i.reshape(nk)
    onehot = (e_flat[:, None] == jnp.arange(N_EXPERTS)[None, :]).astype(jnp.int32)
    csum = jnp.cumsum(onehot, axis=0)
    rank = jnp.take_along_axis(csum, e_flat[:, None], axis=1)[:, 0] - 1
    counts = csum[-1]
    padded = (counts + MOE_ROWS - 1) // MOE_ROWS * MOE_ROWS
    pad_end = jnp.cumsum(padded)
    dest = (pad_end - padded)[e_flat] + rank
    n_blocks = -(-nk // MOE_ROWS) + N_EXPERTS
    block_start = jnp.arange(n_blocks, dtype=pad_end.dtype) * MOE_ROWS
    block_e = jnp.minimum(jnp.sum(pad_end[None, :] <= block_start[:, None], axis=1), N_EXPERTS - 1).astype(jnp.int32)
    order = jnp.argsort(e_flat, stable=True).astype(jnp.int32)
    in_group = (block_start - (pad_end - padded)[block_e])[:, None] + jnp.arange(MOE_ROWS, dtype=pad_end.dtype)[None, :]
    first = (jnp.cumsum(counts) - counts)[block_e][:, None]
    used = in_group < counts[block_e][:, None]
    src = jnp.where(used, order[jnp.clip(first + in_group, 0, nk - 1)] // TOP_K, n).reshape(n_blocks * MOE_ROWS)
    buf = jnp.concatenate([t, jnp.zeros((1, d), t.dtype)], axis=0)[src]
    y = expert_blocks(buf, block_e, layer, w_gu, b_gu, w_dn, b_dn)
    return y[dest.reshape(n, TOP_K).T.reshape(nk)].reshape(TOP_K, n, d), top_w


def kernel(x, c, ctx, c_ctx, w_ada, b_ada, norm1_g, norm2_g, w_in, na_rpb, gdn_conv_w, gdn_a_log, gdn_dt_bias, gdn_norm_g, mla_q_norm_g, mla_w_uq, mla_kv_norm_g, mla_w_ukv, ml_b_if, ml_norm_g, w_mg, b_mg, w_bo, w_out, router_w, router_b, w_gu, b_gu, w_dn, b_dn, final_g):
    depth = w_ada.shape[0]
    bsz, n_lat, d = x.shape
    n_ctx = ctx.shape[1]
    cond = jnp.concatenate([jax.nn.silu(c), jax.nn.silu(c_ctx)[None]], axis=0)
    cond = jnp.pad(cond, ((0, (-cond.shape[0]) % 8), (0, 0)))
    w_in_r = reorder_w_in(w_in)
    w_mg_b, w_bo_b, w_out_b = (cast_bf16(t) for t in (w_mg, w_bo, w_out))
    wq, wqr, wkv = mla_weights(mla_w_uq, mla_w_ukv)
    cos, sin = rope_tables(n_lat, n_ctx)
    xs = jnp.concatenate([x, ctx], axis=1)
    n_tok = n_lat + n_ctx
    for l in range(depth):
        last = l == depth - 1
        n_rows = n_lat if last else n_tok
        mods = (pmm(cond, w_ada, l, tn=1024) + b_ada[l]).reshape(cond.shape[0], 6, 1, d)
        h = norm_mod(xs, norm1_g[l], mods, SC1, SH1, n_lat)
        p3 = pmm(h.reshape(bsz * n_tok, d), w_in_r, l, tm=1024, tn=768).reshape(bsz, n_tok, P_WIDTH)
        ya = neighborhood_attention(p3, na_rpb[l], n_lat)
        gdn_o = gdn_scan(gdn_prep(p3, gdn_conv_w[l], n_lat), p3, gdn_a_log[l], gdn_dt_bias[l], n_lat)
        ml_o = mlstm_scan(p3, ml_b_if[l], n_lat)
        q, k, v = mla_prep(p3, cos, sin, mla_q_norm_g[l], mla_kv_norm_g[l], wq[l], wqr[l], wkv[l])
        lat_rows, ctx_rows = (0, n_lat), (n_lat, n_ctx)
        yc = pair_attention(q, k, v, MLA_HEADS, LANES, LANES, MLA_V, lat_rows, (0, n_tok))
        if not last:
            w = NA_HEADS * NA_DH
            ya_ctx = pair_attention(p3, p3, p3, NA_HEADS, NA_DH, NA_DH, NA_DH, ctx_rows, ctx_rows,
                                    cols=(COL_NA, COL_NA + w, COL_NA + 2 * w), scale=NA_DH ** -0.5)
            yc_ctx = pair_attention(q, k, v, MLA_HEADS, LANES, LANES, MLA_V, ctx_rows, ctx_rows)
            ya = jnp.concatenate([ya, ya_ctx], axis=1)
            yc = jnp.concatenate([yc, yc_ctx], axis=1)
        xs = merge_residual(h, ya, yc, gdn_o, ml_o, p3, xs, mods, n_lat, n_rows, l, gdn_norm_g, ml_norm_g,
                            w_mg_b, b_mg, w_bo_b, w_out_b)
        h2, logits = norm_router(xs, norm2_g[l], mods, n_lat, router_w[l], router_b[l])
        y, top_w = moe_ffn(h2.reshape(bsz * n_rows, d), logits.reshape(bsz * n_rows, N_EXPERTS),
                           l, w_gu, b_gu, w_dn, b_dn)
        xs = combine_residual(xs, y.reshape(TOP_K, bsz, n_rows, d), top_w.reshape(bsz, n_rows, TOP_K),
                              mods, G2, n_lat)
    return norm_mod(xs, final_g, jnp.zeros((cond.shape[0], 6, 1, d), F32), SC1, SH1, n_lat, out_dtype=F32)
```

```python
import functools
import math

import numpy as np
import jax
import jax.numpy as jnp
from jax import lax
from jax.experimental import pallas as pl
from jax.experimental.pallas import tpu as pltpu

F32 = jnp.float32
BF16 = jnp.bfloat16

GRID_W = 64
EPS = 1e-6
ROPE_BASE = 10000.0
NA_HEADS, NA_DH, NA_WIN_H, NA_WIN_W = 8, 64, 8, 16
GDN_HEADS, GDN_DK, GDN_DV, GDN_CHUNK = 4, 128, 128, 64
MLA_HEADS, MLA_NOPE, MLA_ROPE, MLA_V, MLA_Q_RANK, MLA_KV_RANK = 8, 64, 32, 64, 256, 128
ML_HEADS, ML_DK, ML_DV, ML_CHUNK = 4, 64, 128, 64
N_BRANCH, BRANCH_W = 4, 512
N_EXPERTS, TOP_K, D_FF = 32, 4, 1024
SWIGLU_ALPHA, SWIGLU_LIMIT = 1.702, 7.0
MOE_ROWS = 256
LANES = 128

IN_SPLITS = (
    NA_HEADS * NA_DH, NA_HEADS * NA_DH, NA_HEADS * NA_DH,
    GDN_HEADS * (2 * GDN_DK + GDN_DV), GDN_HEADS * GDN_DV,
    2 * GDN_HEADS, 2 * GDN_HEADS,
    MLA_Q_RANK, MLA_KV_RANK + MLA_ROPE,
    ML_HEADS * ML_DK, ML_HEADS * ML_DK, ML_HEADS * ML_DV, ML_HEADS * ML_DV,
    2 * ML_HEADS, 2 * ML_HEADS,
)
COL_NA = 0
COL_GDN = 1536
COL_GDN_GATE = 3072
COL_MLA = 3584
COL_ML = 4096
COL_ML_GATE = 5120
COL_SMALL = 5632
P_WIDTH = 6144
MLA_BLOCK = COL_ML - COL_MLA
SH1, SC1, G1, SH2, SC2, G2 = range(6)

VMEM_LIMIT_BYTES = 56 * 1024 * 1024


def _params(*sem):
    return pltpu.CompilerParams(dimension_semantics=sem, vmem_limit_bytes=VMEM_LIMIT_BYTES)


def _full(a):
    return pl.BlockSpec(a.shape, lambda *_: (0,) * a.ndim)


def _layer(a, layer):
    return pl.BlockSpec((None,) + a.shape[1:], lambda *_: (layer,) + (0,) * (a.ndim - 1))


def _dot(a, b, precise=False):
    if precise:
        return jnp.dot(a, b, preferred_element_type=F32, precision=lax.Precision.HIGHEST)
    return jnp.dot(a.astype(BF16), b.astype(BF16), preferred_element_type=F32)


def _dot_nt(a, b):
    return lax.dot_general(a.astype(BF16), b.astype(BF16), (((1,), (1,)), ((), ())), preferred_element_type=F32)


def _dot_tn(a, b):
    return lax.dot_general(a.astype(BF16), b.astype(BF16), (((0,), (0,)), ((), ())), preferred_element_type=F32)


def _softplus(x):
    return jnp.maximum(x, 0.0) + jnp.log(1.0 + jnp.exp(-jnp.abs(x)))


def _rms(x):
    return x * lax.rsqrt(jnp.mean(x * x, axis=-1, keepdims=True) + EPS)


def _mm_kernel(x_ref, w_ref, o_ref):
    o_ref[...] = _dot(x_ref[...], w_ref[...]).astype(o_ref.dtype)


def pmm(x, w, layer, out_dtype=F32, tm=512, tn=512):
    m, k = x.shape
    n = w.shape[-1]
    tm, tn = math.gcd(tm, m), min(tn, n)
    if n % tn:
        tn = n
    assert m % tm == 0 and n % tn == 0, (m, n, tm, tn)
    return pl.pallas_call(
        _mm_kernel,
        grid=(m // tm, n // tn),
        in_specs=[pl.BlockSpec((tm, k), lambda i, j: (i, 0)),
                  pl.BlockSpec((None, k, tn), lambda i, j: (layer, 0, j))],
        out_specs=pl.BlockSpec((tm, tn), lambda i, j: (i, j)),
        out_shape=jax.ShapeDtypeStruct((m, n), out_dtype),
        compiler_params=_params("parallel", "parallel"),
        name="pmm",
    )(x, w)


def _cast_kernel(x_ref, o_ref):
    o_ref[...] = x_ref[...].astype(o_ref.dtype)


def cast_bf16(w, rows=256):
    w2 = w.reshape(-1, w.shape[-1])
    m, n = w2.shape
    rows = math.gcd(rows, m)
    blk = pl.BlockSpec((rows, n), lambda i: (i, 0))
    out = pl.pallas_call(_cast_kernel, grid=(m // rows,), in_specs=[blk], out_specs=blk,
                         out_shape=jax.ShapeDtypeStruct((m, n), BF16),
                         compiler_params=_params("parallel"), name="cast_bf16")(w2)
    return out.reshape(w.shape)


W_IN_BLOCK = 512


def _w_in_kernel(head_ref, tail_ref, o_ref, *, n_head_blocks):
    @pl.when(pl.program_id(1) < n_head_blocks)
    def _():
        o_ref[...] = head_ref[...].astype(o_ref.dtype)

    @pl.when(pl.program_id(1) >= n_head_blocks)
    def _():
        o_ref[...] = tail_ref[...].astype(o_ref.dtype)


def reorder_w_in(w_in):
    depth, d, _ = w_in.shape
    o = np.cumsum((0,) + IN_SPLITS)
    assert o[5] == COL_MLA and COL_MLA % W_IN_BLOCK == 0 and P_WIDTH % W_IN_BLOCK == 0
    piece = lambda i: w_in[..., o[i]:o[i + 1]]
    zeros = lambda n: jnp.zeros(w_in.shape[:-1] + (n,), w_in.dtype)
    tail = jnp.concatenate([piece(7), piece(8), zeros(MLA_BLOCK - IN_SPLITS[7] - IN_SPLITS[8]),
                            piece(9), piece(10), piece(11), piece(12),
                            piece(5), piece(6), piece(13), piece(14), zeros(P_WIDTH - COL_SMALL - 8 * GDN_HEADS)],
                           axis=-1)
    assert tail.shape[-1] == P_WIDTH - COL_MLA
    nh_blocks, n_blocks = COL_MLA // W_IN_BLOCK, P_WIDTH // W_IN_BLOCK
    return pl.pallas_call(
        functools.partial(_w_in_kernel, n_head_blocks=nh_blocks),
        grid=(depth, n_blocks),
        in_specs=[pl.BlockSpec((1, d, W_IN_BLOCK), lambda l, j: (l, 0, jnp.minimum(j, nh_blocks - 1))),
                  pl.BlockSpec((1, d, W_IN_BLOCK), lambda l, j: (l, 0, jnp.maximum(j - nh_blocks, 0)))],
        out_specs=pl.BlockSpec((1, d, W_IN_BLOCK), lambda l, j: (l, 0, j)),
        out_shape=jax.ShapeDtypeStruct((depth, d, P_WIDTH), BF16),
        compiler_params=_params("parallel", "arbitrary"),
        name="reorder_w_in",
    )(w_in, tail)


def _mod_row(n_lat_blocks, ctx_row):
    return lambda i, j: jnp.where(j >= n_lat_blocks, ctx_row, i)


def _norm_mod_kernel(x_ref, g_ref, sc_ref, sh_ref, o_ref):
    y = _rms(x_ref[0]) * g_ref[...]
    o_ref[0] = (y * (1.0 + sc_ref[0, 0]) + sh_ref[0, 0]).astype(o_ref.dtype)


def norm_mod(x, g, mods, i_sc, i_sh, n_lat, out_dtype=BF16, ts=256):
    b, t, d = x.shape
    ts = min(ts, t - n_lat) if t > n_lat else ts
    assert t % ts == 0 and n_lat % ts == 0
    row = _mod_row(n_lat // ts, b)
    return pl.pallas_call(
        _norm_mod_kernel,
        grid=(b, t // ts),
        in_specs=[pl.BlockSpec((1, ts, d), lambda i, j: (i, j, 0)),
                  pl.BlockSpec((1, d), lambda i, j: (0, 0)),
                  pl.BlockSpec((1, 1, 1, d), lambda i, j: (row(i, j), i_sc, 0, 0)),
                  pl.BlockSpec((1, 1, 1, d), lambda i, j: (row(i, j), i_sh, 0, 0))],
        out_specs=pl.BlockSpec((1, ts, d), lambda i, j: (i, j, 0)),
        out_shape=jax.ShapeDtypeStruct((b, t, d), out_dtype),
        compiler_params=_params("parallel", "parallel"),
        name="norm_mod",
    )(x, g.reshape(1, d), mods, mods)


def _norm_router_kernel(x_ref, g_ref, sc_ref, sh_ref, rw_ref, rb_ref, h_ref, idx_ref, w_ref):
    h = (_rms(x_ref[0]) * g_ref[...]) * (1.0 + sc_ref[0, 0]) + sh_ref[0, 0]
    h_ref[0] = h.astype(h_ref.dtype)
    logits = lax.dot_general(rw_ref[...], h, (((1,), (1,)), ((), ())), preferred_element_type=F32,
                             precision=lax.Precision.HIGHEST) + rb_ref[...]
    n_exp = logits.shape[0]
    expert = lax.broadcasted_iota(jnp.int32, logits.shape, 0)
    vals, ids = [], []
    for _ in range(TOP_K):
        best = jnp.max(logits, axis=0, keepdims=True)
        first = jnp.min(jnp.where(logits == best, expert, n_exp), axis=0, keepdims=True)
        vals.append(best)
        ids.append(first)
        logits = jnp.where(expert == first, -jnp.inf, logits)
    e = [jnp.exp(v - vals[0]) for v in vals]
    denom = functools.reduce(jnp.add, e)
    idx_ref[0] = jnp.concatenate(ids, axis=0)
    w_ref[0] = jnp.concatenate([t / denom for t in e], axis=0)


def norm_router(x, g, mods, n_lat, router_w, router_b, ts=256):
    b, t, d = x.shape
    ts = min(ts, t - n_lat) if t > n_lat else ts
    assert t % ts == 0 and n_lat % ts == 0
    row = _mod_row(n_lat // ts, b)
    rw_t = router_w.T
    rb = router_b.reshape(-1, 1)
    slot = pl.BlockSpec((1, TOP_K, ts), lambda i, j: (i, 0, j))
    h2, idx_t, w_t = pl.pallas_call(
        _norm_router_kernel,
        grid=(b, t // ts),
        in_specs=[pl.BlockSpec((1, ts, d), lambda i, j: (i, j, 0)),
                  pl.BlockSpec((1, d), lambda i, j: (0, 0)),
                  pl.BlockSpec((1, 1, 1, d), lambda i, j: (row(i, j), SC2, 0, 0)),
                  pl.BlockSpec((1, 1, 1, d), lambda i, j: (row(i, j), SH2, 0, 0)),
                  _full(rw_t), _full(rb)],
        out_specs=[pl.BlockSpec((1, ts, d), lambda i, j: (i, j, 0)), slot, slot],
        out_shape=[jax.ShapeDtypeStruct((b, t, d), BF16), jax.ShapeDtypeStruct((b, TOP_K, t), jnp.int32),
                   jax.ShapeDtypeStruct((b, TOP_K, t), F32)],
        compiler_params=_params("parallel", "parallel"),
        name="norm_router",
    )(x, g.reshape(1, d), mods, mods, rw_t, rb)
    return h2, idx_t.transpose(0, 2, 1), w_t.transpose(0, 2, 1)


def _combine_residual_kernel(x_ref, y_ref, w_ref, g_ref, o_ref):
    w = w_ref[0]
    f = functools.reduce(jnp.add, [w[:, k:k + 1] * y_ref[k, 0].astype(F32) for k in range(y_ref.shape[0])])
    o_ref[0] = x_ref[0] + g_ref[0, 0] * f


def combine_residual(x, y, w, mods, i_gate, n_lat, ts=256):
    b, t, d = x.shape
    n_slots = y.shape[0]
    ts = min(ts, t - n_lat) if t > n_lat else ts
    row = _mod_row(n_lat // ts, b)
    blk = pl.BlockSpec((1, ts, d), lambda i, j: (i, j, 0))
    return pl.pallas_call(
        _combine_residual_kernel,
        grid=(b, t // ts),
        in_specs=[blk, pl.BlockSpec((n_slots, 1, ts, d), lambda i, j: (0, i, j, 0)),
                  pl.BlockSpec((1, ts, n_slots), lambda i, j: (i, j, 0)),
                  pl.BlockSpec((1, 1, 1, d), lambda i, j: (row(i, j), i_gate, 0, 0))],
        out_specs=blk,
        out_shape=jax.ShapeDtypeStruct((b, t, d), F32),
        compiler_params=_params("parallel", "parallel"),
        name="combine_residual",
    )(x, y, w, mods)


def _pair_attn_kernel(q_ref, k_ref, v_ref, o_ref, *, tk, dv, scale):
    dq, dvp = q_ref.shape[-1] // 2, v_ref.shape[-1] // 2
    tq = q_ref.shape[1]
    sum_on_mxu = dvp > dv
    outs = []
    for hh in range(2):
        q = q_ref[0][:, hh * dq:(hh + 1) * dq]
        q = (q if scale is None else q * scale).astype(BF16)
        m = jnp.full((tq, 1), -jnp.inf, F32)
        l = jnp.zeros((tq, 1), F32)
        acc = jnp.zeros((tq, dvp), F32)
        for c in range(k_ref.shape[1] // tk):
            k = k_ref[0, c * tk:(c + 1) * tk, hh * dq:(hh + 1) * dq].astype(BF16)
            v = v_ref[0, c * tk:(c + 1) * tk, hh * dvp:(hh + 1) * dvp].astype(BF16)
            s = lax.dot_general(q, k, (((1,), (1,)), ((), ())), preferred_element_type=F32)
            m_new = jnp.maximum(m, jnp.max(s, axis=-1, keepdims=True))
            alpha = jnp.exp(m - m_new)
            p = jnp.exp(s - m_new)
            if not sum_on_mxu:
                l = alpha * l + jnp.sum(p, axis=-1, keepdims=True)
            acc = alpha * acc + jnp.dot(p.astype(BF16), v, preferred_element_type=F32)
            m = m_new
        outs.append(acc[:, :dv] / (acc[:, dv:dv + 1] if sum_on_mxu else l))
    o_ref[0] = jnp.concatenate(outs, axis=-1).astype(o_ref.dtype)


def pair_attention(q, k, v, n_heads, dq, dvp, dv, q_rows, kv_rows, cols=(0, 0, 0), scale=None, tq=512):
    b = q.shape[0]
    (q0, n_q), (kv0, t) = q_rows, kv_rows
    dq2, dvp2 = 2 * dq, 2 * dvp
    tq = min(tq, n_q)
    tk = next(c for c in (768, 512, 384, 256, 128) if t % c == 0)
    assert n_q % tq == 0 and q0 % tq == 0 and kv0 % t == 0
    assert dq2 % LANES == 0 and dvp2 % LANES == 0 and (2 * dv) % LANES == 0
    assert all(c % w == 0 for c, w in zip(cols, (dq2, dq2, dvp2)))
    qc, kc, vc = (c // w for c, w in zip(cols, (dq2, dq2, dvp2)))
    return pl.pallas_call(
        functools.partial(_pair_attn_kernel, tk=tk, dv=dv, scale=scale),
        grid=(b, n_heads // 2, n_q // tq),
        in_specs=[pl.BlockSpec((1, tq, dq2), lambda i, hp, n: (i, q0 // tq + n, qc + hp)),
                  pl.BlockSpec((1, t, dq2), lambda i, hp, n: (i, kv0 // t, kc + hp)),
                  pl.BlockSpec((1, t, dvp2), lambda i, hp, n: (i, kv0 // t, vc + hp))],
        out_specs=pl.BlockSpec((1, tq, 2 * dv), lambda i, hp, n: (i, n, hp)),
        out_shape=jax.ShapeDtypeStruct((b, n_q, n_heads * dv), BF16),
        compiler_params=_params("parallel", "parallel", "parallel"),
        name="pair_attention",
    )(q, k, v)


NA_QROWS = 8
NA_KROWS = 4
NA_NEG = -1e30


def na_bias_tables(rpb, rows):
    n_steps = rows // NA_QROWS
    wh, ww = min(NA_WIN_H, rows), NA_WIN_W
    n_kblocks = rows // NA_KROWS
    n_heads = rpb.shape[0]
    width = 2 * GRID_W - 1
    lo = GRID_W - NA_WIN_W
    padded = jnp.full(rpb.shape[:2] + (width,), NA_NEG, F32).at[..., lo:lo + 2 * NA_WIN_W - 1].set(rpb)
    tiles = jnp.stack([padded[..., GRID_W - 1 - qc:width - qc] for qc in range(GRID_W)], axis=2)
    qc, kc = np.arange(GRID_W)[:, None], np.arange(GRID_W)[None, :]
    c0 = np.clip(qc - ww // 2, 0, GRID_W - ww)
    tiles = jnp.where((kc >= c0) & (kc < c0 + ww), tiles, NA_NEG)
    masked = jnp.full((n_heads, GRID_W, GRID_W), NA_NEG, F32)
    tables = []
    for i in (0, min(1, n_steps - 1), n_steps - 1):
        q_tiles = []
        for il in range(NA_QROWS):
            q = NA_QROWS * i + il
            r0 = min(max(q - wh // 2, 0), rows - wh)
            k_tiles = []
            for ka in range(4 * NA_KROWS):
                blk = 2 * i - 1 + ka // NA_KROWS
                kr = NA_KROWS * blk + ka % NA_KROWS
                ok = 0 <= blk < n_kblocks and r0 <= kr < r0 + wh
                k_tiles.append(tiles[:, kr - q + NA_WIN_H - 1] if ok else masked)
            q_tiles.append(jnp.concatenate(k_tiles, axis=-1))
        tables.append(jnp.concatenate(q_tiles, axis=-2))
    return jnp.stack(tables)


def _na_kernel(q_ref, k0_ref, k1_ref, k2_ref, k3_ref, v0_ref, v1_ref, v2_ref, v3_ref,
               kc_ref, vc_ref, bias_ref, o_ref):
    kw = NA_KROWS * GRID_W
    outs = []
    for hh in range(2):
        sl = slice(hh * NA_DH, (hh + 1) * NA_DH)
        q = (q_ref[0][:, sl] * (NA_DH ** -0.5)).astype(BF16)
        ks = [r[0][:, sl].astype(BF16) for r in (k0_ref, k1_ref, k2_ref, k3_ref, kc_ref)]
        vs = [r[0][:, sl].astype(BF16) for r in (v0_ref, v1_ref, v2_ref, v3_ref, vc_ref)]
        ss = [lax.dot_general(q, kk, (((1,), (1,)), ((), ())), preferred_element_type=F32) for kk in ks]
        ss = [ss[j] + bias_ref[0, 0, hh, :, j * kw:(j + 1) * kw] for j in range(4)] + [ss[4]]
        m = functools.reduce(jnp.maximum, [jnp.max(t, axis=-1, keepdims=True) for t in ss])
        ps = [jnp.exp(t - m) for t in ss]
        l = functools.reduce(jnp.add, [jnp.sum(t, axis=-1, keepdims=True) for t in ps])
        o = functools.reduce(jnp.add, [jnp.dot(p.astype(BF16), vv, preferred_element_type=F32)
                                       for p, vv in zip(ps, vs)])
        outs.append(o / l)
    o_ref[0] = jnp.concatenate(outs, axis=-1).astype(o_ref.dtype)


def neighborhood_attention(p3, rpb, n_lat):
    b, t, _ = p3.shape
    c_len = t - n_lat
    rows = n_lat // GRID_W
    tq, tkv = NA_QROWS * GRID_W, NA_KROWS * GRID_W
    assert rows % NA_QROWS == 0 and rows >= 2 * NA_QROWS and n_lat % c_len == 0
    n_steps = rows // NA_QROWS
    n_kblocks = rows // NA_KROWS
    hw = 2 * NA_DH
    n_pairs = NA_HEADS // 2
    bias = na_bias_tables(rpb, rows)
    bias = bias.reshape(3, n_pairs, 2, *bias.shape[2:])

    def view(j, col0):
        return pl.BlockSpec((1, tkv, hw),
                            lambda hp, i, n: (i, jnp.clip(2 * n - 1 + j, 0, n_kblocks - 1), col0 + hp))

    ctx_spec = lambda col0: pl.BlockSpec((1, c_len, hw), lambda hp, i, n: (i, n_lat // c_len, col0 + hp))
    bias_spec = pl.BlockSpec(
        (1, 1, 2, tq, 4 * tkv),
        lambda hp, i, n: (jnp.where(n == 0, 0, jnp.where(n == n_steps - 1, 2, 1)), hp, 0, 0, 0))
    return pl.pallas_call(
        _na_kernel,
        grid=(n_pairs, b, n_steps),
        in_specs=[pl.BlockSpec((1, tq, hw), lambda hp, i, n: (i, n, hp))]
                 + [view(j, n_pairs) for j in range(4)] + [view(j, 2 * n_pairs) for j in range(4)]
                 + [ctx_spec(n_pairs), ctx_spec(2 * n_pairs), bias_spec],
        out_specs=pl.BlockSpec((1, tq, hw), lambda hp, i, n: (i, n, hp)),
        out_shape=jax.ShapeDtypeStruct((b, n_lat, NA_HEADS * NA_DH), BF16),
        compiler_params=_params("parallel", "parallel", "arbitrary"),
        name="neighborhood_attention",
    )(*([p3] * 11), bias)


def rope_tables(n_lat, n_ctx):
    half = MLA_ROPE // 2
    tpos = np.arange(n_lat)
    inv = ROPE_BASE ** (-jnp.arange(0, half, 2, dtype=F32) / half)
    pos = jnp.stack([tpos // GRID_W, tpos % GRID_W], axis=1).astype(F32)
    ang = pos[:, :, None] * inv
    ang = jnp.concatenate([ang, ang], axis=-1).reshape(n_lat, MLA_ROPE)
    pad = ((0, n_ctx), (MLA_NOPE, LANES - MLA_NOPE - MLA_ROPE))
    return (jnp.pad(jnp.cos(ang), pad, constant_values=1.0), jnp.pad(jnp.sin(ang), pad))


def _rope_swap():
    half = MLA_ROPE // 2
    quarter = half // 2
    r = np.zeros((MLA_ROPE, MLA_ROPE), np.float32)
    for j in range(MLA_ROPE):
        if j % half < quarter:
            r[j + quarter, j] = -1.0
        else:
            r[j - quarter, j] = 1.0
    return r


def mla_weights(w_uq, w_ukv):
    dqh = MLA_NOPE + MLA_ROPE
    lead = w_uq.shape[:-1]
    wq = w_uq.reshape(*lead, MLA_HEADS, dqh)
    padh = lambda a: jnp.pad(a, ((0, 0),) * (a.ndim - 1) + ((0, LANES - a.shape[-1]),))
    wq_p = padh(wq).reshape(*lead, MLA_HEADS * LANES)
    rot = jnp.einsum('...hr,rs->...hs', wq[..., MLA_NOPE:], jnp.asarray(_rope_swap()))
    wqr_p = jnp.pad(rot, ((0, 0),) * (rot.ndim - 1) + ((MLA_NOPE, LANES - dqh),)).reshape(*lead, MLA_HEADS * LANES)
    lead = w_ukv.shape[:-1]
    wkv = w_ukv.reshape(*lead, MLA_HEADS, MLA_NOPE + MLA_V)
    wk_p = padh(wkv[..., :MLA_NOPE]).reshape(*lead, MLA_HEADS * LANES)
    wv_p = padh(wkv[..., MLA_NOPE:]).reshape(*lead, MLA_HEADS * LANES)
    return wq_p.astype(BF16), wqr_p.astype(BF16), jnp.concatenate([wk_p, wv_p], axis=-1).astype(BF16)


def _mla_prep_kernel(p_ref, cos_ref, sin_ref, qg_ref, kvg_ref, wq_ref, wqr_ref, wkv_ref, e_ref, er_ref,
                     q_out, k_out, v_out):
    blk = p_ref[0]
    nq = _rms(blk[:, :MLA_Q_RANK]) * qg_ref[...]
    nkv = _rms(blk[:, MLA_Q_RANK:MLA_Q_RANK + MLA_KV_RANK]) * kvg_ref[...]
    kpe = blk[:, MLA_Q_RANK + MLA_KV_RANK:MLA_Q_RANK + MLA_KV_RANK + MLA_ROPE]
    cos, sin = cos_ref[...], sin_ref[...]
    qf, qr = _dot(nq, wq_ref[...]), _dot(nq, wqr_ref[...])
    kv = _dot(nkv, wkv_ref[...])
    kpe_tile = _dot(kpe, e_ref[...], precise=True) * cos + _dot(kpe, er_ref[...], precise=True) * sin
    scale = (MLA_NOPE + MLA_ROPE) ** -0.5
    for h in range(MLA_HEADS):
        sl = slice(h * LANES, (h + 1) * LANES)
        q_out[0, :, sl] = ((qf[:, sl] * cos + qr[:, sl] * sin) * scale).astype(q_out.dtype)
        k_out[0, :, sl] = (kv[:, sl] + kpe_tile).astype(k_out.dtype)
        ones_col = (lax.broadcasted_iota(jnp.int32, (1, LANES), 1) == MLA_V).astype(F32)
        v_out[0, :, sl] = (kv[:, MLA_HEADS * LANES + h * LANES:MLA_HEADS * LANES + (h + 1) * LANES]
                           + ones_col).astype(v_out.dtype)


def mla_prep(p3, cos, sin, q_norm_g, kv_norm_g, wq, wqr, wkv, tb=256):
    b, t, _ = p3.shape
    tb = next(c for c in (tb, 128) if t % c == 0)
    swap = _rope_swap()
    place = np.zeros((MLA_ROPE, LANES), np.float32)
    place[np.arange(MLA_ROPE), MLA_NOPE + np.arange(MLA_ROPE)] = 1.0
    e, er = jnp.asarray(place), jnp.asarray(swap @ place)
    qg, kvg = q_norm_g.reshape(1, -1), kv_norm_g.reshape(1, -1)
    row = lambda w: pl.BlockSpec((1, tb, w), lambda i, j: (i, j, 0))
    tab = pl.BlockSpec((tb, LANES), lambda i, j: (j, 0))
    return pl.pallas_call(
        _mla_prep_kernel,
        grid=(b, t // tb),
        in_specs=[pl.BlockSpec((1, tb, MLA_BLOCK), lambda i, j: (i, j, COL_MLA // MLA_BLOCK)), tab, tab]
                 + [_full(a) for a in (qg, kvg, wq, wqr, wkv, e, er)],
        out_specs=[row(MLA_HEADS * LANES)] * 3,
        out_shape=[jax.ShapeDtypeStruct((b, t, MLA_HEADS * LANES), BF16),
                   jax.ShapeDtypeStruct((b, t, MLA_HEADS * LANES), BF16),
                   jax.ShapeDtypeStruct((b, t, MLA_HEADS * LANES), BF16)],
        compiler_params=_params("parallel", "parallel"),
        name="mla_prep",
    )(p3, cos, sin, qg, kvg, wq, wqr, wkv, e, er)


def _gdn_prep_kernel(x_ref, prev_ref, next_ref, w_ref, o_ref, *, n_lat, n_tok):
    tb = x_ref.shape[1]
    start = pl.program_id(1) * tb
    x = x_ref[0]
    seg_first = jnp.logical_or(start == 0, start == n_lat)
    seg_last = jnp.logical_or(start + tb == n_lat, start + tb == n_tok)
    prev_row = jnp.where(seg_first, 0.0, prev_ref[0][7:8, :])
    next_row = jnp.where(seg_last, 0.0, next_ref[0][0:1, :])
    rows = lax.broadcasted_iota(jnp.int32, (tb, 1), 0)
    x_m1 = jnp.where(rows == 0, prev_row, pltpu.roll(x, 1, axis=0))
    x_p1 = jnp.where(rows == tb - 1, next_row, pltpu.roll(x, tb - 1, axis=0))
    y = x_m1 * w_ref[0:1, :] + x * w_ref[1:2, :] + x_p1 * w_ref[2:3, :]
    y = y * jax.nn.sigmoid(y)
    for g in range(2 * GDN_HEADS):
        sl = slice(g * GDN_DK, (g + 1) * GDN_DK)
        yh = y[:, sl]
        yh = yh * lax.rsqrt(jnp.sum(yh * yh, axis=-1, keepdims=True) + EPS)
        o_ref[0, :, sl] = yh * (GDN_DK ** -0.5) if g < GDN_HEADS else yh
    o_ref[0, :, 2 * GDN_HEADS * GDN_DK:] = y[:, 2 * GDN_HEADS * GDN_DK:]


def gdn_prep(p3, conv_w, n_lat, tb=256):
    b, t, _ = p3.shape
    w = GDN_HEADS * (2 * GDN_DK + GDN_DV)
    tb = min(tb, t - n_lat)
    assert t % tb == 0 and n_lat % tb == 0 and tb % 8 == 0 and COL_GDN % w == 0
    col = COL_GDN // w
    sub = tb // 8
    return pl.pallas_call(
        functools.partial(_gdn_prep_kernel, n_lat=n_lat, n_tok=t),
        grid=(b, t // tb),
        in_specs=[pl.BlockSpec((1, tb, w), lambda i, j: (i, j, col)),
                  pl.BlockSpec((1, 8, w), lambda i, j: (i, jnp.maximum(j * sub - 1, 0), col)),
                  pl.BlockSpec((1, 8, w), lambda i, j: (i, jnp.minimum((j + 1) * sub, t // 8 - 1), col)),
                  _full(conv_w)],
        out_specs=pl.BlockSpec((1, tb, w), lambda i, j: (i, j, 0)),
        out_shape=jax.ShapeDtypeStruct((b, t, w), F32),
        compiler_params=_params("parallel", "parallel"),
        name="gdn_prep",
    )(p3, p3, p3, conv_w)


def _order_masks(length, reverse):
    r = lax.broadcasted_iota(jnp.int32, (length, length), 0)
    c = lax.broadcasted_iota(jnp.int32, (length, length), 1)
    if reverse:
        return r <= c, r < c, r >= c
    return r >= c, r > c, r <= c


def _as_row(col, mask):
    return jnp.sum(jnp.where(mask, col, 0.0), axis=0, keepdims=True)


def _gdn_kernel(xf_ref, sf_ref, xb_ref, sb_ref, alog_ref, dtb_ref, of_ref, ob_ref, s_ref):
    @pl.when(pl.program_id(1) == 0)
    def _():
        s_ref[...] = jnp.zeros_like(s_ref)

    nh, dk, dv, length = GDN_HEADS, GDN_DK, GDN_DV, GDN_CHUNK
    n_sub = xf_ref.shape[1] // length
    r = lax.broadcasted_iota(jnp.int32, (length, length), 0)
    c = lax.broadcasted_iota(jnp.int32, (length, length), 1)
    eye = (r == c).astype(F32)
    chunks = []
    for sub in range(n_sub):
        seqs = []
        for d, (x_ref, sm_ref, o_ref) in enumerate(((xf_ref, sf_ref, of_ref), (xb_ref, sb_ref, ob_ref))):
            r0 = (n_sub - 1 - sub if d == 1 else sub) * length
            rows = slice(r0, r0 + length)
            incl, strict, incl_t = _order_masks(length, d == 1)
            small = sm_ref[0, rows, :]
            hs = slice(d * nh, (d + 1) * nh)
            la_col = -jnp.exp(alog_ref[:, hs]) * _softplus(small[:, hs] + dtb_ref[:, hs])
            be_col = jax.nn.sigmoid(small[:, 2 * nh + d * nh:2 * nh + (d + 1) * nh])
            g_cols = _dot(incl.astype(F32), la_col, precise=True)
            g_ends = jnp.sum(la_col, axis=0, keepdims=True)
            for h in range(nh):
                q = x_ref[0, rows, h * dk:(h + 1) * dk]
                k = x_ref[0, rows, (nh + h) * dk:(nh + h + 1) * dk]
                v = x_ref[0, rows, 2 * nh * dk + h * dv:2 * nh * dk + (h + 1) * dv]
                gc, ge, bc = g_cols[:, h:h + 1], g_ends[:, h:h + 1], be_col[:, h:h + 1]
                gr = _as_row(la_col[:, h:h + 1], incl_t)
                decay = jnp.exp(jnp.where(incl, gc - gr, -jnp.inf))
                qk = decay * _dot_nt(q, k)
                p = -(bc * jnp.where(strict, decay, 0.0) * _dot_nt(k, k))
                eg = jnp.exp(gc)
                seqs.append(dict(o_ref=o_ref, rows=rows, h=h, idx=d * nh + h, q_dec=eg * q, qk=qk, p=p, t=eye + p,
                                 rhs=jnp.concatenate([bc * v, (bc * eg) * k], axis=-1),
                                 k_end=jnp.exp(ge - gc) * k, e_end=jnp.exp(ge)))
        for _ in range(int(math.log2(length)) - 1):
            for s in seqs:
                s['p'] = _dot(s['p'], s['p'])
            for s in seqs:
                s['t'] = s['t'] + _dot(s['t'], s['p'])
        for s in seqs:
            s['sol'] = _dot(s['t'], s['rhs'])
        chunks.append(seqs)
    states = [s_ref[i] for i in range(2 * nh)]
    for seqs in chunks:
        for s in seqs:
            s['u'] = s['sol'][:, :dv] - _dot_nt(s['sol'][:, dv:], states[s['idx']])
        for s in seqs:
            h, s_old = s['h'], states[s['idx']]
            s['o_ref'][0, s['rows'], h * dv:(h + 1) * dv] = _dot_nt(s['q_dec'], s_old) + _dot(s['qk'], s['u'])
            states[s['idx']] = s['e_end'] * s_old + _dot_tn(s['u'], s['k_end'])
    for i, st in enumerate(states):
        s_ref[i] = st


def _mlstm_kernel(xf_ref, sf_ref, xb_ref, sb_ref, bi_ref, bf_ref, of_ref, ob_ref, c_ref, n_ref, m_ref):
    @pl.when(pl.program_id(1) == 0)
    def _():
        c_ref[...] = jnp.zeros_like(c_ref)
        n_ref[...] = jnp.zeros_like(n_ref)
        m_ref[...] = jnp.zeros_like(m_ref)

    nh, dk, dv, length = ML_HEADS, ML_DK, ML_DV, ML_CHUNK
    n_sub = xf_ref.shape[1] // length
    r = lax.broadcasted_iota(jnp.int32, (length, length), 0)
    c = lax.broadcasted_iota(jnp.int32, (length, length), 1)
    eye = r == c
    col_i, col_f = 4 * GDN_HEADS, 4 * GDN_HEADS + 2 * nh
    chunks = []
    for sub in range(n_sub):
        seqs = []
        for d, (x_ref, sm_ref, o_ref) in enumerate(((xf_ref, sf_ref, of_ref), (xb_ref, sb_ref, ob_ref))):
            r0 = (n_sub - 1 - sub if d == 1 else sub) * length
            rows = slice(r0, r0 + length)
            incl, _, incl_t = _order_masks(length, d == 1)
            small = sm_ref[0, rows, :]
            hs = slice(d * nh, (d + 1) * nh)
            i_col = small[:, col_i + d * nh:col_i + (d + 1) * nh] + bi_ref[:, hs]
            f_col = -_softplus(-(small[:, col_f + d * nh:col_f + (d + 1) * nh] + bf_ref[:, hs]))
            fc_cols = _dot(incl.astype(F32), f_col, precise=True)
            f_ends = jnp.sum(f_col, axis=0, keepdims=True)
            for h in range(nh):
                q = x_ref[0, rows, h * dk:(h + 1) * dk] * (dk ** -0.5)
                k = x_ref[0, rows, (nh + h) * dk:(nh + h + 1) * dk]
                v = x_ref[0, rows, 2 * nh * dk + h * dv:2 * nh * dk + (h + 1) * dv]
                fc, fe, ic = fc_cols[:, h:h + 1], f_ends[:, h:h + 1], i_col[:, h:h + 1]
                fr, ir = _as_row(f_col[:, h:h + 1], incl_t), _as_row(ic, eye)
                d_log = jnp.where(incl, fc - fr + ir, -jnp.inf)
                d_max = jnp.max(d_log, axis=-1, keepdims=True)
                w_log = fe - fc + ic
                w_max = jnp.max(w_log, axis=0, keepdims=True)
                seqs.append(dict(o_ref=o_ref, rows=rows, h=h, idx=d * nh + h, q=q, k=k, v=v, fc=fc, fe=fe,
                                 d_max=d_max, w_max=w_max, decay=jnp.exp(d_log - d_max), wk=jnp.exp(w_log - w_max)))
        for s in seqs:
            s['intra'] = s['decay'] * _dot_nt(s['q'], s['k'])
        for s in seqs:
            s['intra_v'] = _dot(s['intra'], s['v'])
            s['vk'] = _dot_tn(s['wk'] * s['v'], s['k'])
        for s in seqs:
            s['intra_sum'] = jnp.sum(s['intra'], axis=-1, keepdims=True)
            s['k_sum'] = jnp.sum(s['wk'] * s['k'], axis=0, keepdims=True)
        chunks.append(seqs)
    n_seq = 2 * nh
    cs = [c_ref[i] for i in range(n_seq)]
    ns = [n_ref[i] for i in range(n_seq)]
    ms = [m_ref[i][:, 0:1] for i in range(n_seq)]
    for seqs in chunks:
        for s in seqs:
            m_old = ms[s['idx']]
            s['m_q'] = jnp.maximum(s['fc'] + m_old, s['d_max'])
            s['inter'], s['scale'] = jnp.exp(s['fc'] + m_old - s['m_q']), jnp.exp(s['d_max'] - s['m_q'])
            s['m_new'] = jnp.maximum(s['fe'] + m_old, s['w_max'])
            s['dec'], s['w_scale'] = jnp.exp(s['fe'] + m_old - s['m_new']), jnp.exp(s['w_max'] - s['m_new'])
        for s in seqs:
            s['qc'] = _dot_nt(s['q'], cs[s['idx']])
        for s in seqs:
            h, idx = s['h'], s['idx']
            num = s['inter'] * s['qc'] + s['scale'] * s['intra_v']
            den = (s['inter'] * jnp.sum(s['q'] * ns[idx], axis=-1, keepdims=True) + s['scale'] * s['intra_sum'])
            s['o_ref'][0, s['rows'], h * dv:(h + 1) * dv] = num / jnp.maximum(jnp.abs(den), jnp.exp(-s['m_q']))
            cs[idx] = s['dec'] * cs[idx] + s['w_scale'] * s['vk']
            ns[idx] = s['dec'] * ns[idx] + s['w_scale'] * s['k_sum']
            ms[idx] = s['m_new']
    for i in range(n_seq):
        c_ref[i], n_ref[i] = cs[i], ns[i]
        m_ref[i] = jnp.broadcast_to(ms[i], m_ref.shape[1:])


SCAN_BLOCK = 128


def bidirectional_scan(kernel_fn, x, x_col, x_w, p3, params, n_lat, out_w, scratch_shapes, name):
    b, t, _ = p3.shape
    length = SCAN_BLOCK
    nl, nc = n_lat // length, t // length
    ncx = nc - nl
    assert t % length == 0 and n_lat % length == 0
    fwd = lambda n: jnp.where(n < ncx, nl + n, n - ncx)
    bwd = lambda n: jnp.where(n < ncx, nc - 1 - n, nl - 1 - (n - ncx))
    specs = lambda f: [pl.BlockSpec((1, length, x_w), lambda i, n: (i, f(n), x_col)),
                       pl.BlockSpec((1, length, LANES), lambda i, n: (i, f(n), COL_SMALL // LANES))]
    out = lambda f: pl.BlockSpec((1, length, out_w), lambda i, n: (i, f(n), 0))
    return pl.pallas_call(
        kernel_fn,
        grid=(b, nc),
        in_specs=specs(fwd) + specs(bwd) + [_full(a) for a in params],
        out_specs=[out(fwd), out(bwd)],
        out_shape=[jax.ShapeDtypeStruct((b, t, out_w), F32)] * 2,
        scratch_shapes=scratch_shapes,
        compiler_params=_params("parallel", "arbitrary"),
        name=name,
    )(x, p3, x, p3, *params)


def gdn_scan(xg, p3, a_log, dt_bias, n_lat):
    state = pltpu.VMEM((2 * GDN_HEADS, GDN_DV, GDN_DK), F32)
    params = (a_log.reshape(1, -1), dt_bias.reshape(1, -1))
    return bidirectional_scan(_gdn_kernel, xg, 0, xg.shape[-1], p3, params, n_lat,
                              GDN_HEADS * GDN_DV, [state], "gdn_scan")


def mlstm_scan(p3, b_if, n_lat):
    n_seq = 2 * ML_HEADS
    w = ML_HEADS * (2 * ML_DK + ML_DV)
    scratch = [pltpu.VMEM((n_seq, ML_DV, ML_DK), F32), pltpu.VMEM((n_seq, 1, ML_DK), F32),
               pltpu.VMEM((n_seq, 1, LANES), F32)]
    params = (b_if[:, 0].reshape(1, -1), b_if[:, 1].reshape(1, -1))
    return bidirectional_scan(_mlstm_kernel, p3, COL_ML // w, w, p3, params, n_lat,
                              ML_HEADS * ML_DV, scratch, "mlstm_scan")


def _gated_head_norm(of_ref, ob_ref, gate, g_ref, n_heads):
    o = of_ref[0] + ob_ref[0]
    d = o.shape[-1] // n_heads
    parts = [_rms(o[:, h * d:(h + 1) * d]) * g_ref[...] for h in range(n_heads)]
    return (jnp.concatenate(parts, axis=-1) * gate).astype(BF16)


def _merge_kernel(h_ref, ya_ref, yc_ref, gof_ref, gob_ref, gg_ref, mof_ref, mob_ref, mg_ref, x_ref, g1_ref,
                  gn_ref, mn_ref, wmg_ref, bmg_ref, wbo_ref, wout_ref, o_ref):
    d = o_ref.shape[-1]
    h = h_ref[0]
    gg = gg_ref[0]
    yb = _gated_head_norm(gof_ref, gob_ref, gg * jax.nn.sigmoid(gg), gn_ref, GDN_HEADS)
    yd = _gated_head_norm(mof_ref, mob_ref, jax.nn.sigmoid(mg_ref[0]), mn_ref, ML_HEADS)
    acc = jnp.zeros((h.shape[0], d), F32)
    for n, y in enumerate((ya_ref[0], yb, yc_ref[0], yd)):
        z = jnp.dot(y, wbo_ref[n], preferred_element_type=F32)
        gate = jnp.dot(h, wmg_ref[:, n * d:(n + 1) * d], preferred_element_type=F32) + bmg_ref[:, n * d:(n + 1) * d]
        acc = acc + jax.nn.sigmoid(gate) * z
    m = jnp.dot(acc.astype(BF16), wout_ref[...], preferred_element_type=F32)
    o_ref[0] = x_ref[0] + g1_ref[0, 0] * m


def merge_residual(h, ya, yc, gdn_o, ml_o, p3, x, mods, n_lat, n_rows, layer, gdn_norm_g, ml_norm_g,
                   w_mg, b_mg, w_bo, w_out, ts=256):
    b, _, d = x.shape
    ts = min(ts, n_rows - n_lat) if n_rows > n_lat else ts
    assert n_rows % ts == 0 and n_lat % ts == 0
    mod_row = _mod_row(n_lat // ts, b)
    row = lambda w, col=0: pl.BlockSpec((1, ts, w), lambda i, j: (i, j, col))
    depth = w_mg.shape[0]
    b_mg = b_mg.reshape(depth, 1, -1)
    gn, mn = gdn_norm_g.reshape(depth, 1, -1), ml_norm_g.reshape(depth, 1, -1)
    return pl.pallas_call(
        _merge_kernel,
        grid=(b, n_rows // ts),
        in_specs=[row(d), row(BRANCH_W), row(BRANCH_W),
                  row(BRANCH_W), row(BRANCH_W), row(BRANCH_W, COL_GDN_GATE // BRANCH_W),
                  row(BRANCH_W), row(BRANCH_W), row(BRANCH_W, COL_ML_GATE // BRANCH_W),
                  row(d), pl.BlockSpec((1, 1, 1, d), lambda i, j: (mod_row(i, j), G1, 0, 0))]
                 + [_layer(a, layer) for a in (gn, mn, w_mg, b_mg, w_bo, w_out)],
        out_specs=row(d),
        out_shape=jax.ShapeDtypeStruct((b, n_rows, d), F32),
        compiler_params=_params("parallel", "parallel"),
        name="merge_residual",
    )(h, ya, yc, gdn_o[0], gdn_o[1], p3, ml_o[0], ml_o[1], p3, x, mods, gn, mn, w_mg, b_mg, w_bo, w_out)


def _expert_kernel(be_ref, x_ref, wgu_ref, bgu_ref, wdn_ref, bdn_ref, o_ref, wgu_s, wdn_s):
    i = pl.program_id(0)
    fresh = jnp.logical_or(i == 0, be_ref[i] != be_ref[jnp.maximum(i - 1, 0)])

    @pl.when(fresh)
    def _():
        wgu_s[...] = wgu_ref[0].astype(BF16)
        wdn_s[...] = wdn_ref[0].astype(BF16)

    gu = jnp.dot(x_ref[...], wgu_s[...], preferred_element_type=F32) + bgu_ref[0]
    gate = jnp.minimum(gu[:, :D_FF], SWIGLU_LIMIT)
    up = jnp.clip(gu[:, D_FF:], -SWIGLU_LIMIT, SWIGLU_LIMIT)
    act = (up + 1.0) * gate * jax.nn.sigmoid(SWIGLU_ALPHA * gate)
    y = jnp.dot(act.astype(BF16), wdn_s[...], preferred_element_type=F32) + bdn_ref[0]
    o_ref[...] = y.astype(o_ref.dtype)


def expert_blocks(buf, block_e, layer, w_gu, b_gu, w_dn, b_dn):
    rows, d = buf.shape
    n_blocks = rows // MOE_ROWS
    depth, e = w_gu.shape[:2]
    grid_spec = pltpu.PrefetchScalarGridSpec(
        num_scalar_prefetch=1,
        grid=(n_blocks,),
        in_specs=[pl.BlockSpec((MOE_ROWS, d), lambda i, be: (i, 0)),
                  pl.BlockSpec((None, 1, d, 2 * D_FF), lambda i, be: (layer, be[i], 0, 0)),
                  pl.BlockSpec((None, 1, 1, 2 * D_FF), lambda i, be: (layer, be[i], 0, 0)),
                  pl.BlockSpec((None, 1, D_FF, d), lambda i, be: (layer, be[i], 0, 0)),
                  pl.BlockSpec((None, 1, 1, d), lambda i, be: (layer, be[i], 0, 0))],
        out_specs=pl.BlockSpec((MOE_ROWS, d), lambda i, be: (i, 0)),
        scratch_shapes=[pltpu.VMEM((d, 2 * D_FF), BF16), pltpu.VMEM((D_FF, d), BF16)],
    )
    return pl.pallas_call(
        _expert_kernel,
        grid_spec=grid_spec,
        out_shape=jax.ShapeDtypeStruct((rows, d), BF16),
        compiler_params=_params("arbitrary"),
        name="expert_blocks",
    )(block_e, buf, w_gu, b_gu.reshape(depth, e, 1, -1), w_dn, b_dn.reshape(depth, e, 1, -1))


def moe_ffn(t, top_i, layer, w_gu, b_gu, w_dn, b_dn):
    n, d = t.shape
    nk = n * TOP_K
    e_flat = top_i.reshape(nk)
    onehot = (e_flat[:, None] == jnp.arange(N_EXPERTS)[None, :]).astype(jnp.int32)
    csum = jnp.cumsum(onehot, axis=0)
    rank = jnp.take_along_axis(csum, e_flat[:, None], axis=1)[:, 0] - 1
    counts = csum[-1]
    padded = (counts + MOE_ROWS - 1) // MOE_ROWS * MOE_ROWS
    pad_end = jnp.cumsum(padded)
    dest = (pad_end - padded)[e_flat] + rank
    n_blocks = -(-nk // MOE_ROWS) + N_EXPERTS
    block_start = jnp.arange(n_blocks, dtype=pad_end.dtype) * MOE_ROWS
    block_e = jnp.minimum(jnp.sum(pad_end[None, :] <= block_start[:, None], axis=1), N_EXPERTS - 1).astype(jnp.int32)
    order = jnp.argsort(e_flat, stable=True).astype(jnp.int32)
    in_group = (block_start - (pad_end - padded)[block_e])[:, None] + jnp.arange(MOE_ROWS, dtype=pad_end.dtype)[None, :]
    first = (jnp.cumsum(counts) - counts)[block_e][:, None]
    used = in_group < counts[block_e][:, None]
    src = jnp.where(used, order[jnp.clip(first + in_group, 0, nk - 1)] // TOP_K, n).reshape(n_blocks * MOE_ROWS)
    buf = jnp.concatenate([t, jnp.zeros((1, d), t.dtype)], axis=0)[src]
    y = expert_blocks(buf, block_e, layer, w_gu, b_gu, w_dn, b_dn)
    return y[dest.reshape(n, TOP_K).T.reshape(nk)].reshape(TOP_K, n, d)


def kernel(x, c, ctx, c_ctx, w_ada, b_ada, norm1_g, norm2_g, w_in, na_rpb, gdn_conv_w, gdn_a_log, gdn_dt_bias, gdn_norm_g, mla_q_norm_g, mla_w_uq, mla_kv_norm_g, mla_w_ukv, ml_b_if, ml_norm_g, w_mg, b_mg, w_bo, w_out, router_w, router_b, w_gu, b_gu, w_dn, b_dn, final_g):
    depth = w_ada.shape[0]
    bsz, n_lat, d = x.shape
    n_ctx = ctx.shape[1]
    cond = jnp.concatenate([jax.nn.silu(c), jax.nn.silu(c_ctx)[None]], axis=0)
    cond = jnp.pad(cond, ((0, (-cond.shape[0]) % 8), (0, 0)))
    w_in_r = reorder_w_in(w_in)
    w_mg_b, w_bo_b, w_out_b = (cast_bf16(t) for t in (w_mg, w_bo, w_out))
    wq, wqr, wkv = mla_weights(mla_w_uq, mla_w_ukv)
    cos, sin = rope_tables(n_lat, n_ctx)
    xs = jnp.concatenate([x, ctx], axis=1)
    n_tok = n_lat + n_ctx
    for l in range(depth):
        last = l == depth - 1
        n_rows = n_lat if last else n_tok
        mods = (pmm(cond, w_ada, l, tn=1024) + b_ada[l]).reshape(cond.shape[0], 6, 1, d)
        h = norm_mod(xs, norm1_g[l], mods, SC1, SH1, n_lat)
        p3 = pmm(h.reshape(bsz * n_tok, d), w_in_r, l, tm=1024, tn=768).reshape(bsz, n_tok, P_WIDTH)
        ya = neighborhood_attention(p3, na_rpb[l], n_lat)
        gdn_o = gdn_scan(gdn_prep(p3, gdn_conv_w[l], n_lat), p3, gdn_a_log[l], gdn_dt_bias[l], n_lat)
        ml_o = mlstm_scan(p3, ml_b_if[l], n_lat)
        q, k, v = mla_prep(p3, cos, sin, mla_q_norm_g[l], mla_kv_norm_g[l], wq[l], wqr[l], wkv[l])
        lat_rows, ctx_rows = (0, n_lat), (n_lat, n_ctx)
        yc = pair_attention(q, k, v, MLA_HEADS, LANES, LANES, MLA_V, lat_rows, (0, n_tok))
        if not last:
            w = NA_HEADS * NA_DH
            ya_ctx = pair_attention(p3, p3, p3, NA_HEADS, NA_DH, NA_DH, NA_DH, ctx_rows, ctx_rows,
                                    cols=(COL_NA, COL_NA + w, COL_NA + 2 * w), scale=NA_DH ** -0.5)
            yc_ctx = pair_attention(q, k, v, MLA_HEADS, LANES, LANES, MLA_V, ctx_rows, ctx_rows)
            ya = jnp.concatenate([ya, ya_ctx], axis=1)
            yc = jnp.concatenate([yc, yc_ctx], axis=1)
        xs = merge_residual(h, ya, yc, gdn_o, ml_o, p3, xs, mods, n_lat, n_rows, l, gdn_norm_g, ml_norm_g,
                            w_mg_b, b_mg, w_bo_b, w_out_b)
        h2, top_i, top_w = norm_router(xs, norm2_g[l], mods, n_lat, router_w[l], router_b[l])
        y = moe_ffn(h2.reshape(bsz * n_rows, d), top_i.reshape(bsz * n_rows, TOP_K), l, w_gu, b_gu, w_dn, b_dn)
        xs = combine_residual(xs, y.reshape(TOP_K, bsz, n_rows, d), top_w, mods, G2, n_lat)
    return norm_mod(xs, final_g, jnp.zeros((cond.shape[0], 6, 1, d), F32), SC1, SH1, n_lat, out_dtype=F32)
```

```python
import functools
import math

import numpy as np
import jax
import jax.numpy as jnp
from jax import lax
from jax.experimental import pallas as pl
from jax.experimental.pallas import tpu as pltpu

F32 = jnp.float32
BF16 = jnp.bfloat16

GRID_W = 64
EPS = 1e-6
ROPE_BASE = 10000.0
NA_HEADS, NA_DH, NA_WIN_H, NA_WIN_W = 8, 64, 8, 16
GDN_HEADS, GDN_DK, GDN_DV, GDN_CHUNK = 4, 128, 128, 64
MLA_HEADS, MLA_NOPE, MLA_ROPE, MLA_V, MLA_Q_RANK, MLA_KV_RANK = 8, 64, 32, 64, 256, 128
ML_HEADS, ML_DK, ML_DV, ML_CHUNK = 4, 64, 128, 64
N_BRANCH, BRANCH_W = 4, 512
N_EXPERTS, TOP_K, D_FF = 32, 4, 1024
SWIGLU_ALPHA, SWIGLU_LIMIT = 1.702, 7.0
MOE_ROWS = 256
LANES = 128

IN_SPLITS = (
    NA_HEADS * NA_DH, NA_HEADS * NA_DH, NA_HEADS * NA_DH,
    GDN_HEADS * (2 * GDN_DK + GDN_DV), GDN_HEADS * GDN_DV,
    2 * GDN_HEADS, 2 * GDN_HEADS,
    MLA_Q_RANK, MLA_KV_RANK + MLA_ROPE,
    ML_HEADS * ML_DK, ML_HEADS * ML_DK, ML_HEADS * ML_DV, ML_HEADS * ML_DV,
    2 * ML_HEADS, 2 * ML_HEADS,
)
COL_NA = 0
COL_GDN = 1536
COL_GDN_GATE = 3072
COL_MLA = 3584
COL_ML = 4096
COL_ML_GATE = 5120
COL_SMALL = 5632
P_WIDTH = 6144
MLA_BLOCK = COL_ML - COL_MLA
SH1, SC1, G1, SH2, SC2, G2 = range(6)

VMEM_LIMIT_BYTES = 56 * 1024 * 1024


def _params(*sem):
    return pltpu.CompilerParams(dimension_semantics=sem, vmem_limit_bytes=VMEM_LIMIT_BYTES)


def _full(a):
    return pl.BlockSpec(a.shape, lambda *_: (0,) * a.ndim)


def _layer(a, layer):
    return pl.BlockSpec((None,) + a.shape[1:], lambda *_: (layer,) + (0,) * (a.ndim - 1))


def _dot(a, b, precise=False):
    if precise:
        return jnp.dot(a, b, preferred_element_type=F32, precision=lax.Precision.HIGHEST)
    return jnp.dot(a.astype(BF16), b.astype(BF16), preferred_element_type=F32)


def _dot_nt(a, b):
    return lax.dot_general(a.astype(BF16), b.astype(BF16), (((1,), (1,)), ((), ())), preferred_element_type=F32)


def _dot_tn(a, b):
    return lax.dot_general(a.astype(BF16), b.astype(BF16), (((0,), (0,)), ((), ())), preferred_element_type=F32)


def _softplus(x):
    return jnp.maximum(x, 0.0) + jnp.log(1.0 + jnp.exp(-jnp.abs(x)))


def _rms(x):
    return x * lax.rsqrt(jnp.mean(x * x, axis=-1, keepdims=True) + EPS)


def _mm_kernel(x_ref, w_ref, o_ref):
    o_ref[...] = _dot(x_ref[...], w_ref[...]).astype(o_ref.dtype)


def pmm(x, w, layer, out_dtype=F32, tm=512, tn=512):
    m, k = x.shape
    n = w.shape[-1]
    tm, tn = math.gcd(tm, m), min(tn, n)
    if n % tn:
        tn = n
    assert m % tm == 0 and n % tn == 0, (m, n, tm, tn)
    return pl.pallas_call(
        _mm_kernel,
        grid=(m // tm, n // tn),
        in_specs=[pl.BlockSpec((tm, k), lambda i, j: (i, 0)),
                  pl.BlockSpec((None, k, tn), lambda i, j: (layer, 0, j))],
        out_specs=pl.BlockSpec((tm, tn), lambda i, j: (i, j)),
        out_shape=jax.ShapeDtypeStruct((m, n), out_dtype),
        compiler_params=_params("parallel", "parallel"),
        name="pmm",
    )(x, w)


def _cast_kernel(x_ref, o_ref):
    o_ref[...] = x_ref[...].astype(o_ref.dtype)


def cast_bf16(w, rows=256):
    w2 = w.reshape(-1, w.shape[-1])
    m, n = w2.shape
    rows = math.gcd(rows, m)
    blk = pl.BlockSpec((rows, n), lambda i: (i, 0))
    out = pl.pallas_call(_cast_kernel, grid=(m // rows,), in_specs=[blk], out_specs=blk,
                         out_shape=jax.ShapeDtypeStruct((m, n), BF16),
                         compiler_params=_params("parallel"), name="cast_bf16")(w2)
    return out.reshape(w.shape)


W_IN_BLOCK = 512


def _w_in_kernel(head_ref, tail_ref, o_ref, *, n_head_blocks):
    @pl.when(pl.program_id(1) < n_head_blocks)
    def _():
        o_ref[...] = head_ref[...].astype(o_ref.dtype)

    @pl.when(pl.program_id(1) >= n_head_blocks)
    def _():
        o_ref[...] = tail_ref[...].astype(o_ref.dtype)


def reorder_w_in(w_in):
    depth, d, _ = w_in.shape
    o = np.cumsum((0,) + IN_SPLITS)
    assert o[5] == COL_MLA and COL_MLA % W_IN_BLOCK == 0 and P_WIDTH % W_IN_BLOCK == 0
    piece = lambda i: w_in[..., o[i]:o[i + 1]]
    zeros = lambda n: jnp.zeros(w_in.shape[:-1] + (n,), w_in.dtype)
    tail = jnp.concatenate([piece(7), piece(8), zeros(MLA_BLOCK - IN_SPLITS[7] - IN_SPLITS[8]),
                            piece(9), piece(10), piece(11), piece(12),
                            piece(5), piece(6), piece(13), piece(14), zeros(P_WIDTH - COL_SMALL - 8 * GDN_HEADS)],
                           axis=-1)
    assert tail.shape[-1] == P_WIDTH - COL_MLA
    nh_blocks, n_blocks = COL_MLA // W_IN_BLOCK, P_WIDTH // W_IN_BLOCK
    return pl.pallas_call(
        functools.partial(_w_in_kernel, n_head_blocks=nh_blocks),
        grid=(depth, n_blocks),
        in_specs=[pl.BlockSpec((1, d, W_IN_BLOCK), lambda l, j: (l, 0, jnp.minimum(j, nh_blocks - 1))),
                  pl.BlockSpec((1, d, W_IN_BLOCK), lambda l, j: (l, 0, jnp.maximum(j - nh_blocks, 0)))],
        out_specs=pl.BlockSpec((1, d, W_IN_BLOCK), lambda l, j: (l, 0, j)),
        out_shape=jax.ShapeDtypeStruct((depth, d, P_WIDTH), BF16),
        compiler_params=_params("parallel", "arbitrary"),
        name="reorder_w_in",
    )(w_in, tail)


def _mod_row(n_lat_blocks, ctx_row):
    return lambda i, j: jnp.where(j >= n_lat_blocks, ctx_row, i)


def _norm_mod_kernel(x_ref, g_ref, sc_ref, sh_ref, o_ref):
    y = _rms(x_ref[0]) * g_ref[...]
    o_ref[0] = (y * (1.0 + sc_ref[0, 0]) + sh_ref[0, 0]).astype(o_ref.dtype)


def norm_mod(x, g, mods, i_sc, i_sh, n_lat, out_dtype=BF16, ts=256):
    b, t, d = x.shape
    ts = min(ts, t - n_lat) if t > n_lat else ts
    assert t % ts == 0 and n_lat % ts == 0
    row = _mod_row(n_lat // ts, b)
    return pl.pallas_call(
        _norm_mod_kernel,
        grid=(b, t // ts),
        in_specs=[pl.BlockSpec((1, ts, d), lambda i, j: (i, j, 0)),
                  pl.BlockSpec((1, d), lambda i, j: (0, 0)),
                  pl.BlockSpec((1, 1, 1, d), lambda i, j: (row(i, j), i_sc, 0, 0)),
                  pl.BlockSpec((1, 1, 1, d), lambda i, j: (row(i, j), i_sh, 0, 0))],
        out_specs=pl.BlockSpec((1, ts, d), lambda i, j: (i, j, 0)),
        out_shape=jax.ShapeDtypeStruct((b, t, d), out_dtype),
        compiler_params=_params("parallel", "parallel"),
        name="norm_mod",
    )(x, g.reshape(1, d), mods, mods)


def _norm_router_kernel(x_ref, g_ref, sc_ref, sh_ref, rw_ref, rb_ref, h_ref, idx_ref, w_ref):
    h = (_rms(x_ref[0]) * g_ref[...]) * (1.0 + sc_ref[0, 0]) + sh_ref[0, 0]
    h_ref[0] = h.astype(h_ref.dtype)
    logits = lax.dot_general(rw_ref[...], h, (((1,), (1,)), ((), ())), preferred_element_type=F32,
                             precision=lax.Precision.HIGHEST) + rb_ref[...]
    n_exp = logits.shape[0]
    expert = lax.broadcasted_iota(jnp.int32, logits.shape, 0)
    vals, ids = [], []
    for _ in range(TOP_K):
        best = jnp.max(logits, axis=0, keepdims=True)
        first = jnp.min(jnp.where(logits == best, expert, n_exp), axis=0, keepdims=True)
        vals.append(best)
        ids.append(first)
        logits = jnp.where(expert == first, -jnp.inf, logits)
    e = [jnp.exp(v - vals[0]) for v in vals]
    denom = functools.reduce(jnp.add, e)
    idx_ref[0] = jnp.concatenate(ids, axis=0)
    w_ref[0] = jnp.concatenate([t / denom for t in e], axis=0)


def norm_router(x, g, mods, n_lat, router_w, router_b, ts=256):
    b, t, d = x.shape
    ts = min(ts, t - n_lat) if t > n_lat else ts
    assert t % ts == 0 and n_lat % ts == 0
    row = _mod_row(n_lat // ts, b)
    rw_t = router_w.T
    rb = router_b.reshape(-1, 1)
    slot = pl.BlockSpec((1, TOP_K, ts), lambda i, j: (i, 0, j))
    h2, idx_t, w_t = pl.pallas_call(
        _norm_router_kernel,
        grid=(b, t // ts),
        in_specs=[pl.BlockSpec((1, ts, d), lambda i, j: (i, j, 0)),
                  pl.BlockSpec((1, d), lambda i, j: (0, 0)),
                  pl.BlockSpec((1, 1, 1, d), lambda i, j: (row(i, j), SC2, 0, 0)),
                  pl.BlockSpec((1, 1, 1, d), lambda i, j: (row(i, j), SH2, 0, 0)),
                  _full(rw_t), _full(rb)],
        out_specs=[pl.BlockSpec((1, ts, d), lambda i, j: (i, j, 0)), slot, slot],
        out_shape=[jax.ShapeDtypeStruct((b, t, d), BF16), jax.ShapeDtypeStruct((b, TOP_K, t), jnp.int32),
                   jax.ShapeDtypeStruct((b, TOP_K, t), F32)],
        compiler_params=_params("parallel", "parallel"),
        name="norm_router",
    )(x, g.reshape(1, d), mods, mods, rw_t, rb)
    return h2, idx_t.transpose(0, 2, 1), w_t.transpose(0, 2, 1)


def _combine_residual_kernel(x_ref, y_ref, w_ref, g_ref, o_ref):
    w = w_ref[0]
    f = functools.reduce(jnp.add, [w[:, k:k + 1] * y_ref[k, 0].astype(F32) for k in range(y_ref.shape[0])])
    o_ref[0] = x_ref[0] + g_ref[0, 0] * f


def combine_residual(x, y, w, mods, i_gate, n_lat, ts=256):
    b, t, d = x.shape
    n_slots = y.shape[0]
    ts = min(ts, t - n_lat) if t > n_lat else ts
    row = _mod_row(n_lat // ts, b)
    blk = pl.BlockSpec((1, ts, d), lambda i, j: (i, j, 0))
    return pl.pallas_call(
        _combine_residual_kernel,
        grid=(b, t // ts),
        in_specs=[blk, pl.BlockSpec((n_slots, 1, ts, d), lambda i, j: (0, i, j, 0)),
                  pl.BlockSpec((1, ts, n_slots), lambda i, j: (i, j, 0)),
                  pl.BlockSpec((1, 1, 1, d), lambda i, j: (row(i, j), i_gate, 0, 0))],
        out_specs=blk,
        out_shape=jax.ShapeDtypeStruct((b, t, d), F32),
        compiler_params=_params("parallel", "parallel"),
        name="combine_residual",
    )(x, y, w, mods)


def _pair_attn_kernel(q_ref, k_ref, v_ref, o_ref, *, tk, dv, scale):
    dq, dvp = q_ref.shape[-1] // 2, v_ref.shape[-1] // 2
    tq = q_ref.shape[1]
    sum_on_mxu = dvp > dv
    outs = []
    for hh in range(2):
        q = q_ref[0][:, hh * dq:(hh + 1) * dq]
        q = (q if scale is None else q * scale).astype(BF16)
        m = jnp.full((tq, 1), -jnp.inf, F32)
        l = jnp.zeros((tq, 1), F32)
        acc = jnp.zeros((tq, dvp), F32)
        for c in range(k_ref.shape[1] // tk):
            k = k_ref[0, c * tk:(c + 1) * tk, hh * dq:(hh + 1) * dq].astype(BF16)
            v = v_ref[0, c * tk:(c + 1) * tk, hh * dvp:(hh + 1) * dvp].astype(BF16)
            s = lax.dot_general(q, k, (((1,), (1,)), ((), ())), preferred_element_type=F32)
            m_new = jnp.maximum(m, jnp.max(s, axis=-1, keepdims=True))
            alpha = jnp.exp(m - m_new)
            p = jnp.exp(s - m_new)
            if not sum_on_mxu:
                l = alpha * l + jnp.sum(p, axis=-1, keepdims=True)
            acc = alpha * acc + jnp.dot(p.astype(BF16), v, preferred_element_type=F32)
            m = m_new
        outs.append(acc[:, :dv] / (acc[:, dv:dv + 1] if sum_on_mxu else l))
    o_ref[0] = jnp.concatenate(outs, axis=-1).astype(o_ref.dtype)


def pair_attention(q, k, v, n_heads, dq, dvp, dv, q_rows, kv_rows, cols=(0, 0, 0), scale=None, tq=512):
    b = q.shape[0]
    (q0, n_q), (kv0, t) = q_rows, kv_rows
    dq2, dvp2 = 2 * dq, 2 * dvp
    tq = min(tq, n_q)
    tk = next(c for c in (768, 512, 384, 256, 128) if t % c == 0)
    assert n_q % tq == 0 and q0 % tq == 0 and kv0 % t == 0
    assert dq2 % LANES == 0 and dvp2 % LANES == 0 and (2 * dv) % LANES == 0
    assert all(c % w == 0 for c, w in zip(cols, (dq2, dq2, dvp2)))
    qc, kc, vc = (c // w for c, w in zip(cols, (dq2, dq2, dvp2)))
    return pl.pallas_call(
        functools.partial(_pair_attn_kernel, tk=tk, dv=dv, scale=scale),
        grid=(b, n_heads // 2, n_q // tq),
        in_specs=[pl.BlockSpec((1, tq, dq2), lambda i, hp, n: (i, q0 // tq + n, qc + hp)),
                  pl.BlockSpec((1, t, dq2), lambda i, hp, n: (i, kv0 // t, kc + hp)),
                  pl.BlockSpec((1, t, dvp2), lambda i, hp, n: (i, kv0 // t, vc + hp))],
        out_specs=pl.BlockSpec((1, tq, 2 * dv), lambda i, hp, n: (i, n, hp)),
        out_shape=jax.ShapeDtypeStruct((b, n_q, n_heads * dv), BF16),
        compiler_params=_params("parallel", "parallel", "parallel"),
        name="pair_attention",
    )(q, k, v)


NA_QROWS = 8
NA_KROWS = 4
NA_NEG = -1e30


def na_bias_tables(rpb, rows):
    n_steps = rows // NA_QROWS
    wh, ww = min(NA_WIN_H, rows), NA_WIN_W
    n_kblocks = rows // NA_KROWS
    n_heads = rpb.shape[0]
    width = 2 * GRID_W - 1
    lo = GRID_W - NA_WIN_W
    padded = jnp.full(rpb.shape[:2] + (width,), NA_NEG, F32).at[..., lo:lo + 2 * NA_WIN_W - 1].set(rpb)
    tiles = jnp.stack([padded[..., GRID_W - 1 - qc:width - qc] for qc in range(GRID_W)], axis=2)
    qc, kc = np.arange(GRID_W)[:, None], np.arange(GRID_W)[None, :]
    c0 = np.clip(qc - ww // 2, 0, GRID_W - ww)
    tiles = jnp.where((kc >= c0) & (kc < c0 + ww), tiles, NA_NEG)
    masked = jnp.full((n_heads, GRID_W, GRID_W), NA_NEG, F32)
    tables = []
    for i in (0, min(1, n_steps - 1), n_steps - 1):
        q_tiles = []
        for il in range(NA_QROWS):
            q = NA_QROWS * i + il
            r0 = min(max(q - wh // 2, 0), rows - wh)
            k_tiles = []
            for ka in range(4 * NA_KROWS):
                blk = 2 * i - 1 + ka // NA_KROWS
                kr = NA_KROWS * blk + ka % NA_KROWS
                ok = 0 <= blk < n_kblocks and r0 <= kr < r0 + wh
                k_tiles.append(tiles[:, kr - q + NA_WIN_H - 1] if ok else masked)
            q_tiles.append(jnp.concatenate(k_tiles, axis=-1))
        tables.append(jnp.concatenate(q_tiles, axis=-2))
    return jnp.stack(tables)


def _na_kernel(q_ref, k0_ref, k1_ref, k2_ref, k3_ref, v0_ref, v1_ref, v2_ref, v3_ref,
               kc_ref, vc_ref, bias_ref, o_ref):
    kw = NA_KROWS * GRID_W
    lane = lax.broadcasted_iota(jnp.int32, (1, 2 * NA_DH), 1)
    q2 = q_ref[0] * (NA_DH ** -0.5)
    ks = [r[0].astype(BF16) for r in (k0_ref, k1_ref, k2_ref, k3_ref, kc_ref)]
    v2 = [r[0] for r in (v0_ref, v1_ref, v2_ref, v3_ref, vc_ref)]
    outs = []
    for hh in range(2):
        own = (lane >= NA_DH) if hh else (lane < NA_DH)
        q = jnp.where(own, q2, 0.0).astype(BF16)
        vs = [jnp.where(own, t, 1.0).astype(BF16) for t in v2]
        ss = [lax.dot_general(q, kk, (((1,), (1,)), ((), ())), preferred_element_type=F32) for kk in ks]
        ss = [ss[j] + bias_ref[0, 0, hh, :, j * kw:(j + 1) * kw] for j in range(4)] + [ss[4]]
        m = functools.reduce(jnp.maximum, [jnp.max(t, axis=-1, keepdims=True) for t in ss])
        o = functools.reduce(jnp.add, [jnp.dot(jnp.exp(t - m).astype(BF16), vv, preferred_element_type=F32)
                                       for t, vv in zip(ss, vs)])
        denom = o[:, 0:1] if hh else o[:, NA_DH:NA_DH + 1]
        outs.append(o / denom)
    o_ref[0] = jnp.where(lane < NA_DH, outs[0], outs[1]).astype(o_ref.dtype)


def neighborhood_attention(p3, rpb, n_lat):
    b, t, _ = p3.shape
    c_len = t - n_lat
    rows = n_lat // GRID_W
    tq, tkv = NA_QROWS * GRID_W, NA_KROWS * GRID_W
    assert rows % NA_QROWS == 0 and rows >= 2 * NA_QROWS and n_lat % c_len == 0
    n_steps = rows // NA_QROWS
    n_kblocks = rows // NA_KROWS
    hw = 2 * NA_DH
    n_pairs = NA_HEADS // 2
    bias = na_bias_tables(rpb, rows)
    bias = bias.reshape(3, n_pairs, 2, *bias.shape[2:])

    def view(j, col0):
        return pl.BlockSpec((1, tkv, hw),
                            lambda hp, i, n: (i, jnp.clip(2 * n - 1 + j, 0, n_kblocks - 1), col0 + hp))

    ctx_spec = lambda col0: pl.BlockSpec((1, c_len, hw), lambda hp, i, n: (i, n_lat // c_len, col0 + hp))
    bias_spec = pl.BlockSpec(
        (1, 1, 2, tq, 4 * tkv),
        lambda hp, i, n: (jnp.where(n == 0, 0, jnp.where(n == n_steps - 1, 2, 1)), hp, 0, 0, 0))
    return pl.pallas_call(
        _na_kernel,
        grid=(n_pairs, b, n_steps),
        in_specs=[pl.BlockSpec((1, tq, hw), lambda hp, i, n: (i, n, hp))]
                 + [view(j, n_pairs) for j in range(4)] + [view(j, 2 * n_pairs) for j in range(4)]
                 + [ctx_spec(n_pairs), ctx_spec(2 * n_pairs), bias_spec],
        out_specs=pl.BlockSpec((1, tq, hw), lambda hp, i, n: (i, n, hp)),
        out_shape=jax.ShapeDtypeStruct((b, n_lat, NA_HEADS * NA_DH), BF16),
        compiler_params=_params("parallel", "parallel", "arbitrary"),
        name="neighborhood_attention",
    )(*([p3] * 11), bias)


def rope_tables(n_lat, n_ctx):
    half = MLA_ROPE // 2
    tpos = np.arange(n_lat)
    inv = ROPE_BASE ** (-jnp.arange(0, half, 2, dtype=F32) / half)
    pos = jnp.stack([tpos // GRID_W, tpos % GRID_W], axis=1).astype(F32)
    ang = pos[:, :, None] * inv
    ang = jnp.concatenate([ang, ang], axis=-1).reshape(n_lat, MLA_ROPE)
    pad = ((0, n_ctx), (MLA_NOPE, LANES - MLA_NOPE - MLA_ROPE))
    return (jnp.pad(jnp.cos(ang), pad, constant_values=1.0), jnp.pad(jnp.sin(ang), pad))


def _rope_swap():
    half = MLA_ROPE // 2
    quarter = half // 2
    r = np.zeros((MLA_ROPE, MLA_ROPE), np.float32)
    for j in range(MLA_ROPE):
        if j % half < quarter:
            r[j + quarter, j] = -1.0
        else:
            r[j - quarter, j] = 1.0
    return r


def mla_weights(w_uq, w_ukv):
    dqh = MLA_NOPE + MLA_ROPE
    lead = w_uq.shape[:-1]
    wq = w_uq.reshape(*lead, MLA_HEADS, dqh)
    padh = lambda a: jnp.pad(a, ((0, 0),) * (a.ndim - 1) + ((0, LANES - a.shape[-1]),))
    wq_p = padh(wq).reshape(*lead, MLA_HEADS * LANES)
    rot = jnp.einsum('...hr,rs->...hs', wq[..., MLA_NOPE:], jnp.asarray(_rope_swap()))
    wqr_p = jnp.pad(rot, ((0, 0),) * (rot.ndim - 1) + ((MLA_NOPE, LANES - dqh),)).reshape(*lead, MLA_HEADS * LANES)
    lead = w_ukv.shape[:-1]
    wkv = w_ukv.reshape(*lead, MLA_HEADS, MLA_NOPE + MLA_V)
    wk_p = padh(wkv[..., :MLA_NOPE]).reshape(*lead, MLA_HEADS * LANES)
    wv_p = padh(wkv[..., MLA_NOPE:]).reshape(*lead, MLA_HEADS * LANES)
    return wq_p.astype(BF16), wqr_p.astype(BF16), jnp.concatenate([wk_p, wv_p], axis=-1).astype(BF16)


def _mla_prep_kernel(p_ref, cos_ref, sin_ref, qg_ref, kvg_ref, wq_ref, wqr_ref, wkv_ref, e_ref, er_ref,
                     q_out, k_out, v_out):
    blk = p_ref[0]
    nq = _rms(blk[:, :MLA_Q_RANK]) * qg_ref[...]
    nkv = _rms(blk[:, MLA_Q_RANK:MLA_Q_RANK + MLA_KV_RANK]) * kvg_ref[...]
    kpe = blk[:, MLA_Q_RANK + MLA_KV_RANK:MLA_Q_RANK + MLA_KV_RANK + MLA_ROPE]
    cos, sin = cos_ref[...], sin_ref[...]
    qf, qr = _dot(nq, wq_ref[...]), _dot(nq, wqr_ref[...])
    kv = _dot(nkv, wkv_ref[...])
    kpe_tile = _dot(kpe, e_ref[...], precise=True) * cos + _dot(kpe, er_ref[...], precise=True) * sin
    scale = (MLA_NOPE + MLA_ROPE) ** -0.5
    for h in range(MLA_HEADS):
        sl = slice(h * LANES, (h + 1) * LANES)
        q_out[0, :, sl] = ((qf[:, sl] * cos + qr[:, sl] * sin) * scale).astype(q_out.dtype)
        k_out[0, :, sl] = (kv[:, sl] + kpe_tile).astype(k_out.dtype)
        ones_col = (lax.broadcasted_iota(jnp.int32, (1, LANES), 1) == MLA_V).astype(F32)
        v_out[0, :, sl] = (kv[:, MLA_HEADS * LANES + h * LANES:MLA_HEADS * LANES + (h + 1) * LANES]
                           + ones_col).astype(v_out.dtype)


def mla_prep(p3, cos, sin, q_norm_g, kv_norm_g, wq, wqr, wkv, tb=256):
    b, t, _ = p3.shape
    tb = next(c for c in (tb, 128) if t % c == 0)
    swap = _rope_swap()
    place = np.zeros((MLA_ROPE, LANES), np.float32)
    place[np.arange(MLA_ROPE), MLA_NOPE + np.arange(MLA_ROPE)] = 1.0
    e, er = jnp.asarray(place), jnp.asarray(swap @ place)
    qg, kvg = q_norm_g.reshape(1, -1), kv_norm_g.reshape(1, -1)
    row = lambda w: pl.BlockSpec((1, tb, w), lambda i, j: (i, j, 0))
    tab = pl.BlockSpec((tb, LANES), lambda i, j: (j, 0))
    return pl.pallas_call(
        _mla_prep_kernel,
        grid=(b, t // tb),
        in_specs=[pl.BlockSpec((1, tb, MLA_BLOCK), lambda i, j: (i, j, COL_MLA // MLA_BLOCK)), tab, tab]
                 + [_full(a) for a in (qg, kvg, wq, wqr, wkv, e, er)],
        out_specs=[row(MLA_HEADS * LANES)] * 3,
        out_shape=[jax.ShapeDtypeStruct((b, t, MLA_HEADS * LANES), BF16),
                   jax.ShapeDtypeStruct((b, t, MLA_HEADS * LANES), BF16),
                   jax.ShapeDtypeStruct((b, t, MLA_HEADS * LANES), BF16)],
        compiler_params=_params("parallel", "parallel"),
        name="mla_prep",
    )(p3, cos, sin, qg, kvg, wq, wqr, wkv, e, er)


def _gdn_prep_kernel(x_ref, prev_ref, next_ref, w_ref, o_ref, *, n_lat, n_tok):
    tb = x_ref.shape[1]
    start = pl.program_id(1) * tb
    x = x_ref[0]
    seg_first = jnp.logical_or(start == 0, start == n_lat)
    seg_last = jnp.logical_or(start + tb == n_lat, start + tb == n_tok)
    prev_row = jnp.where(seg_first, 0.0, prev_ref[0][7:8, :])
    next_row = jnp.where(seg_last, 0.0, next_ref[0][0:1, :])
    rows = lax.broadcasted_iota(jnp.int32, (tb, 1), 0)
    x_m1 = jnp.where(rows == 0, prev_row, pltpu.roll(x, 1, axis=0))
    x_p1 = jnp.where(rows == tb - 1, next_row, pltpu.roll(x, tb - 1, axis=0))
    y = x_m1 * w_ref[0:1, :] + x * w_ref[1:2, :] + x_p1 * w_ref[2:3, :]
    y = y * jax.nn.sigmoid(y)
    for g in range(2 * GDN_HEADS):
        sl = slice(g * GDN_DK, (g + 1) * GDN_DK)
        yh = y[:, sl]
        yh = yh * lax.rsqrt(jnp.sum(yh * yh, axis=-1, keepdims=True) + EPS)
        o_ref[0, :, sl] = yh * (GDN_DK ** -0.5) if g < GDN_HEADS else yh
    o_ref[0, :, 2 * GDN_HEADS * GDN_DK:] = y[:, 2 * GDN_HEADS * GDN_DK:]


def gdn_prep(p3, conv_w, n_lat, tb=256):
    b, t, _ = p3.shape
    w = GDN_HEADS * (2 * GDN_DK + GDN_DV)
    tb = min(tb, t - n_lat)
    assert t % tb == 0 and n_lat % tb == 0 and tb % 8 == 0 and COL_GDN % w == 0
    col = COL_GDN // w
    sub = tb // 8
    return pl.pallas_call(
        functools.partial(_gdn_prep_kernel, n_lat=n_lat, n_tok=t),
        grid=(b, t // tb),
        in_specs=[pl.BlockSpec((1, tb, w), lambda i, j: (i, j, col)),
                  pl.BlockSpec((1, 8, w), lambda i, j: (i, jnp.maximum(j * sub - 1, 0), col)),
                  pl.BlockSpec((1, 8, w), lambda i, j: (i, jnp.minimum((j + 1) * sub, t // 8 - 1), col)),
                  _full(conv_w)],
        out_specs=pl.BlockSpec((1, tb, w), lambda i, j: (i, j, 0)),
        out_shape=jax.ShapeDtypeStruct((b, t, w), F32),
        compiler_params=_params("parallel", "parallel"),
        name="gdn_prep",
    )(p3, p3, p3, conv_w)


def _order_masks(length, reverse):
    r = lax.broadcasted_iota(jnp.int32, (length, length), 0)
    c = lax.broadcasted_iota(jnp.int32, (length, length), 1)
    if reverse:
        return r <= c, r < c, r >= c
    return r >= c, r > c, r <= c


def _as_row(col, mask):
    return jnp.sum(jnp.where(mask, col, 0.0), axis=0, keepdims=True)


def _gdn_kernel(xf_ref, sf_ref, xb_ref, sb_ref, alog_ref, dtb_ref, of_ref, ob_ref, s_ref):
    @pl.when(pl.program_id(1) == 0)
    def _():
        s_ref[...] = jnp.zeros_like(s_ref)

    nh, dk, dv, length = GDN_HEADS, GDN_DK, GDN_DV, GDN_CHUNK
    n_sub = xf_ref.shape[1] // length
    r = lax.broadcasted_iota(jnp.int32, (length, length), 0)
    c = lax.broadcasted_iota(jnp.int32, (length, length), 1)
    eye = (r == c).astype(F32)
    chunks = []
    for sub in range(n_sub):
        seqs = []
        for d, (x_ref, sm_ref, o_ref) in enumerate(((xf_ref, sf_ref, of_ref), (xb_ref, sb_ref, ob_ref))):
            r0 = (n_sub - 1 - sub if d == 1 else sub) * length
            rows = slice(r0, r0 + length)
            incl, strict, incl_t = _order_masks(length, d == 1)
            small = sm_ref[0, rows, :]
            hs = slice(d * nh, (d + 1) * nh)
            la_col = -jnp.exp(alog_ref[:, hs]) * _softplus(small[:, hs] + dtb_ref[:, hs])
            be_col = jax.nn.sigmoid(small[:, 2 * nh + d * nh:2 * nh + (d + 1) * nh])
            g_cols = _dot(incl.astype(F32), la_col, precise=True)
            g_ends = jnp.sum(la_col, axis=0, keepdims=True)
            for h in range(nh):
                q = x_ref[0, rows, h * dk:(h + 1) * dk]
                k = x_ref[0, rows, (nh + h) * dk:(nh + h + 1) * dk]
                v = x_ref[0, rows, 2 * nh * dk + h * dv:2 * nh * dk + (h + 1) * dv]
                gc, ge, bc = g_cols[:, h:h + 1], g_ends[:, h:h + 1], be_col[:, h:h + 1]
                gr = _as_row(la_col[:, h:h + 1], incl_t)
                decay = jnp.exp(jnp.where(incl, gc - gr, -jnp.inf))
                qk = decay * _dot_nt(q, k)
                p = -(bc * jnp.where(strict, decay, 0.0) * _dot_nt(k, k))
                eg = jnp.exp(gc)
                seqs.append(dict(o_ref=o_ref, rows=rows, h=h, idx=d * nh + h, q_dec=eg * q, qk=qk, p=p, t=eye + p,
                                 rhs=jnp.concatenate([bc * v, (bc * eg) * k], axis=-1),
                                 k_end=jnp.exp(ge - gc) * k, e_end=jnp.exp(ge)))
        for _ in range(int(math.log2(length)) - 1):
            for s in seqs:
                s['p'] = _dot(s['p'], s['p'])
            for s in seqs:
                s['t'] = s['t'] + _dot(s['t'], s['p'])
        for s in seqs:
            s['sol'] = _dot(s['t'], s['rhs'])
        chunks.append(seqs)
    states = [s_ref[i] for i in range(2 * nh)]
    for seqs in chunks:
        for s in seqs:
            s['u'] = s['sol'][:, :dv] - _dot_nt(s['sol'][:, dv:], states[s['idx']])
        for s in seqs:
            h, s_old = s['h'], states[s['idx']]
            s['o_ref'][0, s['rows'], h * dv:(h + 1) * dv] = _dot_nt(s['q_dec'], s_old) + _dot(s['qk'], s['u'])
            states[s['idx']] = s['e_end'] * s_old + _dot_tn(s['u'], s['k_end'])
    for i, st in enumerate(states):
        s_ref[i] = st


def _mlstm_kernel(xf_ref, sf_ref, xb_ref, sb_ref, bi_ref, bf_ref, of_ref, ob_ref, c_ref, n_ref, m_ref):
    @pl.when(pl.program_id(1) == 0)
    def _():
        c_ref[...] = jnp.zeros_like(c_ref)
        n_ref[...] = jnp.zeros_like(n_ref)
        m_ref[...] = jnp.zeros_like(m_ref)

    nh, dk, dv, length = ML_HEADS, ML_DK, ML_DV, ML_CHUNK
    n_sub = xf_ref.shape[1] // length
    r = lax.broadcasted_iota(jnp.int32, (length, length), 0)
    c = lax.broadcasted_iota(jnp.int32, (length, length), 1)
    eye = r == c
    col_i, col_f = 4 * GDN_HEADS, 4 * GDN_HEADS + 2 * nh
    chunks = []
    for sub in range(n_sub):
        seqs = []
        for d, (x_ref, sm_ref, o_ref) in enumerate(((xf_ref, sf_ref, of_ref), (xb_ref, sb_ref, ob_ref))):
            r0 = (n_sub - 1 - sub if d == 1 else sub) * length
            rows = slice(r0, r0 + length)
            incl, _, incl_t = _order_masks(length, d == 1)
            small = sm_ref[0, rows, :]
            hs = slice(d * nh, (d + 1) * nh)
            i_col = small[:, col_i + d * nh:col_i + (d + 1) * nh] + bi_ref[:, hs]
            f_col = -_softplus(-(small[:, col_f + d * nh:col_f + (d + 1) * nh] + bf_ref[:, hs]))
            fc_cols = _dot(incl.astype(F32), f_col, precise=True)
            f_ends = jnp.sum(f_col, axis=0, keepdims=True)
            for h in range(nh):
                q = x_ref[0, rows, h * dk:(h + 1) * dk] * (dk ** -0.5)
                k = x_ref[0, rows, (nh + h) * dk:(nh + h + 1) * dk]
                v = x_ref[0, rows, 2 * nh * dk + h * dv:2 * nh * dk + (h + 1) * dv]
                fc, fe, ic = fc_cols[:, h:h + 1], f_ends[:, h:h + 1], i_col[:, h:h + 1]
                fr, ir = _as_row(f_col[:, h:h + 1], incl_t), _as_row(ic, eye)
                d_log = jnp.where(incl, fc - fr + ir, -jnp.inf)
                d_max = jnp.max(d_log, axis=-1, keepdims=True)
                w_log = fe - fc + ic
                w_max = jnp.max(w_log, axis=0, keepdims=True)
                seqs.append(dict(o_ref=o_ref, rows=rows, h=h, idx=d * nh + h, q=q, k=k, v=v, fc=fc, fe=fe,
                                 d_max=d_max, w_max=w_max, decay=jnp.exp(d_log - d_max), wk=jnp.exp(w_log - w_max)))
        for s in seqs:
            s['intra'] = s['decay'] * _dot_nt(s['q'], s['k'])
        for s in seqs:
            s['intra_v'] = _dot(s['intra'], s['v'])
            s['vk'] = _dot_tn(s['wk'] * s['v'], s['k'])
        for s in seqs:
            s['intra_sum'] = jnp.sum(s['intra'], axis=-1, keepdims=True)
            s['k_sum'] = jnp.sum(s['wk'] * s['k'], axis=0, keepdims=True)
        chunks.append(seqs)
    n_seq = 2 * nh
    cs = [c_ref[i] for i in range(n_seq)]
    ns = [n_ref[i] for i in range(n_seq)]
    ms = [m_ref[i][:, 0:1] for i in range(n_seq)]
    for seqs in chunks:
        for s in seqs:
            m_old = ms[s['idx']]
            s['m_q'] = jnp.maximum(s['fc'] + m_old, s['d_max'])
            s['inter'], s['scale'] = jnp.exp(s['fc'] + m_old - s['m_q']), jnp.exp(s['d_max'] - s['m_q'])
            s['m_new'] = jnp.maximum(s['fe'] + m_old, s['w_max'])
            s['dec'], s['w_scale'] = jnp.exp(s['fe'] + m_old - s['m_new']), jnp.exp(s['w_max'] - s['m_new'])
        for s in seqs:
            s['qc'] = _dot_nt(s['q'], cs[s['idx']])
        for s in seqs:
            h, idx = s['h'], s['idx']
            num = s['inter'] * s['qc'] + s['scale'] * s['intra_v']
            den = (s['inter'] * jnp.sum(s['q'] * ns[idx], axis=-1, keepdims=True) + s['scale'] * s['intra_sum'])
            s['o_ref'][0, s['rows'], h * dv:(h + 1) * dv] = num / jnp.maximum(jnp.abs(den), jnp.exp(-s['m_q']))
            cs[idx] = s['dec'] * cs[idx] + s['w_scale'] * s['vk']
            ns[idx] = s['dec'] * ns[idx] + s['w_scale'] * s['k_sum']
            ms[idx] = s['m_new']
    for i in range(n_seq):
        c_ref[i], n_ref[i] = cs[i], ns[i]
        m_ref[i] = jnp.broadcast_to(ms[i], m_ref.shape[1:])


SCAN_BLOCK = 128


def bidirectional_scan(kernel_fn, x, x_col, x_w, p3, params, n_lat, out_w, scratch_shapes, name):
    b, t, _ = p3.shape
    length = SCAN_BLOCK
    nl, nc = n_lat // length, t // length
    ncx = nc - nl
    assert t % length == 0 and n_lat % length == 0
    fwd = lambda n: jnp.where(n < ncx, nl + n, n - ncx)
    bwd = lambda n: jnp.where(n < ncx, nc - 1 - n, nl - 1 - (n - ncx))
    specs = lambda f: [pl.BlockSpec((1, length, x_w), lambda i, n: (i, f(n), x_col)),
                       pl.BlockSpec((1, length, LANES), lambda i, n: (i, f(n), COL_SMALL // LANES))]
    out = lambda f: pl.BlockSpec((1, length, out_w), lambda i, n: (i, f(n), 0))
    return pl.pallas_call(
        kernel_fn,
        grid=(b, nc),
        in_specs=specs(fwd) + specs(bwd) + [_full(a) for a in params],
        out_specs=[out(fwd), out(bwd)],
        out_shape=[jax.ShapeDtypeStruct((b, t, out_w), F32)] * 2,
        scratch_shapes=scratch_shapes,
        compiler_params=_params("parallel", "arbitrary"),
        name=name,
    )(x, p3, x, p3, *params)


def gdn_scan(xg, p3, a_log, dt_bias, n_lat):
    state = pltpu.VMEM((2 * GDN_HEADS, GDN_DV, GDN_DK), F32)
    params = (a_log.reshape(1, -1), dt_bias.reshape(1, -1))
    return bidirectional_scan(_gdn_kernel, xg, 0, xg.shape[-1], p3, params, n_lat,
                              GDN_HEADS * GDN_DV, [state], "gdn_scan")


def mlstm_scan(p3, b_if, n_lat):
    n_seq = 2 * ML_HEADS
    w = ML_HEADS * (2 * ML_DK + ML_DV)
    scratch = [pltpu.VMEM((n_seq, ML_DV, ML_DK), F32), pltpu.VMEM((n_seq, 1, ML_DK), F32),
               pltpu.VMEM((n_seq, 1, LANES), F32)]
    params = (b_if[:, 0].reshape(1, -1), b_if[:, 1].reshape(1, -1))
    return bidirectional_scan(_mlstm_kernel, p3, COL_ML // w, w, p3, params, n_lat,
                              ML_HEADS * ML_DV, scratch, "mlstm_scan")


def _gated_head_norm(of_ref, ob_ref, gate, g_ref, n_heads):
    o = of_ref[0] + ob_ref[0]
    d = o.shape[-1] // n_heads
    parts = [_rms(o[:, h * d:(h + 1) * d]) * g_ref[...] for h in range(n_heads)]
    return (jnp.concatenate(parts, axis=-1) * gate).astype(BF16)


def _merge_kernel(h_ref, ya_ref, yc_ref, gof_ref, gob_ref, gg_ref, mof_ref, mob_ref, mg_ref, x_ref, g1_ref,
                  gn_ref, mn_ref, wmg_ref, bmg_ref, wbo_ref, wout_ref, o_ref):
    d = o_ref.shape[-1]
    h = h_ref[0]
    gg = gg_ref[0]
    yb = _gated_head_norm(gof_ref, gob_ref, gg * jax.nn.sigmoid(gg), gn_ref, GDN_HEADS)
    yd = _gated_head_norm(mof_ref, mob_ref, jax.nn.sigmoid(mg_ref[0]), mn_ref, ML_HEADS)
    acc = jnp.zeros((h.shape[0], d), F32)
    for n, y in enumerate((ya_ref[0], yb, yc_ref[0], yd)):
        z = jnp.dot(y, wbo_ref[n], preferred_element_type=F32)
        gate = jnp.dot(h, wmg_ref[:, n * d:(n + 1) * d], preferred_element_type=F32) + bmg_ref[:, n * d:(n + 1) * d]
        acc = acc + jax.nn.sigmoid(gate) * z
    m = jnp.dot(acc.astype(BF16), wout_ref[...], preferred_element_type=F32)
    o_ref[0] = x_ref[0] + g1_ref[0, 0] * m


def merge_residual(h, ya, yc, gdn_o, ml_o, p3, x, mods, n_lat, n_rows, layer, gdn_norm_g, ml_norm_g,
                   w_mg, b_mg, w_bo, w_out, ts=256):
    b, _, d = x.shape
    ts = min(ts, n_rows - n_lat) if n_rows > n_lat else ts
    assert n_rows % ts == 0 and n_lat % ts == 0
    mod_row = _mod_row(n_lat // ts, b)
    row = lambda w, col=0: pl.BlockSpec((1, ts, w), lambda i, j: (i, j, col))
    depth = w_mg.shape[0]
    b_mg = b_mg.reshape(depth, 1, -1)
    gn, mn = gdn_norm_g.reshape(depth, 1, -1), ml_norm_g.reshape(depth, 1, -1)
    return pl.pallas_call(
        _merge_kernel,
        grid=(b, n_rows // ts),
        in_specs=[row(d), row(BRANCH_W), row(BRANCH_W),
                  row(BRANCH_W), row(BRANCH_W), row(BRANCH_W, COL_GDN_GATE // BRANCH_W),
                  row(BRANCH_W), row(BRANCH_W), row(BRANCH_W, COL_ML_GATE // BRANCH_W),
                  row(d), pl.BlockSpec((1, 1, 1, d), lambda i, j: (mod_row(i, j), G1, 0, 0))]
                 + [_layer(a, layer) for a in (gn, mn, w_mg, b_mg, w_bo, w_out)],
        out_specs=row(d),
        out_shape=jax.ShapeDtypeStruct((b, n_rows, d), F32),
        compiler_params=_params("parallel", "parallel"),
        name="merge_residual",
    )(h, ya, yc, gdn_o[0], gdn_o[1], p3, ml_o[0], ml_o[1], p3, x, mods, gn, mn, w_mg, b_mg, w_bo, w_out)


def _expert_kernel(be_ref, x_ref, wgu_ref, bgu_ref, wdn_ref, bdn_ref, o_ref, wgu_s, wdn_s):
    i = pl.program_id(0)
    fresh = jnp.logical_or(i == 0, be_ref[i] != be_ref[jnp.maximum(i - 1, 0)])

    @pl.when(fresh)
    def _():
        wgu_s[...] = wgu_ref[0].astype(BF16)
        wdn_s[...] = wdn_ref[0].astype(BF16)

    gu = jnp.dot(x_ref[...], wgu_s[...], preferred_element_type=F32) + bgu_ref[0]
    gate = jnp.minimum(gu[:, :D_FF], SWIGLU_LIMIT)
    up = jnp.clip(gu[:, D_FF:], -SWIGLU_LIMIT, SWIGLU_LIMIT)
    act = (up + 1.0) * gate * jax.nn.sigmoid(SWIGLU_ALPHA * gate)
    y = jnp.dot(act.astype(BF16), wdn_s[...], preferred_element_type=F32) + bdn_ref[0]
    o_ref[...] = y.astype(o_ref.dtype)


def expert_blocks(buf, block_e, layer, w_gu, b_gu, w_dn, b_dn):
    rows, d = buf.shape
    n_blocks = rows // MOE_ROWS
    depth, e = w_gu.shape[:2]
    grid_spec = pltpu.PrefetchScalarGridSpec(
        num_scalar_prefetch=1,
        grid=(n_blocks,),
        in_specs=[pl.BlockSpec((MOE_ROWS, d), lambda i, be: (i, 0)),
                  pl.BlockSpec((None, 1, d, 2 * D_FF), lambda i, be: (layer, be[i], 0, 0)),
                  pl.BlockSpec((None, 1, 1, 2 * D_FF), lambda i, be: (layer, be[i], 0, 0)),
                  pl.BlockSpec((None, 1, D_FF, d), lambda i, be: (layer, be[i], 0, 0)),
                  pl.BlockSpec((None, 1, 1, d), lambda i, be: (layer, be[i], 0, 0))],
        out_specs=pl.BlockSpec((MOE_ROWS, d), lambda i, be: (i, 0)),
        scratch_shapes=[pltpu.VMEM((d, 2 * D_FF), BF16), pltpu.VMEM((D_FF, d), BF16)],
    )
    return pl.pallas_call(
        _expert_kernel,
        grid_spec=grid_spec,
        out_shape=jax.ShapeDtypeStruct((rows, d), BF16),
        compiler_params=_params("arbitrary"),
        name="expert_blocks",
    )(block_e, buf, w_gu, b_gu.reshape(depth, e, 1, -1), w_dn, b_dn.reshape(depth, e, 1, -1))


def moe_ffn(t, top_i, layer, w_gu, b_gu, w_dn, b_dn):
    n, d = t.shape
    nk = n * TOP_K
    e_flat = top_i.reshape(nk)
    onehot = (e_flat[:, None] == jnp.arange(N_EXPERTS)[None, :]).astype(jnp.int32)
    csum = jnp.cumsum(onehot, axis=0)
    rank = jnp.take_along_axis(csum, e_flat[:, None], axis=1)[:, 0] - 1
    counts = csum[-1]
    padded = (counts + MOE_ROWS - 1) // MOE_ROWS * MOE_ROWS
    pad_end = jnp.cumsum(padded)
    dest = (pad_end - padded)[e_flat] + rank
    n_blocks = -(-nk // MOE_ROWS) + N_EXPERTS
    block_start = jnp.arange(n_blocks, dtype=pad_end.dtype) * MOE_ROWS
    block_e = jnp.minimum(jnp.sum(pad_end[None, :] <= block_start[:, None], axis=1), N_EXPERTS - 1).astype(jnp.int32)
    order = jnp.argsort(e_flat, stable=True).astype(jnp.int32)
    in_group = (block_start - (pad_end - padded)[block_e])[:, None] + jnp.arange(MOE_ROWS, dtype=pad_end.dtype)[None, :]
    first = (jnp.cumsum(counts) - counts)[block_e][:, None]
    used = in_group < counts[block_e][:, None]
    src = jnp.where(used, order[jnp.clip(first + in_group, 0, nk - 1)] // TOP_K, n).reshape(n_blocks * MOE_ROWS)
    buf = jnp.concatenate([t, jnp.zeros((1, d), t.dtype)], axis=0)[src]
    y = expert_blocks(buf, block_e, layer, w_gu, b_gu, w_dn, b_dn)
    return y[dest.reshape(n, TOP_K).T.reshape(nk)].reshape(TOP_K, n, d)


def kernel(x, c, ctx, c_ctx, w_ada, b_ada, norm1_g, norm2_g, w_in, na_rpb, gdn_conv_w, gdn_a_log, gdn_dt_bias, gdn_norm_g, mla_q_norm_g, mla_w_uq, mla_kv_norm_g, mla_w_ukv, ml_b_if, ml_norm_g, w_mg, b_mg, w_bo, w_out, router_w, router_b, w_gu, b_gu, w_dn, b_dn, final_g):
    depth = w_ada.shape[0]
    bsz, n_lat, d = x.shape
    n_ctx = ctx.shape[1]
    cond = jnp.concatenate([jax.nn.silu(c), jax.nn.silu(c_ctx)[None]], axis=0)
    cond = jnp.pad(cond, ((0, (-cond.shape[0]) % 8), (0, 0)))
    w_in_r = reorder_w_in(w_in)
    w_mg_b, w_bo_b, w_out_b = (cast_bf16(t) for t in (w_mg, w_bo, w_out))
    wq, wqr, wkv = mla_weights(mla_w_uq, mla_w_ukv)
    cos, sin = rope_tables(n_lat, n_ctx)
    xs = jnp.concatenate([x, ctx], axis=1)
    n_tok = n_lat + n_ctx
    for l in range(depth):
        last = l == depth - 1
        n_rows = n_lat if last else n_tok
        mods = (pmm(cond, w_ada, l, tn=1024) + b_ada[l]).reshape(cond.shape[0], 6, 1, d)
        h = norm_mod(xs, norm1_g[l], mods, SC1, SH1, n_lat)
        p3 = pmm(h.reshape(bsz * n_tok, d), w_in_r, l, tm=1024, tn=768).reshape(bsz, n_tok, P_WIDTH)
        ya = neighborhood_attention(p3, na_rpb[l], n_lat)
        gdn_o = gdn_scan(gdn_prep(p3, gdn_conv_w[l], n_lat), p3, gdn_a_log[l], gdn_dt_bias[l], n_lat)
        ml_o = mlstm_scan(p3, ml_b_if[l], n_lat)
        q, k, v = mla_prep(p3, cos, sin, mla_q_norm_g[l], mla_kv_norm_g[l], wq[l], wqr[l], wkv[l])
        lat_rows, ctx_rows = (0, n_lat), (n_lat, n_ctx)
        yc = pair_attention(q, k, v, MLA_HEADS, LANES, LANES, MLA_V, lat_rows, (0, n_tok))
        if not last:
            w = NA_HEADS * NA_DH
            ya_ctx = pair_attention(p3, p3, p3, NA_HEADS, NA_DH, NA_DH, NA_DH, ctx_rows, ctx_rows,
                                    cols=(COL_NA, COL_NA + w, COL_NA + 2 * w), scale=NA_DH ** -0.5)
            yc_ctx = pair_attention(q, k, v, MLA_HEADS, LANES, LANES, MLA_V, ctx_rows, ctx_rows)
            ya = jnp.concatenate([ya, ya_ctx], axis=1)
            yc = jnp.concatenate([yc, yc_ctx], axis=1)
        xs = merge_residual(h, ya, yc, gdn_o, ml_o, p3, xs, mods, n_lat, n_rows, l, gdn_norm_g, ml_norm_g,
                            w_mg_b, b_mg, w_bo_b, w_out_b)
        h2, top_i, top_w = norm_router(xs, norm2_g[l], mods, n_lat, router_w[l], router_b[l])
        y = moe_ffn(h2.reshape(bsz * n_rows, d), top_i.reshape(bsz * n_rows, TOP_K), l, w_gu, b_gu, w_dn, b_dn)
        xs = combine_residual(xs, y.reshape(TOP_K, bsz, n_rows, d), top_w, mods, G2, n_lat)
    return norm_mod(xs, final_g, jnp.zeros((cond.shape[0], 6, 1, d), F32), SC1, SH1, n_lat, out_dtype=F32)
```

```python
import functools
import math

import numpy as np
import jax
import jax.numpy as jnp
from jax import lax
from jax.experimental import pallas as pl
from jax.experimental.pallas import tpu as pltpu

F32 = jnp.float32
BF16 = jnp.bfloat16

GRID_W = 64
EPS = 1e-6
ROPE_BASE = 10000.0
NA_HEADS, NA_DH, NA_WIN_H, NA_WIN_W = 8, 64, 8, 16
GDN_HEADS, GDN_DK, GDN_DV, GDN_CHUNK = 4, 128, 128, 64
MLA_HEADS, MLA_NOPE, MLA_ROPE, MLA_V, MLA_Q_RANK, MLA_KV_RANK = 8, 64, 32, 64, 256, 128
ML_HEADS, ML_DK, ML_DV, ML_CHUNK = 4, 64, 128, 64
N_BRANCH, BRANCH_W = 4, 512
N_EXPERTS, TOP_K, D_FF = 32, 4, 1024
SWIGLU_ALPHA, SWIGLU_LIMIT = 1.702, 7.0
MOE_ROWS = 256
LANES = 128

IN_SPLITS = (
    NA_HEADS * NA_DH, NA_HEADS * NA_DH, NA_HEADS * NA_DH,
    GDN_HEADS * (2 * GDN_DK + GDN_DV), GDN_HEADS * GDN_DV,
    2 * GDN_HEADS, 2 * GDN_HEADS,
    MLA_Q_RANK, MLA_KV_RANK + MLA_ROPE,
    ML_HEADS * ML_DK, ML_HEADS * ML_DK, ML_HEADS * ML_DV, ML_HEADS * ML_DV,
    2 * ML_HEADS, 2 * ML_HEADS,
)
COL_NA = 0
COL_GDN = 1536
COL_GDN_GATE = 3072
COL_MLA = 3584
COL_ML = 4096
COL_ML_GATE = 5120
COL_SMALL = 5632
P_WIDTH = 6144
MLA_BLOCK = COL_ML - COL_MLA
SH1, SC1, G1, SH2, SC2, G2 = range(6)

VMEM_LIMIT_BYTES = 56 * 1024 * 1024


def _params(*sem):
    return pltpu.CompilerParams(dimension_semantics=sem, vmem_limit_bytes=VMEM_LIMIT_BYTES)


def _full(a):
    return pl.BlockSpec(a.shape, lambda *_: (0,) * a.ndim)


def _layer(a, layer):
    return pl.BlockSpec((None,) + a.shape[1:], lambda *_: (layer,) + (0,) * (a.ndim - 1))


def _dot(a, b, precise=False):
    if precise:
        return jnp.dot(a, b, preferred_element_type=F32, precision=lax.Precision.HIGHEST)
    return jnp.dot(a.astype(BF16), b.astype(BF16), preferred_element_type=F32)


def _dot_nt(a, b):
    return lax.dot_general(a.astype(BF16), b.astype(BF16), (((1,), (1,)), ((), ())), preferred_element_type=F32)


def _dot_tn(a, b):
    return lax.dot_general(a.astype(BF16), b.astype(BF16), (((0,), (0,)), ((), ())), preferred_element_type=F32)


def _softplus(x):
    return jnp.maximum(x, 0.0) + jnp.log(1.0 + jnp.exp(-jnp.abs(x)))


def _rms(x):
    return x * lax.rsqrt(jnp.mean(x * x, axis=-1, keepdims=True) + EPS)


def _mm_kernel(x_ref, w_ref, o_ref):
    o_ref[...] = _dot(x_ref[...], w_ref[...]).astype(o_ref.dtype)


def pmm(x, w, layer, out_dtype=F32, tm=512, tn=512):
    m, k = x.shape
    n = w.shape[-1]
    tm, tn = math.gcd(tm, m), min(tn, n)
    if n % tn:
        tn = n
    assert m % tm == 0 and n % tn == 0, (m, n, tm, tn)
    return pl.pallas_call(
        _mm_kernel,
        grid=(m // tm, n // tn),
        in_specs=[pl.BlockSpec((tm, k), lambda i, j: (i, 0)),
                  pl.BlockSpec((None, k, tn), lambda i, j: (layer, 0, j))],
        out_specs=pl.BlockSpec((tm, tn), lambda i, j: (i, j)),
        out_shape=jax.ShapeDtypeStruct((m, n), out_dtype),
        compiler_params=_params("parallel", "parallel"),
        name="pmm",
    )(x, w)


def _cast_kernel(x_ref, o_ref):
    o_ref[...] = x_ref[...].astype(o_ref.dtype)


def cast_bf16(w, rows=256):
    w2 = w.reshape(-1, w.shape[-1])
    m, n = w2.shape
    rows = math.gcd(rows, m)
    blk = pl.BlockSpec((rows, n), lambda i: (i, 0))
    out = pl.pallas_call(_cast_kernel, grid=(m // rows,), in_specs=[blk], out_specs=blk,
                         out_shape=jax.ShapeDtypeStruct((m, n), BF16),
                         compiler_params=_params("parallel"), name="cast_bf16")(w2)
    return out.reshape(w.shape)


W_IN_BLOCK = 512


def _w_in_kernel(head_ref, tail_ref, o_ref, *, n_head_blocks):
    @pl.when(pl.program_id(1) < n_head_blocks)
    def _():
        o_ref[...] = head_ref[...].astype(o_ref.dtype)

    @pl.when(pl.program_id(1) >= n_head_blocks)
    def _():
        o_ref[...] = tail_ref[...].astype(o_ref.dtype)


def reorder_w_in(w_in):
    depth, d, _ = w_in.shape
    o = np.cumsum((0,) + IN_SPLITS)
    assert o[5] == COL_MLA and COL_MLA % W_IN_BLOCK == 0 and P_WIDTH % W_IN_BLOCK == 0
    piece = lambda i: w_in[..., o[i]:o[i + 1]]
    zeros = lambda n: jnp.zeros(w_in.shape[:-1] + (n,), w_in.dtype)
    tail = jnp.concatenate([piece(7), piece(8), zeros(MLA_BLOCK - IN_SPLITS[7] - IN_SPLITS[8]),
                            piece(9), piece(10), piece(11), piece(12),
                            piece(5), piece(6), piece(13), piece(14), zeros(P_WIDTH - COL_SMALL - 8 * GDN_HEADS)],
                           axis=-1)
    assert tail.shape[-1] == P_WIDTH - COL_MLA
    nh_blocks, n_blocks = COL_MLA // W_IN_BLOCK, P_WIDTH // W_IN_BLOCK
    return pl.pallas_call(
        functools.partial(_w_in_kernel, n_head_blocks=nh_blocks),
        grid=(depth, n_blocks),
        in_specs=[pl.BlockSpec((1, d, W_IN_BLOCK), lambda l, j: (l, 0, jnp.minimum(j, nh_blocks - 1))),
                  pl.BlockSpec((1, d, W_IN_BLOCK), lambda l, j: (l, 0, jnp.maximum(j - nh_blocks, 0)))],
        out_specs=pl.BlockSpec((1, d, W_IN_BLOCK), lambda l, j: (l, 0, j)),
        out_shape=jax.ShapeDtypeStruct((depth, d, P_WIDTH), BF16),
        compiler_params=_params("parallel", "arbitrary"),
        name="reorder_w_in",
    )(w_in, tail)


def _mod_row(n_lat_blocks, ctx_row):
    return lambda i, j: jnp.where(j >= n_lat_blocks, ctx_row, i)


def _norm_mod_kernel(x_ref, g_ref, sc_ref, sh_ref, o_ref):
    y = _rms(x_ref[0]) * g_ref[...]
    o_ref[0] = (y * (1.0 + sc_ref[0, 0]) + sh_ref[0, 0]).astype(o_ref.dtype)


def norm_mod(x, g, mods, i_sc, i_sh, n_lat, out_dtype=BF16, ts=256):
    b, t, d = x.shape
    ts = min(ts, t - n_lat) if t > n_lat else ts
    assert t % ts == 0 and n_lat % ts == 0
    row = _mod_row(n_lat // ts, b)
    return pl.pallas_call(
        _norm_mod_kernel,
        grid=(b, t // ts),
        in_specs=[pl.BlockSpec((1, ts, d), lambda i, j: (i, j, 0)),
                  pl.BlockSpec((1, d), lambda i, j: (0, 0)),
                  pl.BlockSpec((1, 1, 1, d), lambda i, j: (row(i, j), i_sc, 0, 0)),
                  pl.BlockSpec((1, 1, 1, d), lambda i, j: (row(i, j), i_sh, 0, 0))],
        out_specs=pl.BlockSpec((1, ts, d), lambda i, j: (i, j, 0)),
        out_shape=jax.ShapeDtypeStruct((b, t, d), out_dtype),
        compiler_params=_params("parallel", "parallel"),
        name="norm_mod",
    )(x, g.reshape(1, d), mods, mods)


def _norm_router_kernel(x_ref, g_ref, sc_ref, sh_ref, rw_ref, rb_ref, h_ref, idx_ref, w_ref):
    h = (_rms(x_ref[0]) * g_ref[...]) * (1.0 + sc_ref[0, 0]) + sh_ref[0, 0]
    h_ref[0] = h.astype(h_ref.dtype)
    logits = lax.dot_general(rw_ref[...], h, (((1,), (1,)), ((), ())), preferred_element_type=F32,
                             precision=lax.Precision.HIGHEST) + rb_ref[...]
    n_exp = logits.shape[0]
    expert = lax.broadcasted_iota(jnp.int32, logits.shape, 0)
    vals, ids = [], []
    for _ in range(TOP_K):
        best = jnp.max(logits, axis=0, keepdims=True)
        first = jnp.min(jnp.where(logits == best, expert, n_exp), axis=0, keepdims=True)
        vals.append(best)
        ids.append(first)
        logits = jnp.where(expert == first, -jnp.inf, logits)
    e = [jnp.exp(v - vals[0]) for v in vals]
    denom = functools.reduce(jnp.add, e)
    idx_ref[0] = jnp.concatenate(ids, axis=0)
    w_ref[0] = jnp.concatenate([t / denom for t in e], axis=0)


def norm_router(x, g, mods, n_lat, router_w, router_b, ts=256):
    b, t, d = x.shape
    ts = min(ts, t - n_lat) if t > n_lat else ts
    assert t % ts == 0 and n_lat % ts == 0
    row = _mod_row(n_lat // ts, b)
    rw_t = router_w.T
    rb = router_b.reshape(-1, 1)
    slot = pl.BlockSpec((1, TOP_K, ts), lambda i, j: (i, 0, j))
    h2, idx_t, w_t = pl.pallas_call(
        _norm_router_kernel,
        grid=(b, t // ts),
        in_specs=[pl.BlockSpec((1, ts, d), lambda i, j: (i, j, 0)),
                  pl.BlockSpec((1, d), lambda i, j: (0, 0)),
                  pl.BlockSpec((1, 1, 1, d), lambda i, j: (row(i, j), SC2, 0, 0)),
                  pl.BlockSpec((1, 1, 1, d), lambda i, j: (row(i, j), SH2, 0, 0)),
                  _full(rw_t), _full(rb)],
        out_specs=[pl.BlockSpec((1, ts, d), lambda i, j: (i, j, 0)), slot, slot],
        out_shape=[jax.ShapeDtypeStruct((b, t, d), BF16), jax.ShapeDtypeStruct((b, TOP_K, t), jnp.int32),
                   jax.ShapeDtypeStruct((b, TOP_K, t), F32)],
        compiler_params=_params("parallel", "parallel"),
        name="norm_router",
    )(x, g.reshape(1, d), mods, mods, rw_t, rb)
    return h2, idx_t.transpose(0, 2, 1), w_t.transpose(0, 2, 1)


def _combine_residual_kernel(x_ref, y_ref, w_ref, g_ref, o_ref):
    w = w_ref[0]
    f = functools.reduce(jnp.add, [w[:, k:k + 1] * y_ref[k, 0].astype(F32) for k in range(y_ref.shape[0])])
    o_ref[0] = x_ref[0] + g_ref[0, 0] * f


def combine_residual(x, y, w, mods, i_gate, n_lat, ts=256):
    b, t, d = x.shape
    n_slots = y.shape[0]
    ts = min(ts, t - n_lat) if t > n_lat else ts
    row = _mod_row(n_lat // ts, b)
    blk = pl.BlockSpec((1, ts, d), lambda i, j: (i, j, 0))
    return pl.pallas_call(
        _combine_residual_kernel,
        grid=(b, t // ts),
        in_specs=[blk, pl.BlockSpec((n_slots, 1, ts, d), lambda i, j: (0, i, j, 0)),
                  pl.BlockSpec((1, ts, n_slots), lambda i, j: (i, j, 0)),
                  pl.BlockSpec((1, 1, 1, d), lambda i, j: (row(i, j), i_gate, 0, 0))],
        out_specs=blk,
        out_shape=jax.ShapeDtypeStruct((b, t, d), F32),
        compiler_params=_params("parallel", "parallel"),
        name="combine_residual",
    )(x, y, w, mods)


def _pair_attn_kernel(q_ref, k_ref, v_ref, o_ref, *, tk, dv, scale):
    dq, dvp = q_ref.shape[-1] // 2, v_ref.shape[-1] // 2
    tq = q_ref.shape[1]
    sum_on_mxu = dvp > dv
    outs = []
    for hh in range(2):
        q = q_ref[0][:, hh * dq:(hh + 1) * dq]
        q = (q if scale is None else q * scale).astype(BF16)
        m = jnp.full((tq, 1), -jnp.inf, F32)
        l = jnp.zeros((tq, 1), F32)
        acc = jnp.zeros((tq, dvp), F32)
        for c in range(k_ref.shape[1] // tk):
            k = k_ref[0, c * tk:(c + 1) * tk, hh * dq:(hh + 1) * dq].astype(BF16)
            v = v_ref[0, c * tk:(c + 1) * tk, hh * dvp:(hh + 1) * dvp].astype(BF16)
            s = lax.dot_general(q, k, (((1,), (1,)), ((), ())), preferred_element_type=F32)
            m_new = jnp.maximum(m, jnp.max(s, axis=-1, keepdims=True))
            alpha = jnp.exp(m - m_new)
            p = jnp.exp(s - m_new)
            if not sum_on_mxu:
                l = alpha * l + jnp.sum(p, axis=-1, keepdims=True)
            acc = alpha * acc + jnp.dot(p.astype(BF16), v, preferred_element_type=F32)
            m = m_new
        outs.append(acc[:, :dv] / (acc[:, dv:dv + 1] if sum_on_mxu else l))
    o_ref[0] = jnp.concatenate(outs, axis=-1).astype(o_ref.dtype)


def pair_attention(q, k, v, n_heads, dq, dvp, dv, q_rows, kv_rows, cols=(0, 0, 0), scale=None, tq=512):
    b = q.shape[0]
    (q0, n_q), (kv0, t) = q_rows, kv_rows
    dq2, dvp2 = 2 * dq, 2 * dvp
    tq = min(tq, n_q)
    tk = next(c for c in (768, 512, 384, 256, 128) if t % c == 0)
    assert n_q % tq == 0 and q0 % tq == 0 and kv0 % t == 0
    assert dq2 % LANES == 0 and dvp2 % LANES == 0 and (2 * dv) % LANES == 0
    assert all(c % w == 0 for c, w in zip(cols, (dq2, dq2, dvp2)))
    qc, kc, vc = (c // w for c, w in zip(cols, (dq2, dq2, dvp2)))
    return pl.pallas_call(
        functools.partial(_pair_attn_kernel, tk=tk, dv=dv, scale=scale),
        grid=(b, n_heads // 2, n_q // tq),
        in_specs=[pl.BlockSpec((1, tq, dq2), lambda i, hp, n: (i, q0 // tq + n, qc + hp)),
                  pl.BlockSpec((1, t, dq2), lambda i, hp, n: (i, kv0 // t, kc + hp)),
                  pl.BlockSpec((1, t, dvp2), lambda i, hp, n: (i, kv0 // t, vc + hp))],
        out_specs=pl.BlockSpec((1, tq, 2 * dv), lambda i, hp, n: (i, n, hp)),
        out_shape=jax.ShapeDtypeStruct((b, n_q, n_heads * dv), BF16),
        compiler_params=_params("parallel", "parallel", "parallel"),
        name="pair_attention",
    )(q, k, v)


NA_QROWS = 8
NA_KROWS = 4
NA_NEG = -1e30


def na_bias_tables(rpb, rows):
    n_steps = rows // NA_QROWS
    wh, ww = min(NA_WIN_H, rows), NA_WIN_W
    n_kblocks = rows // NA_KROWS
    n_heads = rpb.shape[0]
    width = 2 * GRID_W - 1
    lo = GRID_W - NA_WIN_W
    padded = jnp.full(rpb.shape[:2] + (width,), NA_NEG, F32).at[..., lo:lo + 2 * NA_WIN_W - 1].set(rpb)
    tiles = jnp.stack([padded[..., GRID_W - 1 - qc:width - qc] for qc in range(GRID_W)], axis=2)
    qc, kc = np.arange(GRID_W)[:, None], np.arange(GRID_W)[None, :]
    c0 = np.clip(qc - ww // 2, 0, GRID_W - ww)
    tiles = jnp.where((kc >= c0) & (kc < c0 + ww), tiles, NA_NEG)
    masked = jnp.full((n_heads, GRID_W, GRID_W), NA_NEG, F32)
    tables = []
    for i in (0, min(1, n_steps - 1), n_steps - 1):
        q_tiles = []
        for il in range(NA_QROWS):
            q = NA_QROWS * i + il
            r0 = min(max(q - wh // 2, 0), rows - wh)
            k_tiles = []
            for ka in range(4 * NA_KROWS):
                blk = 2 * i - 1 + ka // NA_KROWS
                kr = NA_KROWS * blk + ka % NA_KROWS
                ok = 0 <= blk < n_kblocks and r0 <= kr < r0 + wh
                k_tiles.append(tiles[:, kr - q + NA_WIN_H - 1] if ok else masked)
            q_tiles.append(jnp.concatenate(k_tiles, axis=-1))
        tables.append(jnp.concatenate(q_tiles, axis=-2))
    return jnp.stack(tables)


def _na_kernel(q_ref, k0_ref, k1_ref, k2_ref, k3_ref, v0_ref, v1_ref, v2_ref, v3_ref,
               kc_ref, vc_ref, bias_ref, o_ref):
    kw = NA_KROWS * GRID_W
    lane = lax.broadcasted_iota(jnp.int32, (1, 2 * NA_DH), 1)
    q2 = q_ref[0] * (NA_DH ** -0.5)
    ks = [r[0].astype(BF16) for r in (k0_ref, k1_ref, k2_ref, k3_ref, kc_ref)]
    v2 = [r[0] for r in (v0_ref, v1_ref, v2_ref, v3_ref, vc_ref)]
    outs = []
    for hh in range(2):
        own = (lane >= NA_DH) if hh else (lane < NA_DH)
        q = jnp.where(own, q2, 0.0).astype(BF16)
        vs = [jnp.where(own, t, 1.0).astype(BF16) for t in v2]
        ss = [lax.dot_general(q, kk, (((1,), (1,)), ((), ())), preferred_element_type=F32) for kk in ks]
        ss = [ss[j] + bias_ref[0, 0, hh, :, j * kw:(j + 1) * kw] for j in range(4)] + [ss[4]]
        m = functools.reduce(jnp.maximum, [jnp.max(t, axis=-1, keepdims=True) for t in ss])
        o = functools.reduce(jnp.add, [jnp.dot(jnp.exp(t - m).astype(BF16), vv, preferred_element_type=F32)
                                       for t, vv in zip(ss, vs)])
        denom = o[:, 0:1] if hh else o[:, NA_DH:NA_DH + 1]
        outs.append(o / denom)
    o_ref[0] = jnp.where(lane < NA_DH, outs[0], outs[1]).astype(o_ref.dtype)


def neighborhood_attention(p3, rpb, n_lat):
    b, t, _ = p3.shape
    c_len = t - n_lat
    rows = n_lat // GRID_W
    tq, tkv = NA_QROWS * GRID_W, NA_KROWS * GRID_W
    assert rows % NA_QROWS == 0 and rows >= 2 * NA_QROWS and n_lat % c_len == 0
    n_steps = rows // NA_QROWS
    n_kblocks = rows // NA_KROWS
    hw = 2 * NA_DH
    n_pairs = NA_HEADS // 2
    bias = na_bias_tables(rpb, rows)
    bias = bias.reshape(3, n_pairs, 2, *bias.shape[2:])

    def view(j, col0):
        return pl.BlockSpec((1, tkv, hw),
                            lambda hp, i, n: (i, jnp.clip(2 * n - 1 + j, 0, n_kblocks - 1), col0 + hp))

    ctx_spec = lambda col0: pl.BlockSpec((1, c_len, hw), lambda hp, i, n: (i, n_lat // c_len, col0 + hp))
    bias_spec = pl.BlockSpec(
        (1, 1, 2, tq, 4 * tkv),
        lambda hp, i, n: (jnp.where(n == 0, 0, jnp.where(n == n_steps - 1, 2, 1)), hp, 0, 0, 0))
    return pl.pallas_call(
        _na_kernel,
        grid=(n_pairs, b, n_steps),
        in_specs=[pl.BlockSpec((1, tq, hw), lambda hp, i, n: (i, n, hp))]
                 + [view(j, n_pairs) for j in range(4)] + [view(j, 2 * n_pairs) for j in range(4)]
                 + [ctx_spec(n_pairs), ctx_spec(2 * n_pairs), bias_spec],
        out_specs=pl.BlockSpec((1, tq, hw), lambda hp, i, n: (i, n, hp)),
        out_shape=jax.ShapeDtypeStruct((b, n_lat, NA_HEADS * NA_DH), BF16),
        compiler_params=_params("parallel", "parallel", "arbitrary"),
        name="neighborhood_attention",
    )(*([p3] * 11), bias)


def rope_tables(n_lat, n_ctx):
    half = MLA_ROPE // 2
    tpos = np.arange(n_lat)
    inv = ROPE_BASE ** (-jnp.arange(0, half, 2, dtype=F32) / half)
    pos = jnp.stack([tpos // GRID_W, tpos % GRID_W], axis=1).astype(F32)
    ang = pos[:, :, None] * inv
    ang = jnp.concatenate([ang, ang], axis=-1).reshape(n_lat, MLA_ROPE)
    pad = ((0, n_ctx), (MLA_NOPE, LANES - MLA_NOPE - MLA_ROPE))
    return (jnp.pad(jnp.cos(ang), pad, constant_values=1.0), jnp.pad(jnp.sin(ang), pad))


def _rope_swap():
    half = MLA_ROPE // 2
    quarter = half // 2
    r = np.zeros((MLA_ROPE, MLA_ROPE), np.float32)
    for j in range(MLA_ROPE):
        if j % half < quarter:
            r[j + quarter, j] = -1.0
        else:
            r[j - quarter, j] = 1.0
    return r


def mla_weights(w_uq, w_ukv):
    dqh = MLA_NOPE + MLA_ROPE
    lead = w_uq.shape[:-1]
    wq = w_uq.reshape(*lead, MLA_HEADS, dqh)
    padh = lambda a: jnp.pad(a, ((0, 0),) * (a.ndim - 1) + ((0, LANES - a.shape[-1]),))
    wq_p = padh(wq).reshape(*lead, MLA_HEADS * LANES)
    rot = jnp.einsum('...hr,rs->...hs', wq[..., MLA_NOPE:], jnp.asarray(_rope_swap()))
    wqr_p = jnp.pad(rot, ((0, 0),) * (rot.ndim - 1) + ((MLA_NOPE, LANES - dqh),)).reshape(*lead, MLA_HEADS * LANES)
    lead = w_ukv.shape[:-1]
    wkv = w_ukv.reshape(*lead, MLA_HEADS, MLA_NOPE + MLA_V)
    wk_p = padh(wkv[..., :MLA_NOPE]).reshape(*lead, MLA_HEADS * LANES)
    wv_p = padh(wkv[..., MLA_NOPE:]).reshape(*lead, MLA_HEADS * LANES)
    return wq_p.astype(BF16), wqr_p.astype(BF16), jnp.concatenate([wk_p, wv_p], axis=-1).astype(BF16)


def _mla_prep_kernel(p_ref, cos_ref, sin_ref, qg_ref, kvg_ref, wq_ref, wqr_ref, wkv_ref, e_ref, er_ref,
                     q_out, k_out, v_out):
    blk = p_ref[0]
    nq = _rms(blk[:, :MLA_Q_RANK]) * qg_ref[...]
    nkv = _rms(blk[:, MLA_Q_RANK:MLA_Q_RANK + MLA_KV_RANK]) * kvg_ref[...]
    kpe = blk[:, MLA_Q_RANK + MLA_KV_RANK:MLA_Q_RANK + MLA_KV_RANK + MLA_ROPE]
    cos, sin = cos_ref[...], sin_ref[...]
    qf, qr = _dot(nq, wq_ref[...]), _dot(nq, wqr_ref[...])
    kv = _dot(nkv, wkv_ref[...])
    kpe_tile = _dot(kpe, e_ref[...], precise=True) * cos + _dot(kpe, er_ref[...], precise=True) * sin
    scale = (MLA_NOPE + MLA_ROPE) ** -0.5
    for h in range(MLA_HEADS):
        sl = slice(h * LANES, (h + 1) * LANES)
        q_out[0, :, sl] = ((qf[:, sl] * cos + qr[:, sl] * sin) * scale).astype(q_out.dtype)
        k_out[0, :, sl] = (kv[:, sl] + kpe_tile).astype(k_out.dtype)
        ones_col = (lax.broadcasted_iota(jnp.int32, (1, LANES), 1) == MLA_V).astype(F32)
        v_out[0, :, sl] = (kv[:, MLA_HEADS * LANES + h * LANES:MLA_HEADS * LANES + (h + 1) * LANES]
                           + ones_col).astype(v_out.dtype)


def mla_prep(p3, cos, sin, q_norm_g, kv_norm_g, wq, wqr, wkv, tb=256):
    b, t, _ = p3.shape
    tb = next(c for c in (tb, 128) if t % c == 0)
    swap = _rope_swap()
    place = np.zeros((MLA_ROPE, LANES), np.float32)
    place[np.arange(MLA_ROPE), MLA_NOPE + np.arange(MLA_ROPE)] = 1.0
    e, er = jnp.asarray(place), jnp.asarray(swap @ place)
    qg, kvg = q_norm_g.reshape(1, -1), kv_norm_g.reshape(1, -1)
    row = lambda w: pl.BlockSpec((1, tb, w), lambda i, j: (i, j, 0))
    tab = pl.BlockSpec((tb, LANES), lambda i, j: (j, 0))
    return pl.pallas_call(
        _mla_prep_kernel,
        grid=(b, t // tb),
        in_specs=[pl.BlockSpec((1, tb, MLA_BLOCK), lambda i, j: (i, j, COL_MLA // MLA_BLOCK)), tab, tab]
                 + [_full(a) for a in (qg, kvg, wq, wqr, wkv, e, er)],
        out_specs=[row(MLA_HEADS * LANES)] * 3,
        out_shape=[jax.ShapeDtypeStruct((b, t, MLA_HEADS * LANES), BF16),
                   jax.ShapeDtypeStruct((b, t, MLA_HEADS * LANES), BF16),
                   jax.ShapeDtypeStruct((b, t, MLA_HEADS * LANES), BF16)],
        compiler_params=_params("parallel", "parallel"),
        name="mla_prep",
    )(p3, cos, sin, qg, kvg, wq, wqr, wkv, e, er)


def _gdn_prep_kernel(x_ref, prev_ref, next_ref, w_ref, o_ref, *, n_lat, n_tok):
    tb = x_ref.shape[1]
    start = pl.program_id(1) * tb
    x = x_ref[0]
    seg_first = jnp.logical_or(start == 0, start == n_lat)
    seg_last = jnp.logical_or(start + tb == n_lat, start + tb == n_tok)
    prev_row = jnp.where(seg_first, 0.0, prev_ref[0][7:8, :])
    next_row = jnp.where(seg_last, 0.0, next_ref[0][0:1, :])
    rows = lax.broadcasted_iota(jnp.int32, (tb, 1), 0)
    x_m1 = jnp.where(rows == 0, prev_row, pltpu.roll(x, 1, axis=0))
    x_p1 = jnp.where(rows == tb - 1, next_row, pltpu.roll(x, tb - 1, axis=0))
    y = x_m1 * w_ref[0:1, :] + x * w_ref[1:2, :] + x_p1 * w_ref[2:3, :]
    y = y * jax.nn.sigmoid(y)
    for g in range(2 * GDN_HEADS):
        sl = slice(g * GDN_DK, (g + 1) * GDN_DK)
        yh = y[:, sl]
        yh = yh * lax.rsqrt(jnp.sum(yh * yh, axis=-1, keepdims=True) + EPS)
        o_ref[0, :, sl] = yh * (GDN_DK ** -0.5) if g < GDN_HEADS else yh
    o_ref[0, :, 2 * GDN_HEADS * GDN_DK:] = y[:, 2 * GDN_HEADS * GDN_DK:]


def gdn_prep(p3, conv_w, n_lat, tb=256):
    b, t, _ = p3.shape
    w = GDN_HEADS * (2 * GDN_DK + GDN_DV)
    tb = min(tb, t - n_lat)
    assert t % tb == 0 and n_lat % tb == 0 and tb % 8 == 0 and COL_GDN % w == 0
    col = COL_GDN // w
    sub = tb // 8
    return pl.pallas_call(
        functools.partial(_gdn_prep_kernel, n_lat=n_lat, n_tok=t),
        grid=(b, t // tb),
        in_specs=[pl.BlockSpec((1, tb, w), lambda i, j: (i, j, col)),
                  pl.BlockSpec((1, 8, w), lambda i, j: (i, jnp.maximum(j * sub - 1, 0), col)),
                  pl.BlockSpec((1, 8, w), lambda i, j: (i, jnp.minimum((j + 1) * sub, t // 8 - 1), col)),
                  _full(conv_w)],
        out_specs=pl.BlockSpec((1, tb, w), lambda i, j: (i, j, 0)),
        out_shape=jax.ShapeDtypeStruct((b, t, w), F32),
        compiler_params=_params("parallel", "parallel"),
        name="gdn_prep",
    )(p3, p3, p3, conv_w)


def _order_masks(length, reverse):
    r = lax.broadcasted_iota(jnp.int32, (length, length), 0)
    c = lax.broadcasted_iota(jnp.int32, (length, length), 1)
    if reverse:
        return r <= c, r < c, r >= c
    return r >= c, r > c, r <= c


def _as_row(col, mask):
    return jnp.sum(jnp.where(mask, col, 0.0), axis=0, keepdims=True)


def _gdn_kernel(xf_ref, sf_ref, xb_ref, sb_ref, alog_ref, dtb_ref, of_ref, ob_ref, s_ref):
    @pl.when(pl.program_id(1) == 0)
    def _():
        s_ref[...] = jnp.zeros_like(s_ref)

    nh, dk, dv, length = GDN_HEADS, GDN_DK, GDN_DV, GDN_CHUNK
    n_sub = xf_ref.shape[1] // length
    r = lax.broadcasted_iota(jnp.int32, (length, length), 0)
    c = lax.broadcasted_iota(jnp.int32, (length, length), 1)
    eye = (r == c).astype(F32)
    chunks = []
    for sub in range(n_sub):
        seqs = []
        for d, (x_ref, sm_ref, o_ref) in enumerate(((xf_ref, sf_ref, of_ref), (xb_ref, sb_ref, ob_ref))):
            r0 = (n_sub - 1 - sub if d == 1 else sub) * length
            rows = slice(r0, r0 + length)
            incl, strict, incl_t = _order_masks(length, d == 1)
            small = sm_ref[0, rows, :]
            hs = slice(d * nh, (d + 1) * nh)
            la_col = -jnp.exp(alog_ref[:, hs]) * _softplus(small[:, hs] + dtb_ref[:, hs])
            be_col = jax.nn.sigmoid(small[:, 2 * nh + d * nh:2 * nh + (d + 1) * nh])
            g_cols = _dot(incl.astype(F32), la_col, precise=True)
            g_ends = jnp.sum(la_col, axis=0, keepdims=True)
            for h in range(nh):
                q = x_ref[0, rows, h * dk:(h + 1) * dk]
                k = x_ref[0, rows, (nh + h) * dk:(nh + h + 1) * dk]
                v = x_ref[0, rows, 2 * nh * dk + h * dv:2 * nh * dk + (h + 1) * dv]
                gc, ge, bc = g_cols[:, h:h + 1], g_ends[:, h:h + 1], be_col[:, h:h + 1]
                gr = _as_row(la_col[:, h:h + 1], incl_t)
                decay = jnp.exp(jnp.where(incl, gc - gr, -jnp.inf))
                qk = decay * _dot_nt(q, k)
                p = -(bc * jnp.where(strict, decay, 0.0) * _dot_nt(k, k))
                eg = jnp.exp(gc)
                seqs.append(dict(o_ref=o_ref, rows=rows, h=h, idx=d * nh + h, q_dec=eg * q, qk=qk, p=p, t=eye + p,
                                 rhs=jnp.concatenate([bc * v, (bc * eg) * k], axis=-1),
                                 k_end=jnp.exp(ge - gc) * k, e_end=jnp.exp(ge)))
        for _ in range(int(math.log2(length)) - 1):
            for s in seqs:
                s['p'] = _dot(s['p'], s['p'])
            for s in seqs:
                s['t'] = s['t'] + _dot(s['t'], s['p'])
        for s in seqs:
            s['sol'] = _dot(s['t'], s['rhs'])
        chunks.append(seqs)
    states = [s_ref[i] for i in range(2 * nh)]
    for seqs in chunks:
        for s in seqs:
            s['u'] = s['sol'][:, :dv] - _dot_nt(s['sol'][:, dv:], states[s['idx']])
        for s in seqs:
            h, s_old = s['h'], states[s['idx']]
            s['o_ref'][0, s['rows'], h * dv:(h + 1) * dv] = _dot_nt(s['q_dec'], s_old) + _dot(s['qk'], s['u'])
            states[s['idx']] = s['e_end'] * s_old + _dot_tn(s['u'], s['k_end'])
    for i, st in enumerate(states):
        s_ref[i] = st


def _mlstm_kernel(xf_ref, sf_ref, xb_ref, sb_ref, bi_ref, bf_ref, of_ref, ob_ref, c_ref, n_ref, m_ref):
    @pl.when(pl.program_id(1) == 0)
    def _():
        c_ref[...] = jnp.zeros_like(c_ref)
        n_ref[...] = jnp.zeros_like(n_ref)
        m_ref[...] = jnp.zeros_like(m_ref)

    nh, dk, dv, length = ML_HEADS, ML_DK, ML_DV, ML_CHUNK
    n_sub = xf_ref.shape[1] // length
    r = lax.broadcasted_iota(jnp.int32, (length, length), 0)
    c = lax.broadcasted_iota(jnp.int32, (length, length), 1)
    eye = r == c
    col_i, col_f = 4 * GDN_HEADS, 4 * GDN_HEADS + 2 * nh
    chunks = []
    for sub in range(n_sub):
        seqs = []
        for d, (x_ref, sm_ref, o_ref) in enumerate(((xf_ref, sf_ref, of_ref), (xb_ref, sb_ref, ob_ref))):
            r0 = (n_sub - 1 - sub if d == 1 else sub) * length
            rows = slice(r0, r0 + length)
            incl, _, incl_t = _order_masks(length, d == 1)
            small = sm_ref[0, rows, :]
            hs = slice(d * nh, (d + 1) * nh)
            i_col = small[:, col_i + d * nh:col_i + (d + 1) * nh] + bi_ref[:, hs]
            f_col = -_softplus(-(small[:, col_f + d * nh:col_f + (d + 1) * nh] + bf_ref[:, hs]))
            fc_cols = _dot(incl.astype(F32), f_col, precise=True)
            f_ends = jnp.sum(f_col, axis=0, keepdims=True)
            for h in range(nh):
                q = x_ref[0, rows, h * dk:(h + 1) * dk] * (dk ** -0.5)
                k = x_ref[0, rows, (nh + h) * dk:(nh + h + 1) * dk]
                v = x_ref[0, rows, 2 * nh * dk + h * dv:2 * nh * dk + (h + 1) * dv]
                fc, fe, ic = fc_cols[:, h:h + 1], f_ends[:, h:h + 1], i_col[:, h:h + 1]
                fr, ir = _as_row(f_col[:, h:h + 1], incl_t), _as_row(ic, eye)
                d_log = jnp.where(incl, fc - fr + ir, -jnp.inf)
                d_max = jnp.max(d_log, axis=-1, keepdims=True)
                w_log = fe - fc + ic
                w_max = jnp.max(w_log, axis=0, keepdims=True)
                seqs.append(dict(o_ref=o_ref, rows=rows, h=h, idx=d * nh + h, q=q, k=k, v=v, fc=fc, fe=fe,
                                 d_max=d_max, w_max=w_max, decay=jnp.exp(d_log - d_max), wk=jnp.exp(w_log - w_max)))
        for s in seqs:
            s['intra'] = s['decay'] * _dot_nt(s['q'], s['k'])
        for s in seqs:
            s['intra_v'] = _dot(s['intra'], s['v'])
            s['vk'] = _dot_tn(s['wk'] * s['v'], s['k'])
        for s in seqs:
            s['intra_sum'] = jnp.sum(s['intra'], axis=-1, keepdims=True)
            s['k_sum'] = jnp.sum(s['wk'] * s['k'], axis=0, keepdims=True)
        chunks.append(seqs)
    n_seq = 2 * nh
    cs = [c_ref[i] for i in range(n_seq)]
    ns = [n_ref[i] for i in range(n_seq)]
    ms = [m_ref[i][:, 0:1] for i in range(n_seq)]
    for seqs in chunks:
        for s in seqs:
            m_old = ms[s['idx']]
            s['m_q'] = jnp.maximum(s['fc'] + m_old, s['d_max'])
            s['inter'], s['scale'] = jnp.exp(s['fc'] + m_old - s['m_q']), jnp.exp(s['d_max'] - s['m_q'])
            s['m_new'] = jnp.maximum(s['fe'] + m_old, s['w_max'])
            s['dec'], s['w_scale'] = jnp.exp(s['fe'] + m_old - s['m_new']), jnp.exp(s['w_max'] - s['m_new'])
        for s in seqs:
            s['qc'] = _dot_nt(s['q'], cs[s['idx']])
        for s in seqs:
            h, idx = s['h'], s['idx']
            num = s['inter'] * s['qc'] + s['scale'] * s['intra_v']
            den = (s['inter'] * jnp.sum(s['q'] * ns[idx], axis=-1, keepdims=True) + s['scale'] * s['intra_sum'])
            s['o_ref'][0, s['rows'], h * dv:(h + 1) * dv] = num / jnp.maximum(jnp.abs(den), jnp.exp(-s['m_q']))
            cs[idx] = s['dec'] * cs[idx] + s['w_scale'] * s['vk']
            ns[idx] = s['dec'] * ns[idx] + s['w_scale'] * s['k_sum']
            ms[idx] = s['m_new']
    for i in range(n_seq):
        c_ref[i], n_ref[i] = cs[i], ns[i]
        m_ref[i] = jnp.broadcast_to(ms[i], m_ref.shape[1:])


GDN_SCAN_BLOCK = 256
ML_SCAN_BLOCK = 128


def bidirectional_scan(kernel_fn, x, x_col, x_w, p3, params, n_lat, block, out_w, scratch_shapes, name):
    b, t, _ = p3.shape
    length = math.gcd(block, t - n_lat)
    nl, nc = n_lat // length, t // length
    ncx = nc - nl
    assert t % length == 0 and n_lat % length == 0
    fwd = lambda n: jnp.where(n < ncx, nl + n, n - ncx)
    bwd = lambda n: jnp.where(n < ncx, nc - 1 - n, nl - 1 - (n - ncx))
    specs = lambda f: [pl.BlockSpec((1, length, x_w), lambda i, n: (i, f(n), x_col)),
                       pl.BlockSpec((1, length, LANES), lambda i, n: (i, f(n), COL_SMALL // LANES))]
    out = lambda f: pl.BlockSpec((1, length, out_w), lambda i, n: (i, f(n), 0))
    return pl.pallas_call(
        kernel_fn,
        grid=(b, nc),
        in_specs=specs(fwd) + specs(bwd) + [_full(a) for a in params],
        out_specs=[out(fwd), out(bwd)],
        out_shape=[jax.ShapeDtypeStruct((b, t, out_w), F32)] * 2,
        scratch_shapes=scratch_shapes,
        compiler_params=_params("parallel", "arbitrary"),
        name=name,
    )(x, p3, x, p3, *params)


def gdn_scan(xg, p3, a_log, dt_bias, n_lat):
    state = pltpu.VMEM((2 * GDN_HEADS, GDN_DV, GDN_DK), F32)
    params = (a_log.reshape(1, -1), dt_bias.reshape(1, -1))
    return bidirectional_scan(_gdn_kernel, xg, 0, xg.shape[-1], p3, params, n_lat, GDN_SCAN_BLOCK,
                              GDN_HEADS * GDN_DV, [state], "gdn_scan")


def mlstm_scan(p3, b_if, n_lat):
    n_seq = 2 * ML_HEADS
    w = ML_HEADS * (2 * ML_DK + ML_DV)
    scratch = [pltpu.VMEM((n_seq, ML_DV, ML_DK), F32), pltpu.VMEM((n_seq, 1, ML_DK), F32),
               pltpu.VMEM((n_seq, 1, LANES), F32)]
    params = (b_if[:, 0].reshape(1, -1), b_if[:, 1].reshape(1, -1))
    return bidirectional_scan(_mlstm_kernel, p3, COL_ML // w, w, p3, params, n_lat, ML_SCAN_BLOCK,
                              ML_HEADS * ML_DV, scratch, "mlstm_scan")


def _gated_head_norm(of_ref, ob_ref, gate, g_ref, n_heads):
    o = of_ref[0] + ob_ref[0]
    d = o.shape[-1] // n_heads
    parts = [_rms(o[:, h * d:(h + 1) * d]) * g_ref[...] for h in range(n_heads)]
    return (jnp.concatenate(parts, axis=-1) * gate).astype(BF16)


def _merge_kernel(h_ref, ya_ref, yc_ref, gof_ref, gob_ref, gg_ref, mof_ref, mob_ref, mg_ref, x_ref, g1_ref,
                  gn_ref, mn_ref, wmg_ref, bmg_ref, wbo_ref, wout_ref, o_ref):
    d = o_ref.shape[-1]
    h = h_ref[0]
    gg = gg_ref[0]
    yb = _gated_head_norm(gof_ref, gob_ref, gg * jax.nn.sigmoid(gg), gn_ref, GDN_HEADS)
    yd = _gated_head_norm(mof_ref, mob_ref, jax.nn.sigmoid(mg_ref[0]), mn_ref, ML_HEADS)
    acc = jnp.zeros((h.shape[0], d), F32)
    for n, y in enumerate((ya_ref[0], yb, yc_ref[0], yd)):
        z = jnp.dot(y, wbo_ref[n], preferred_element_type=F32)
        gate = jnp.dot(h, wmg_ref[:, n * d:(n + 1) * d], preferred_element_type=F32) + bmg_ref[:, n * d:(n + 1) * d]
        acc = acc + jax.nn.sigmoid(gate) * z
    m = jnp.dot(acc.astype(BF16), wout_ref[...], preferred_element_type=F32)
    o_ref[0] = x_ref[0] + g1_ref[0, 0] * m


def merge_residual(h, ya, yc, gdn_o, ml_o, p3, x, mods, n_lat, n_rows, layer, gdn_norm_g, ml_norm_g,
                   w_mg, b_mg, w_bo, w_out, ts=256):
    b, _, d = x.shape
    ts = min(ts, n_rows - n_lat) if n_rows > n_lat else ts
    assert n_rows % ts == 0 and n_lat % ts == 0
    mod_row = _mod_row(n_lat // ts, b)
    row = lambda w, col=0: pl.BlockSpec((1, ts, w), lambda i, j: (i, j, col))
    depth = w_mg.shape[0]
    b_mg = b_mg.reshape(depth, 1, -1)
    gn, mn = gdn_norm_g.reshape(depth, 1, -1), ml_norm_g.reshape(depth, 1, -1)
    return pl.pallas_call(
        _merge_kernel,
        grid=(b, n_rows // ts),
        in_specs=[row(d), row(BRANCH_W), row(BRANCH_W),
                  row(BRANCH_W), row(BRANCH_W), row(BRANCH_W, COL_GDN_GATE // BRANCH_W),
                  row(BRANCH_W), row(BRANCH_W), row(BRANCH_W, COL_ML_GATE // BRANCH_W),
                  row(d), pl.BlockSpec((1, 1, 1, d), lambda i, j: (mod_row(i, j), G1, 0, 0))]
                 + [_layer(a, layer) for a in (gn, mn, w_mg, b_mg, w_bo, w_out)],
        out_specs=row(d),
        out_shape=jax.ShapeDtypeStruct((b, n_rows, d), F32),
        compiler_params=_params("parallel", "parallel"),
        name="merge_residual",
    )(h, ya, yc, gdn_o[0], gdn_o[1], p3, ml_o[0], ml_o[1], p3, x, mods, gn, mn, w_mg, b_mg, w_bo, w_out)


def _expert_kernel(be_ref, x_ref, wgu_ref, bgu_ref, wdn_ref, bdn_ref, o_ref, wgu_s, wdn_s):
    i = pl.program_id(0)
    fresh = jnp.logical_or(i == 0, be_ref[i] != be_ref[jnp.maximum(i - 1, 0)])

    @pl.when(fresh)
    def _():
        wgu_s[...] = wgu_ref[0].astype(BF16)
        wdn_s[...] = wdn_ref[0].astype(BF16)

    gu = jnp.dot(x_ref[...], wgu_s[...], preferred_element_type=F32) + bgu_ref[0]
    gate = jnp.minimum(gu[:, :D_FF], SWIGLU_LIMIT)
    up = jnp.clip(gu[:, D_FF:], -SWIGLU_LIMIT, SWIGLU_LIMIT)
    act = (up + 1.0) * gate * jax.nn.sigmoid(SWIGLU_ALPHA * gate)
    y = jnp.dot(act.astype(BF16), wdn_s[...], preferred_element_type=F32) + bdn_ref[0]
    o_ref[...] = y.astype(o_ref.dtype)


def expert_blocks(buf, block_e, layer, w_gu, b_gu, w_dn, b_dn):
    rows, d = buf.shape
    n_blocks = rows // MOE_ROWS
    depth, e = w_gu.shape[:2]
    grid_spec = pltpu.PrefetchScalarGridSpec(
        num_scalar_prefetch=1,
        grid=(n_blocks,),
        in_specs=[pl.BlockSpec((MOE_ROWS, d), lambda i, be: (i, 0)),
                  pl.BlockSpec((None, 1, d, 2 * D_FF), lambda i, be: (layer, be[i], 0, 0)),
                  pl.BlockSpec((None, 1, 1, 2 * D_FF), lambda i, be: (layer, be[i], 0, 0)),
                  pl.BlockSpec((None, 1, D_FF, d), lambda i, be: (layer, be[i], 0, 0)),
                  pl.BlockSpec((None, 1, 1, d), lambda i, be: (layer, be[i], 0, 0))],
        out_specs=pl.BlockSpec((MOE_ROWS, d), lambda i, be: (i, 0)),
        scratch_shapes=[pltpu.VMEM((d, 2 * D_FF), BF16), pltpu.VMEM((D_FF, d), BF16)],
    )
    return pl.pallas_call(
        _expert_kernel,
        grid_spec=grid_spec,
        out_shape=jax.ShapeDtypeStruct((rows, d), BF16),
        compiler_params=_params("arbitrary"),
        name="expert_blocks",
    )(block_e, buf, w_gu, b_gu.reshape(depth, e, 1, -1), w_dn, b_dn.reshape(depth, e, 1, -1))


def moe_ffn(t, top_i, layer, w_gu, b_gu, w_dn, b_dn):
    n, d = t.shape
    nk = n * TOP_K
    e_flat = top_i.reshape(nk)
    onehot = (e_flat[:, None] == jnp.arange(N_EXPERTS)[None, :]).astype(jnp.int32)
    csum = jnp.cumsum(onehot, axis=0)
    rank = jnp.take_along_axis(csum, e_flat[:, None], axis=1)[:, 0] - 1
    counts = csum[-1]
    padded = (counts + MOE_ROWS - 1) // MOE_ROWS * MOE_ROWS
    pad_end = jnp.cumsum(padded)
    dest = (pad_end - padded)[e_flat] + rank
    n_blocks = -(-nk // MOE_ROWS) + N_EXPERTS
    block_start = jnp.arange(n_blocks, dtype=pad_end.dtype) * MOE_ROWS
    block_e = jnp.minimum(jnp.sum(pad_end[None, :] <= block_start[:, None], axis=1), N_EXPERTS - 1).astype(jnp.int32)
    order = jnp.argsort(e_flat, stable=True).astype(jnp.int32)
    in_group = (block_start - (pad_end - padded)[block_e])[:, None] + jnp.arange(MOE_ROWS, dtype=pad_end.dtype)[None, :]
    first = (jnp.cumsum(counts) - counts)[block_e][:, None]
    used = in_group < counts[block_e][:, None]
    src = jnp.where(used, order[jnp.clip(first + in_group, 0, nk - 1)] // TOP_K, n).reshape(n_blocks * MOE_ROWS)
    buf = jnp.concatenate([t, jnp.zeros((1, d), t.dtype)], axis=0)[src]
    y = expert_blocks(buf, block_e, layer, w_gu, b_gu, w_dn, b_dn)
    return y[dest.reshape(n, TOP_K).T.reshape(nk)].reshape(TOP_K, n, d)


def kernel(x, c, ctx, c_ctx, w_ada, b_ada, norm1_g, norm2_g, w_in, na_rpb, gdn_conv_w, gdn_a_log, gdn_dt_bias, gdn_norm_g, mla_q_norm_g, mla_w_uq, mla_kv_norm_g, mla_w_ukv, ml_b_if, ml_norm_g, w_mg, b_mg, w_bo, w_out, router_w, router_b, w_gu, b_gu, w_dn, b_dn, final_g):
    depth = w_ada.shape[0]
    bsz, n_lat, d = x.shape
    n_ctx = ctx.shape[1]
    cond = jnp.concatenate([jax.nn.silu(c), jax.nn.silu(c_ctx)[None]], axis=0)
    cond = jnp.pad(cond, ((0, (-cond.shape[0]) % 8), (0, 0)))
    w_in_r = reorder_w_in(w_in)
    w_mg_b, w_bo_b, w_out_b = (cast_bf16(t) for t in (w_mg, w_bo, w_out))
    wq, wqr, wkv = mla_weights(mla_w_uq, mla_w_ukv)
    cos, sin = rope_tables(n_lat, n_ctx)
    xs = jnp.concatenate([x, ctx], axis=1)
    n_tok = n_lat + n_ctx
    for l in range(depth):
        last = l == depth - 1
        n_rows = n_lat if last else n_tok
        mods = (pmm(cond, w_ada, l, tn=1024) + b_ada[l]).reshape(cond.shape[0], 6, 1, d)
        h = norm_mod(xs, norm1_g[l], mods, SC1, SH1, n_lat)
        p3 = pmm(h.reshape(bsz * n_tok, d), w_in_r, l, tm=1024, tn=768).reshape(bsz, n_tok, P_WIDTH)
        ya = neighborhood_attention(p3, na_rpb[l], n_lat)
        gdn_o = gdn_scan(gdn_prep(p3, gdn_conv_w[l], n_lat), p3, gdn_a_log[l], gdn_dt_bias[l], n_lat)
        ml_o = mlstm_scan(p3, ml_b_if[l], n_lat)
        q, k, v = mla_prep(p3, cos, sin, mla_q_norm_g[l], mla_kv_norm_g[l], wq[l], wqr[l], wkv[l])
        lat_rows, ctx_rows = (0, n_lat), (n_lat, n_ctx)
        yc = pair_attention(q, k, v, MLA_HEADS, LANES, LANES, MLA_V, lat_rows, (0, n_tok))
        if not last:
            w = NA_HEADS * NA_DH
            ya_ctx = pair_attention(p3, p3, p3, NA_HEADS, NA_DH, NA_DH, NA_DH, ctx_rows, ctx_rows,
                                    cols=(COL_NA, COL_NA + w, COL_NA + 2 * w), scale=NA_DH ** -0.5)
            yc_ctx = pair_attention(q, k, v, MLA_HEADS, LANES, LANES, MLA_V, ctx_rows, ctx_rows)
            ya = jnp.concatenate([ya, ya_ctx], axis=1)
            yc = jnp.concatenate([yc, yc_ctx], axis=1)
        xs = merge_residual(h, ya, yc, gdn_o, ml_o, p3, xs, mods, n_lat, n_rows, l, gdn_norm_g, ml_norm_g,
                            w_mg_b, b_mg, w_bo_b, w_out_b)
        h2, top_i, top_w = norm_router(xs, norm2_g[l], mods, n_lat, router_w[l], router_b[l])
        y = moe_ffn(h2.reshape(bsz * n_rows, d), top_i.reshape(bsz * n_rows, TOP_K), l, w_gu, b_gu, w_dn, b_dn)
        xs = combine_residual(xs, y.reshape(TOP_K, bsz, n_rows, d), top_w, mods, G2, n_lat)
    return norm_mod(xs, final_g, jnp.zeros((cond.shape[0], 6, 1, d), F32), SC1, SH1, n_lat, out_dtype=F32)
```

```python
import functools
import math

import numpy as np
import jax
import jax.numpy as jnp
from jax import lax
from jax.experimental import pallas as pl
from jax.experimental.pallas import tpu as pltpu

F32 = jnp.float32
BF16 = jnp.bfloat16

GRID_W = 64
EPS = 1e-6
ROPE_BASE = 10000.0
NA_HEADS, NA_DH, NA_WIN_H, NA_WIN_W = 8, 64, 8, 16
GDN_HEADS, GDN_DK, GDN_DV, GDN_CHUNK = 4, 128, 128, 64
MLA_HEADS, MLA_NOPE, MLA_ROPE, MLA_V, MLA_Q_RANK, MLA_KV_RANK = 8, 64, 32, 64, 256, 128
ML_HEADS, ML_DK, ML_DV, ML_CHUNK = 4, 64, 128, 64
N_BRANCH, BRANCH_W = 4, 512
N_EXPERTS, TOP_K, D_FF = 32, 4, 1024
SWIGLU_ALPHA, SWIGLU_LIMIT = 1.702, 7.0
MOE_ROWS = 512
LANES = 128

IN_SPLITS = (
    NA_HEADS * NA_DH, NA_HEADS * NA_DH, NA_HEADS * NA_DH,
    GDN_HEADS * (2 * GDN_DK + GDN_DV), GDN_HEADS * GDN_DV,
    2 * GDN_HEADS, 2 * GDN_HEADS,
    MLA_Q_RANK, MLA_KV_RANK + MLA_ROPE,
    ML_HEADS * ML_DK, ML_HEADS * ML_DK, ML_HEADS * ML_DV, ML_HEADS * ML_DV,
    2 * ML_HEADS, 2 * ML_HEADS,
)
COL_NA = 0
COL_GDN = 1536
COL_GDN_GATE = 3072
COL_MLA = 3584
COL_ML = 4096
COL_ML_GATE = 5120
COL_SMALL = 5632
P_WIDTH = 6144
MLA_BLOCK = COL_ML - COL_MLA
SH1, SC1, G1, SH2, SC2, G2 = range(6)

VMEM_LIMIT_BYTES = 56 * 1024 * 1024


def _params(*sem):
    return pltpu.CompilerParams(dimension_semantics=sem, vmem_limit_bytes=VMEM_LIMIT_BYTES)


def _full(a):
    return pl.BlockSpec(a.shape, lambda *_: (0,) * a.ndim)


def _layer(a, layer):
    return pl.BlockSpec((None,) + a.shape[1:], lambda *_: (layer,) + (0,) * (a.ndim - 1))


def _dot(a, b, precise=False):
    if precise:
        return jnp.dot(a, b, preferred_element_type=F32, precision=lax.Precision.HIGHEST)
    return jnp.dot(a.astype(BF16), b.astype(BF16), preferred_element_type=F32)


def _dot_nt(a, b):
    return lax.dot_general(a.astype(BF16), b.astype(BF16), (((1,), (1,)), ((), ())), preferred_element_type=F32)


def _dot_tn(a, b):
    return lax.dot_general(a.astype(BF16), b.astype(BF16), (((0,), (0,)), ((), ())), preferred_element_type=F32)


def _softplus(x):
    return jnp.maximum(x, 0.0) + jnp.log(1.0 + jnp.exp(-jnp.abs(x)))


def _rms(x):
    return x * lax.rsqrt(jnp.mean(x * x, axis=-1, keepdims=True) + EPS)


def _mm_kernel(x_ref, w_ref, o_ref):
    o_ref[...] = _dot(x_ref[...], w_ref[...]).astype(o_ref.dtype)


def pmm(x, w, layer, out_dtype=F32, tm=512, tn=512):
    m, k = x.shape
    n = w.shape[-1]
    tm, tn = math.gcd(tm, m), min(tn, n)
    if n % tn:
        tn = n
    assert m % tm == 0 and n % tn == 0, (m, n, tm, tn)
    return pl.pallas_call(
        _mm_kernel,
        grid=(m // tm, n // tn),
        in_specs=[pl.BlockSpec((tm, k), lambda i, j: (i, 0)),
                  pl.BlockSpec((None, k, tn), lambda i, j: (layer, 0, j))],
        out_specs=pl.BlockSpec((tm, tn), lambda i, j: (i, j)),
        out_shape=jax.ShapeDtypeStruct((m, n), out_dtype),
        compiler_params=_params("parallel", "parallel"),
        name="pmm",
    )(x, w)


def _cast_kernel(x_ref, o_ref):
    o_ref[...] = x_ref[...].astype(o_ref.dtype)


def cast_bf16(w, rows=256):
    w2 = w.reshape(-1, w.shape[-1])
    m, n = w2.shape
    rows = math.gcd(rows, m)
    blk = pl.BlockSpec((rows, n), lambda i: (i, 0))
    out = pl.pallas_call(_cast_kernel, grid=(m // rows,), in_specs=[blk], out_specs=blk,
                         out_shape=jax.ShapeDtypeStruct((m, n), BF16),
                         compiler_params=_params("parallel"), name="cast_bf16")(w2)
    return out.reshape(w.shape)


W_IN_BLOCK = 512


def _w_in_kernel(head_ref, tail_ref, o_ref, *, n_head_blocks):
    @pl.when(pl.program_id(1) < n_head_blocks)
    def _():
        o_ref[...] = head_ref[...].astype(o_ref.dtype)

    @pl.when(pl.program_id(1) >= n_head_blocks)
    def _():
        o_ref[...] = tail_ref[...].astype(o_ref.dtype)


def reorder_w_in(w_in):
    depth, d, _ = w_in.shape
    o = np.cumsum((0,) + IN_SPLITS)
    assert o[5] == COL_MLA and COL_MLA % W_IN_BLOCK == 0 and P_WIDTH % W_IN_BLOCK == 0
    piece = lambda i: w_in[..., o[i]:o[i + 1]]
    zeros = lambda n: jnp.zeros(w_in.shape[:-1] + (n,), w_in.dtype)
    tail = jnp.concatenate([piece(7), piece(8), zeros(MLA_BLOCK - IN_SPLITS[7] - IN_SPLITS[8]),
                            piece(9), piece(10), piece(11), piece(12),
                            piece(5), piece(6), piece(13), piece(14), zeros(P_WIDTH - COL_SMALL - 8 * GDN_HEADS)],
                           axis=-1)
    assert tail.shape[-1] == P_WIDTH - COL_MLA
    nh_blocks, n_blocks = COL_MLA // W_IN_BLOCK, P_WIDTH // W_IN_BLOCK
    return pl.pallas_call(
        functools.partial(_w_in_kernel, n_head_blocks=nh_blocks),
        grid=(depth, n_blocks),
        in_specs=[pl.BlockSpec((1, d, W_IN_BLOCK), lambda l, j: (l, 0, jnp.minimum(j, nh_blocks - 1))),
                  pl.BlockSpec((1, d, W_IN_BLOCK), lambda l, j: (l, 0, jnp.maximum(j - nh_blocks, 0)))],
        out_specs=pl.BlockSpec((1, d, W_IN_BLOCK), lambda l, j: (l, 0, j)),
        out_shape=jax.ShapeDtypeStruct((depth, d, P_WIDTH), BF16),
        compiler_params=_params("parallel", "arbitrary"),
        name="reorder_w_in",
    )(w_in, tail)


def _mod_row(n_lat_blocks, ctx_row):
    return lambda i, j: jnp.where(j >= n_lat_blocks, ctx_row, i)


def _norm_mod_kernel(x_ref, g_ref, sc_ref, sh_ref, o_ref):
    y = _rms(x_ref[0]) * g_ref[...]
    o_ref[0] = (y * (1.0 + sc_ref[0, 0]) + sh_ref[0, 0]).astype(o_ref.dtype)


def norm_mod(x, g, mods, i_sc, i_sh, n_lat, out_dtype=BF16, ts=256):
    b, t, d = x.shape
    ts = min(ts, t - n_lat) if t > n_lat else ts
    assert t % ts == 0 and n_lat % ts == 0
    row = _mod_row(n_lat // ts, b)
    return pl.pallas_call(
        _norm_mod_kernel,
        grid=(b, t // ts),
        in_specs=[pl.BlockSpec((1, ts, d), lambda i, j: (i, j, 0)),
                  pl.BlockSpec((1, d), lambda i, j: (0, 0)),
                  pl.BlockSpec((1, 1, 1, d), lambda i, j: (row(i, j), i_sc, 0, 0)),
                  pl.BlockSpec((1, 1, 1, d), lambda i, j: (row(i, j), i_sh, 0, 0))],
        out_specs=pl.BlockSpec((1, ts, d), lambda i, j: (i, j, 0)),
        out_shape=jax.ShapeDtypeStruct((b, t, d), out_dtype),
        compiler_params=_params("parallel", "parallel"),
        name="norm_mod",
    )(x, g.reshape(1, d), mods, mods)


def _norm_router_kernel(x_ref, g_ref, sc_ref, sh_ref, rw_ref, rb_ref, h_ref, idx_ref, w_ref):
    h = (_rms(x_ref[0]) * g_ref[...]) * (1.0 + sc_ref[0, 0]) + sh_ref[0, 0]
    h_ref[0] = h.astype(h_ref.dtype)
    logits = lax.dot_general(rw_ref[...], h, (((1,), (1,)), ((), ())), preferred_element_type=F32,
                             precision=lax.Precision.HIGHEST) + rb_ref[...]
    n_exp = logits.shape[0]
    expert = lax.broadcasted_iota(jnp.int32, logits.shape, 0)
    vals, ids = [], []
    for _ in range(TOP_K):
        best = jnp.max(logits, axis=0, keepdims=True)
        first = jnp.min(jnp.where(logits == best, expert, n_exp), axis=0, keepdims=True)
        vals.append(best)
        ids.append(first)
        logits = jnp.where(expert == first, -jnp.inf, logits)
    e = [jnp.exp(v - vals[0]) for v in vals]
    denom = functools.reduce(jnp.add, e)
    idx_ref[0] = jnp.concatenate(ids, axis=0)
    w_ref[0] = jnp.concatenate([t / denom for t in e], axis=0)


def norm_router(x, g, mods, n_lat, router_w, router_b, ts=256):
    b, t, d = x.shape
    ts = min(ts, t - n_lat) if t > n_lat else ts
    assert t % ts == 0 and n_lat % ts == 0
    row = _mod_row(n_lat // ts, b)
    rw_t = router_w.T
    rb = router_b.reshape(-1, 1)
    slot = pl.BlockSpec((1, TOP_K, ts), lambda i, j: (i, 0, j))
    h2, idx_t, w_t = pl.pallas_call(
        _norm_router_kernel,
        grid=(b, t // ts),
        in_specs=[pl.BlockSpec((1, ts, d), lambda i, j: (i, j, 0)),
                  pl.BlockSpec((1, d), lambda i, j: (0, 0)),
                  pl.BlockSpec((1, 1, 1, d), lambda i, j: (row(i, j), SC2, 0, 0)),
                  pl.BlockSpec((1, 1, 1, d), lambda i, j: (row(i, j), SH2, 0, 0)),
                  _full(rw_t), _full(rb)],
        out_specs=[pl.BlockSpec((1, ts, d), lambda i, j: (i, j, 0)), slot, slot],
        out_shape=[jax.ShapeDtypeStruct((b, t, d), BF16), jax.ShapeDtypeStruct((b, TOP_K, t), jnp.int32),
                   jax.ShapeDtypeStruct((b, TOP_K, t), F32)],
        compiler_params=_params("parallel", "parallel"),
        name="norm_router",
    )(x, g.reshape(1, d), mods, mods, rw_t, rb)
    return h2, idx_t.transpose(0, 2, 1), w_t.transpose(0, 2, 1)


def _combine_residual_kernel(x_ref, y_ref, w_ref, g_ref, o_ref):
    w = w_ref[0]
    f = functools.reduce(jnp.add, [w[:, k:k + 1] * y_ref[k, 0].astype(F32) for k in range(y_ref.shape[0])])
    o_ref[0] = x_ref[0] + g_ref[0, 0] * f


def combine_residual(x, y, w, mods, i_gate, n_lat, ts=256):
    b, t, d = x.shape
    n_slots = y.shape[0]
    ts = min(ts, t - n_lat) if t > n_lat else ts
    row = _mod_row(n_lat // ts, b)
    blk = pl.BlockSpec((1, ts, d), lambda i, j: (i, j, 0))
    return pl.pallas_call(
        _combine_residual_kernel,
        grid=(b, t // ts),
        in_specs=[blk, pl.BlockSpec((n_slots, 1, ts, d), lambda i, j: (0, i, j, 0)),
                  pl.BlockSpec((1, ts, n_slots), lambda i, j: (i, j, 0)),
                  pl.BlockSpec((1, 1, 1, d), lambda i, j: (row(i, j), i_gate, 0, 0))],
        out_specs=blk,
        out_shape=jax.ShapeDtypeStruct((b, t, d), F32),
        compiler_params=_params("parallel", "parallel"),
        name="combine_residual",
    )(x, y, w, mods)


def _pair_attn_kernel(q_ref, k_ref, v_ref, o_ref, *, tk, dv, scale):
    dq, dvp = q_ref.shape[-1] // 2, v_ref.shape[-1] // 2
    tq = q_ref.shape[1]
    sum_on_mxu = dvp > dv
    outs = []
    for hh in range(2):
        q = q_ref[0][:, hh * dq:(hh + 1) * dq]
        q = (q if scale is None else q * scale).astype(BF16)
        m = jnp.full((tq, 1), -jnp.inf, F32)
        l = jnp.zeros((tq, 1), F32)
        acc = jnp.zeros((tq, dvp), F32)
        for c in range(k_ref.shape[1] // tk):
            k = k_ref[0, c * tk:(c + 1) * tk, hh * dq:(hh + 1) * dq].astype(BF16)
            v = v_ref[0, c * tk:(c + 1) * tk, hh * dvp:(hh + 1) * dvp].astype(BF16)
            s = lax.dot_general(q, k, (((1,), (1,)), ((), ())), preferred_element_type=F32)
            m_new = jnp.maximum(m, jnp.max(s, axis=-1, keepdims=True))
            alpha = jnp.exp(m - m_new)
            p = jnp.exp(s - m_new)
            if not sum_on_mxu:
                l = alpha * l + jnp.sum(p, axis=-1, keepdims=True)
            acc = alpha * acc + jnp.dot(p.astype(BF16), v, preferred_element_type=F32)
            m = m_new
        outs.append(acc[:, :dv] / (acc[:, dv:dv + 1] if sum_on_mxu else l))
    o_ref[0] = jnp.concatenate(outs, axis=-1).astype(o_ref.dtype)


def pair_attention(q, k, v, n_heads, dq, dvp, dv, q_rows, kv_rows, cols=(0, 0, 0), scale=None, tq=512):
    b = q.shape[0]
    (q0, n_q), (kv0, t) = q_rows, kv_rows
    dq2, dvp2 = 2 * dq, 2 * dvp
    tq = min(tq, n_q)
    tk = next(c for c in (768, 512, 384, 256, 128) if t % c == 0)
    assert n_q % tq == 0 and q0 % tq == 0 and kv0 % t == 0
    assert dq2 % LANES == 0 and dvp2 % LANES == 0 and (2 * dv) % LANES == 0
    assert all(c % w == 0 for c, w in zip(cols, (dq2, dq2, dvp2)))
    qc, kc, vc = (c // w for c, w in zip(cols, (dq2, dq2, dvp2)))
    return pl.pallas_call(
        functools.partial(_pair_attn_kernel, tk=tk, dv=dv, scale=scale),
        grid=(b, n_heads // 2, n_q // tq),
        in_specs=[pl.BlockSpec((1, tq, dq2), lambda i, hp, n: (i, q0 // tq + n, qc + hp)),
                  pl.BlockSpec((1, t, dq2), lambda i, hp, n: (i, kv0 // t, kc + hp)),
                  pl.BlockSpec((1, t, dvp2), lambda i, hp, n: (i, kv0 // t, vc + hp))],
        out_specs=pl.BlockSpec((1, tq, 2 * dv), lambda i, hp, n: (i, n, hp)),
        out_shape=jax.ShapeDtypeStruct((b, n_q, n_heads * dv), BF16),
        compiler_params=_params("parallel", "parallel", "parallel"),
        name="pair_attention",
    )(q, k, v)


NA_QROWS = 8
NA_KROWS = 4
NA_NEG = -1e30


def na_bias_tables(rpb, rows):
    n_steps = rows // NA_QROWS
    wh, ww = min(NA_WIN_H, rows), NA_WIN_W
    n_kblocks = rows // NA_KROWS
    n_heads = rpb.shape[0]
    width = 2 * GRID_W - 1
    lo = GRID_W - NA_WIN_W
    padded = jnp.full(rpb.shape[:2] + (width,), NA_NEG, F32).at[..., lo:lo + 2 * NA_WIN_W - 1].set(rpb)
    tiles = jnp.stack([padded[..., GRID_W - 1 - qc:width - qc] for qc in range(GRID_W)], axis=2)
    qc, kc = np.arange(GRID_W)[:, None], np.arange(GRID_W)[None, :]
    c0 = np.clip(qc - ww // 2, 0, GRID_W - ww)
    tiles = jnp.where((kc >= c0) & (kc < c0 + ww), tiles, NA_NEG)
    masked = jnp.full((n_heads, GRID_W, GRID_W), NA_NEG, F32)
    tables = []
    for i in (0, min(1, n_steps - 1), n_steps - 1):
        q_tiles = []
        for il in range(NA_QROWS):
            q = NA_QROWS * i + il
            r0 = min(max(q - wh // 2, 0), rows - wh)
            k_tiles = []
            for ka in range(4 * NA_KROWS):
                blk = 2 * i - 1 + ka // NA_KROWS
                kr = NA_KROWS * blk + ka % NA_KROWS
                ok = 0 <= blk < n_kblocks and r0 <= kr < r0 + wh
                k_tiles.append(tiles[:, kr - q + NA_WIN_H - 1] if ok else masked)
            q_tiles.append(jnp.concatenate(k_tiles, axis=-1))
        tables.append(jnp.concatenate(q_tiles, axis=-2))
    return jnp.stack(tables)


def _na_kernel(q_ref, k0_ref, k1_ref, k2_ref, k3_ref, v0_ref, v1_ref, v2_ref, v3_ref,
               kc_ref, vc_ref, bias_ref, o_ref):
    kw = NA_KROWS * GRID_W
    lane = lax.broadcasted_iota(jnp.int32, (1, 2 * NA_DH), 1)
    q2 = q_ref[0] * (NA_DH ** -0.5)
    ks = [r[0].astype(BF16) for r in (k0_ref, k1_ref, k2_ref, k3_ref, kc_ref)]
    v2 = [r[0] for r in (v0_ref, v1_ref, v2_ref, v3_ref, vc_ref)]
    outs = []
    for hh in range(2):
        own = (lane >= NA_DH) if hh else (lane < NA_DH)
        q = jnp.where(own, q2, 0.0).astype(BF16)
        vs = [jnp.where(own, t, 1.0).astype(BF16) for t in v2]
        ss = [lax.dot_general(q, kk, (((1,), (1,)), ((), ())), preferred_element_type=F32) for kk in ks]
        ss = [ss[j] + bias_ref[0, 0, hh, :, j * kw:(j + 1) * kw] for j in range(4)] + [ss[4]]
        m = functools.reduce(jnp.maximum, [jnp.max(t, axis=-1, keepdims=True) for t in ss])
        o = functools.reduce(jnp.add, [jnp.dot(jnp.exp(t - m).astype(BF16), vv, preferred_element_type=F32)
                                       for t, vv in zip(ss, vs)])
        denom = o[:, 0:1] if hh else o[:, NA_DH:NA_DH + 1]
        outs.append(o / denom)
    o_ref[0] = jnp.where(lane < NA_DH, outs[0], outs[1]).astype(o_ref.dtype)


def neighborhood_attention(p3, rpb, n_lat):
    b, t, _ = p3.shape
    c_len = t - n_lat
    rows = n_lat // GRID_W
    tq, tkv = NA_QROWS * GRID_W, NA_KROWS * GRID_W
    assert rows % NA_QROWS == 0 and rows >= 2 * NA_QROWS and n_lat % c_len == 0
    n_steps = rows // NA_QROWS
    n_kblocks = rows // NA_KROWS
    hw = 2 * NA_DH
    n_pairs = NA_HEADS // 2
    bias = na_bias_tables(rpb, rows)
    bias = bias.reshape(3, n_pairs, 2, *bias.shape[2:])

    def view(j, col0):
        return pl.BlockSpec((1, tkv, hw),
                            lambda hp, i, n: (i, jnp.clip(2 * n - 1 + j, 0, n_kblocks - 1), col0 + hp))

    ctx_spec = lambda col0: pl.BlockSpec((1, c_len, hw), lambda hp, i, n: (i, n_lat // c_len, col0 + hp))
    bias_spec = pl.BlockSpec(
        (1, 1, 2, tq, 4 * tkv),
        lambda hp, i, n: (jnp.where(n == 0, 0, jnp.where(n == n_steps - 1, 2, 1)), hp, 0, 0, 0))
    return pl.pallas_call(
        _na_kernel,
        grid=(n_pairs, b, n_steps),
        in_specs=[pl.BlockSpec((1, tq, hw), lambda hp, i, n: (i, n, hp))]
                 + [view(j, n_pairs) for j in range(4)] + [view(j, 2 * n_pairs) for j in range(4)]
                 + [ctx_spec(n_pairs), ctx_spec(2 * n_pairs), bias_spec],
        out_specs=pl.BlockSpec((1, tq, hw), lambda hp, i, n: (i, n, hp)),
        out_shape=jax.ShapeDtypeStruct((b, n_lat, NA_HEADS * NA_DH), BF16),
        compiler_params=_params("parallel", "parallel", "arbitrary"),
        name="neighborhood_attention",
    )(*([p3] * 11), bias)


def rope_tables(n_lat, n_ctx):
    half = MLA_ROPE // 2
    tpos = np.arange(n_lat)
    inv = ROPE_BASE ** (-jnp.arange(0, half, 2, dtype=F32) / half)
    pos = jnp.stack([tpos // GRID_W, tpos % GRID_W], axis=1).astype(F32)
    ang = pos[:, :, None] * inv
    ang = jnp.concatenate([ang, ang], axis=-1).reshape(n_lat, MLA_ROPE)
    pad = ((0, n_ctx), (MLA_NOPE, LANES - MLA_NOPE - MLA_ROPE))
    return (jnp.pad(jnp.cos(ang), pad, constant_values=1.0), jnp.pad(jnp.sin(ang), pad))


def _rope_swap():
    half = MLA_ROPE // 2
    quarter = half // 2
    r = np.zeros((MLA_ROPE, MLA_ROPE), np.float32)
    for j in range(MLA_ROPE):
        if j % half < quarter:
            r[j + quarter, j] = -1.0
        else:
            r[j - quarter, j] = 1.0
    return r


def mla_weights(w_uq, w_ukv):
    dqh = MLA_NOPE + MLA_ROPE
    lead = w_uq.shape[:-1]
    wq = w_uq.reshape(*lead, MLA_HEADS, dqh)
    padh = lambda a: jnp.pad(a, ((0, 0),) * (a.ndim - 1) + ((0, LANES - a.shape[-1]),))
    wq_p = padh(wq).reshape(*lead, MLA_HEADS * LANES)
    rot = jnp.einsum('...hr,rs->...hs', wq[..., MLA_NOPE:], jnp.asarray(_rope_swap()))
    wqr_p = jnp.pad(rot, ((0, 0),) * (rot.ndim - 1) + ((MLA_NOPE, LANES - dqh),)).reshape(*lead, MLA_HEADS * LANES)
    lead = w_ukv.shape[:-1]
    wkv = w_ukv.reshape(*lead, MLA_HEADS, MLA_NOPE + MLA_V)
    wk_p = padh(wkv[..., :MLA_NOPE]).reshape(*lead, MLA_HEADS * LANES)
    wv_p = padh(wkv[..., MLA_NOPE:]).reshape(*lead, MLA_HEADS * LANES)
    return wq_p.astype(BF16), wqr_p.astype(BF16), jnp.concatenate([wk_p, wv_p], axis=-1).astype(BF16)


def _mla_prep_kernel(p_ref, cos_ref, sin_ref, qg_ref, kvg_ref, wq_ref, wqr_ref, wkv_ref, e_ref, er_ref,
                     q_out, k_out, v_out):
    blk = p_ref[0]
    nq = _rms(blk[:, :MLA_Q_RANK]) * qg_ref[...]
    nkv = _rms(blk[:, MLA_Q_RANK:MLA_Q_RANK + MLA_KV_RANK]) * kvg_ref[...]
    kpe = blk[:, MLA_Q_RANK + MLA_KV_RANK:MLA_Q_RANK + MLA_KV_RANK + MLA_ROPE]
    cos, sin = cos_ref[...], sin_ref[...]
    qf, qr = _dot(nq, wq_ref[...]), _dot(nq, wqr_ref[...])
    kv = _dot(nkv, wkv_ref[...])
    kpe_tile = _dot(kpe, e_ref[...], precise=True) * cos + _dot(kpe, er_ref[...], precise=True) * sin
    scale = (MLA_NOPE + MLA_ROPE) ** -0.5
    for h in range(MLA_HEADS):
        sl = slice(h * LANES, (h + 1) * LANES)
        q_out[0, :, sl] = ((qf[:, sl] * cos + qr[:, sl] * sin) * scale).astype(q_out.dtype)
        k_out[0, :, sl] = (kv[:, sl] + kpe_tile).astype(k_out.dtype)
        ones_col = (lax.broadcasted_iota(jnp.int32, (1, LANES), 1) == MLA_V).astype(F32)
        v_out[0, :, sl] = (kv[:, MLA_HEADS * LANES + h * LANES:MLA_HEADS * LANES + (h + 1) * LANES]
                           + ones_col).astype(v_out.dtype)


def mla_prep(p3, cos, sin, q_norm_g, kv_norm_g, wq, wqr, wkv, tb=256):
    b, t, _ = p3.shape
    tb = next(c for c in (tb, 128) if t % c == 0)
    swap = _rope_swap()
    place = np.zeros((MLA_ROPE, LANES), np.float32)
    place[np.arange(MLA_ROPE), MLA_NOPE + np.arange(MLA_ROPE)] = 1.0
    e, er = jnp.asarray(place), jnp.asarray(swap @ place)
    qg, kvg = q_norm_g.reshape(1, -1), kv_norm_g.reshape(1, -1)
    row = lambda w: pl.BlockSpec((1, tb, w), lambda i, j: (i, j, 0))
    tab = pl.BlockSpec((tb, LANES), lambda i, j: (j, 0))
    return pl.pallas_call(
        _mla_prep_kernel,
        grid=(b, t // tb),
        in_specs=[pl.BlockSpec((1, tb, MLA_BLOCK), lambda i, j: (i, j, COL_MLA // MLA_BLOCK)), tab, tab]
                 + [_full(a) for a in (qg, kvg, wq, wqr, wkv, e, er)],
        out_specs=[row(MLA_HEADS * LANES)] * 3,
        out_shape=[jax.ShapeDtypeStruct((b, t, MLA_HEADS * LANES), BF16),
                   jax.ShapeDtypeStruct((b, t, MLA_HEADS * LANES), BF16),
                   jax.ShapeDtypeStruct((b, t, MLA_HEADS * LANES), BF16)],
        compiler_params=_params("parallel", "parallel"),
        name="mla_prep",
    )(p3, cos, sin, qg, kvg, wq, wqr, wkv, e, er)


def _gdn_prep_kernel(x_ref, prev_ref, next_ref, w_ref, o_ref, *, n_lat, n_tok):
    tb = x_ref.shape[1]
    start = pl.program_id(1) * tb
    x = x_ref[0]
    seg_first = jnp.logical_or(start == 0, start == n_lat)
    seg_last = jnp.logical_or(start + tb == n_lat, start + tb == n_tok)
    prev_row = jnp.where(seg_first, 0.0, prev_ref[0][7:8, :])
    next_row = jnp.where(seg_last, 0.0, next_ref[0][0:1, :])
    rows = lax.broadcasted_iota(jnp.int32, (tb, 1), 0)
    x_m1 = jnp.where(rows == 0, prev_row, pltpu.roll(x, 1, axis=0))
    x_p1 = jnp.where(rows == tb - 1, next_row, pltpu.roll(x, tb - 1, axis=0))
    y = x_m1 * w_ref[0:1, :] + x * w_ref[1:2, :] + x_p1 * w_ref[2:3, :]
    y = y * jax.nn.sigmoid(y)
    for g in range(2 * GDN_HEADS):
        sl = slice(g * GDN_DK, (g + 1) * GDN_DK)
        yh = y[:, sl]
        yh = yh * lax.rsqrt(jnp.sum(yh * yh, axis=-1, keepdims=True) + EPS)
        o_ref[0, :, sl] = yh * (GDN_DK ** -0.5) if g < GDN_HEADS else yh
    o_ref[0, :, 2 * GDN_HEADS * GDN_DK:] = y[:, 2 * GDN_HEADS * GDN_DK:]


def gdn_prep(p3, conv_w, n_lat, tb=256):
    b, t, _ = p3.shape
    w = GDN_HEADS * (2 * GDN_DK + GDN_DV)
    tb = min(tb, t - n_lat)
    assert t % tb == 0 and n_lat % tb == 0 and tb % 8 == 0 and COL_GDN % w == 0
    col = COL_GDN // w
    sub = tb // 8
    return pl.pallas_call(
        functools.partial(_gdn_prep_kernel, n_lat=n_lat, n_tok=t),
        grid=(b, t // tb),
        in_specs=[pl.BlockSpec((1, tb, w), lambda i, j: (i, j, col)),
                  pl.BlockSpec((1, 8, w), lambda i, j: (i, jnp.maximum(j * sub - 1, 0), col)),
                  pl.BlockSpec((1, 8, w), lambda i, j: (i, jnp.minimum((j + 1) * sub, t // 8 - 1), col)),
                  _full(conv_w)],
        out_specs=pl.BlockSpec((1, tb, w), lambda i, j: (i, j, 0)),
        out_shape=jax.ShapeDtypeStruct((b, t, w), F32),
        compiler_params=_params("parallel", "parallel"),
        name="gdn_prep",
    )(p3, p3, p3, conv_w)


def _order_masks(length, reverse):
    r = lax.broadcasted_iota(jnp.int32, (length, length), 0)
    c = lax.broadcasted_iota(jnp.int32, (length, length), 1)
    if reverse:
        return r <= c, r < c, r >= c
    return r >= c, r > c, r <= c


def _as_row(col, mask):
    return jnp.sum(jnp.where(mask, col, 0.0), axis=0, keepdims=True)


def _gdn_kernel(xf_ref, sf_ref, xb_ref, sb_ref, alog_ref, dtb_ref, of_ref, ob_ref, s_ref):
    @pl.when(pl.program_id(1) == 0)
    def _():
        s_ref[...] = jnp.zeros_like(s_ref)

    nh, dk, dv, length = GDN_HEADS, GDN_DK, GDN_DV, GDN_CHUNK
    n_sub = xf_ref.shape[1] // length
    r = lax.broadcasted_iota(jnp.int32, (length, length), 0)
    c = lax.broadcasted_iota(jnp.int32, (length, length), 1)
    eye = (r == c).astype(F32)
    chunks = []
    for sub in range(n_sub):
        seqs = []
        for d, (x_ref, sm_ref, o_ref) in enumerate(((xf_ref, sf_ref, of_ref), (xb_ref, sb_ref, ob_ref))):
            r0 = (n_sub - 1 - sub if d == 1 else sub) * length
            rows = slice(r0, r0 + length)
            incl, strict, incl_t = _order_masks(length, d == 1)
            small = sm_ref[0, rows, :]
            hs = slice(d * nh, (d + 1) * nh)
            la_col = -jnp.exp(alog_ref[:, hs]) * _softplus(small[:, hs] + dtb_ref[:, hs])
            be_col = jax.nn.sigmoid(small[:, 2 * nh + d * nh:2 * nh + (d + 1) * nh])
            g_cols = _dot(incl.astype(F32), la_col, precise=True)
            g_ends = jnp.sum(la_col, axis=0, keepdims=True)
            for h in range(nh):
                q = x_ref[0, rows, h * dk:(h + 1) * dk]
                k = x_ref[0, rows, (nh + h) * dk:(nh + h + 1) * dk]
                v = x_ref[0, rows, 2 * nh * dk + h * dv:2 * nh * dk + (h + 1) * dv]
                gc, ge, bc = g_cols[:, h:h + 1], g_ends[:, h:h + 1], be_col[:, h:h + 1]
                gr = _as_row(la_col[:, h:h + 1], incl_t)
                decay = jnp.exp(jnp.where(incl, gc - gr, -jnp.inf))
                qk = decay * _dot_nt(q, k)
                p = -(bc * jnp.where(strict, decay, 0.0) * _dot_nt(k, k))
                eg = jnp.exp(gc)
                seqs.append(dict(o_ref=o_ref, rows=rows, h=h, idx=d * nh + h, q_dec=eg * q, qk=qk, p=p, t=eye + p,
                                 rhs=jnp.concatenate([bc * v, (bc * eg) * k], axis=-1),
                                 k_end=jnp.exp(ge - gc) * k, e_end=jnp.exp(ge)))
        for _ in range(int(math.log2(length)) - 1):
            for s in seqs:
                s['p'] = _dot(s['p'], s['p'])
            for s in seqs:
                s['t'] = s['t'] + _dot(s['t'], s['p'])
        for s in seqs:
            s['sol'] = _dot(s['t'], s['rhs'])
        chunks.append(seqs)
    states = [s_ref[i] for i in range(2 * nh)]
    for seqs in chunks:
        for s in seqs:
            s['u'] = s['sol'][:, :dv] - _dot_nt(s['sol'][:, dv:], states[s['idx']])
        for s in seqs:
            h, s_old = s['h'], states[s['idx']]
            s['o_ref'][0, s['rows'], h * dv:(h + 1) * dv] = _dot_nt(s['q_dec'], s_old) + _dot(s['qk'], s['u'])
            states[s['idx']] = s['e_end'] * s_old + _dot_tn(s['u'], s['k_end'])
    for i, st in enumerate(states):
        s_ref[i] = st


def _mlstm_kernel(xf_ref, sf_ref, xb_ref, sb_ref, bi_ref, bf_ref, of_ref, ob_ref, c_ref, n_ref, m_ref):
    @pl.when(pl.program_id(1) == 0)
    def _():
        c_ref[...] = jnp.zeros_like(c_ref)
        n_ref[...] = jnp.zeros_like(n_ref)
        m_ref[...] = jnp.zeros_like(m_ref)

    nh, dk, dv, length = ML_HEADS, ML_DK, ML_DV, ML_CHUNK
    n_sub = xf_ref.shape[1] // length
    r = lax.broadcasted_iota(jnp.int32, (length, length), 0)
    c = lax.broadcasted_iota(jnp.int32, (length, length), 1)
    eye = r == c
    col_i, col_f = 4 * GDN_HEADS, 4 * GDN_HEADS + 2 * nh
    chunks = []
    for sub in range(n_sub):
        seqs = []
        for d, (x_ref, sm_ref, o_ref) in enumerate(((xf_ref, sf_ref, of_ref), (xb_ref, sb_ref, ob_ref))):
            r0 = (n_sub - 1 - sub if d == 1 else sub) * length
            rows = slice(r0, r0 + length)
            incl, _, incl_t = _order_masks(length, d == 1)
            small = sm_ref[0, rows, :]
            hs = slice(d * nh, (d + 1) * nh)
            i_col = small[:, col_i + d * nh:col_i + (d + 1) * nh] + bi_ref[:, hs]
            f_col = -_softplus(-(small[:, col_f + d * nh:col_f + (d + 1) * nh] + bf_ref[:, hs]))
            fc_cols = _dot(incl.astype(F32), f_col, precise=True)
            f_ends = jnp.sum(f_col, axis=0, keepdims=True)
            for h in range(nh):
                q = x_ref[0, rows, h * dk:(h + 1) * dk] * (dk ** -0.5)
                k = x_ref[0, rows, (nh + h) * dk:(nh + h + 1) * dk]
                v = x_ref[0, rows, 2 * nh * dk + h * dv:2 * nh * dk + (h + 1) * dv]
                fc, fe, ic = fc_cols[:, h:h + 1], f_ends[:, h:h + 1], i_col[:, h:h + 1]
                fr, ir = _as_row(f_col[:, h:h + 1], incl_t), _as_row(ic, eye)
                d_log = jnp.where(incl, fc - fr + ir, -jnp.inf)
                d_max = jnp.max(d_log, axis=-1, keepdims=True)
                w_log = fe - fc + ic
                w_max = jnp.max(w_log, axis=0, keepdims=True)
                seqs.append(dict(o_ref=o_ref, rows=rows, h=h, idx=d * nh + h, q=q, k=k, v=v, fc=fc, fe=fe,
                                 d_max=d_max, w_max=w_max, decay=jnp.exp(d_log - d_max), wk=jnp.exp(w_log - w_max)))
        for s in seqs:
            s['intra'] = s['decay'] * _dot_nt(s['q'], s['k'])
        for s in seqs:
            s['intra_v'] = _dot(s['intra'], s['v'])
            s['vk'] = _dot_tn(s['wk'] * s['v'], s['k'])
        for s in seqs:
            s['intra_sum'] = jnp.sum(s['intra'], axis=-1, keepdims=True)
            s['k_sum'] = jnp.sum(s['wk'] * s['k'], axis=0, keepdims=True)
        chunks.append(seqs)
    n_seq = 2 * nh
    cs = [c_ref[i] for i in range(n_seq)]
    ns = [n_ref[i] for i in range(n_seq)]
    ms = [m_ref[i][:, 0:1] for i in range(n_seq)]
    for seqs in chunks:
        for s in seqs:
            m_old = ms[s['idx']]
            s['m_q'] = jnp.maximum(s['fc'] + m_old, s['d_max'])
            s['inter'], s['scale'] = jnp.exp(s['fc'] + m_old - s['m_q']), jnp.exp(s['d_max'] - s['m_q'])
            s['m_new'] = jnp.maximum(s['fe'] + m_old, s['w_max'])
            s['dec'], s['w_scale'] = jnp.exp(s['fe'] + m_old - s['m_new']), jnp.exp(s['w_max'] - s['m_new'])
        for s in seqs:
            s['qc'] = _dot_nt(s['q'], cs[s['idx']])
        for s in seqs:
            h, idx = s['h'], s['idx']
            num = s['inter'] * s['qc'] + s['scale'] * s['intra_v']
            den = (s['inter'] * jnp.sum(s['q'] * ns[idx], axis=-1, keepdims=True) + s['scale'] * s['intra_sum'])
            s['o_ref'][0, s['rows'], h * dv:(h + 1) * dv] = num / jnp.maximum(jnp.abs(den), jnp.exp(-s['m_q']))
            cs[idx] = s['dec'] * cs[idx] + s['w_scale'] * s['vk']
            ns[idx] = s['dec'] * ns[idx] + s['w_scale'] * s['k_sum']
            ms[idx] = s['m_new']
    for i in range(n_seq):
        c_ref[i], n_ref[i] = cs[i], ns[i]
        m_ref[i] = jnp.broadcast_to(ms[i], m_ref.shape[1:])


GDN_SCAN_BLOCK = 256
ML_SCAN_BLOCK = 128


def bidirectional_scan(kernel_fn, x, x_col, x_w, p3, params, n_lat, block, out_w, scratch_shapes, name):
    b, t, _ = p3.shape
    length = math.gcd(block, t - n_lat)
    nl, nc = n_lat // length, t // length
    ncx = nc - nl
    assert t % length == 0 and n_lat % length == 0
    fwd = lambda n: jnp.where(n < ncx, nl + n, n - ncx)
    bwd = lambda n: jnp.where(n < ncx, nc - 1 - n, nl - 1 - (n - ncx))
    specs = lambda f: [pl.BlockSpec((1, length, x_w), lambda i, n: (i, f(n), x_col)),
                       pl.BlockSpec((1, length, LANES), lambda i, n: (i, f(n), COL_SMALL // LANES))]
    out = lambda f: pl.BlockSpec((1, length, out_w), lambda i, n: (i, f(n), 0))
    return pl.pallas_call(
        kernel_fn,
        grid=(b, nc),
        in_specs=specs(fwd) + specs(bwd) + [_full(a) for a in params],
        out_specs=[out(fwd), out(bwd)],
        out_shape=[jax.ShapeDtypeStruct((b, t, out_w), F32)] * 2,
        scratch_shapes=scratch_shapes,
        compiler_params=_params("parallel", "arbitrary"),
        name=name,
    )(x, p3, x, p3, *params)


def gdn_scan(xg, p3, a_log, dt_bias, n_lat):
    state = pltpu.VMEM((2 * GDN_HEADS, GDN_DV, GDN_DK), F32)
    params = (a_log.reshape(1, -1), dt_bias.reshape(1, -1))
    return bidirectional_scan(_gdn_kernel, xg, 0, xg.shape[-1], p3, params, n_lat, GDN_SCAN_BLOCK,
                              GDN_HEADS * GDN_DV, [state], "gdn_scan")


def mlstm_scan(p3, b_if, n_lat):
    n_seq = 2 * ML_HEADS
    w = ML_HEADS * (2 * ML_DK + ML_DV)
    scratch = [pltpu.VMEM((n_seq, ML_DV, ML_DK), F32), pltpu.VMEM((n_seq, 1, ML_DK), F32),
               pltpu.VMEM((n_seq, 1, LANES), F32)]
    params = (b_if[:, 0].reshape(1, -1), b_if[:, 1].reshape(1, -1))
    return bidirectional_scan(_mlstm_kernel, p3, COL_ML // w, w, p3, params, n_lat, ML_SCAN_BLOCK,
                              ML_HEADS * ML_DV, scratch, "mlstm_scan")


def _gated_head_norm(of_ref, ob_ref, gate, g_ref, n_heads):
    o = of_ref[0] + ob_ref[0]
    d = o.shape[-1] // n_heads
    parts = [_rms(o[:, h * d:(h + 1) * d]) * g_ref[...] for h in range(n_heads)]
    return (jnp.concatenate(parts, axis=-1) * gate).astype(BF16)


def _merge_kernel(h_ref, ya_ref, yc_ref, gof_ref, gob_ref, gg_ref, mof_ref, mob_ref, mg_ref, x_ref, g1_ref,
                  gn_ref, mn_ref, wmg_ref, bmg_ref, wbo_ref, wout_ref, o_ref):
    d = o_ref.shape[-1]
    h = h_ref[0]
    gg = gg_ref[0]
    yb = _gated_head_norm(gof_ref, gob_ref, gg * jax.nn.sigmoid(gg), gn_ref, GDN_HEADS)
    yd = _gated_head_norm(mof_ref, mob_ref, jax.nn.sigmoid(mg_ref[0]), mn_ref, ML_HEADS)
    acc = jnp.zeros((h.shape[0], d), F32)
    for n, y in enumerate((ya_ref[0], yb, yc_ref[0], yd)):
        z = jnp.dot(y, wbo_ref[n], preferred_element_type=F32)
        gate = jnp.dot(h, wmg_ref[:, n * d:(n + 1) * d], preferred_element_type=F32) + bmg_ref[:, n * d:(n + 1) * d]
        acc = acc + jax.nn.sigmoid(gate) * z
    m = jnp.dot(acc.astype(BF16), wout_ref[...], preferred_element_type=F32)
    o_ref[0] = x_ref[0] + g1_ref[0, 0] * m


def merge_residual(h, ya, yc, gdn_o, ml_o, p3, x, mods, n_lat, n_rows, layer, gdn_norm_g, ml_norm_g,
                   w_mg, b_mg, w_bo, w_out, ts=256):
    b, _, d = x.shape
    ts = min(ts, n_rows - n_lat) if n_rows > n_lat else ts
    assert n_rows % ts == 0 and n_lat % ts == 0
    mod_row = _mod_row(n_lat // ts, b)
    row = lambda w, col=0: pl.BlockSpec((1, ts, w), lambda i, j: (i, j, col))
    depth = w_mg.shape[0]
    b_mg = b_mg.reshape(depth, 1, -1)
    gn, mn = gdn_norm_g.reshape(depth, 1, -1), ml_norm_g.reshape(depth, 1, -1)
    return pl.pallas_call(
        _merge_kernel,
        grid=(b, n_rows // ts),
        in_specs=[row(d), row(BRANCH_W), row(BRANCH_W),
                  row(BRANCH_W), row(BRANCH_W), row(BRANCH_W, COL_GDN_GATE // BRANCH_W),
                  row(BRANCH_W), row(BRANCH_W), row(BRANCH_W, COL_ML_GATE // BRANCH_W),
                  row(d), pl.BlockSpec((1, 1, 1, d), lambda i, j: (mod_row(i, j), G1, 0, 0))]
                 + [_layer(a, layer) for a in (gn, mn, w_mg, b_mg, w_bo, w_out)],
        out_specs=row(d),
        out_shape=jax.ShapeDtypeStruct((b, n_rows, d), F32),
        compiler_params=_params("parallel", "parallel"),
        name="merge_residual",
    )(h, ya, yc, gdn_o[0], gdn_o[1], p3, ml_o[0], ml_o[1], p3, x, mods, gn, mn, w_mg, b_mg, w_bo, w_out)


def _expert_kernel(be_ref, x_ref, wgu_ref, bgu_ref, wdn_ref, bdn_ref, o_ref, wgu_s, wdn_s):
    i = pl.program_id(0)
    fresh = jnp.logical_or(i == 0, be_ref[i] != be_ref[jnp.maximum(i - 1, 0)])

    @pl.when(fresh)
    def _():
        wgu_s[...] = wgu_ref[0].astype(BF16)
        wdn_s[...] = wdn_ref[0].astype(BF16)

    gu = jnp.dot(x_ref[...], wgu_s[...], preferred_element_type=F32) + bgu_ref[0]
    gate = jnp.minimum(gu[:, :D_FF], SWIGLU_LIMIT)
    up = jnp.clip(gu[:, D_FF:], -SWIGLU_LIMIT, SWIGLU_LIMIT)
    act = (up + 1.0) * gate * jax.nn.sigmoid(SWIGLU_ALPHA * gate)
    y = jnp.dot(act.astype(BF16), wdn_s[...], preferred_element_type=F32) + bdn_ref[0]
    o_ref[...] = y.astype(o_ref.dtype)


def expert_blocks(buf, block_e, layer, w_gu, b_gu, w_dn, b_dn):
    rows, d = buf.shape
    n_blocks = rows // MOE_ROWS
    depth, e = w_gu.shape[:2]
    grid_spec = pltpu.PrefetchScalarGridSpec(
        num_scalar_prefetch=1,
        grid=(n_blocks,),
        in_specs=[pl.BlockSpec((MOE_ROWS, d), lambda i, be: (i, 0)),
                  pl.BlockSpec((None, 1, d, 2 * D_FF), lambda i, be: (layer, be[i], 0, 0)),
                  pl.BlockSpec((None, 1, 1, 2 * D_FF), lambda i, be: (layer, be[i], 0, 0)),
                  pl.BlockSpec((None, 1, D_FF, d), lambda i, be: (layer, be[i], 0, 0)),
                  pl.BlockSpec((None, 1, 1, d), lambda i, be: (layer, be[i], 0, 0))],
        out_specs=pl.BlockSpec((MOE_ROWS, d), lambda i, be: (i, 0)),
        scratch_shapes=[pltpu.VMEM((d, 2 * D_FF), BF16), pltpu.VMEM((D_FF, d), BF16)],
    )
    return pl.pallas_call(
        _expert_kernel,
        grid_spec=grid_spec,
        out_shape=jax.ShapeDtypeStruct((rows, d), BF16),
        compiler_params=_params("arbitrary"),
        name="expert_blocks",
    )(block_e, buf, w_gu, b_gu.reshape(depth, e, 1, -1), w_dn, b_dn.reshape(depth, e, 1, -1))


def moe_ffn(t, top_i, layer, w_gu, b_gu, w_dn, b_dn):
    n, d = t.shape
    nk = n * TOP_K
    e_flat = top_i.reshape(nk)
    onehot = (e_flat[:, None] == jnp.arange(N_EXPERTS)[None, :]).astype(jnp.int32)
    csum = jnp.cumsum(onehot, axis=0)
    rank = jnp.take_along_axis(csum, e_flat[:, None], axis=1)[:, 0] - 1
    counts = csum[-1]
    padded = (counts + MOE_ROWS - 1) // MOE_ROWS * MOE_ROWS
    pad_end = jnp.cumsum(padded)
    dest = (pad_end - padded)[e_flat] + rank
    n_blocks = -(-nk // MOE_ROWS) + N_EXPERTS
    block_start = jnp.arange(n_blocks, dtype=pad_end.dtype) * MOE_ROWS
    block_e = jnp.minimum(jnp.sum(pad_end[None, :] <= block_start[:, None], axis=1), N_EXPERTS - 1).astype(jnp.int32)
    order = jnp.argsort(e_flat, stable=True).astype(jnp.int32)
    in_group = (block_start - (pad_end - padded)[block_e])[:, None] + jnp.arange(MOE_ROWS, dtype=pad_end.dtype)[None, :]
    first = (jnp.cumsum(counts) - counts)[block_e][:, None]
    used = in_group < counts[block_e][:, None]
    src = jnp.where(used, order[jnp.clip(first + in_group, 0, nk - 1)] // TOP_K, n).reshape(n_blocks * MOE_ROWS)
    buf = jnp.concatenate([t, jnp.zeros((1, d), t.dtype)], axis=0)[src]
    y = expert_blocks(buf, block_e, layer, w_gu, b_gu, w_dn, b_dn)
    return y[dest.reshape(n, TOP_K).T.reshape(nk)].reshape(TOP_K, n, d)


def kernel(x, c, ctx, c_ctx, w_ada, b_ada, norm1_g, norm2_g, w_in, na_rpb, gdn_conv_w, gdn_a_log, gdn_dt_bias, gdn_norm_g, mla_q_norm_g, mla_w_uq, mla_kv_norm_g, mla_w_ukv, ml_b_if, ml_norm_g, w_mg, b_mg, w_bo, w_out, router_w, router_b, w_gu, b_gu, w_dn, b_dn, final_g):
    depth = w_ada.shape[0]
    bsz, n_lat, d = x.shape
    n_ctx = ctx.shape[1]
    cond = jnp.concatenate([jax.nn.silu(c), jax.nn.silu(c_ctx)[None]], axis=0)
    cond = jnp.pad(cond, ((0, (-cond.shape[0]) % 8), (0, 0)))
    w_in_r = reorder_w_in(w_in)
    w_mg_b, w_bo_b, w_out_b = (cast_bf16(t) for t in (w_mg, w_bo, w_out))
    wq, wqr, wkv = mla_weights(mla_w_uq, mla_w_ukv)
    cos, sin = rope_tables(n_lat, n_ctx)
    xs = jnp.concatenate([x, ctx], axis=1)
    n_tok = n_lat + n_ctx
    for l in range(depth):
        last = l == depth - 1
        n_rows = n_lat if last else n_tok
        mods = (pmm(cond, w_ada, l, tn=1024) + b_ada[l]).reshape(cond.shape[0], 6, 1, d)
        h = norm_mod(xs, norm1_g[l], mods, SC1, SH1, n_lat)
        p3 = pmm(h.reshape(bsz * n_tok, d), w_in_r, l, tm=1024, tn=768).reshape(bsz, n_tok, P_WIDTH)
        ya = neighborhood_attention(p3, na_rpb[l], n_lat)
        gdn_o = gdn_scan(gdn_prep(p3, gdn_conv_w[l], n_lat), p3, gdn_a_log[l], gdn_dt_bias[l], n_lat)
        ml_o = mlstm_scan(p3, ml_b_if[l], n_lat)
        q, k, v = mla_prep(p3, cos, sin, mla_q_norm_g[l], mla_kv_norm_g[l], wq[l], wqr[l], wkv[l])
        lat_rows, ctx_rows = (0, n_lat), (n_lat, n_ctx)
        yc = pair_attention(q, k, v, MLA_HEADS, LANES, LANES, MLA_V, lat_rows, (0, n_tok))
        if not last:
            w = NA_HEADS * NA_DH
            ya_ctx = pair_attention(p3, p3, p3, NA_HEADS, NA_DH, NA_DH, NA_DH, ctx_rows, ctx_rows,
                                    cols=(COL_NA, COL_NA + w, COL_NA + 2 * w), scale=NA_DH ** -0.5)
            yc_ctx = pair_attention(q, k, v, MLA_HEADS, LANES, LANES, MLA_V, ctx_rows, ctx_rows)
            ya = jnp.concatenate([ya, ya_ctx], axis=1)
            yc = jnp.concatenate([yc, yc_ctx], axis=1)
        xs = merge_residual(h, ya, yc, gdn_o, ml_o, p3, xs, mods, n_lat, n_rows, l, gdn_norm_g, ml_norm_g,
                            w_mg_b, b_mg, w_bo_b, w_out_b)
        h2, top_i, top_w = norm_router(xs, norm2_g[l], mods, n_lat, router_w[l], router_b[l])
        y = moe_ffn(h2.reshape(bsz * n_rows, d), top_i.reshape(bsz * n_rows, TOP_K), l, w_gu, b_gu, w_dn, b_dn)
        xs = combine_residual(xs, y.reshape(TOP_K, bsz, n_rows, d), top_w, mods, G2, n_lat)
    return norm_mod(xs, final_g, jnp.zeros((cond.shape[0], 6, 1, d), F32), SC1, SH1, n_lat, out_dtype=F32)
```
